```python
import math
import jax, jax.numpy as jnp
from jax import lax
import numpy as np

D_MODEL = 1024
BATCH = 8
SEQ = 2048
DEPTH = 2
DEC_BATCH = 128
DEC_SEQ = 1
PAST_LEN = 16384
PAGE_SIZE = 128

N_EVEN = (DEPTH + 1) // 2
N_ODD = DEPTH // 2
CONV_W = 4
CHUNK = 64
EPS = 1e-6

GDN_HEADS = 8
GDN_DK = 128
GDN_DV = 128
GDN_KW = GDN_HEADS * GDN_DK
GDN_VW = GDN_HEADS * GDN_DV
GDN_CONV_DIM = 2 * GDN_KW + GDN_VW
GLA_HEADS = 4
GLA_DK = D_MODEL // 2 // GLA_HEADS
GLA_DV = D_MODEL // GLA_HEADS
GLA_KW = GLA_HEADS * GLA_DK
GLA_VW = GLA_HEADS * GLA_DV
GLA_RANK = 16
GLA_TAU = 16.0
SSD_DI = 2 * D_MODEL
SSD_P = 64
SSD_HEADS = SSD_DI // SSD_P
SSD_N = 128
SSD_G = 4
SSD_CONV_DIM = SSD_DI + 2 * SSD_G * SSD_N

EVEN_SPLITS = (GDN_KW, GDN_KW, GDN_VW, GDN_VW, GDN_HEADS, GDN_HEADS, GLA_KW, GLA_KW, GLA_VW, GLA_VW, GLA_RANK)
EVEN_IN = sum(EVEN_SPLITS)
EVEN_OUT = GDN_VW + GLA_VW
ODD_SPLITS = (SSD_DI, SSD_CONV_DIM, SSD_HEADS)
ODD_IN = sum(ODD_SPLITS)

kernel_name = 'hybrid_gdn_gla_ssd_decode_step'


def _split(t, sizes):
    return jnp.split(t, np.cumsum(sizes)[:-1].tolist(), axis=-1)


def rmsnorm(x, g):
    xf = x.astype(jnp.float32)
    y = xf * lax.rsqrt(jnp.mean(xf * xf, axis=-1, keepdims=True) + EPS)
    return (y * g.astype(jnp.float32)).astype(x.dtype)


def l2norm(x):
    return x * lax.rsqrt(jnp.sum(x * x, axis=-1, keepdims=True) + EPS)


def causal_conv(x, buf, w, b):
    L = x.shape[1]
    xp = jnp.concatenate([buf.astype(x.dtype), x], axis=1)
    out = b.astype(x.dtype)
    for i in range(CONV_W):
        out = out + xp[:, i:i + L] * w[i].astype(x.dtype)
    return jax.nn.silu(out), xp[:, L:]


def _chunk(t, C):
    L = t.shape[1]
    nc = -(-L // C)
    pad = nc * C - L
    if pad:
        t = jnp.pad(t, [(0, 0), (0, pad)] + [(0, 0)] * (t.ndim - 2))
    return t.reshape((t.shape[0], nc, C) + t.shape[2:])


def _heads_first(t):
    return jnp.moveaxis(jnp.moveaxis(t, 1, 0), 3, 2)


def _unchunk(o, L):
    nc, Bsz, H, C, E = o.shape
    o = jnp.swapaxes(jnp.moveaxis(o, 0, 1), 2, 3)
    return o.reshape(Bsz, nc * C, H, E)[:, :L]


def gated_delta_chunked(q, k, v, beta, log_g, S0):
    L, DV = v.shape[1], v.shape[-1]
    C = min(CHUNK, L)
    q, k, v, beta, log_g = [_heads_first(_chunk(t, C)) for t in (q, k, v, beta, log_g)]
    gam = jnp.cumsum(log_g, axis=-1)
    incl = jnp.tril(jnp.ones((C, C), dtype=bool))
    strict = jnp.tril(jnp.ones((C, C), dtype=bool), -1)
    decay = jnp.exp(jnp.where(incl, gam[..., :, None] - gam[..., None, :], -jnp.inf))
    kk = jnp.einsum('nbhtd,nbhjd->nbhtj', k, k)
    lmat = jnp.eye(C, dtype=jnp.float32) + beta[..., None] * jnp.where(strict, kk * decay, 0.0)
    eg = jnp.exp(gam)
    rhs = jnp.concatenate([beta[..., None] * v, (beta * eg)[..., None] * k], axis=-1)
    sol = lax.linalg.triangular_solve(lmat, rhs, left_side=True, lower=True, unit_diagonal=True)
    u_part, w = sol[..., :DV], sol[..., DV:]
    p = jnp.einsum('nbhtd,nbhjd->nbhtj', q, k) * decay
    qg = eg[..., None] * q
    kdec = jnp.exp(gam[..., -1:] - gam)[..., None] * k
    glast = jnp.exp(gam[..., -1])

    def step(S, inp):
        u_p, w_c, p_c, qg_c, kd_c, gl_c = inp
        u = u_p - jnp.einsum('bhcd,bhde->bhce', w_c, S)
        o = jnp.einsum('bhcd,bhde->bhce', qg_c, S) + jnp.einsum('bhtj,bhje->bhte', p_c, u)
        S = gl_c[..., None, None] * S + jnp.einsum('bhcd,bhce->bhde', kd_c, u)
        return S, o

    S, o = lax.scan(step, S0, (u_part, w, p, qg, kdec, glast))
    return _unchunk(o, L), S


def gla_chunked(q, k, v, log_a, S0):
    L = v.shape[1]
    C = min(CHUNK, L)
    q, k, v, log_a = [_heads_first(_chunk(t, C)) for t in (q, k, v, log_a)]
    b = jnp.cumsum(log_a, axis=-2)
    incl = jnp.tril(jnp.ones((C, C), dtype=bool))
    qe = q * jnp.exp(b)
    p = jnp.where(incl, jnp.einsum('nbhtd,nbhjd->nbhtj', qe, k * jnp.exp(-b)), 0.0)
    o_intra = jnp.einsum('nbhtj,nbhje->nbhte', p, v)
    kdec = k * jnp.exp(b[..., -1:, :] - b)
    alast = jnp.exp(b[..., -1, :])

    def step(S, inp):
        qe_c, kd_c, v_c, al_c = inp
        o = jnp.einsum('bhcd,bhde->bhce', qe_c, S)
        S = al_c[..., None] * S + jnp.einsum('bhcd,bhce->bhde', kd_c, v_c)
        return S, o

    S, o_inter = lax.scan(step, S0, (qe, kdec, v, alast))
    return _unchunk(o_intra + o_inter, L), S


def ssd_chunked(x, dt, A, Bm, Cm, h0):
    Bsz, L, H, P = x.shape
    G, N = Bm.shape[2], Bm.shape[3]
    K = H // G
    C = min(CHUNK, L)
    a = dt * A
    xdt = x * dt[..., None]
    xdt, a, Bm, Cm = [jnp.moveaxis(_chunk(t, C), 1, 0) for t in (xdt, a, Bm, Cm)]
    nc = a.shape[0]
    lam = jnp.cumsum(a, axis=2)
    lam_h = jnp.swapaxes(lam, 2, 3)
    incl = jnp.tril(jnp.ones((C, C), dtype=bool))
    decay = jnp.exp(jnp.where(incl, lam_h[..., :, None] - lam_h[..., None, :], -jnp.inf))
    cb = jnp.einsum('nbtgs,nbjgs->nbgtj', Cm, Bm)
    m = cb[:, :, :, None] * decay.reshape(nc, Bsz, G, K, C, C)
    xdt = xdt.reshape(nc, Bsz, C, G, K, P)
    y_intra = jnp.einsum('nbgktj,nbjgkp->nbtgkp', m, xdt)
    elam = jnp.exp(lam).reshape(nc, Bsz, C, G, K)
    xdec = xdt * jnp.exp(lam[:, :, -1:, :] - lam).reshape(nc, Bsz, C, G, K)[..., None]
    elast = jnp.exp(lam[:, :, -1, :]).reshape(nc, Bsz, G, K)

    def step(h, inp):
        c_c, b_c, xd_c, el_c, last_c = inp
        y = el_c[..., None] * jnp.einsum('btgs,bgkps->btgkp', c_c, h)
        h = last_c[..., None, None] * h + jnp.einsum('bjgkp,bjgs->bgkps', xd_c, b_c)
        return h, y

    h, y_inter = lax.scan(step, h0.reshape(Bsz, G, K, P, N), (Cm, Bm, xdec, elam, elast))
    y = jnp.moveaxis(y_intra + y_inter, 0, 1).reshape(Bsz, nc * C, H, P)[:, :L]
    return y, h.reshape(Bsz, H, P, N)


def even_layer(h, conv_buf, s_gdn, s_gla, pre_g, post_g, w_in, conv_w, conv_b, gdn_a_log, gdn_dt_bias,
               gdn_norm_g, gla_w_lr, gla_b_lr, gla_norm_g, w_out):
    Bsz, L, _ = h.shape
    f32 = jnp.float32
    u = rmsnorm(h, pre_g)
    proj = jnp.einsum('bld,de->ble', u, w_in).astype(f32)
    gq, gk, gv, g_gate, g_beta, g_a, lq, lk, lv, l_gate, l_lr = _split(proj, EVEN_SPLITS)
    qkv, new_buf = causal_conv(jnp.concatenate([gq, gk, gv], axis=-1), conv_buf, conv_w, conv_b)
    q, k, v = _split(qkv, (GDN_KW, GDN_KW, GDN_VW))
    q = l2norm(q.reshape(Bsz, L, GDN_HEADS, GDN_DK)) * (GDN_DK ** -0.5)
    k = l2norm(k.reshape(Bsz, L, GDN_HEADS, GDN_DK))
    v = v.reshape(Bsz, L, GDN_HEADS, GDN_DV)
    beta = jax.nn.sigmoid(g_beta)
    log_g = -jnp.exp(gdn_a_log.astype(f32)) * jax.nn.softplus(g_a + gdn_dt_bias.astype(f32))
    o1, s_gdn_new = gated_delta_chunked(q, k, v, beta, log_g, s_gdn.astype(f32))
    o1 = rmsnorm(o1, gdn_norm_g) * jax.nn.silu(g_gate.reshape(Bsz, L, GDN_HEADS, GDN_DV))
    log_a = jax.nn.log_sigmoid(l_lr @ gla_w_lr.astype(f32) + gla_b_lr.astype(f32)) / GLA_TAU
    o2, s_gla_new = gla_chunked(lq.reshape(Bsz, L, GLA_HEADS, GLA_DK) * (GLA_DK ** -0.5),
                                lk.reshape(Bsz, L, GLA_HEADS, GLA_DK),
                                lv.reshape(Bsz, L, GLA_HEADS, GLA_DV),
                                log_a.reshape(Bsz, L, GLA_HEADS, GLA_DK), s_gla.astype(f32))
    o2 = rmsnorm(o2, gla_norm_g) * jax.nn.silu(l_gate.reshape(Bsz, L, GLA_HEADS, GLA_DV))
    o = jnp.concatenate([o1.reshape(Bsz, L, GDN_VW), o2.reshape(Bsz, L, GLA_VW)], axis=-1).astype(h.dtype)
    o = jnp.einsum('ble,ed->bld', o, w_out)
    h = h + rmsnorm(o, post_g)
    return h, new_buf.astype(conv_buf.dtype), s_gdn_new.astype(s_gdn.dtype), s_gla_new.astype(s_gla.dtype)


def odd_layer(h, conv_buf, s_ssd, pre_g, post_g, w_in, conv_w, conv_b, dt_bias, a_log, d_skip, norm_g, w_out):
    Bsz, L, _ = h.shape
    f32 = jnp.float32
    u = rmsnorm(h, pre_g)
    proj = jnp.einsum('bld,de->ble', u, w_in).astype(f32)
    z, xbc, dt = _split(proj, ODD_SPLITS)
    xbc, new_buf = causal_conv(xbc, conv_buf, conv_w, conv_b)
    xs, Bm, Cm = _split(xbc, (SSD_DI, SSD_G * SSD_N, SSD_G * SSD_N))
    xs = xs.reshape(Bsz, L, SSD_HEADS, SSD_P)
    Bm = Bm.reshape(Bsz, L, SSD_G, SSD_N)
    Cm = Cm.reshape(Bsz, L, SSD_G, SSD_N)
    dt = jax.nn.softplus(dt + dt_bias.astype(f32))
    A = -jnp.exp(a_log.astype(f32))
    y, s_new = ssd_chunked(xs, dt, A, Bm, Cm, s_ssd.astype(f32))
    y = y + d_skip.astype(f32)[:, None] * xs
    y = (y.reshape(Bsz, L, SSD_DI) * jax.nn.silu(z)).reshape(Bsz, L, SSD_G, SSD_DI // SSD_G)
    y = rmsnorm(y, norm_g.reshape(SSD_G, SSD_DI // SSD_G)).reshape(Bsz, L, SSD_DI).astype(h.dtype)
    o = jnp.einsum('ble,ed->bld', y, w_out)
    h = h + rmsnorm(o, post_g)
    return h, new_buf.astype(conv_buf.dtype), s_new.astype(s_ssd.dtype)


def setup_inputs(seed: int = 0) -> dict:
    key = jax.random.key(seed)
    ks = iter(jax.random.split(key, 40))

    def nrm(shape, scale):
        return jax.random.normal(next(ks), shape, jnp.float32) * scale

    def gain(shape):
        return 1.0 + nrm(shape, 0.02)

    def dt_bias(shape):
        lo, hi = math.log(1e-3), math.log(1e-1)
        dt = jnp.exp(jax.random.uniform(next(ks), shape, jnp.float32) * (hi - lo) + lo)
        return dt + jnp.log(-jnp.expm1(-dt))

    def a_log(shape):
        return jnp.log(jax.random.uniform(next(ks), shape, jnp.float32, 1.0, 16.0))

    return {
        'x_prompt': nrm((BATCH, SEQ, D_MODEL), 1.0),
        'x_sample': nrm((DEC_BATCH, DEC_SEQ, D_MODEL), 1.0),
        'state_gdn_conv': nrm((N_EVEN, DEC_BATCH, CONV_W - 1, GDN_CONV_DIM), 1.0),
        'state_gdn': nrm((N_EVEN, DEC_BATCH, GDN_HEADS, GDN_DK, GDN_DV), 0.1),
        'state_gla': nrm((N_EVEN, DEC_BATCH, GLA_HEADS, GLA_DK, GLA_DV), 0.1),
        'state_ssd_conv': nrm((N_ODD, DEC_BATCH, CONV_W - 1, SSD_CONV_DIM), 1.0),
        'state_ssd': nrm((N_ODD, DEC_BATCH, SSD_HEADS, SSD_P, SSD_N), 0.1),
        'e_pre_g': gain((N_EVEN, D_MODEL)),
        'e_post_g': gain((N_EVEN, D_MODEL)),
        'e_w_in': nrm((N_EVEN, D_MODEL, EVEN_IN), D_MODEL ** -0.5),
        'e_conv_w': nrm((N_EVEN, CONV_W, GDN_CONV_DIM), CONV_W ** -0.5),
        'e_conv_b': nrm((N_EVEN, GDN_CONV_DIM), 0.02),
        'gdn_a_log': a_log((N_EVEN, GDN_HEADS)),
        'gdn_dt_bias': dt_bias((N_EVEN, GDN_HEADS)),
        'gdn_norm_g': gain((N_EVEN, GDN_DV)),
        'gla_w_lr': nrm((N_EVEN, GLA_RANK, GLA_KW), GLA_RANK ** -0.5),
        'gla_b_lr': nrm((N_EVEN, GLA_KW), 0.02),
        'gla_norm_g': gain((N_EVEN, GLA_DV)),
        'e_w_out': nrm((N_EVEN, EVEN_OUT, D_MODEL), EVEN_OUT ** -0.5),
        'o_pre_g': gain((N_ODD, D_MODEL)),
        'o_post_g': gain((N_ODD, D_MODEL)),
        'o_w_in': nrm((N_ODD, D_MODEL, ODD_IN), D_MODEL ** -0.5),
        'ssd_conv_w': nrm((N_ODD, CONV_W, SSD_CONV_DIM), CONV_W ** -0.5),
        'ssd_conv_b': nrm((N_ODD, SSD_CONV_DIM), 0.02),
        'ssd_dt_bias': dt_bias((N_ODD, SSD_HEADS)),
        'ssd_a_log': a_log((N_ODD, SSD_HEADS)),
        'ssd_d': gain((N_ODD, SSD_HEADS)),
        'ssd_norm_g': gain((N_ODD, SSD_DI)),
        'o_w_out': nrm((N_ODD, SSD_DI, D_MODEL), SSD_DI ** -0.5),
    }


def reference(x_prompt, x_sample, state_gdn_conv, state_gdn, state_gla, state_ssd_conv, state_ssd,
              e_pre_g, e_post_g, e_w_in, e_conv_w, e_conv_b, gdn_a_log, gdn_dt_bias, gdn_norm_g,
              gla_w_lr, gla_b_lr, gla_norm_g, e_w_out,
              o_pre_g, o_post_g, o_w_in, ssd_conv_w, ssd_conv_b, ssd_dt_bias, ssd_a_log, ssd_d,
              ssd_norm_g, o_w_out):
    hp, hs = x_prompt, x_sample
    bp = x_prompt.shape[0]
    gdn_conv_p, gdn_p, gla_p, ssd_conv_p, ssd_p = [], [], [], [], []
    gdn_conv_s, gdn_s, gla_s, ssd_conv_s, ssd_s = [], [], [], [], []
    for i in range(DEPTH):
        j = i // 2
        if i % 2 == 0:
            w = (e_pre_g[j], e_post_g[j], e_w_in[j], e_conv_w[j], e_conv_b[j], gdn_a_log[j], gdn_dt_bias[j],
                 gdn_norm_g[j], gla_w_lr[j], gla_b_lr[j], gla_norm_g[j], e_w_out[j])
            zc = jnp.zeros((bp, CONV_W - 1, GDN_CONV_DIM), state_gdn_conv.dtype)
            z1 = jnp.zeros((bp, GDN_HEADS, GDN_DK, GDN_DV), state_gdn.dtype)
            z2 = jnp.zeros((bp, GLA_HEADS, GLA_DK, GLA_DV), state_gla.dtype)
            hp, c, s1, s2 = even_layer(hp, zc, z1, z2, *w)
            gdn_conv_p.append(c); gdn_p.append(s1); gla_p.append(s2)
            hs, c, s1, s2 = even_layer(hs, state_gdn_conv[j], state_gdn[j], state_gla[j], *w)
            gdn_conv_s.append(c); gdn_s.append(s1); gla_s.append(s2)
        else:
            w = (o_pre_g[j], o_post_g[j], o_w_in[j], ssd_conv_w[j], ssd_conv_b[j], ssd_dt_bias[j], ssd_a_log[j],
                 ssd_d[j], ssd_norm_g[j], o_w_out[j])
            zc = jnp.zeros((bp, CONV_W - 1, SSD_CONV_DIM), state_ssd_conv.dtype)
            z1 = jnp.zeros((bp, SSD_HEADS, SSD_P, SSD_N), state_ssd.dtype)
            hp, c, s1 = odd_layer(hp, zc, z1, *w)
            ssd_conv_p.append(c); ssd_p.append(s1)
            hs, c, s1 = odd_layer(hs, state_ssd_conv[j], state_ssd[j], *w)
            ssd_conv_s.append(c); ssd_s.append(s1)
    return (hp, hs,
            jnp.stack(gdn_conv_p), jnp.stack(gdn_p), jnp.stack(gla_p), jnp.stack(ssd_conv_p), jnp.stack(ssd_p),
            jnp.stack(gdn_conv_s), jnp.stack(gdn_s), jnp.stack(gla_s), jnp.stack(ssd_conv_s), jnp.stack(ssd_s))
```

```python
import functools

import jax
import jax.numpy as jnp
from jax import lax
from jax.experimental import pallas as pl
from jax.experimental.pallas import tpu as pltpu

f32 = jnp.float32
bf16 = jnp.bfloat16
MM_DTYPE = jnp.bfloat16

D_MODEL = 1024
CONV_W = 4
EPS = 1e-6
LANES = 128

GDN_HEADS = 8
GDN_DK = 128
GDN_DV = 128
GDN_KW = GDN_HEADS * GDN_DK
GDN_VW = GDN_HEADS * GDN_DV
GDN_CONV_DIM = 2 * GDN_KW + GDN_VW
GLA_HEADS = 4
GLA_DK = 128
GLA_DV = 256
GLA_KW = GLA_HEADS * GLA_DK
GLA_VW = GLA_HEADS * GLA_DV
GLA_RANK = 16
GLA_TAU = 16.0
SSD_DI = 2 * D_MODEL
SSD_P = 64
SSD_HEADS = SSD_DI // SSD_P
SSD_N = 128
SSD_G = 4
SSD_K = SSD_HEADS // SSD_G
SSD_GW = SSD_DI // SSD_G
SSD_BC = SSD_G * SSD_N
SSD_CONV_DIM = SSD_DI + 2 * SSD_BC

E_OFF_GATE = 3 * GDN_KW
E_OFF_LQ = 4 * GDN_KW
E_OFF_LK = E_OFF_LQ + GLA_KW
E_OFF_LV = E_OFF_LK + GLA_KW
E_OFF_LGATE = E_OFF_LV + GLA_VW
E_OFF_SMALL = E_OFF_LGATE + GLA_VW
E_N = 7680
SMALL_BETA = 0
SMALL_A = GDN_HEADS
SMALL_LR = 2 * GDN_HEADS
O_OFF_Z = SSD_DI
O_OFF_BC = 2 * SSD_DI
O_OFF_DT = 2 * SSD_DI + 2 * SSD_BC
O_N = 5376

CHUNK = 64
VMEM_LIMIT = 56 * 1024 * 1024


def _mm(a, b):
    return jnp.dot(a.astype(MM_DTYPE), b.astype(MM_DTYPE), preferred_element_type=f32)


def _mm_nt(a, b):
    return lax.dot_general(a.astype(MM_DTYPE), b.astype(MM_DTYPE), (((1,), (1,)), ((), ())),
                           preferred_element_type=f32)


def _mm_tn(a, b):
    return lax.dot_general(a.astype(MM_DTYPE), b.astype(MM_DTYPE), (((0,), (0,)), ((), ())),
                           preferred_element_type=f32)


def _split3(x):
    hi = x.astype(bf16)
    r = x - hi.astype(f32)
    mid = r.astype(bf16)
    lo = (r - mid.astype(f32)).astype(bf16)
    return hi, mid, lo


def _sel_mm(sel, x):
    hi, mid, lo = _split3(x)
    s = sel.astype(bf16)
    return (jnp.dot(s, hi, preferred_element_type=f32) + jnp.dot(s, mid, preferred_element_type=f32)
            + jnp.dot(s, lo, preferred_element_type=f32))


def _sel_rows(x, sel):
    hi, mid, lo = _split3(x)
    s = sel.astype(bf16)
    return (jnp.dot(hi, s, preferred_element_type=f32) + jnp.dot(mid, s, preferred_element_type=f32)
            + jnp.dot(lo, s, preferred_element_type=f32))


def _sigmoid(x):
    return 1.0 / (1.0 + jnp.exp(-x))


def _silu(x):
    return x * _sigmoid(x)


def _softplus(x):
    return jnp.maximum(x, 0.0) + jnp.log1p(jnp.exp(-jnp.abs(x)))


def _tril(n, strict=False):
    r = lax.broadcasted_iota(jnp.int32, (n, n), 0)
    c = lax.broadcasted_iota(jnp.int32, (n, n), 1)
    return (r > c) if strict else (r >= c)


def _masked_decay(col, row, incl):
    d = col - row
    return jnp.where(incl, jnp.exp(jnp.where(incl, d, 0.0)), 0.0)


def _rms(x, width):
    return x * lax.rsqrt(jnp.sum(x * x, axis=-1, keepdims=True) * (1.0 / width) + EPS)


def _norm_proj_kernel(x_ref, g_ref, w_ref, o_ref, u_scr):
    @pl.when(pl.program_id(1) == 0)
    def _():
        u_scr[...] = (_rms(x_ref[...], D_MODEL) * g_ref[...]).astype(u_scr.dtype)

    o_ref[...] = jnp.dot(u_scr[...], w_ref[...], preferred_element_type=f32)


def _norm_proj(x, g, w, tn):
    t, d = x.shape
    n = w.shape[1]
    tm = min(t, 1024)
    return pl.pallas_call(
        _norm_proj_kernel,
        grid=(t // tm, n // tn),
        in_specs=[pl.BlockSpec((tm, d), lambda i, j: (i, 0)),
                  pl.BlockSpec((1, d), lambda i, j: (0, 0)),
                  pl.BlockSpec((d, tn), lambda i, j: (0, j))],
        out_specs=pl.BlockSpec((tm, tn), lambda i, j: (i, j)),
        out_shape=jax.ShapeDtypeStruct((t, n), f32),
        scratch_shapes=[pltpu.VMEM((tm, d), MM_DTYPE)],
        compiler_params=pltpu.CompilerParams(dimension_semantics=("parallel", "arbitrary"),
                                             vmem_limit_bytes=VMEM_LIMIT),
        name="norm_proj",
    )(x, g, w)


def _out_proj_kernel(o_ref, w_ref, g_ref, h_ref, y_ref):
    p = jnp.dot(o_ref[...], w_ref[...], preferred_element_type=f32)
    y_ref[...] = h_ref[...] + _rms(p, D_MODEL) * g_ref[...]


def _out_proj(o, w, g, h):
    t, e = o.shape
    d = w.shape[1]
    tm = min(t, 512)
    return pl.pallas_call(
        _out_proj_kernel,
        grid=(t // tm,),
        in_specs=[pl.BlockSpec((tm, e), lambda i: (i, 0)),
                  pl.BlockSpec((e, d), lambda i: (0, 0)),
                  pl.BlockSpec((1, d), lambda i: (0, 0)),
                  pl.BlockSpec((tm, d), lambda i: (i, 0))],
        out_specs=pl.BlockSpec((tm, d), lambda i: (i, 0)),
        out_shape=jax.ShapeDtypeStruct((t, d), f32),
        compiler_params=pltpu.CompilerParams(dimension_semantics=("parallel",),
                                             vmem_limit_bytes=VMEM_LIMIT),
        name="out_proj",
    )(o, w, g, h)


def _conv_block(xp_scr, parts, cw_ref, cb_ref, tb, first):
    @pl.when(first)
    def _():
        xp_scr[0:8, :] = jnp.zeros((8, xp_scr.shape[1]), f32)

    @pl.when(jnp.logical_not(first))
    def _():
        xp_scr[5:8, :] = xp_scr[tb + 5:tb + 8, :]

    off = 0
    for p in parts:
        w = p.shape[1]
        xp_scr[8:8 + tb, off:off + w] = p[...]
        off += w

    def conv_cols(c0, c1):
        acc = cb_ref[:, c0:c1] + xp_scr[8:8 + tb, c0:c1] * cw_ref[3:4, c0:c1]
        for s in range(CONV_W - 1):
            acc = acc + xp_scr[5 + s:5 + s + tb, c0:c1] * cw_ref[s:s + 1, c0:c1]
        return _silu(acc)

    return conv_cols


def _gdn_prompt_kernel(q_ref, k_ref, v_ref, gate_ref, small_ref, cw_ref, cb_ref, hp_ref, ng_ref,
                       o_ref, conv_out_ref, s_out_ref,
                       xp_scr, qkv_scr, beta_scr, lg_scr, s_scr, *, tb):
    i = pl.program_id(1)
    first = i == 0
    conv_cols = _conv_block(xp_scr, (q_ref, k_ref, v_ref), cw_ref, cb_ref, tb, first)

    @pl.when(first)
    def _():
        s_scr[...] = jnp.zeros(s_scr.shape, f32)

    for h in range(GDN_HEADS):
        c0 = h * GDN_DK
        q = conv_cols(c0, c0 + GDN_DK)
        q = q * lax.rsqrt(jnp.sum(q * q, axis=-1, keepdims=True) + EPS) * (GDN_DK ** -0.5)
        qkv_scr[:, c0:c0 + GDN_DK] = q
        k = conv_cols(GDN_KW + c0, GDN_KW + c0 + GDN_DK)
        k = k * lax.rsqrt(jnp.sum(k * k, axis=-1, keepdims=True) + EPS)
        qkv_scr[:, GDN_KW + c0:GDN_KW + c0 + GDN_DK] = k
        qkv_scr[:, 2 * GDN_KW + c0:2 * GDN_KW + c0 + GDN_DV] = conv_cols(2 * GDN_KW + c0, 2 * GDN_KW + c0 + GDN_DV)

    small = small_ref[...]
    beta_scr[...] = _sigmoid(small)
    lg_scr[...] = hp_ref[0:1, :] * _softplus(small + hp_ref[1:2, :])

    incl = _tril(CHUNK)
    strict = _tril(CHUNK, strict=True)
    lmat = incl.astype(f32)
    ng = ng_ref[...]

    def chunk_body(c, carry):
        r0 = pl.multiple_of(c * CHUNK, CHUNK)
        rows = pl.ds(r0, CHUNK)
        gam = _sel_mm(lmat, lg_scr[rows, :])
        gam_t = gam.T
        beta_c = beta_scr[rows, :]
        for h in range(GDN_HEADS):
            c0 = h * GDN_DK
            q = qkv_scr[rows, c0:c0 + GDN_DK]
            k = qkv_scr[rows, GDN_KW + c0:GDN_KW + c0 + GDN_DK]
            v = qkv_scr[rows, 2 * GDN_KW + c0:2 * GDN_KW + c0 + GDN_DV]
            gcol = gam[:, SMALL_A + h:SMALL_A + h + 1]
            grow = gam_t[SMALL_A + h:SMALL_A + h + 1, :]
            glast = gam[CHUNK - 1:CHUNK, SMALL_A + h:SMALL_A + h + 1]
            bcol = beta_c[:, SMALL_BETA + h:SMALL_BETA + h + 1]
            decay = _masked_decay(gcol, grow, incl)
            eg = jnp.exp(gcol)
            kb = k.astype(MM_DTYPE)
            kk = _mm_nt(kb, kb)
            qk = _mm_nt(q, kb)
            nmat = jnp.where(strict, bcol * (kk * decay), 0.0)
            x = jnp.concatenate([bcol * v, (bcol * eg) * k], axis=-1)
            x = x - _mm(nmat, x)
            npow = nmat
            for _ in range(5):
                npow = _mm(npow, npow)
                x = x + _mm(npow, x)
            u_part = x[:, :GDN_DV]
            w = x[:, GDN_DV:]
            s_old = s_scr[h]
            r = _mm(jnp.concatenate([w, eg * q], axis=0), s_old)
            u = u_part - r[:CHUNK]
            o = r[CHUNK:] + _mm(qk * decay, u)
            kdec = jnp.exp(glast - gcol) * k
            s_scr[h] = jnp.exp(glast) * s_old + _mm_tn(kdec, u)
            gate = gate_ref[rows, c0:c0 + GDN_DV]
            o_ref[rows, c0:c0 + GDN_DV] = (_rms(o, GDN_DV) * ng * _silu(gate)).astype(o_ref.dtype)
        return carry

    lax.fori_loop(0, tb // CHUNK, chunk_body, 0)

    @pl.when(i == pl.num_programs(1) - 1)
    def _():
        conv_out_ref[...] = xp_scr[tb + 5:tb + 8, :]
        s_out_ref[...] = s_scr[...]


def _gdn_prompt(proj, cw, cb, hp, ng, tb):
    b, l, _ = proj.shape
    wq = GDN_KW
    kern = functools.partial(_gdn_prompt_kernel, tb=tb)
    col = lambda blk: (lambda bi, i: (bi, i, blk))
    const = lambda bi, i: (0, 0)
    return pl.pallas_call(
        kern,
        grid=(b, l // tb),
        in_specs=[pl.BlockSpec((None, tb, wq), col(0)),
                  pl.BlockSpec((None, tb, wq), col(1)),
                  pl.BlockSpec((None, tb, wq), col(2)),
                  pl.BlockSpec((None, tb, wq), col(3)),
                  pl.BlockSpec((None, tb, LANES), col(E_OFF_SMALL // LANES)),
                  pl.BlockSpec((CONV_W, GDN_CONV_DIM), const),
                  pl.BlockSpec((1, GDN_CONV_DIM), const),
                  pl.BlockSpec((8, LANES), const),
                  pl.BlockSpec((1, GDN_DV), const)],
        out_specs=[pl.BlockSpec((None, tb, GDN_VW), lambda bi, i: (bi, i, 0)),
                   pl.BlockSpec((None, CONV_W - 1, GDN_CONV_DIM), lambda bi, i: (bi, 0, 0)),
                   pl.BlockSpec((None, GDN_HEADS, GDN_DK, GDN_DV), lambda bi, i: (bi, 0, 0, 0))],
        out_shape=[jax.ShapeDtypeStruct((b, l, GDN_VW), MM_DTYPE),
                   jax.ShapeDtypeStruct((b, CONV_W - 1, GDN_CONV_DIM), f32),
                   jax.ShapeDtypeStruct((b, GDN_HEADS, GDN_DK, GDN_DV), f32)],
        scratch_shapes=[pltpu.VMEM((tb + 8, GDN_CONV_DIM), f32),
                        pltpu.VMEM((tb, GDN_CONV_DIM), f32),
                        pltpu.VMEM((tb, LANES), f32),
                        pltpu.VMEM((tb, LANES), f32),
                        pltpu.VMEM((GDN_HEADS, GDN_DK, GDN_DV), f32)],
        compiler_params=pltpu.CompilerParams(dimension_semantics=("parallel", "arbitrary"),
                                             vmem_limit_bytes=VMEM_LIMIT),
        name="gdn_prompt",
    )(proj, proj, proj, proj, proj, cw, cb, hp, ng)


def _gla_prompt_kernel(q_ref, k_ref, v_ref, gate_ref, small_ref, wlr_ref, blr_ref, ng_ref,
                       o_ref, s_out_ref, la_scr, st_scr, *, tb):
    i = pl.program_id(1)

    @pl.when(i == 0)
    def _():
        st_scr[...] = jnp.zeros(st_scr.shape, f32)

    z = _mm(small_ref[...], wlr_ref[...]) + blr_ref[...]
    la_scr[...] = -_softplus(-z) * (1.0 / GLA_TAU)

    incl = _tril(CHUNK)
    lmat = incl.astype(f32)
    ng = ng_ref[...]

    def chunk_body(c, carry):
        r0 = pl.multiple_of(c * CHUNK, CHUNK)
        rows = pl.ds(r0, CHUNK)
        for h in range(GLA_HEADS):
            k0 = h * GLA_DK
            v0 = h * GLA_DV
            bcum = _sel_mm(lmat, la_scr[rows, k0:k0 + GLA_DK])
            q = q_ref[rows, k0:k0 + GLA_DK] * (GLA_DK ** -0.5)
            k = k_ref[rows, k0:k0 + GLA_DK]
            v = v_ref[rows, v0:v0 + GLA_DV]
            qe = q * jnp.exp(bcum)
            ke = k * jnp.exp(-bcum)
            p = jnp.where(incl, _mm_nt(qe, ke), 0.0)
            st_old = st_scr[h]
            o = _mm(p, v) + _mm_nt(qe, st_old)
            blast = bcum[CHUNK - 1:CHUNK, :]
            kdec = k * jnp.exp(blast - bcum)
            st_scr[h] = jnp.exp(blast) * st_old + _mm_tn(v, kdec)
            gate = gate_ref[rows, v0:v0 + GLA_DV]
            o_ref[rows, v0:v0 + GLA_DV] = (_rms(o, GLA_DV) * ng * _silu(gate)).astype(o_ref.dtype)
        return carry

    lax.fori_loop(0, tb // CHUNK, chunk_body, 0)

    @pl.when(i == pl.num_programs(1) - 1)
    def _():
        for h in range(GLA_HEADS):
            s_out_ref[h] = st_scr[h].T


def _gla_prompt(proj, wlr, blr, ng, tb):
    b, l, _ = proj.shape
    kern = functools.partial(_gla_prompt_kernel, tb=tb)
    col = lambda blk: (lambda bi, i: (bi, i, blk))
    const = lambda bi, i: (0, 0)
    return pl.pallas_call(
        kern,
        grid=(b, l // tb),
        in_specs=[pl.BlockSpec((None, tb, GLA_KW), col(E_OFF_LQ // GLA_KW)),
                  pl.BlockSpec((None, tb, GLA_KW), col(E_OFF_LK // GLA_KW)),
                  pl.BlockSpec((None, tb, GLA_VW), col(E_OFF_LV // GLA_VW)),
                  pl.BlockSpec((None, tb, GLA_VW), col(E_OFF_LGATE // GLA_VW)),
                  pl.BlockSpec((None, tb, LANES), col(E_OFF_SMALL // LANES)),
                  pl.BlockSpec((LANES, GLA_KW), const),
                  pl.BlockSpec((1, GLA_KW), const),
                  pl.BlockSpec((1, GLA_DV), const)],
        out_specs=[pl.BlockSpec((None, tb, GLA_VW), lambda bi, i: (bi, i, 0)),
                   pl.BlockSpec((None, GLA_HEADS, GLA_DK, GLA_DV), lambda bi, i: (bi, 0, 0, 0))],
        out_shape=[jax.ShapeDtypeStruct((b, l, GLA_VW), MM_DTYPE),
                   jax.ShapeDtypeStruct((b, GLA_HEADS, GLA_DK, GLA_DV), f32)],
        scratch_shapes=[pltpu.VMEM((tb, GLA_KW), f32),
                        pltpu.VMEM((GLA_HEADS, GLA_DV, GLA_DK), f32)],
        compiler_params=pltpu.CompilerParams(dimension_semantics=("parallel", "arbitrary"),
                                             vmem_limit_bytes=VMEM_LIMIT),
        name="gla_prompt",
    )(proj, proj, proj, proj, proj, wlr, blr, ng)


def _ssd_prompt_kernel(x_ref, z_ref, bc_ref, dt_ref, cw_ref, cb_ref, hp_ref, dskip_ref, ng_ref,
                       y_ref, conv_out_ref, h_out_ref,
                       xp_scr, xbc_scr, dt_scr, a_scr, h_scr, *, tb):
    i = pl.program_id(1)
    first = i == 0
    conv_cols = _conv_block(xp_scr, (x_ref, bc_ref), cw_ref, cb_ref, tb, first)

    @pl.when(first)
    def _():
        h_scr[...] = jnp.zeros(h_scr.shape, f32)

    for c0 in range(0, SSD_CONV_DIM, 512):
        xbc_scr[:, c0:c0 + 512] = conv_cols(c0, c0 + 512)
    dt = _softplus(dt_ref[...] + hp_ref[0:1, :])
    dt_scr[...] = dt
    a_scr[...] = dt * hp_ref[1:2, :]

    incl = _tril(CHUNK)
    lmat = incl.astype(f32)

    def chunk_body(c, carry):
        r0 = pl.multiple_of(c * CHUNK, CHUNK)
        rows = pl.ds(r0, CHUNK)
        lam = _sel_mm(lmat, a_scr[rows, :])
        lam_t = lam.T
        dt_c = dt_scr[rows, :]
        for g in range(SSD_G):
            bm = xbc_scr[rows, SSD_DI + g * SSD_N:SSD_DI + (g + 1) * SSD_N]
            cm = xbc_scr[rows, SSD_DI + SSD_BC + g * SSD_N:SSD_DI + SSD_BC + (g + 1) * SSD_N]
            cb = _mm_nt(cm, bm)
            hg = h_scr[g * SSD_K * SSD_P:(g + 1) * SSD_K * SSD_P, :]
            y_inter = _mm_nt(cm, hg)
            ys = []
            xdecs = []
            for kk in range(SSD_K):
                hd = g * SSD_K + kk
                x0 = hd * SSD_P
                xs = xbc_scr[rows, x0:x0 + SSD_P]
                lcol = lam[:, hd:hd + 1]
                lrow = lam_t[hd:hd + 1, :]
                llast = lam[CHUNK - 1:CHUNK, hd:hd + 1]
                xdt = xs * dt_c[:, hd:hd + 1]
                m = cb * _masked_decay(lcol, lrow, incl)
                y = _mm(m, xdt) + jnp.exp(lcol) * y_inter[:, kk * SSD_P:(kk + 1) * SSD_P]
                ys.append(y + dskip_ref[:, x0:x0 + SSD_P] * xs)
                xdecs.append(xdt * jnp.exp(llast - lcol))
                h_scr[x0:x0 + SSD_P, :] = jnp.exp(llast) * h_scr[x0:x0 + SSD_P, :]
            upd = _mm_tn(jnp.concatenate(xdecs, axis=-1), bm)
            h_scr[g * SSD_GW:(g + 1) * SSD_GW, :] = h_scr[g * SSD_GW:(g + 1) * SSD_GW, :] + upd
            yg = jnp.concatenate(ys, axis=-1) * _silu(z_ref[rows, g * SSD_GW:(g + 1) * SSD_GW])
            y_ref[rows, g * SSD_GW:(g + 1) * SSD_GW] = (
                _rms(yg, SSD_GW) * ng_ref[:, g * SSD_GW:(g + 1) * SSD_GW]).astype(y_ref.dtype)
        return carry

    lax.fori_loop(0, tb // CHUNK, chunk_body, 0)

    @pl.when(i == pl.num_programs(1) - 1)
    def _():
        conv_out_ref[...] = xp_scr[tb + 5:tb + 8, :]
        h_out_ref[...] = h_scr[...]


def _ssd_prompt(proj, cw, cb, hp, dskip, ng, tb):
    b, l, _ = proj.shape
    kern = functools.partial(_ssd_prompt_kernel, tb=tb)
    col = lambda blk: (lambda bi, i: (bi, i, blk))
    const = lambda bi, i: (0, 0)
    return pl.pallas_call(
        kern,
        grid=(b, l // tb),
        in_specs=[pl.BlockSpec((None, tb, SSD_DI), col(0)),
                  pl.BlockSpec((None, tb, SSD_DI), col(O_OFF_Z // SSD_DI)),
                  pl.BlockSpec((None, tb, 2 * SSD_BC), col(O_OFF_BC // (2 * SSD_BC))),
                  pl.BlockSpec((None, tb, LANES), col(O_OFF_DT // LANES)),
                  pl.BlockSpec((CONV_W, SSD_CONV_DIM), const),
                  pl.BlockSpec((1, SSD_CONV_DIM), const),
                  pl.BlockSpec((8, LANES), const),
                  pl.BlockSpec((1, SSD_DI), const),
                  pl.BlockSpec((1, SSD_DI), const)],
        out_specs=[pl.BlockSpec((None, tb, SSD_DI), lambda bi, i: (bi, i, 0)),
                   pl.BlockSpec((None, CONV_W - 1, SSD_CONV_DIM), lambda bi, i: (bi, 0, 0)),
                   pl.BlockSpec((None, SSD_HEADS * SSD_P, SSD_N), lambda bi, i: (bi, 0, 0))],
        out_shape=[jax.ShapeDtypeStruct((b, l, SSD_DI), MM_DTYPE),
                   jax.ShapeDtypeStruct((b, CONV_W - 1, SSD_CONV_DIM), f32),
                   jax.ShapeDtypeStruct((b, SSD_HEADS * SSD_P, SSD_N), f32)],
        scratch_shapes=[pltpu.VMEM((tb + 8, SSD_CONV_DIM), f32),
                        pltpu.VMEM((tb, SSD_CONV_DIM), f32),
                        pltpu.VMEM((tb, LANES), f32),
                        pltpu.VMEM((tb, LANES), f32),
                        pltpu.VMEM((SSD_HEADS * SSD_P, SSD_N), f32)],
        compiler_params=pltpu.CompilerParams(dimension_semantics=("parallel", "arbitrary"),
                                             vmem_limit_bytes=VMEM_LIMIT),
        name="ssd_prompt",
    )(proj, proj, proj, proj, cw, cb, hp, dskip, ng)


def _pad_rows(row, n):
    return jnp.concatenate([row, jnp.zeros((n - 1, row.shape[1]), row.dtype)], axis=0)


def _bcast_cols(row, width):
    hi, mid, lo = _split3(row)
    lhs = jnp.concatenate([hi, mid, lo, jnp.zeros((13, row.shape[1]), bf16)], axis=0)
    sel = (lax.broadcasted_iota(jnp.int32, (16, width), 0) < 3).astype(bf16)
    return lax.dot_general(lhs, sel, (((0,), (0,)), ((), ())), preferred_element_type=f32)


def _conv_step(buf_ref, x, cw_ref, cb_ref, new_buf_ref):
    cdim = x.shape[1]
    acc = cb_ref[...] + x * cw_ref[3:4, :]
    for s in range(CONV_W - 1):
        acc = acc + buf_ref[:, s * cdim:(s + 1) * cdim] * cw_ref[s:s + 1, :]
    new_buf_ref[:, 0:2 * cdim] = buf_ref[:, cdim:3 * cdim]
    new_buf_ref[:, 2 * cdim:3 * cdim] = x
    return _silu(acc)


def _gdn_decode_kernel(q_ref, k_ref, v_ref, gate_ref, small_ref, buf_ref, s_ref, cw_ref, cb_ref, hp_ref, ng_ref,
                       o_ref, new_buf_ref, s_out_ref, *, bb):
    x = jnp.concatenate([q_ref[...], k_ref[...], v_ref[...]], axis=-1)
    qkv = _conv_step(buf_ref, x, cw_ref, cb_ref, new_buf_ref)
    small = small_ref[...]
    beta = _sigmoid(small)
    eg = jnp.exp(hp_ref[0:1, :] * _softplus(small + hp_ref[1:2, :]))
    ng = ng_ref[...]
    gate = gate_ref[...]
    for h in range(GDN_HEADS):
        c0 = h * GDN_DK
        q = qkv[:, c0:c0 + GDN_DK]
        q = q * lax.rsqrt(jnp.sum(q * q, axis=-1, keepdims=True) + EPS) * (GDN_DK ** -0.5)
        k = qkv[:, GDN_KW + c0:GDN_KW + c0 + GDN_DK]
        k = k * lax.rsqrt(jnp.sum(k * k, axis=-1, keepdims=True) + EPS)
        v = qkv[:, 2 * GDN_KW + c0:2 * GDN_KW + c0 + GDN_DV]
        qk = jnp.sum(q * k, axis=-1, keepdims=True)
        b_h = beta[:, SMALL_BETA + h:SMALL_BETA + h + 1]
        eg_h = eg[:, SMALL_A + h:SMALL_A + h + 1]
        outs = []
        for b in range(bb):
            s_old = s_ref[b, h]
            kq = jnp.concatenate([k[b:b + 1], q[b:b + 1], jnp.zeros((6, GDN_DK), f32)], axis=0)
            r = _mm(kq, s_old)
            bs = b_h[b:b + 1]
            es = eg_h[b:b + 1]
            u = bs * v[b:b + 1] - (bs * es) * r[0:1]
            outs.append(es * r[1:2] + qk[b:b + 1] * u)
            s_out_ref[b, h] = es * s_old + _mm_tn(_pad_rows(k[b:b + 1], 16), _pad_rows(u, 16))
        o = jnp.concatenate(outs, axis=0)
        o_ref[:, c0:c0 + GDN_DV] = (_rms(o, GDN_DV) * ng * _silu(gate[:, c0:c0 + GDN_DV])).astype(o_ref.dtype)


def _gdn_decode(proj, buf, s, cw, cb, hp, ng, bb):
    b = proj.shape[0]
    wq = GDN_KW
    kern = functools.partial(_gdn_decode_kernel, bb=bb)
    col = lambda blk: (lambda i: (i, blk))
    const = lambda i: (0, 0)
    return pl.pallas_call(
        kern,
        grid=(b // bb,),
        in_specs=[pl.BlockSpec((bb, wq), col(0)),
                  pl.BlockSpec((bb, wq), col(1)),
                  pl.BlockSpec((bb, wq), col(2)),
                  pl.BlockSpec((bb, wq), col(3)),
                  pl.BlockSpec((bb, LANES), col(E_OFF_SMALL // LANES)),
                  pl.BlockSpec((bb, 3 * GDN_CONV_DIM), col(0)),
                  pl.BlockSpec((bb, GDN_HEADS, GDN_DK, GDN_DV), lambda i: (i, 0, 0, 0)),
                  pl.BlockSpec((CONV_W, GDN_CONV_DIM), const),
                  pl.BlockSpec((1, GDN_CONV_DIM), const),
                  pl.BlockSpec((8, LANES), const),
                  pl.BlockSpec((1, GDN_DV), const)],
        out_specs=[pl.BlockSpec((bb, GDN_VW), col(0)),
                   pl.BlockSpec((bb, 3 * GDN_CONV_DIM), col(0)),
                   pl.BlockSpec((bb, GDN_HEADS, GDN_DK, GDN_DV), lambda i: (i, 0, 0, 0))],
        out_shape=[jax.ShapeDtypeStruct((b, GDN_VW), MM_DTYPE),
                   jax.ShapeDtypeStruct((b, 3 * GDN_CONV_DIM), f32),
                   jax.ShapeDtypeStruct((b, GDN_HEADS, GDN_DK, GDN_DV), f32)],
        compiler_params=pltpu.CompilerParams(dimension_semantics=("parallel",), vmem_limit_bytes=VMEM_LIMIT),
        name="gdn_decode",
    )(proj, proj, proj, proj, proj, buf, s, cw, cb, hp, ng)


def _gla_decode_kernel(q_ref, k_ref, v_ref, gate_ref, small_ref, s_ref, wlr_ref, blr_ref, ng_ref,
                       o_ref, s_out_ref, *, bb):
    z = _mm(small_ref[...], wlr_ref[...]) + blr_ref[...]
    la = -_softplus(-z) * (1.0 / GLA_TAU)
    dec = jnp.exp(la)
    qe = q_ref[...] * (GLA_DK ** -0.5) * dec
    kx = k_ref[...]
    ke = kx * jnp.exp(-la)
    vx = v_ref[...]
    ng = ng_ref[...]
    gate = gate_ref[...]
    for h in range(GLA_HEADS):
        k0 = h * GLA_DK
        v0 = h * GLA_DV
        p = jnp.sum(qe[:, k0:k0 + GLA_DK] * ke[:, k0:k0 + GLA_DK], axis=-1, keepdims=True)
        outs = []
        for b in range(bb):
            s_old = s_ref[b, h]
            vrow = vx[b:b + 1, v0:v0 + GLA_DV]
            outs.append(_mm(_pad_rows(qe[b:b + 1, k0:k0 + GLA_DK], 8), s_old)[0:1] + p[b:b + 1] * vrow)
            s_out_ref[b, h] = (_bcast_cols(dec[b:b + 1, k0:k0 + GLA_DK], GLA_DV) * s_old
                               + _mm_tn(_pad_rows(kx[b:b + 1, k0:k0 + GLA_DK], 16), _pad_rows(vrow, 16)))
        o = jnp.concatenate(outs, axis=0)
        o_ref[:, v0:v0 + GLA_DV] = (_rms(o, GLA_DV) * ng * _silu(gate[:, v0:v0 + GLA_DV])).astype(o_ref.dtype)


def _gla_decode(proj, s, wlr, blr, ng, bb):
    b = proj.shape[0]
    kern = functools.partial(_gla_decode_kernel, bb=bb)
    col = lambda blk: (lambda i: (i, blk))
    const = lambda i: (0, 0)
    return pl.pallas_call(
        kern,
        grid=(b // bb,),
        in_specs=[pl.BlockSpec((bb, GLA_KW), col(E_OFF_LQ // GLA_KW)),
                  pl.BlockSpec((bb, GLA_KW), col(E_OFF_LK // GLA_KW)),
                  pl.BlockSpec((bb, GLA_VW), col(E_OFF_LV // GLA_VW)),
                  pl.BlockSpec((bb, GLA_VW), col(E_OFF_LGATE // GLA_VW)),
                  pl.BlockSpec((bb, LANES), col(E_OFF_SMALL // LANES)),
                  pl.BlockSpec((bb, GLA_HEADS, GLA_DK, GLA_DV), lambda i: (i, 0, 0, 0)),
                  pl.BlockSpec((LANES, GLA_KW), const),
                  pl.BlockSpec((1, GLA_KW), const),
                  pl.BlockSpec((1, GLA_DV), const)],
        out_specs=[pl.BlockSpec((bb, GLA_VW), col(0)),
                   pl.BlockSpec((bb, GLA_HEADS, GLA_DK, GLA_DV), lambda i: (i, 0, 0, 0))],
        out_shape=[jax.ShapeDtypeStruct((b, GLA_VW), MM_DTYPE),
                   jax.ShapeDtypeStruct((b, GLA_HEADS, GLA_DK, GLA_DV), f32)],
        compiler_params=pltpu.CompilerParams(dimension_semantics=("parallel",), vmem_limit_bytes=VMEM_LIMIT),
        name="gla_decode",
    )(proj, proj, proj, proj, proj, s, wlr, blr, ng)


def _ssd_decode_kernel(x_ref, z_ref, bc_ref, dt_ref, buf_ref, h_ref, cw_ref, cb_ref, hp_ref, dskip_ref, ng_ref,
                       expand_ref, y_ref, new_buf_ref, h_out_ref, *, bb):
    xin = jnp.concatenate([x_ref[...], bc_ref[...]], axis=-1)
    xbc = _conv_step(buf_ref, xin, cw_ref, cb_ref, new_buf_ref)
    xs = xbc[:, :SSD_DI]
    dt = _softplus(dt_ref[...] + hp_ref[0:1, :])
    el = jnp.exp(dt * hp_ref[1:2, :])
    expand = expand_ref[...]
    dt_x = _sel_rows(dt, expand)
    el_x = _sel_rows(el, expand)
    xdt = xs * dt_x
    z = z_ref[...]
    for g in range(SSD_G):
        gs = slice(g * SSD_GW, (g + 1) * SSD_GW)
        bm = xbc[:, SSD_DI + g * SSD_N:SSD_DI + (g + 1) * SSD_N]
        cm = xbc[:, SSD_DI + SSD_BC + g * SSD_N:SSD_DI + SSD_BC + (g + 1) * SSD_N]
        cbs = jnp.sum(cm * bm, axis=-1, keepdims=True)
        outs = []
        for b in range(bb):
            hg = h_ref[b, gs, :]
            y_inter = _mm_nt(_pad_rows(cm[b:b + 1], 8), hg)[0:1]
            outs.append(el_x[b:b + 1, gs] * y_inter)
            h_out_ref[b, gs, :] = (_bcast_cols(el_x[b:b + 1, gs], SSD_N) * hg
                                   + _mm_tn(_pad_rows(xdt[b:b + 1, gs], 16), _pad_rows(bm[b:b + 1], 16)))
        y = jnp.concatenate(outs, axis=0) + cbs * xdt[:, gs] + dskip_ref[:, gs] * xs[:, gs]
        yg = y * _silu(z[:, gs])
        y_ref[:, gs] = (_rms(yg, SSD_GW) * ng_ref[:, gs]).astype(y_ref.dtype)


def _ssd_decode(proj, buf, hst, cw, cb, hp, dskip, ng, expand, bb):
    b = proj.shape[0]
    kern = functools.partial(_ssd_decode_kernel, bb=bb)
    col = lambda blk: (lambda i: (i, blk))
    const = lambda i: (0, 0)
    return pl.pallas_call(
        kern,
        grid=(b // bb,),
        in_specs=[pl.BlockSpec((bb, SSD_DI), col(0)),
                  pl.BlockSpec((bb, SSD_DI), col(O_OFF_Z // SSD_DI)),
                  pl.BlockSpec((bb, 2 * SSD_BC), col(O_OFF_BC // (2 * SSD_BC))),
                  pl.BlockSpec((bb, LANES), col(O_OFF_DT // LANES)),
                  pl.BlockSpec((bb, 3 * SSD_CONV_DIM), col(0)),
                  pl.BlockSpec((bb, SSD_HEADS * SSD_P, SSD_N), lambda i: (i, 0, 0)),
                  pl.BlockSpec((CONV_W, SSD_CONV_DIM), const),
                  pl.BlockSpec((1, SSD_CONV_DIM), const),
                  pl.BlockSpec((8, LANES), const),
                  pl.BlockSpec((1, SSD_DI), const),
                  pl.BlockSpec((1, SSD_DI), const),
                  pl.BlockSpec((LANES, SSD_DI), const)],
        out_specs=[pl.BlockSpec((bb, SSD_DI), col(0)),
                   pl.BlockSpec((bb, 3 * SSD_CONV_DIM), col(0)),
                   pl.BlockSpec((bb, SSD_HEADS * SSD_P, SSD_N), lambda i: (i, 0, 0))],
        out_shape=[jax.ShapeDtypeStruct((b, SSD_DI), MM_DTYPE),
                   jax.ShapeDtypeStruct((b, 3 * SSD_CONV_DIM), f32),
                   jax.ShapeDtypeStruct((b, SSD_HEADS * SSD_P, SSD_N), f32)],
        compiler_params=pltpu.CompilerParams(dimension_semantics=("parallel",), vmem_limit_bytes=VMEM_LIMIT),
        name="ssd_decode",
    )(proj, proj, proj, proj, buf, hst, cw, cb, hp, dskip, ng, expand)


def _pack_even_w_in(w):
    d = w.shape[0]
    o_beta = 4 * GDN_KW
    o_lq = o_beta + 2 * GDN_HEADS
    o_lr = o_lq + 2 * GLA_KW + 2 * GLA_VW
    small = jnp.concatenate([w[:, o_beta:o_lq], w[:, o_lr:o_lr + GLA_RANK],
                             jnp.zeros((d, LANES - 2 * GDN_HEADS - GLA_RANK), w.dtype)], axis=1)
    packed = jnp.concatenate([w[:, :o_beta], w[:, o_lq:o_lr], small], axis=1)
    return jnp.pad(packed, ((0, 0), (0, E_N - packed.shape[1]))).astype(MM_DTYPE)


def _pack_odd_w_in(w):
    d = w.shape[0]
    z = w[:, :SSD_DI]
    xbc = w[:, SSD_DI:SSD_DI + SSD_CONV_DIM]
    dt = w[:, SSD_DI + SSD_CONV_DIM:]
    packed = jnp.concatenate([xbc[:, :SSD_DI], z, xbc[:, SSD_DI:], dt], axis=1)
    return jnp.pad(packed, ((0, 0), (0, O_N - packed.shape[1]))).astype(MM_DTYPE)


def _lane_row(vals, offset):
    return jnp.zeros((LANES,), f32).at[offset:offset + vals.shape[0]].set(vals.astype(f32))


def kernel(x_prompt, x_sample, state_gdn_conv, state_gdn, state_gla, state_ssd_conv, state_ssd, e_pre_g, e_post_g, e_w_in, e_conv_w, e_conv_b, gdn_a_log, gdn_dt_bias, gdn_norm_g, gla_w_lr, gla_b_lr, gla_norm_g, e_w_out, o_pre_g, o_post_g, o_w_in, ssd_conv_w, ssd_conv_b, ssd_dt_bias, ssd_a_log, ssd_d, ssd_norm_g, o_w_out):
    bp, lp, d = x_prompt.shape
    bs = x_sample.shape[0]
    tp = bp * lp
    tb = min(lp, 256)
    bb = min(bs, 8)

    w_in0 = _pack_even_w_in(e_w_in[0])
    w_out0 = e_w_out[0].astype(MM_DTYPE)
    pre0 = e_pre_g[0].reshape(1, d)
    post0 = e_post_g[0].reshape(1, d)
    cw0 = e_conv_w[0]
    cb0 = e_conv_b[0].reshape(1, GDN_CONV_DIM)
    hp0 = jnp.zeros((8, LANES), f32)
    hp0 = hp0.at[0].set(_lane_row(-jnp.exp(gdn_a_log[0].astype(f32)), SMALL_A))
    hp0 = hp0.at[1].set(_lane_row(gdn_dt_bias[0], SMALL_A))
    ng_gdn = gdn_norm_g[0].reshape(1, GDN_DV)
    wlr = jnp.zeros((LANES, GLA_KW), f32).at[SMALL_LR:SMALL_LR + GLA_RANK].set(gla_w_lr[0]).astype(MM_DTYPE)
    blr = gla_b_lr[0].reshape(1, GLA_KW)
    ng_gla = gla_norm_g[0].reshape(1, GLA_DV)

    hp_flat = x_prompt.reshape(tp, d)
    hs_flat = x_sample.reshape(bs, d)

    proj_p = _norm_proj(hp_flat, pre0, w_in0, 1280).reshape(bp, lp, E_N)
    o1_p, gdn_conv_p, gdn_p = _gdn_prompt(proj_p, cw0, cb0, hp0, ng_gdn, tb)
    o2_p, gla_p = _gla_prompt(proj_p, wlr, blr, ng_gla, tb)
    o_p = jnp.concatenate([o1_p, o2_p], axis=-1).reshape(tp, GDN_VW + GLA_VW)
    hp_flat = _out_proj(o_p, w_out0, post0, hp_flat)

    proj_s = _norm_proj(hs_flat, pre0, w_in0, 1280)
    o1_s, gdn_conv_s, gdn_s = _gdn_decode(proj_s, state_gdn_conv[0].reshape(bs, 3 * GDN_CONV_DIM), state_gdn[0],
                                          cw0, cb0, hp0, ng_gdn, bb)
    o2_s, gla_s = _gla_decode(proj_s, state_gla[0], wlr, blr, ng_gla, bb)
    hs_flat = _out_proj(jnp.concatenate([o1_s, o2_s], axis=-1), w_out0, post0, hs_flat)

    w_in1 = _pack_odd_w_in(o_w_in[0])
    w_out1 = o_w_out[0].astype(MM_DTYPE)
    pre1 = o_pre_g[0].reshape(1, d)
    post1 = o_post_g[0].reshape(1, d)
    cw1 = ssd_conv_w[0]
    cb1 = ssd_conv_b[0].reshape(1, SSD_CONV_DIM)
    hp1 = jnp.zeros((8, LANES), f32)
    hp1 = hp1.at[0].set(_lane_row(ssd_dt_bias[0], 0))
    hp1 = hp1.at[1].set(_lane_row(-jnp.exp(ssd_a_log[0].astype(f32)), 0))
    dskip = jnp.repeat(ssd_d[0].astype(f32), SSD_P).reshape(1, SSD_DI)
    ng_ssd = ssd_norm_g[0].reshape(1, SSD_DI)
    expand = (lax.broadcasted_iota(jnp.int32, (LANES, SSD_DI), 0)
              == lax.broadcasted_iota(jnp.int32, (LANES, SSD_DI), 1) // SSD_P).astype(MM_DTYPE)

    proj_p = _norm_proj(hp_flat, pre1, w_in1, 1792).reshape(bp, lp, O_N)
    y_p, ssd_conv_p, ssd_p = _ssd_prompt(proj_p, cw1, cb1, hp1, dskip, ng_ssd, tb)
    hp_flat = _out_proj(y_p.reshape(tp, SSD_DI), w_out1, post1, hp_flat)

    proj_s = _norm_proj(hs_flat, pre1, w_in1, 1792)
    y_s, ssd_conv_s, ssd_s = _ssd_decode(proj_s, state_ssd_conv[0].reshape(bs, 3 * SSD_CONV_DIM),
                                         state_ssd[0].reshape(bs, SSD_HEADS * SSD_P, SSD_N),
                                         cw1, cb1, hp1, dskip, ng_ssd, expand, bb)
    hs_flat = _out_proj(y_s, w_out1, post1, hs_flat)

    return (hp_flat.reshape(bp, lp, d), hs_flat.reshape(bs, 1, d),
            gdn_conv_p[None], gdn_p[None], gla_p[None],
            ssd_conv_p[None], ssd_p.reshape(bp, SSD_HEADS, SSD_P, SSD_N)[None],
            gdn_conv_s.reshape(bs, CONV_W - 1, GDN_CONV_DIM)[None], gdn_s[None], gla_s[None],
            ssd_conv_s.reshape(bs, CONV_W - 1, SSD_CONV_DIM)[None],
            ssd_s.reshape(bs, SSD_HEADS, SSD_P, SSD_N)[None])
```

```python
import functools

import jax
import jax.numpy as jnp
from jax import lax
from jax.experimental import pallas as pl
from jax.experimental.pallas import tpu as pltpu

f32 = jnp.float32
bf16 = jnp.bfloat16
MM_DTYPE = jnp.bfloat16

D_MODEL = 1024
CONV_W = 4
EPS = 1e-6
LANES = 128

GDN_HEADS = 8
GDN_DK = 128
GDN_DV = 128
GDN_KW = GDN_HEADS * GDN_DK
GDN_VW = GDN_HEADS * GDN_DV
GDN_CONV_DIM = 2 * GDN_KW + GDN_VW
GLA_HEADS = 4
GLA_DK = 128
GLA_DV = 256
GLA_KW = GLA_HEADS * GLA_DK
GLA_VW = GLA_HEADS * GLA_DV
GLA_RANK = 16
GLA_TAU = 16.0
SSD_DI = 2 * D_MODEL
SSD_P = 64
SSD_HEADS = SSD_DI // SSD_P
SSD_N = 128
SSD_G = 4
SSD_K = SSD_HEADS // SSD_G
SSD_GW = SSD_DI // SSD_G
SSD_BC = SSD_G * SSD_N
SSD_CONV_DIM = SSD_DI + 2 * SSD_BC

E_OFF_GATE = 3 * GDN_KW
E_OFF_LQ = 4 * GDN_KW
E_OFF_LK = E_OFF_LQ + GLA_KW
E_OFF_LV = E_OFF_LK + GLA_KW
E_OFF_LGATE = E_OFF_LV + GLA_VW
E_OFF_SMALL = E_OFF_LGATE + GLA_VW
E_N = 7680
SMALL_BETA = 0
SMALL_A = GDN_HEADS
SMALL_LR = 2 * GDN_HEADS
O_OFF_Z = SSD_DI
O_OFF_BC = 2 * SSD_DI
O_OFF_DT = 2 * SSD_DI + 2 * SSD_BC
O_N = 5376

CHUNK = 64
assert SSD_P == CHUNK and 2 * SSD_P == LANES
VMEM_LIMIT = 56 * 1024 * 1024


def _mm(a, b):
    return jnp.dot(a.astype(MM_DTYPE), b.astype(MM_DTYPE), preferred_element_type=f32)


def _mm_nt(a, b):
    return lax.dot_general(a.astype(MM_DTYPE), b.astype(MM_DTYPE), (((1,), (1,)), ((), ())),
                           preferred_element_type=f32)


def _mm_tn(a, b):
    return lax.dot_general(a.astype(MM_DTYPE), b.astype(MM_DTYPE), (((0,), (0,)), ((), ())),
                           preferred_element_type=f32)


def _split3(x):
    hi = x.astype(bf16)
    r = x - hi.astype(f32)
    mid = r.astype(bf16)
    lo = (r - mid.astype(f32)).astype(bf16)
    return hi, mid, lo


def _sel_mm(sel, x):
    hi, mid, lo = _split3(x)
    s = sel.astype(bf16)
    return (jnp.dot(s, hi, preferred_element_type=f32) + jnp.dot(s, mid, preferred_element_type=f32)
            + jnp.dot(s, lo, preferred_element_type=f32))


def _sel_rows(x, sel):
    hi, mid, lo = _split3(x)
    s = sel.astype(bf16)
    return (jnp.dot(hi, s, preferred_element_type=f32) + jnp.dot(mid, s, preferred_element_type=f32)
            + jnp.dot(lo, s, preferred_element_type=f32))


def _sigmoid(x):
    return 1.0 / (1.0 + jnp.exp(-x))


def _silu(x):
    return x * _sigmoid(x)


def _softplus(x):
    return jnp.maximum(x, 0.0) + jnp.log1p(jnp.exp(-jnp.abs(x)))


def _tril(n, strict=False):
    r = lax.broadcasted_iota(jnp.int32, (n, n), 0)
    c = lax.broadcasted_iota(jnp.int32, (n, n), 1)
    return (r > c) if strict else (r >= c)


def _masked_decay(col, row, incl):
    d = col - row
    return jnp.where(incl, jnp.exp(jnp.where(incl, d, 0.0)), 0.0)


def _rms(x, width):
    return x * lax.rsqrt(jnp.sum(x * x, axis=-1, keepdims=True) * (1.0 / width) + EPS)


def _norm_proj_kernel(x_ref, g_ref, w_ref, o_ref, u_scr):
    @pl.when(pl.program_id(1) == 0)
    def _():
        u_scr[...] = (_rms(x_ref[...], D_MODEL) * g_ref[...]).astype(u_scr.dtype)

    o_ref[...] = jnp.dot(u_scr[...], w_ref[...], preferred_element_type=f32)


def _norm_proj(x, g, w, tn):
    t, d = x.shape
    n = w.shape[1]
    tm = min(t, 1024)
    return pl.pallas_call(
        _norm_proj_kernel,
        grid=(t // tm, n // tn),
        in_specs=[pl.BlockSpec((tm, d), lambda i, j: (i, 0)),
                  pl.BlockSpec((1, d), lambda i, j: (0, 0)),
                  pl.BlockSpec((d, tn), lambda i, j: (0, j))],
        out_specs=pl.BlockSpec((tm, tn), lambda i, j: (i, j)),
        out_shape=jax.ShapeDtypeStruct((t, n), f32),
        scratch_shapes=[pltpu.VMEM((tm, d), MM_DTYPE)],
        compiler_params=pltpu.CompilerParams(dimension_semantics=("parallel", "arbitrary"),
                                             vmem_limit_bytes=VMEM_LIMIT),
        name="norm_proj",
    )(x, g, w)


def _out_proj_kernel(oa_ref, ob_ref, wa_ref, wb_ref, g_ref, h_ref, y_ref):
    p = (jnp.dot(oa_ref[...], wa_ref[...], preferred_element_type=f32)
         + jnp.dot(ob_ref[...], wb_ref[...], preferred_element_type=f32))
    y_ref[...] = h_ref[...] + _rms(p, D_MODEL) * g_ref[...]


def _out_proj(oa, ob, w, g, h):
    t = oa.shape[0]
    e = w.shape[0] // 2
    assert oa.shape[1] == ob.shape[1] and oa.shape[1] in (e, 2 * e)
    col_b = 1 if ob.shape[1] == 2 * e else 0
    d = w.shape[1]
    tm = min(t, 512)
    return pl.pallas_call(
        _out_proj_kernel,
        grid=(t // tm,),
        in_specs=[pl.BlockSpec((tm, e), lambda i: (i, 0)),
                  pl.BlockSpec((tm, e), lambda i: (i, col_b)),
                  pl.BlockSpec((e, d), lambda i: (0, 0)),
                  pl.BlockSpec((e, d), lambda i: (1, 0)),
                  pl.BlockSpec((1, d), lambda i: (0, 0)),
                  pl.BlockSpec((tm, d), lambda i: (i, 0))],
        out_specs=pl.BlockSpec((tm, d), lambda i: (i, 0)),
        out_shape=jax.ShapeDtypeStruct((t, d), f32),
        compiler_params=pltpu.CompilerParams(dimension_semantics=("parallel",),
                                             vmem_limit_bytes=VMEM_LIMIT),
        name="out_proj",
    )(oa, ob, w, w, g, h)


def _conv_block(xp_scr, parts, cw_ref, cb_ref, tb, first):
    @pl.when(first)
    def _():
        xp_scr[0:8, :] = jnp.zeros((8, xp_scr.shape[1]), f32)

    @pl.when(jnp.logical_not(first))
    def _():
        xp_scr[5:8, :] = xp_scr[tb + 5:tb + 8, :]

    off = 0
    for p in parts:
        w = p.shape[1]
        xp_scr[8:8 + tb, off:off + w] = p[...]
        off += w

    def conv_cols(c0, c1):
        acc = cb_ref[:, c0:c1] + xp_scr[8:8 + tb, c0:c1] * cw_ref[3:4, c0:c1]
        for s in range(CONV_W - 1):
            acc = acc + xp_scr[5 + s:5 + s + tb, c0:c1] * cw_ref[s:s + 1, c0:c1]
        return _silu(acc)

    return conv_cols


def _gdn_prompt_kernel(q_ref, k_ref, v_ref, gate_ref, small_ref, cw_ref, cb_ref, hp_ref, ng_ref,
                       o_ref, conv_out_ref, s_out_ref,
                       xp_scr, qkv_scr, beta_scr, lg_scr, s_scr, gl_scr, up_scr, w_scr, qg_scr, kd_scr, p_scr, *, tb):
    i = pl.program_id(1)
    first = i == 0
    conv_cols = _conv_block(xp_scr, (q_ref, k_ref, v_ref), cw_ref, cb_ref, tb, first)

    @pl.when(first)
    def _():
        s_scr[...] = jnp.zeros(s_scr.shape, f32)

    for h in range(GDN_HEADS):
        c0 = h * GDN_DK
        q = conv_cols(c0, c0 + GDN_DK)
        q = q * lax.rsqrt(jnp.sum(q * q, axis=-1, keepdims=True) + EPS) * (GDN_DK ** -0.5)
        qkv_scr[:, c0:c0 + GDN_DK] = q
        k = conv_cols(GDN_KW + c0, GDN_KW + c0 + GDN_DK)
        k = k * lax.rsqrt(jnp.sum(k * k, axis=-1, keepdims=True) + EPS)
        qkv_scr[:, GDN_KW + c0:GDN_KW + c0 + GDN_DK] = k
        qkv_scr[:, 2 * GDN_KW + c0:2 * GDN_KW + c0 + GDN_DV] = conv_cols(2 * GDN_KW + c0, 2 * GDN_KW + c0 + GDN_DV)

    small = small_ref[...]
    beta_scr[...] = _sigmoid(small)
    lg_scr[...] = hp_ref[0:1, :] * _softplus(small + hp_ref[1:2, :])

    incl = _tril(CHUNK)
    strict = _tril(CHUNK, strict=True)
    lmat = incl.astype(f32)
    ng = ng_ref[...]
    heads = range(GDN_HEADS)

    def prep_body(c, carry):
        r0 = pl.multiple_of(c * CHUNK, CHUNK)
        rows = pl.ds(r0, CHUNK)
        gam = _sel_mm(lmat, lg_scr[rows, :])
        gam_t = gam.T
        beta_c = beta_scr[rows, :]
        gl_scr[pl.ds(c, 1), :] = jnp.exp(gam[CHUNK - 1:CHUNK, :])
        q = [qkv_scr[rows, h * GDN_DK:(h + 1) * GDN_DK] for h in heads]
        k = [qkv_scr[rows, GDN_KW + h * GDN_DK:GDN_KW + (h + 1) * GDN_DK] for h in heads]
        v = [qkv_scr[rows, 2 * GDN_KW + h * GDN_DV:2 * GDN_KW + (h + 1) * GDN_DV] for h in heads]
        gcol = [gam[:, SMALL_A + h:SMALL_A + h + 1] for h in heads]
        glast = [gam[CHUNK - 1:CHUNK, SMALL_A + h:SMALL_A + h + 1] for h in heads]
        bcol = [beta_c[:, SMALL_BETA + h:SMALL_BETA + h + 1] for h in heads]
        kb = [k[h].astype(MM_DTYPE) for h in heads]
        kk = [_mm_nt(kb[h], kb[h]) for h in heads]
        qk = [_mm_nt(q[h], kb[h]) for h in heads]
        decay = [_masked_decay(gcol[h], gam_t[SMALL_A + h:SMALL_A + h + 1, :], incl) for h in heads]
        eg = [jnp.exp(gcol[h]) for h in heads]
        for h in heads:
            c0 = h * GDN_DK
            qg_scr[rows, c0:c0 + GDN_DK] = (eg[h] * q[h]).astype(MM_DTYPE)
            kd_scr[rows, c0:c0 + GDN_DK] = (jnp.exp(glast[h] - gcol[h]) * k[h]).astype(MM_DTYPE)
            p_scr[h, rows, :] = (qk[h] * decay[h]).astype(MM_DTYPE)
        nmat = [jnp.where(strict, bcol[h] * (kk[h] * decay[h]), 0.0) for h in heads]
        x = [jnp.concatenate([bcol[h] * v[h], (bcol[h] * eg[h]) * k[h]], axis=-1) for h in heads]
        x = [x[h] - _mm(nmat[h], x[h]) for h in heads]
        npow = nmat
        for _ in range(5):
            npow = [_mm(npow[h], npow[h]) for h in heads]
            x = [x[h] + _mm(npow[h], x[h]) for h in heads]
        for h in heads:
            c0 = h * GDN_DK
            up_scr[rows, c0:c0 + GDN_DV] = x[h][:, :GDN_DV]
            w_scr[rows, c0:c0 + GDN_DK] = x[h][:, GDN_DV:].astype(MM_DTYPE)
        return carry

    lax.fori_loop(0, tb // CHUNK, prep_body, 0)

    def scan_body(c, carry):
        r0 = pl.multiple_of(c * CHUNK, CHUNK)
        rows = pl.ds(r0, CHUNK)
        gl = gl_scr[pl.ds(c, 1), :]
        s_old = [s_scr[h] for h in heads]
        r = [_mm(jnp.concatenate([w_scr[rows, h * GDN_DK:(h + 1) * GDN_DK],
                                  qg_scr[rows, h * GDN_DK:(h + 1) * GDN_DK]], axis=0), s_old[h]) for h in heads]
        u = [(up_scr[rows, h * GDN_DV:(h + 1) * GDN_DV] - r[h][:CHUNK]).astype(MM_DTYPE) for h in heads]
        o = [r[h][CHUNK:] + _mm(p_scr[h, rows, :], u[h]) for h in heads]
        upd = [_mm_tn(kd_scr[rows, h * GDN_DK:(h + 1) * GDN_DK], u[h]) for h in heads]
        for h in heads:
            c0 = h * GDN_DV
            s_scr[h] = gl[:, SMALL_A + h:SMALL_A + h + 1] * s_old[h] + upd[h]
            gate = gate_ref[rows, c0:c0 + GDN_DV]
            o_ref[rows, c0:c0 + GDN_DV] = (_rms(o[h], GDN_DV) * ng * _silu(gate)).astype(o_ref.dtype)
        return carry

    lax.fori_loop(0, tb // CHUNK, scan_body, 0)

    @pl.when(i == pl.num_programs(1) - 1)
    def _():
        conv_out_ref[...] = xp_scr[tb + 5:tb + 8, :]
        s_out_ref[...] = s_scr[...]


def _gdn_prompt(proj, cw, cb, hp, ng, tb):
    b, l, _ = proj.shape
    wq = GDN_KW
    kern = functools.partial(_gdn_prompt_kernel, tb=tb)
    col = lambda blk: (lambda bi, i: (bi, i, blk))
    const = lambda bi, i: (0, 0)
    return pl.pallas_call(
        kern,
        grid=(b, l // tb),
        in_specs=[pl.BlockSpec((None, tb, wq), col(0)),
                  pl.BlockSpec((None, tb, wq), col(1)),
                  pl.BlockSpec((None, tb, wq), col(2)),
                  pl.BlockSpec((None, tb, wq), col(3)),
                  pl.BlockSpec((None, tb, LANES), col(E_OFF_SMALL // LANES)),
                  pl.BlockSpec((CONV_W, GDN_CONV_DIM), const),
                  pl.BlockSpec((1, GDN_CONV_DIM), const),
                  pl.BlockSpec((8, LANES), const),
                  pl.BlockSpec((1, GDN_DV), const)],
        out_specs=[pl.BlockSpec((None, tb, GDN_VW), lambda bi, i: (bi, i, 0)),
                   pl.BlockSpec((None, CONV_W - 1, GDN_CONV_DIM), lambda bi, i: (bi, 0, 0)),
                   pl.BlockSpec((None, GDN_HEADS, GDN_DK, GDN_DV), lambda bi, i: (bi, 0, 0, 0))],
        out_shape=[jax.ShapeDtypeStruct((b, l, GDN_VW), MM_DTYPE),
                   jax.ShapeDtypeStruct((b, CONV_W - 1, GDN_CONV_DIM), f32),
                   jax.ShapeDtypeStruct((b, GDN_HEADS, GDN_DK, GDN_DV), f32)],
        scratch_shapes=[pltpu.VMEM((tb + 8, GDN_CONV_DIM), f32),
                        pltpu.VMEM((tb, GDN_CONV_DIM), f32),
                        pltpu.VMEM((tb, LANES), f32),
                        pltpu.VMEM((tb, LANES), f32),
                        pltpu.VMEM((GDN_HEADS, GDN_DK, GDN_DV), f32),
                        pltpu.VMEM((max(8, tb // CHUNK), LANES), f32),
                        pltpu.VMEM((tb, GDN_VW), f32),
                        pltpu.VMEM((tb, GDN_KW), MM_DTYPE),
                        pltpu.VMEM((tb, GDN_KW), MM_DTYPE),
                        pltpu.VMEM((tb, GDN_KW), MM_DTYPE),
                        pltpu.VMEM((GDN_HEADS, tb, CHUNK), MM_DTYPE)],
        compiler_params=pltpu.CompilerParams(dimension_semantics=("parallel", "arbitrary"),
                                             vmem_limit_bytes=VMEM_LIMIT),
        name="gdn_prompt",
    )(proj, proj, proj, proj, proj, cw, cb, hp, ng)


def _gla_prompt_kernel(q_ref, k_ref, v_ref, gate_ref, small_ref, wlr_ref, blr_ref, ng_ref,
                       o_ref, s_out_ref, la_scr, st_scr, *, tb):
    i = pl.program_id(1)

    @pl.when(i == 0)
    def _():
        st_scr[...] = jnp.zeros(st_scr.shape, f32)

    z = _mm(small_ref[...], wlr_ref[...]) + blr_ref[...]
    la_scr[...] = -_softplus(-z) * (1.0 / GLA_TAU)

    incl = _tril(CHUNK)
    lmat = incl.astype(f32)
    ng = ng_ref[...]

    def chunk_body(c, carry):
        r0 = pl.multiple_of(c * CHUNK, CHUNK)
        rows = pl.ds(r0, CHUNK)
        heads = range(GLA_HEADS)
        bcum = [_sel_mm(lmat, la_scr[rows, h * GLA_DK:(h + 1) * GLA_DK]) for h in heads]
        k = [k_ref[rows, h * GLA_DK:(h + 1) * GLA_DK] for h in heads]
        v = [v_ref[rows, h * GLA_DV:(h + 1) * GLA_DV].astype(MM_DTYPE) for h in heads]
        qe = [(q_ref[rows, h * GLA_DK:(h + 1) * GLA_DK] * (GLA_DK ** -0.5) * jnp.exp(bcum[h])).astype(MM_DTYPE)
              for h in heads]
        ke = [k[h] * jnp.exp(-bcum[h]) for h in heads]
        blast = [bcum[h][CHUNK - 1:CHUNK, :] for h in heads]
        kdec = [k[h] * jnp.exp(blast[h] - bcum[h]) for h in heads]
        st_old = [st_scr[h] for h in heads]
        p = [jnp.where(incl, _mm_nt(qe[h], ke[h]), 0.0) for h in heads]
        o_inter = [_mm_nt(qe[h], st_old[h]) for h in heads]
        upd = [_mm_tn(v[h], kdec[h]) for h in heads]
        o = [_mm(p[h], v[h]) + o_inter[h] for h in heads]
        for h in heads:
            v0 = h * GLA_DV
            st_scr[h] = jnp.exp(blast[h]) * st_old[h] + upd[h]
            gate = gate_ref[rows, v0:v0 + GLA_DV]
            o_ref[rows, v0:v0 + GLA_DV] = (_rms(o[h], GLA_DV) * ng * _silu(gate)).astype(o_ref.dtype)
        return carry

    lax.fori_loop(0, tb // CHUNK, chunk_body, 0)

    @pl.when(i == pl.num_programs(1) - 1)
    def _():
        for h in range(GLA_HEADS):
            s_out_ref[h] = st_scr[h].T


def _gla_prompt(proj, wlr, blr, ng, tb):
    b, l, _ = proj.shape
    kern = functools.partial(_gla_prompt_kernel, tb=tb)
    col = lambda blk: (lambda bi, i: (bi, i, blk))
    const = lambda bi, i: (0, 0)
    return pl.pallas_call(
        kern,
        grid=(b, l // tb),
        in_specs=[pl.BlockSpec((None, tb, GLA_KW), col(E_OFF_LQ // GLA_KW)),
                  pl.BlockSpec((None, tb, GLA_KW), col(E_OFF_LK // GLA_KW)),
                  pl.BlockSpec((None, tb, GLA_VW), col(E_OFF_LV // GLA_VW)),
                  pl.BlockSpec((None, tb, GLA_VW), col(E_OFF_LGATE // GLA_VW)),
                  pl.BlockSpec((None, tb, LANES), col(E_OFF_SMALL // LANES)),
                  pl.BlockSpec((LANES, GLA_KW), const),
                  pl.BlockSpec((1, GLA_KW), const),
                  pl.BlockSpec((1, GLA_DV), const)],
        out_specs=[pl.BlockSpec((None, tb, GLA_VW), lambda bi, i: (bi, i, 0)),
                   pl.BlockSpec((None, GLA_HEADS, GLA_DK, GLA_DV), lambda bi, i: (bi, 0, 0, 0))],
        out_shape=[jax.ShapeDtypeStruct((b, l, GLA_VW), MM_DTYPE),
                   jax.ShapeDtypeStruct((b, GLA_HEADS, GLA_DK, GLA_DV), f32)],
        scratch_shapes=[pltpu.VMEM((tb, GLA_KW), f32),
                        pltpu.VMEM((GLA_HEADS, GLA_DV, GLA_DK), f32)],
        compiler_params=pltpu.CompilerParams(dimension_semantics=("parallel", "arbitrary"),
                                             vmem_limit_bytes=VMEM_LIMIT),
        name="gla_prompt",
    )(proj, proj, proj, proj, proj, wlr, blr, ng)


def _ssd_prompt_kernel(x_ref, z_ref, bc_ref, dt_ref, cw_ref, cb_ref, hp_ref, dskip_ref, ng_ref,
                       y_ref, conv_out_ref, h_out_ref,
                       xp_scr, xbc_scr, dt_scr, a_scr, h_scr, *, tb):
    i = pl.program_id(1)
    first = i == 0
    conv_cols = _conv_block(xp_scr, (x_ref, bc_ref), cw_ref, cb_ref, tb, first)

    @pl.when(first)
    def _():
        h_scr[...] = jnp.zeros(h_scr.shape, f32)

    for c0 in range(0, SSD_CONV_DIM, 512):
        xbc_scr[:, c0:c0 + 512] = conv_cols(c0, c0 + 512)
    dt = _softplus(dt_ref[...] + hp_ref[0:1, :])
    dt_scr[...] = dt
    a_scr[...] = dt * hp_ref[1:2, :]

    lmat = _tril(CHUNK).astype(f32)
    lane = lax.broadcasted_iota(jnp.int32, (CHUNK, LANES), 1)
    lo = lane < SSD_P
    lo_row = lo[0:1, :]
    incl2 = lax.broadcasted_iota(jnp.int32, (CHUNK, LANES), 0) >= lane % CHUNK
    top = lax.broadcasted_iota(jnp.int32, (2 * SSD_P, SSD_N), 0) < SSD_P
    n_pairs = SSD_HEADS // 2
    ppg = SSD_K // 2
    groups = range(SSD_G)

    def chunk_body(c, carry):
        r0 = pl.multiple_of(c * CHUNK, CHUNK)
        rows = pl.ds(r0, CHUNK)
        lam = _sel_mm(lmat, a_scr[rows, :])
        lam_t = lam.T
        dt_c = dt_scr[rows, :]
        llast = lam[CHUNK - 1:CHUNK, :]
        bm = [xbc_scr[rows, SSD_DI + g * SSD_N:SSD_DI + (g + 1) * SSD_N].astype(MM_DTYPE) for g in groups]
        cm = [xbc_scr[rows, SSD_DI + SSD_BC + g * SSD_N:SSD_DI + SSD_BC + (g + 1) * SSD_N].astype(MM_DTYPE)
              for g in groups]
        h_old = [h_scr[g * SSD_GW:(g + 1) * SSD_GW, :] for g in groups]
        cb2 = [_mm_nt(cm[g], jnp.concatenate([bm[g], bm[g]], axis=0)) for g in groups]
        y_inter = [_mm_nt(cm[g], h_old[g]) for g in groups]
        xs, xdt, xdec, lcol = [], [], [], []
        for p in range(n_pairs):
            k0 = 2 * p
            xs.append(xbc_scr[rows, p * LANES:(p + 1) * LANES])
            lcol.append(jnp.where(lo, lam[:, k0:k0 + 1], lam[:, k0 + 1:k0 + 2]))
            xdt.append(xs[p] * jnp.where(lo, dt_c[:, k0:k0 + 1], dt_c[:, k0 + 1:k0 + 2]))
            ll = jnp.where(lo_row, llast[:, k0:k0 + 1], llast[:, k0 + 1:k0 + 2])
            xdec.append((xdt[p] * jnp.exp(ll - lcol[p])).astype(MM_DTYPE))
        upd = [_mm_tn(jnp.concatenate(xdec[g * ppg:(g + 1) * ppg], axis=-1), bm[g]) for g in groups]
        y_intra = []
        for p in range(n_pairs):
            k0 = 2 * p
            lrow = jnp.concatenate([lam_t[k0:k0 + 1, :], lam_t[k0 + 1:k0 + 2, :]], axis=-1)
            m = cb2[p // ppg] * _masked_decay(lcol[p], lrow, incl2)
            blockdiag = jnp.concatenate([jnp.where(lo, xdt[p], 0.0), jnp.where(lo, 0.0, xdt[p])], axis=0)
            y_intra.append(_mm(m, blockdiag))
        for g in groups:
            ys = []
            for pp in range(ppg):
                p = g * ppg + pp
                k0 = 2 * p
                ys.append(y_intra[p] + jnp.exp(lcol[p]) * y_inter[g][:, pp * LANES:(pp + 1) * LANES]
                          + dskip_ref[:, p * LANES:(p + 1) * LANES] * xs[p])
                el = jnp.where(top, jnp.exp(llast[:, k0:k0 + 1]), jnp.exp(llast[:, k0 + 1:k0 + 2]))
                h_scr[p * LANES:(p + 1) * LANES, :] = (el * h_old[g][pp * LANES:(pp + 1) * LANES, :]
                                                        + upd[g][pp * LANES:(pp + 1) * LANES, :])
            yg = jnp.concatenate(ys, axis=-1) * _silu(z_ref[rows, g * SSD_GW:(g + 1) * SSD_GW])
            y_ref[rows, g * SSD_GW:(g + 1) * SSD_GW] = (
                _rms(yg, SSD_GW) * ng_ref[:, g * SSD_GW:(g + 1) * SSD_GW]).astype(y_ref.dtype)
        return carry

    lax.fori_loop(0, tb // CHUNK, chunk_body, 0)

    @pl.when(i == pl.num_programs(1) - 1)
    def _():
        conv_out_ref[...] = xp_scr[tb + 5:tb + 8, :]
        h_out_ref[...] = h_scr[...]


def _ssd_prompt(proj, cw, cb, hp, dskip, ng, tb):
    b, l, _ = proj.shape
    kern = functools.partial(_ssd_prompt_kernel, tb=tb)
    col = lambda blk: (lambda bi, i: (bi, i, blk))
    const = lambda bi, i: (0, 0)
    return pl.pallas_call(
        kern,
        grid=(b, l // tb),
        in_specs=[pl.BlockSpec((None, tb, SSD_DI), col(0)),
                  pl.BlockSpec((None, tb, SSD_DI), col(O_OFF_Z // SSD_DI)),
                  pl.BlockSpec((None, tb, 2 * SSD_BC), col(O_OFF_BC // (2 * SSD_BC))),
                  pl.BlockSpec((None, tb, LANES), col(O_OFF_DT // LANES)),
                  pl.BlockSpec((CONV_W, SSD_CONV_DIM), const),
                  pl.BlockSpec((1, SSD_CONV_DIM), const),
                  pl.BlockSpec((8, LANES), const),
                  pl.BlockSpec((1, SSD_DI), const),
                  pl.BlockSpec((1, SSD_DI), const)],
        out_specs=[pl.BlockSpec((None, tb, SSD_DI), lambda bi, i: (bi, i, 0)),
                   pl.BlockSpec((None, CONV_W - 1, SSD_CONV_DIM), lambda bi, i: (bi, 0, 0)),
                   pl.BlockSpec((None, SSD_HEADS * SSD_P, SSD_N), lambda bi, i: (bi, 0, 0))],
        out_shape=[jax.ShapeDtypeStruct((b, l, SSD_DI), MM_DTYPE),
                   jax.ShapeDtypeStruct((b, CONV_W - 1, SSD_CONV_DIM), f32),
                   jax.ShapeDtypeStruct((b, SSD_HEADS * SSD_P, SSD_N), f32)],
        scratch_shapes=[pltpu.VMEM((tb + 8, SSD_CONV_DIM), f32),
                        pltpu.VMEM((tb, SSD_CONV_DIM), f32),
                        pltpu.VMEM((tb, LANES), f32),
                        pltpu.VMEM((tb, LANES), f32),
                        pltpu.VMEM((SSD_HEADS * SSD_P, SSD_N), f32)],
        compiler_params=pltpu.CompilerParams(dimension_semantics=("parallel", "arbitrary"),
                                             vmem_limit_bytes=VMEM_LIMIT),
        name="ssd_prompt",
    )(proj, proj, proj, proj, cw, cb, hp, dskip, ng)


def _pad_rows(row, n):
    return jnp.concatenate([row, jnp.zeros((n - 1, row.shape[1]), row.dtype)], axis=0)


def _bcast_cols(row, width):
    hi, mid, lo = _split3(row)
    lhs = jnp.concatenate([hi, mid, lo, jnp.zeros((13, row.shape[1]), bf16)], axis=0)
    sel = (lax.broadcasted_iota(jnp.int32, (16, width), 0) < 3).astype(bf16)
    return lax.dot_general(lhs, sel, (((0,), (0,)), ((), ())), preferred_element_type=f32)


def _conv_step(buf_ref, x, cw_ref, cb_ref, new_buf_ref):
    cdim = x.shape[1]
    acc = cb_ref[...] + x * cw_ref[3:4, :]
    for s in range(CONV_W - 1):
        acc = acc + buf_ref[:, s * cdim:(s + 1) * cdim] * cw_ref[s:s + 1, :]
    new_buf_ref[:, 0:2 * cdim] = buf_ref[:, cdim:3 * cdim]
    new_buf_ref[:, 2 * cdim:3 * cdim] = x
    return _silu(acc)


def _gdn_decode_kernel(q_ref, k_ref, v_ref, gate_ref, small_ref, buf_ref, s_ref, cw_ref, cb_ref, hp_ref, ng_ref,
                       o_ref, new_buf_ref, s_out_ref, *, bb):
    x = jnp.concatenate([q_ref[...], k_ref[...], v_ref[...]], axis=-1)
    qkv = _conv_step(buf_ref, x, cw_ref, cb_ref, new_buf_ref)
    small = small_ref[...]
    beta = _sigmoid(small)
    eg = jnp.exp(hp_ref[0:1, :] * _softplus(small + hp_ref[1:2, :]))
    ng = ng_ref[...]
    gate = gate_ref[...]
    for h in range(GDN_HEADS):
        c0 = h * GDN_DK
        q = qkv[:, c0:c0 + GDN_DK]
        q = q * lax.rsqrt(jnp.sum(q * q, axis=-1, keepdims=True) + EPS) * (GDN_DK ** -0.5)
        k = qkv[:, GDN_KW + c0:GDN_KW + c0 + GDN_DK]
        k = k * lax.rsqrt(jnp.sum(k * k, axis=-1, keepdims=True) + EPS)
        v = qkv[:, 2 * GDN_KW + c0:2 * GDN_KW + c0 + GDN_DV]
        qk = jnp.sum(q * k, axis=-1, keepdims=True)
        b_h = beta[:, SMALL_BETA + h:SMALL_BETA + h + 1]
        eg_h = eg[:, SMALL_A + h:SMALL_A + h + 1]
        seqs = range(bb)
        s_old = [s_ref[b, h] for b in seqs]
        r = [_mm(jnp.concatenate([k[b:b + 1], q[b:b + 1], jnp.zeros((6, GDN_DK), f32)], axis=0), s_old[b])
             for b in seqs]
        u = [b_h[b:b + 1] * v[b:b + 1] - (b_h[b:b + 1] * eg_h[b:b + 1]) * r[b][0:1] for b in seqs]
        upd = [_mm_tn(_pad_rows(k[b:b + 1], 16), _pad_rows(u[b], 16)) for b in seqs]
        for b in seqs:
            s_out_ref[b, h] = eg_h[b:b + 1] * s_old[b] + upd[b]
        o = jnp.concatenate([eg_h[b:b + 1] * r[b][1:2] + qk[b:b + 1] * u[b] for b in seqs], axis=0)
        o_ref[:, c0:c0 + GDN_DV] = (_rms(o, GDN_DV) * ng * _silu(gate[:, c0:c0 + GDN_DV])).astype(o_ref.dtype)


def _gdn_decode(proj, buf, s, cw, cb, hp, ng, bb):
    b = proj.shape[0]
    wq = GDN_KW
    kern = functools.partial(_gdn_decode_kernel, bb=bb)
    col = lambda blk: (lambda i: (i, blk))
    const = lambda i: (0, 0)
    return pl.pallas_call(
        kern,
        grid=(b // bb,),
        in_specs=[pl.BlockSpec((bb, wq), col(0)),
                  pl.BlockSpec((bb, wq), col(1)),
                  pl.BlockSpec((bb, wq), col(2)),
                  pl.BlockSpec((bb, wq), col(3)),
                  pl.BlockSpec((bb, LANES), col(E_OFF_SMALL // LANES)),
                  pl.BlockSpec((bb, 3 * GDN_CONV_DIM), col(0)),
                  pl.BlockSpec((bb, GDN_HEADS, GDN_DK, GDN_DV), lambda i: (i, 0, 0, 0)),
                  pl.BlockSpec((CONV_W, GDN_CONV_DIM), const),
                  pl.BlockSpec((1, GDN_CONV_DIM), const),
                  pl.BlockSpec((8, LANES), const),
                  pl.BlockSpec((1, GDN_DV), const)],
        out_specs=[pl.BlockSpec((bb, GDN_VW), col(0)),
                   pl.BlockSpec((bb, 3 * GDN_CONV_DIM), col(0)),
                   pl.BlockSpec((bb, GDN_HEADS, GDN_DK, GDN_DV), lambda i: (i, 0, 0, 0))],
        out_shape=[jax.ShapeDtypeStruct((b, GDN_VW), MM_DTYPE),
                   jax.ShapeDtypeStruct((b, 3 * GDN_CONV_DIM), f32),
                   jax.ShapeDtypeStruct((b, GDN_HEADS, GDN_DK, GDN_DV), f32)],
        compiler_params=pltpu.CompilerParams(dimension_semantics=("parallel",), vmem_limit_bytes=VMEM_LIMIT),
        name="gdn_decode",
    )(proj, proj, proj, proj, proj, buf, s, cw, cb, hp, ng)


def _gla_decode_kernel(q_ref, k_ref, v_ref, gate_ref, small_ref, s_ref, wlr_ref, blr_ref, ng_ref,
                       o_ref, s_out_ref, *, bb):
    z = _mm(small_ref[...], wlr_ref[...]) + blr_ref[...]
    la = -_softplus(-z) * (1.0 / GLA_TAU)
    dec = jnp.exp(la)
    qe = q_ref[...] * (GLA_DK ** -0.5) * dec
    kx = k_ref[...]
    ke = kx * jnp.exp(-la)
    vx = v_ref[...]
    ng = ng_ref[...]
    gate = gate_ref[...]
    for h in range(GLA_HEADS):
        k0 = h * GLA_DK
        v0 = h * GLA_DV
        p = jnp.sum(qe[:, k0:k0 + GLA_DK] * ke[:, k0:k0 + GLA_DK], axis=-1, keepdims=True)
        outs = []
        for b in range(bb):
            s_old = s_ref[b, h]
            vrow = vx[b:b + 1, v0:v0 + GLA_DV]
            outs.append(_mm(_pad_rows(qe[b:b + 1, k0:k0 + GLA_DK], 8), s_old)[0:1] + p[b:b + 1] * vrow)
            s_out_ref[b, h] = (_bcast_cols(dec[b:b + 1, k0:k0 + GLA_DK], GLA_DV) * s_old
                               + _mm_tn(_pad_rows(kx[b:b + 1, k0:k0 + GLA_DK], 16), _pad_rows(vrow, 16)))
        o = jnp.concatenate(outs, axis=0)
        o_ref[:, v0:v0 + GLA_DV] = (_rms(o, GLA_DV) * ng * _silu(gate[:, v0:v0 + GLA_DV])).astype(o_ref.dtype)


def _gla_decode(proj, s, wlr, blr, ng, bb):
    b = proj.shape[0]
    kern = functools.partial(_gla_decode_kernel, bb=bb)
    col = lambda blk: (lambda i: (i, blk))
    const = lambda i: (0, 0)
    return pl.pallas_call(
        kern,
        grid=(b // bb,),
        in_specs=[pl.BlockSpec((bb, GLA_KW), col(E_OFF_LQ // GLA_KW)),
                  pl.BlockSpec((bb, GLA_KW), col(E_OFF_LK // GLA_KW)),
                  pl.BlockSpec((bb, GLA_VW), col(E_OFF_LV // GLA_VW)),
                  pl.BlockSpec((bb, GLA_VW), col(E_OFF_LGATE // GLA_VW)),
                  pl.BlockSpec((bb, LANES), col(E_OFF_SMALL // LANES)),
                  pl.BlockSpec((bb, GLA_HEADS, GLA_DK, GLA_DV), lambda i: (i, 0, 0, 0)),
                  pl.BlockSpec((LANES, GLA_KW), const),
                  pl.BlockSpec((1, GLA_KW), const),
                  pl.BlockSpec((1, GLA_DV), const)],
        out_specs=[pl.BlockSpec((bb, GLA_VW), col(0)),
                   pl.BlockSpec((bb, GLA_HEADS, GLA_DK, GLA_DV), lambda i: (i, 0, 0, 0))],
        out_shape=[jax.ShapeDtypeStruct((b, GLA_VW), MM_DTYPE),
                   jax.ShapeDtypeStruct((b, GLA_HEADS, GLA_DK, GLA_DV), f32)],
        compiler_params=pltpu.CompilerParams(dimension_semantics=("parallel",), vmem_limit_bytes=VMEM_LIMIT),
        name="gla_decode",
    )(proj, proj, proj, proj, proj, s, wlr, blr, ng)


def _ssd_decode_kernel(x_ref, z_ref, bc_ref, dt_ref, buf_ref, h_ref, cw_ref, cb_ref, hp_ref, dskip_ref, ng_ref,
                       expand_ref, y_ref, new_buf_ref, h_out_ref, *, bb):
    xin = jnp.concatenate([x_ref[...], bc_ref[...]], axis=-1)
    xbc = _conv_step(buf_ref, xin, cw_ref, cb_ref, new_buf_ref)
    xs = xbc[:, :SSD_DI]
    dt = _softplus(dt_ref[...] + hp_ref[0:1, :])
    el = jnp.exp(dt * hp_ref[1:2, :])
    expand = expand_ref[...]
    dt_x = _sel_rows(dt, expand)
    el_x = _sel_rows(el, expand)
    xdt = xs * dt_x
    z = z_ref[...]
    for g in range(SSD_G):
        gs = slice(g * SSD_GW, (g + 1) * SSD_GW)
        bm = xbc[:, SSD_DI + g * SSD_N:SSD_DI + (g + 1) * SSD_N]
        cm = xbc[:, SSD_DI + SSD_BC + g * SSD_N:SSD_DI + SSD_BC + (g + 1) * SSD_N]
        cbs = jnp.sum(cm * bm, axis=-1, keepdims=True)
        outs = []
        for b in range(bb):
            hg = h_ref[b, gs, :]
            y_inter = _mm_nt(_pad_rows(cm[b:b + 1], 8), hg)[0:1]
            outs.append(el_x[b:b + 1, gs] * y_inter)
            h_out_ref[b, gs, :] = (_bcast_cols(el_x[b:b + 1, gs], SSD_N) * hg
                                   + _mm_tn(_pad_rows(xdt[b:b + 1, gs], 16), _pad_rows(bm[b:b + 1], 16)))
        y = jnp.concatenate(outs, axis=0) + cbs * xdt[:, gs] + dskip_ref[:, gs] * xs[:, gs]
        yg = y * _silu(z[:, gs])
        y_ref[:, gs] = (_rms(yg, SSD_GW) * ng_ref[:, gs]).astype(y_ref.dtype)


def _ssd_decode(proj, buf, hst, cw, cb, hp, dskip, ng, expand, bb):
    b = proj.shape[0]
    kern = functools.partial(_ssd_decode_kernel, bb=bb)
    col = lambda blk: (lambda i: (i, blk))
    const = lambda i: (0, 0)
    return pl.pallas_call(
        kern,
        grid=(b // bb,),
        in_specs=[pl.BlockSpec((bb, SSD_DI), col(0)),
                  pl.BlockSpec((bb, SSD_DI), col(O_OFF_Z // SSD_DI)),
                  pl.BlockSpec((bb, 2 * SSD_BC), col(O_OFF_BC // (2 * SSD_BC))),
                  pl.BlockSpec((bb, LANES), col(O_OFF_DT // LANES)),
                  pl.BlockSpec((bb, 3 * SSD_CONV_DIM), col(0)),
                  pl.BlockSpec((bb, SSD_HEADS * SSD_P, SSD_N), lambda i: (i, 0, 0)),
                  pl.BlockSpec((CONV_W, SSD_CONV_DIM), const),
                  pl.BlockSpec((1, SSD_CONV_DIM), const),
                  pl.BlockSpec((8, LANES), const),
                  pl.BlockSpec((1, SSD_DI), const),
                  pl.BlockSpec((1, SSD_DI), const),
                  pl.BlockSpec((LANES, SSD_DI), const)],
        out_specs=[pl.BlockSpec((bb, SSD_DI), col(0)),
                   pl.BlockSpec((bb, 3 * SSD_CONV_DIM), col(0)),
                   pl.BlockSpec((bb, SSD_HEADS * SSD_P, SSD_N), lambda i: (i, 0, 0))],
        out_shape=[jax.ShapeDtypeStruct((b, SSD_DI), MM_DTYPE),
                   jax.ShapeDtypeStruct((b, 3 * SSD_CONV_DIM), f32),
                   jax.ShapeDtypeStruct((b, SSD_HEADS * SSD_P, SSD_N), f32)],
        compiler_params=pltpu.CompilerParams(dimension_semantics=("parallel",), vmem_limit_bytes=VMEM_LIMIT),
        name="ssd_decode",
    )(proj, proj, proj, proj, buf, hst, cw, cb, hp, dskip, ng, expand)


def _pack_even_w_in(w):
    d = w.shape[0]
    o_beta = 4 * GDN_KW
    o_lq = o_beta + 2 * GDN_HEADS
    o_lr = o_lq + 2 * GLA_KW + 2 * GLA_VW
    small = jnp.concatenate([w[:, o_beta:o_lq], w[:, o_lr:o_lr + GLA_RANK],
                             jnp.zeros((d, LANES - 2 * GDN_HEADS - GLA_RANK), w.dtype)], axis=1)
    packed = jnp.concatenate([w[:, :o_beta], w[:, o_lq:o_lr], small], axis=1)
    return jnp.pad(packed, ((0, 0), (0, E_N - packed.shape[1]))).astype(MM_DTYPE)


def _pack_odd_w_in(w):
    d = w.shape[0]
    z = w[:, :SSD_DI]
    xbc = w[:, SSD_DI:SSD_DI + SSD_CONV_DIM]
    dt = w[:, SSD_DI + SSD_CONV_DIM:]
    packed = jnp.concatenate([xbc[:, :SSD_DI], z, xbc[:, SSD_DI:], dt], axis=1)
    return jnp.pad(packed, ((0, 0), (0, O_N - packed.shape[1]))).astype(MM_DTYPE)


def _lane_row(vals, offset):
    return jnp.zeros((LANES,), f32).at[offset:offset + vals.shape[0]].set(vals.astype(f32))


def kernel(x_prompt, x_sample, state_gdn_conv, state_gdn, state_gla, state_ssd_conv, state_ssd, e_pre_g, e_post_g, e_w_in, e_conv_w, e_conv_b, gdn_a_log, gdn_dt_bias, gdn_norm_g, gla_w_lr, gla_b_lr, gla_norm_g, e_w_out, o_pre_g, o_post_g, o_w_in, ssd_conv_w, ssd_conv_b, ssd_dt_bias, ssd_a_log, ssd_d, ssd_norm_g, o_w_out):
    bp, lp, d = x_prompt.shape
    bs = x_sample.shape[0]
    tp = bp * lp
    tb = min(lp, 256)
    bb = min(bs, 8)

    w_in0 = _pack_even_w_in(e_w_in[0])
    w_out0 = e_w_out[0].astype(MM_DTYPE)
    pre0 = e_pre_g[0].reshape(1, d)
    post0 = e_post_g[0].reshape(1, d)
    cw0 = e_conv_w[0]
    cb0 = e_conv_b[0].reshape(1, GDN_CONV_DIM)
    hp0 = jnp.zeros((8, LANES), f32)
    hp0 = hp0.at[0].set(_lane_row(-jnp.exp(gdn_a_log[0].astype(f32)), SMALL_A))
    hp0 = hp0.at[1].set(_lane_row(gdn_dt_bias[0], SMALL_A))
    ng_gdn = gdn_norm_g[0].reshape(1, GDN_DV)
    wlr = jnp.zeros((LANES, GLA_KW), f32).at[SMALL_LR:SMALL_LR + GLA_RANK].set(gla_w_lr[0]).astype(MM_DTYPE)
    blr = gla_b_lr[0].reshape(1, GLA_KW)
    ng_gla = gla_norm_g[0].reshape(1, GLA_DV)

    hp_flat = x_prompt.reshape(tp, d)
    hs_flat = x_sample.reshape(bs, d)

    proj_p = _norm_proj(hp_flat, pre0, w_in0, 1280).reshape(bp, lp, E_N)
    o1_p, gdn_conv_p, gdn_p = _gdn_prompt(proj_p, cw0, cb0, hp0, ng_gdn, tb)
    o2_p, gla_p = _gla_prompt(proj_p, wlr, blr, ng_gla, tb)
    hp_flat = _out_proj(o1_p.reshape(tp, GDN_VW), o2_p.reshape(tp, GLA_VW), w_out0, post0, hp_flat)

    proj_s = _norm_proj(hs_flat, pre0, w_in0, 1280)
    o1_s, gdn_conv_s, gdn_s = _gdn_decode(proj_s, state_gdn_conv[0].reshape(bs, 3 * GDN_CONV_DIM), state_gdn[0],
                                          cw0, cb0, hp0, ng_gdn, bb)
    o2_s, gla_s = _gla_decode(proj_s, state_gla[0], wlr, blr, ng_gla, bb)
    hs_flat = _out_proj(o1_s, o2_s, w_out0, post0, hs_flat)

    w_in1 = _pack_odd_w_in(o_w_in[0])
    w_out1 = o_w_out[0].astype(MM_DTYPE)
    pre1 = o_pre_g[0].reshape(1, d)
    post1 = o_post_g[0].reshape(1, d)
    cw1 = ssd_conv_w[0]
    cb1 = ssd_conv_b[0].reshape(1, SSD_CONV_DIM)
    hp1 = jnp.zeros((8, LANES), f32)
    hp1 = hp1.at[0].set(_lane_row(ssd_dt_bias[0], 0))
    hp1 = hp1.at[1].set(_lane_row(-jnp.exp(ssd_a_log[0].astype(f32)), 0))
    dskip = jnp.repeat(ssd_d[0].astype(f32), SSD_P).reshape(1, SSD_DI)
    ng_ssd = ssd_norm_g[0].reshape(1, SSD_DI)
    expand = (lax.broadcasted_iota(jnp.int32, (LANES, SSD_DI), 0)
              == lax.broadcasted_iota(jnp.int32, (LANES, SSD_DI), 1) // SSD_P).astype(MM_DTYPE)

    proj_p = _norm_proj(hp_flat, pre1, w_in1, 1792).reshape(bp, lp, O_N)
    y_p, ssd_conv_p, ssd_p = _ssd_prompt(proj_p, cw1, cb1, hp1, dskip, ng_ssd, tb)
    y_p = y_p.reshape(tp, SSD_DI)
    hp_flat = _out_proj(y_p, y_p, w_out1, post1, hp_flat)

    proj_s = _norm_proj(hs_flat, pre1, w_in1, 1792)
    y_s, ssd_conv_s, ssd_s = _ssd_decode(proj_s, state_ssd_conv[0].reshape(bs, 3 * SSD_CONV_DIM),
                                         state_ssd[0].reshape(bs, SSD_HEADS * SSD_P, SSD_N),
                                         cw1, cb1, hp1, dskip, ng_ssd, expand, bb)
    hs_flat = _out_proj(y_s, y_s, w_out1, post1, hs_flat)

    return (hp_flat.reshape(bp, lp, d), hs_flat.reshape(bs, 1, d),
            gdn_conv_p[None], gdn_p[None], gla_p[None],
            ssd_conv_p[None], ssd_p.reshape(bp, SSD_HEADS, SSD_P, SSD_N)[None],
            gdn_conv_s.reshape(bs, CONV_W - 1, GDN_CONV_DIM)[None], gdn_s[None], gla_s[None],
            ssd_conv_s.reshape(bs, CONV_W - 1, SSD_CONV_DIM)[None],
            ssd_s.reshape(bs, SSD_HEADS, SSD_P, SSD_N)[None])
```

```python
import functools

import jax
import jax.numpy as jnp
from jax import lax
from jax.experimental import pallas as pl
from jax.experimental.pallas import tpu as pltpu

f32 = jnp.float32
bf16 = jnp.bfloat16
MM_DTYPE = jnp.bfloat16

D_MODEL = 1024
CONV_W = 4
EPS = 1e-6
LANES = 128

GDN_HEADS = 8
GDN_DK = 128
GDN_DV = 128
GDN_KW = GDN_HEADS * GDN_DK
GDN_VW = GDN_HEADS * GDN_DV
GDN_CONV_DIM = 2 * GDN_KW + GDN_VW
GLA_HEADS = 4
GLA_DK = 128
GLA_DV = 256
GLA_KW = GLA_HEADS * GLA_DK
GLA_VW = GLA_HEADS * GLA_DV
GLA_RANK = 16
GLA_TAU = 16.0
SSD_DI = 2 * D_MODEL
SSD_P = 64
SSD_HEADS = SSD_DI // SSD_P
SSD_N = 128
SSD_G = 4
SSD_K = SSD_HEADS // SSD_G
SSD_GW = SSD_DI // SSD_G
SSD_BC = SSD_G * SSD_N
SSD_CONV_DIM = SSD_DI + 2 * SSD_BC

E_OFF_GATE = 3 * GDN_KW
E_OFF_LQ = 4 * GDN_KW
E_OFF_LK = E_OFF_LQ + GLA_KW
E_OFF_LV = E_OFF_LK + GLA_KW
E_OFF_LGATE = E_OFF_LV + GLA_VW
E_OFF_SMALL = E_OFF_LGATE + GLA_VW
E_N = 7680
SMALL_BETA = 0
SMALL_A = GDN_HEADS
SMALL_LR = 2 * GDN_HEADS
O_OFF_Z = SSD_DI
O_OFF_BC = 2 * SSD_DI
O_OFF_DT = 2 * SSD_DI + 2 * SSD_BC
O_N = 5376

CHUNK = 64
GDN_PREP_CHUNKS = 2
assert SSD_P == CHUNK and 2 * SSD_P == LANES
VMEM_LIMIT = 56 * 1024 * 1024


def _mm(a, b):
    return jnp.dot(a.astype(MM_DTYPE), b.astype(MM_DTYPE), preferred_element_type=f32)


def _mm_nt(a, b):
    return lax.dot_general(a.astype(MM_DTYPE), b.astype(MM_DTYPE), (((1,), (1,)), ((), ())),
                           preferred_element_type=f32)


def _mm_tn(a, b):
    return lax.dot_general(a.astype(MM_DTYPE), b.astype(MM_DTYPE), (((0,), (0,)), ((), ())),
                           preferred_element_type=f32)


def _split3(x):
    hi = x.astype(bf16)
    r = x - hi.astype(f32)
    mid = r.astype(bf16)
    lo = (r - mid.astype(f32)).astype(bf16)
    return hi, mid, lo


def _sel_mm(sel, x):
    hi, mid, lo = _split3(x)
    s = sel.astype(bf16)
    return (jnp.dot(s, hi, preferred_element_type=f32) + jnp.dot(s, mid, preferred_element_type=f32)
            + jnp.dot(s, lo, preferred_element_type=f32))


def _sel_rows(x, sel):
    hi, mid, lo = _split3(x)
    s = sel.astype(bf16)
    return (jnp.dot(hi, s, preferred_element_type=f32) + jnp.dot(mid, s, preferred_element_type=f32)
            + jnp.dot(lo, s, preferred_element_type=f32))


def _sigmoid(x):
    return 1.0 / (1.0 + jnp.exp(-x))


def _silu(x):
    return x * _sigmoid(x)


def _softplus(x):
    return jnp.maximum(x, 0.0) + jnp.log1p(jnp.exp(-jnp.abs(x)))


def _tril(n, strict=False):
    r = lax.broadcasted_iota(jnp.int32, (n, n), 0)
    c = lax.broadcasted_iota(jnp.int32, (n, n), 1)
    return (r > c) if strict else (r >= c)


def _masked_decay(col, row, incl):
    d = col - row
    return jnp.where(incl, jnp.exp(jnp.where(incl, d, 0.0)), 0.0)


def _rms(x, width):
    return x * lax.rsqrt(jnp.sum(x * x, axis=-1, keepdims=True) * (1.0 / width) + EPS)


def _norm_proj_kernel(x_ref, g_ref, w_ref, o_ref, u_scr):
    @pl.when(pl.program_id(1) == 0)
    def _():
        u_scr[...] = (_rms(x_ref[...], D_MODEL) * g_ref[...]).astype(u_scr.dtype)

    o_ref[...] = jnp.dot(u_scr[...], w_ref[...], preferred_element_type=f32)


def _norm_proj(x, g, w, tn):
    t, d = x.shape
    n = w.shape[1]
    tm = min(t, 1024)
    return pl.pallas_call(
        _norm_proj_kernel,
        grid=(t // tm, n // tn),
        in_specs=[pl.BlockSpec((tm, d), lambda i, j: (i, 0)),
                  pl.BlockSpec((1, d), lambda i, j: (0, 0)),
                  pl.BlockSpec((d, tn), lambda i, j: (0, j))],
        out_specs=pl.BlockSpec((tm, tn), lambda i, j: (i, j)),
        out_shape=jax.ShapeDtypeStruct((t, n), f32),
        scratch_shapes=[pltpu.VMEM((tm, d), MM_DTYPE)],
        compiler_params=pltpu.CompilerParams(dimension_semantics=("parallel", "arbitrary"),
                                             vmem_limit_bytes=VMEM_LIMIT),
        name="norm_proj",
    )(x, g, w)


def _out_proj_kernel(oa_ref, ob_ref, wa_ref, wb_ref, g_ref, h_ref, y_ref):
    p = (jnp.dot(oa_ref[...], wa_ref[...], preferred_element_type=f32)
         + jnp.dot(ob_ref[...], wb_ref[...], preferred_element_type=f32))
    y_ref[...] = h_ref[...] + _rms(p, D_MODEL) * g_ref[...]


def _out_proj(oa, ob, w, g, h):
    t = oa.shape[0]
    e = w.shape[0] // 2
    assert oa.shape[1] == ob.shape[1] and oa.shape[1] in (e, 2 * e)
    col_b = 1 if ob.shape[1] == 2 * e else 0
    d = w.shape[1]
    tm = min(t, 512)
    return pl.pallas_call(
        _out_proj_kernel,
        grid=(t // tm,),
        in_specs=[pl.BlockSpec((tm, e), lambda i: (i, 0)),
                  pl.BlockSpec((tm, e), lambda i: (i, col_b)),
                  pl.BlockSpec((e, d), lambda i: (0, 0)),
                  pl.BlockSpec((e, d), lambda i: (1, 0)),
                  pl.BlockSpec((1, d), lambda i: (0, 0)),
                  pl.BlockSpec((tm, d), lambda i: (i, 0))],
        out_specs=pl.BlockSpec((tm, d), lambda i: (i, 0)),
        out_shape=jax.ShapeDtypeStruct((t, d), f32),
        compiler_params=pltpu.CompilerParams(dimension_semantics=("parallel",),
                                             vmem_limit_bytes=VMEM_LIMIT),
        name="out_proj",
    )(oa, ob, w, w, g, h)


def _normed(x_ref, g_ref):
    return (_rms(x_ref[...], D_MODEL) * g_ref[...]).astype(MM_DTYPE)


def _proj(u, w_ref, c0, c1):
    return jnp.dot(u, w_ref[:, c0:c1], preferred_element_type=f32)


def _conv_block(xp_scr, cw_ref, cb_ref, tb, first):
    @pl.when(first)
    def _():
        xp_scr[0:8, :] = jnp.zeros((8, xp_scr.shape[1]), f32)

    @pl.when(jnp.logical_not(first))
    def _():
        xp_scr[5:8, :] = xp_scr[tb + 5:tb + 8, :]

    def conv_cols(c0, c1):
        acc = cb_ref[:, c0:c1] + xp_scr[8:8 + tb, c0:c1] * cw_ref[3:4, c0:c1]
        for s in range(CONV_W - 1):
            acc = acc + xp_scr[5 + s:5 + s + tb, c0:c1] * cw_ref[s:s + 1, c0:c1]
        return _silu(acc)

    return conv_cols


def _gdn_prompt_kernel(x_ref, pg_ref, wm_ref, ws_ref, cw_ref, cb_ref, hp_ref, ng_ref,
                       o_ref, conv_out_ref, s_out_ref,
                       xp_scr, qkv_scr, gate_scr, beta_scr, lg_scr, s_scr, gl_scr, up_scr, w_scr, qg_scr, kd_scr,
                       p_scr, *, tb):
    i = pl.program_id(1)
    first = i == 0
    conv_cols = _conv_block(xp_scr, cw_ref, cb_ref, tb, first)

    @pl.when(first)
    def _():
        s_scr[...] = jnp.zeros(s_scr.shape, f32)

    u = _normed(x_ref, pg_ref)
    for c0 in range(0, GDN_CONV_DIM, GDN_KW):
        xp_scr[8:8 + tb, c0:c0 + GDN_KW] = _proj(u, wm_ref, c0, c0 + GDN_KW)
    gate_scr[...] = _proj(u, wm_ref, E_OFF_GATE, E_OFF_GATE + GDN_VW)
    small = _proj(u, ws_ref, 0, LANES)
    beta_scr[...] = _sigmoid(small)
    lg_scr[...] = hp_ref[0:1, :] * _softplus(small + hp_ref[1:2, :])

    for h in range(GDN_HEADS):
        c0 = h * GDN_DK
        q = conv_cols(c0, c0 + GDN_DK)
        q = q * lax.rsqrt(jnp.sum(q * q, axis=-1, keepdims=True) + EPS) * (GDN_DK ** -0.5)
        qkv_scr[:, c0:c0 + GDN_DK] = q
        k = conv_cols(GDN_KW + c0, GDN_KW + c0 + GDN_DK)
        k = k * lax.rsqrt(jnp.sum(k * k, axis=-1, keepdims=True) + EPS)
        qkv_scr[:, GDN_KW + c0:GDN_KW + c0 + GDN_DK] = k
        qkv_scr[:, 2 * GDN_KW + c0:2 * GDN_KW + c0 + GDN_DV] = conv_cols(2 * GDN_KW + c0, 2 * GDN_KW + c0 + GDN_DV)

    incl = _tril(CHUNK)
    strict = _tril(CHUNK, strict=True)
    eye = jnp.logical_and(incl, jnp.logical_not(strict)).astype(f32)
    span = GDN_PREP_CHUNKS * CHUNK
    sr = lax.broadcasted_iota(jnp.int32, (span, span), 0)
    sc = lax.broadcasted_iota(jnp.int32, (span, span), 1)
    lmat = jnp.logical_and(sr >= sc, sr // CHUNK == sc // CHUNK).astype(f32)
    ng = ng_ref[...]
    heads = range(GDN_HEADS)
    probs = [(ci, h) for ci in range(GDN_PREP_CHUNKS) for h in heads]
    n = range(len(probs))

    def prep_body(c, carry):
        r0 = pl.multiple_of(c * span, span)
        gam_all = _sel_mm(lmat, lg_scr[pl.ds(r0, span), :])
        gam_t = gam_all.T
        beta_all = beta_scr[pl.ds(r0, span), :]
        rows, gcol, grow, glast, bcol = [], [], [], [], []
        for ci, h in probs:
            lo, hi = ci * CHUNK, (ci + 1) * CHUNK
            rows.append(pl.ds(r0 + lo, CHUNK))
            gcol.append(gam_all[lo:hi, SMALL_A + h:SMALL_A + h + 1])
            grow.append(gam_t[SMALL_A + h:SMALL_A + h + 1, lo:hi])
            glast.append(gam_all[hi - 1:hi, SMALL_A + h:SMALL_A + h + 1])
            bcol.append(beta_all[lo:hi, SMALL_BETA + h:SMALL_BETA + h + 1])
        for ci in range(GDN_PREP_CHUNKS):
            gl_scr[pl.ds(c * GDN_PREP_CHUNKS + ci, 1), :] = jnp.exp(gam_all[(ci + 1) * CHUNK - 1:(ci + 1) * CHUNK, :])
        q = [qkv_scr[rows[j], h * GDN_DK:(h + 1) * GDN_DK] for j, (_, h) in enumerate(probs)]
        k = [qkv_scr[rows[j], GDN_KW + h * GDN_DK:GDN_KW + (h + 1) * GDN_DK] for j, (_, h) in enumerate(probs)]
        v = [qkv_scr[rows[j], 2 * GDN_KW + h * GDN_DV:2 * GDN_KW + (h + 1) * GDN_DV] for j, (_, h) in enumerate(probs)]
        kb = [k[j].astype(MM_DTYPE) for j in n]
        kk = [_mm_nt(kb[j], kb[j]) for j in n]
        qk = [_mm_nt(q[j], kb[j]) for j in n]
        decay = [_masked_decay(gcol[j], grow[j], incl) for j in n]
        eg = [jnp.exp(gcol[j]) for j in n]
        for j, (_, h) in enumerate(probs):
            c0 = h * GDN_DK
            qg_scr[rows[j], c0:c0 + GDN_DK] = (eg[j] * q[j]).astype(MM_DTYPE)
            kd_scr[rows[j], c0:c0 + GDN_DK] = (jnp.exp(glast[j] - gcol[j]) * k[j]).astype(MM_DTYPE)
            p_scr[h, rows[j], :] = (qk[j] * decay[j]).astype(MM_DTYPE)
        nmat = [jnp.where(strict, bcol[j] * (kk[j] * decay[j]), 0.0) for j in n]
        tinv = [eye - nmat[j] for j in n]
        npow = nmat
        for _ in range(5):
            npow = [_mm(npow[j], npow[j]) for j in n]
            tinv = [tinv[j] + _mm(tinv[j], npow[j]) for j in n]
        x = [_mm(tinv[j], jnp.concatenate([bcol[j] * v[j], (bcol[j] * eg[j]) * k[j]], axis=-1)) for j in n]
        for j, (_, h) in enumerate(probs):
            c0 = h * GDN_DK
            up_scr[rows[j], c0:c0 + GDN_DV] = x[j][:, :GDN_DV]
            w_scr[rows[j], c0:c0 + GDN_DK] = x[j][:, GDN_DV:].astype(MM_DTYPE)
        return carry

    lax.fori_loop(0, tb // span, prep_body, 0)

    def scan_body(c, carry):
        r0 = pl.multiple_of(c * CHUNK, CHUNK)
        rows = pl.ds(r0, CHUNK)
        gl = gl_scr[pl.ds(c, 1), :]
        s_old = [s_scr[h] for h in heads]
        r = [_mm(jnp.concatenate([w_scr[rows, h * GDN_DK:(h + 1) * GDN_DK],
                                  qg_scr[rows, h * GDN_DK:(h + 1) * GDN_DK]], axis=0), s_old[h]) for h in heads]
        u = [(up_scr[rows, h * GDN_DV:(h + 1) * GDN_DV] - r[h][:CHUNK]).astype(MM_DTYPE) for h in heads]
        o = [r[h][CHUNK:] + _mm(p_scr[h, rows, :], u[h]) for h in heads]
        upd = [_mm_tn(kd_scr[rows, h * GDN_DK:(h + 1) * GDN_DK], u[h]) for h in heads]
        for h in heads:
            c0 = h * GDN_DV
            s_scr[h] = gl[:, SMALL_A + h:SMALL_A + h + 1] * s_old[h] + upd[h]
            gate = gate_scr[rows, c0:c0 + GDN_DV]
            o_ref[rows, c0:c0 + GDN_DV] = (_rms(o[h], GDN_DV) * ng * _silu(gate)).astype(o_ref.dtype)
        return carry

    lax.fori_loop(0, tb // CHUNK, scan_body, 0)

    @pl.when(i == pl.num_programs(1) - 1)
    def _():
        conv_out_ref[...] = xp_scr[tb + 5:tb + 8, :]
        s_out_ref[...] = s_scr[...]


def _gdn_prompt(x, pre_g, w_in, cw, cb, hp, ng, tb):
    b, l, d = x.shape
    assert tb % (GDN_PREP_CHUNKS * CHUNK) == 0 and l % tb == 0
    kern = functools.partial(_gdn_prompt_kernel, tb=tb)
    const = lambda bi, i: (0, 0)
    return pl.pallas_call(
        kern,
        grid=(b, l // tb),
        in_specs=[pl.BlockSpec((None, tb, d), lambda bi, i: (bi, i, 0)),
                  pl.BlockSpec((1, d), const),
                  pl.BlockSpec((d, E_OFF_LQ), const),
                  pl.BlockSpec((d, LANES), lambda bi, i: (0, E_OFF_SMALL // LANES)),
                  pl.BlockSpec((CONV_W, GDN_CONV_DIM), const),
                  pl.BlockSpec((1, GDN_CONV_DIM), const),
                  pl.BlockSpec((8, LANES), const),
                  pl.BlockSpec((1, GDN_DV), const)],
        out_specs=[pl.BlockSpec((None, tb, GDN_VW), lambda bi, i: (bi, i, 0)),
                   pl.BlockSpec((None, CONV_W - 1, GDN_CONV_DIM), lambda bi, i: (bi, 0, 0)),
                   pl.BlockSpec((None, GDN_HEADS, GDN_DK, GDN_DV), lambda bi, i: (bi, 0, 0, 0))],
        out_shape=[jax.ShapeDtypeStruct((b, l, GDN_VW), MM_DTYPE),
                   jax.ShapeDtypeStruct((b, CONV_W - 1, GDN_CONV_DIM), f32),
                   jax.ShapeDtypeStruct((b, GDN_HEADS, GDN_DK, GDN_DV), f32)],
        scratch_shapes=[pltpu.VMEM((tb + 8, GDN_CONV_DIM), f32),
                        pltpu.VMEM((tb, GDN_CONV_DIM), f32),
                        pltpu.VMEM((tb, GDN_VW), f32),
                        pltpu.VMEM((tb, LANES), f32),
                        pltpu.VMEM((tb, LANES), f32),
                        pltpu.VMEM((GDN_HEADS, GDN_DK, GDN_DV), f32),
                        pltpu.VMEM((max(8, tb // CHUNK), LANES), f32),
                        pltpu.VMEM((tb, GDN_VW), f32),
                        pltpu.VMEM((tb, GDN_KW), MM_DTYPE),
                        pltpu.VMEM((tb, GDN_KW), MM_DTYPE),
                        pltpu.VMEM((tb, GDN_KW), MM_DTYPE),
                        pltpu.VMEM((GDN_HEADS, tb, CHUNK), MM_DTYPE)],
        compiler_params=pltpu.CompilerParams(dimension_semantics=("parallel", "arbitrary"),
                                             vmem_limit_bytes=VMEM_LIMIT),
        name="gdn_prompt",
    )(x, pre_g, w_in, w_in, cw, cb, hp, ng)


def _gla_prompt_kernel(x_ref, pg_ref, wq_ref, wk_ref, wv_ref, wg_ref, ws_ref, wlr_ref, blr_ref, ng_ref,
                       o_ref, s_out_ref, q_ref, k_ref, v_ref, gate_ref, la_scr, st_scr, *, tb):
    i = pl.program_id(1)

    @pl.when(i == 0)
    def _():
        st_scr[...] = jnp.zeros(st_scr.shape, f32)

    u = _normed(x_ref, pg_ref)
    q_ref[...] = _proj(u, wq_ref, 0, GLA_KW)
    k_ref[...] = _proj(u, wk_ref, 0, GLA_KW)
    v_ref[...] = _proj(u, wv_ref, 0, GLA_VW).astype(v_ref.dtype)
    gate_ref[...] = _proj(u, wg_ref, 0, GLA_VW)
    small = _proj(u, ws_ref, 0, LANES)
    z = _mm(small, wlr_ref[...]) + blr_ref[...]
    la_scr[...] = -_softplus(-z) * (1.0 / GLA_TAU)

    incl = _tril(CHUNK)
    lmat = incl.astype(f32)
    ng = ng_ref[...]

    def chunk_body(c, carry):
        r0 = pl.multiple_of(c * CHUNK, CHUNK)
        rows = pl.ds(r0, CHUNK)
        heads = range(GLA_HEADS)
        bcum = [_sel_mm(lmat, la_scr[rows, h * GLA_DK:(h + 1) * GLA_DK]) for h in heads]
        k = [k_ref[rows, h * GLA_DK:(h + 1) * GLA_DK] for h in heads]
        v = [v_ref[rows, h * GLA_DV:(h + 1) * GLA_DV].astype(MM_DTYPE) for h in heads]
        qe = [(q_ref[rows, h * GLA_DK:(h + 1) * GLA_DK] * (GLA_DK ** -0.5) * jnp.exp(bcum[h])).astype(MM_DTYPE)
              for h in heads]
        ke = [k[h] * jnp.exp(-bcum[h]) for h in heads]
        blast = [bcum[h][CHUNK - 1:CHUNK, :] for h in heads]
        kdec = [k[h] * jnp.exp(blast[h] - bcum[h]) for h in heads]
        st_old = [st_scr[h] for h in heads]
        p = [jnp.where(incl, _mm_nt(qe[h], ke[h]), 0.0) for h in heads]
        o_inter = [_mm_nt(qe[h], st_old[h]) for h in heads]
        upd = [_mm_tn(v[h], kdec[h]) for h in heads]
        o = [_mm(p[h], v[h]) + o_inter[h] for h in heads]
        for h in heads:
            v0 = h * GLA_DV
            st_scr[h] = jnp.exp(blast[h]) * st_old[h] + upd[h]
            gate = gate_ref[rows, v0:v0 + GLA_DV]
            o_ref[rows, v0:v0 + GLA_DV] = (_rms(o[h], GLA_DV) * ng * _silu(gate)).astype(o_ref.dtype)
        return carry

    lax.fori_loop(0, tb // CHUNK, chunk_body, 0)

    @pl.when(i == pl.num_programs(1) - 1)
    def _():
        for h in range(GLA_HEADS):
            s_out_ref[h] = st_scr[h].T


def _gla_prompt(x, pre_g, w_in, wlr, blr, ng, tb):
    b, l, d = x.shape
    kern = functools.partial(_gla_prompt_kernel, tb=tb)
    wcol = lambda blk: (lambda bi, i: (0, blk))
    const = lambda bi, i: (0, 0)
    return pl.pallas_call(
        kern,
        grid=(b, l // tb),
        in_specs=[pl.BlockSpec((None, tb, d), lambda bi, i: (bi, i, 0)),
                  pl.BlockSpec((1, d), const),
                  pl.BlockSpec((d, GLA_KW), wcol(E_OFF_LQ // GLA_KW)),
                  pl.BlockSpec((d, GLA_KW), wcol(E_OFF_LK // GLA_KW)),
                  pl.BlockSpec((d, GLA_VW), wcol(E_OFF_LV // GLA_VW)),
                  pl.BlockSpec((d, GLA_VW), wcol(E_OFF_LGATE // GLA_VW)),
                  pl.BlockSpec((d, LANES), wcol(E_OFF_SMALL // LANES)),
                  pl.BlockSpec((LANES, GLA_KW), const),
                  pl.BlockSpec((1, GLA_KW), const),
                  pl.BlockSpec((1, GLA_DV), const)],
        out_specs=[pl.BlockSpec((None, tb, GLA_VW), lambda bi, i: (bi, i, 0)),
                   pl.BlockSpec((None, GLA_HEADS, GLA_DK, GLA_DV), lambda bi, i: (bi, 0, 0, 0))],
        out_shape=[jax.ShapeDtypeStruct((b, l, GLA_VW), MM_DTYPE),
                   jax.ShapeDtypeStruct((b, GLA_HEADS, GLA_DK, GLA_DV), f32)],
        scratch_shapes=[pltpu.VMEM((tb, GLA_KW), f32),
                        pltpu.VMEM((tb, GLA_KW), f32),
                        pltpu.VMEM((tb, GLA_VW), MM_DTYPE),
                        pltpu.VMEM((tb, GLA_VW), f32),
                        pltpu.VMEM((tb, GLA_KW), f32),
                        pltpu.VMEM((GLA_HEADS, GLA_DV, GLA_DK), f32)],
        compiler_params=pltpu.CompilerParams(dimension_semantics=("parallel", "arbitrary"),
                                             vmem_limit_bytes=VMEM_LIMIT),
        name="gla_prompt",
    )(x, pre_g, w_in, w_in, w_in, w_in, w_in, wlr, blr, ng)


def _ssd_prompt_kernel(x_ref, pg_ref, wx_ref, wz_ref, wbc_ref, wdt_ref, cw_ref, cb_ref, hp_ref, dskip_ref, ng_ref,
                       expand_ref, y_ref, conv_out_ref, h_out_ref,
                       xp_scr, xbc_scr, z_ref, lam_scr, lamx_scr, xdt_scr, h_scr, *, tb):
    i = pl.program_id(1)
    first = i == 0
    conv_cols = _conv_block(xp_scr, cw_ref, cb_ref, tb, first)

    @pl.when(first)
    def _():
        h_scr[...] = jnp.zeros(h_scr.shape, f32)

    u = _normed(x_ref, pg_ref)
    for c0 in range(0, SSD_DI, 1024):
        xp_scr[8:8 + tb, c0:c0 + 1024] = _proj(u, wx_ref, c0, c0 + 1024)
        z_ref[:, c0:c0 + 1024] = _proj(u, wz_ref, c0, c0 + 1024)
    xp_scr[8:8 + tb, SSD_DI:SSD_CONV_DIM] = _proj(u, wbc_ref, 0, 2 * SSD_BC)
    dt_raw = _proj(u, wdt_ref, 0, LANES)

    for c0 in range(0, SSD_CONV_DIM, 512):
        xbc_scr[:, c0:c0 + 512] = conv_cols(c0, c0 + 512)
    dt = _softplus(dt_raw + hp_ref[0:1, :])
    tr = lax.broadcasted_iota(jnp.int32, (tb, tb), 0)
    tc = lax.broadcasted_iota(jnp.int32, (tb, tb), 1)
    lam_all = _sel_mm(jnp.logical_and(tr >= tc, tr // CHUNK == tc // CHUNK).astype(f32), dt * hp_ref[1:2, :])
    lam_scr[...] = lam_all
    lo_blk = lax.broadcasted_iota(jnp.int32, (tb, LANES), 1) < SSD_P
    for p in range(SSD_HEADS // 2):
        cols = slice(p * LANES, (p + 1) * LANES)
        k0 = 2 * p
        xdt_scr[:, cols] = xbc_scr[:, cols] * jnp.where(lo_blk, dt[:, k0:k0 + 1], dt[:, k0 + 1:k0 + 2])
    for c0 in range(0, SSD_DI, 512):
        lamx_scr[:, c0:c0 + 512] = _sel_rows(lam_all, expand_ref[:, c0:c0 + 512])

    lane = lax.broadcasted_iota(jnp.int32, (CHUNK, LANES), 1)
    lo = lane < SSD_P
    incl2 = lax.broadcasted_iota(jnp.int32, (CHUNK, LANES), 0) >= lane % CHUNK
    top = lax.broadcasted_iota(jnp.int32, (2 * SSD_P, SSD_N), 0) < SSD_P
    n_pairs = SSD_HEADS // 2
    ppg = SSD_K // 2
    groups = range(SSD_G)

    def chunk_body(c, carry):
        r0 = pl.multiple_of(c * CHUNK, CHUNK)
        rows = pl.ds(r0, CHUNK)
        lam = lam_scr[rows, :]
        lam_t = lam.T
        llast = lam[CHUNK - 1:CHUNK, :]
        bm = [xbc_scr[rows, SSD_DI + g * SSD_N:SSD_DI + (g + 1) * SSD_N].astype(MM_DTYPE) for g in groups]
        cm = [xbc_scr[rows, SSD_DI + SSD_BC + g * SSD_N:SSD_DI + SSD_BC + (g + 1) * SSD_N].astype(MM_DTYPE)
              for g in groups]
        h_old = [h_scr[g * SSD_GW:(g + 1) * SSD_GW, :] for g in groups]
        cb2 = [_mm_nt(cm[g], jnp.concatenate([bm[g], bm[g]], axis=0)) for g in groups]
        y_inter = [_mm_nt(cm[g], h_old[g]) for g in groups]
        xs, xdt, xdec, lcol = [], [], [], []
        for p in range(n_pairs):
            cols = slice(p * LANES, (p + 1) * LANES)
            xs.append(xbc_scr[rows, cols])
            lcol.append(lamx_scr[rows, cols])
            xdt.append(xdt_scr[rows, cols])
            xdec.append((xdt[p] * jnp.exp(lcol[p][CHUNK - 1:CHUNK, :] - lcol[p])).astype(MM_DTYPE))
        upd = [_mm_tn(jnp.concatenate(xdec[g * ppg:(g + 1) * ppg], axis=-1), bm[g]) for g in groups]
        y_intra = []
        for p in range(n_pairs):
            k0 = 2 * p
            lrow = jnp.concatenate([lam_t[k0:k0 + 1, :], lam_t[k0 + 1:k0 + 2, :]], axis=-1)
            m = cb2[p // ppg] * _masked_decay(lcol[p], lrow, incl2)
            blockdiag = jnp.concatenate([jnp.where(lo, xdt[p], 0.0), jnp.where(lo, 0.0, xdt[p])], axis=0)
            y_intra.append(_mm(m, blockdiag))
        for g in groups:
            ys = []
            for pp in range(ppg):
                p = g * ppg + pp
                k0 = 2 * p
                ys.append(y_intra[p] + jnp.exp(lcol[p]) * y_inter[g][:, pp * LANES:(pp + 1) * LANES]
                          + dskip_ref[:, p * LANES:(p + 1) * LANES] * xs[p])
                el = jnp.where(top, jnp.exp(llast[:, k0:k0 + 1]), jnp.exp(llast[:, k0 + 1:k0 + 2]))
                h_scr[p * LANES:(p + 1) * LANES, :] = (el * h_old[g][pp * LANES:(pp + 1) * LANES, :]
                                                        + upd[g][pp * LANES:(pp + 1) * LANES, :])
            yg = jnp.concatenate(ys, axis=-1) * _silu(z_ref[rows, g * SSD_GW:(g + 1) * SSD_GW])
            y_ref[rows, g * SSD_GW:(g + 1) * SSD_GW] = (
                _rms(yg, SSD_GW) * ng_ref[:, g * SSD_GW:(g + 1) * SSD_GW]).astype(y_ref.dtype)
        return carry

    lax.fori_loop(0, tb // CHUNK, chunk_body, 0)

    @pl.when(i == pl.num_programs(1) - 1)
    def _():
        conv_out_ref[...] = xp_scr[tb + 5:tb + 8, :]
        h_out_ref[...] = h_scr[...]


def _ssd_prompt(x, pre_g, w_in, cw, cb, hp, dskip, ng, expand, tb):
    b, l, d = x.shape
    kern = functools.partial(_ssd_prompt_kernel, tb=tb)
    wcol = lambda blk: (lambda bi, i: (0, blk))
    const = lambda bi, i: (0, 0)
    return pl.pallas_call(
        kern,
        grid=(b, l // tb),
        in_specs=[pl.BlockSpec((None, tb, d), lambda bi, i: (bi, i, 0)),
                  pl.BlockSpec((1, d), const),
                  pl.BlockSpec((d, SSD_DI), wcol(0)),
                  pl.BlockSpec((d, SSD_DI), wcol(O_OFF_Z // SSD_DI)),
                  pl.BlockSpec((d, 2 * SSD_BC), wcol(O_OFF_BC // (2 * SSD_BC))),
                  pl.BlockSpec((d, LANES), wcol(O_OFF_DT // LANES)),
                  pl.BlockSpec((CONV_W, SSD_CONV_DIM), const),
                  pl.BlockSpec((1, SSD_CONV_DIM), const),
                  pl.BlockSpec((8, LANES), const),
                  pl.BlockSpec((1, SSD_DI), const),
                  pl.BlockSpec((1, SSD_DI), const),
                  pl.BlockSpec((LANES, SSD_DI), const)],
        out_specs=[pl.BlockSpec((None, tb, SSD_DI), lambda bi, i: (bi, i, 0)),
                   pl.BlockSpec((None, CONV_W - 1, SSD_CONV_DIM), lambda bi, i: (bi, 0, 0)),
                   pl.BlockSpec((None, SSD_HEADS * SSD_P, SSD_N), lambda bi, i: (bi, 0, 0))],
        out_shape=[jax.ShapeDtypeStruct((b, l, SSD_DI), MM_DTYPE),
                   jax.ShapeDtypeStruct((b, CONV_W - 1, SSD_CONV_DIM), f32),
                   jax.ShapeDtypeStruct((b, SSD_HEADS * SSD_P, SSD_N), f32)],
        scratch_shapes=[pltpu.VMEM((tb + 8, SSD_CONV_DIM), f32),
                        pltpu.VMEM((tb, SSD_CONV_DIM), f32),
                        pltpu.VMEM((tb, SSD_DI), f32),
                        pltpu.VMEM((tb, LANES), f32),
                        pltpu.VMEM((tb, SSD_DI), f32),
                        pltpu.VMEM((tb, SSD_DI), f32),
                        pltpu.VMEM((SSD_HEADS * SSD_P, SSD_N), f32)],
        compiler_params=pltpu.CompilerParams(dimension_semantics=("parallel", "arbitrary"),
                                             vmem_limit_bytes=VMEM_LIMIT),
        name="ssd_prompt",
    )(x, pre_g, w_in, w_in, w_in, w_in, cw, cb, hp, dskip, ng, expand)


def _pad_rows(row, n):
    return jnp.concatenate([row, jnp.zeros((n - 1, row.shape[1]), row.dtype)], axis=0)


def _bcast_cols(row, width):
    hi, mid, lo = _split3(row)
    lhs = jnp.concatenate([hi, mid, lo, jnp.zeros((13, row.shape[1]), bf16)], axis=0)
    sel = (lax.broadcasted_iota(jnp.int32, (16, width), 0) < 3).astype(bf16)
    return lax.dot_general(lhs, sel, (((0,), (0,)), ((), ())), preferred_element_type=f32)


def _conv_step(buf_ref, x, cw_ref, cb_ref, new_buf_ref):
    cdim = x.shape[1]
    acc = cb_ref[...] + x * cw_ref[3:4, :]
    for s in range(CONV_W - 1):
        acc = acc + buf_ref[:, s * cdim:(s + 1) * cdim] * cw_ref[s:s + 1, :]
    new_buf_ref[:, 0:2 * cdim] = buf_ref[:, cdim:3 * cdim]
    new_buf_ref[:, 2 * cdim:3 * cdim] = x
    return _silu(acc)


def _gdn_decode_kernel(q_ref, k_ref, v_ref, gate_ref, small_ref, buf_ref, s_ref, cw_ref, cb_ref, hp_ref, ng_ref,
                       o_ref, new_buf_ref, s_out_ref, *, bb):
    x = jnp.concatenate([q_ref[...], k_ref[...], v_ref[...]], axis=-1)
    qkv = _conv_step(buf_ref, x, cw_ref, cb_ref, new_buf_ref)
    small = small_ref[...]
    beta = _sigmoid(small)
    eg = jnp.exp(hp_ref[0:1, :] * _softplus(small + hp_ref[1:2, :]))
    ng = ng_ref[...]
    gate = gate_ref[...]
    for h in range(GDN_HEADS):
        c0 = h * GDN_DK
        q = qkv[:, c0:c0 + GDN_DK]
        q = q * lax.rsqrt(jnp.sum(q * q, axis=-1, keepdims=True) + EPS) * (GDN_DK ** -0.5)
        k = qkv[:, GDN_KW + c0:GDN_KW + c0 + GDN_DK]
        k = k * lax.rsqrt(jnp.sum(k * k, axis=-1, keepdims=True) + EPS)
        v = qkv[:, 2 * GDN_KW + c0:2 * GDN_KW + c0 + GDN_DV]
        qk = jnp.sum(q * k, axis=-1, keepdims=True)
        b_h = beta[:, SMALL_BETA + h:SMALL_BETA + h + 1]
        eg_h = eg[:, SMALL_A + h:SMALL_A + h + 1]
        seqs = range(bb)
        s_old = [s_ref[b, h] for b in seqs]
        r = [_mm(jnp.concatenate([k[b:b + 1], q[b:b + 1], jnp.zeros((6, GDN_DK), f32)], axis=0), s_old[b])
             for b in seqs]
        u = [b_h[b:b + 1] * v[b:b + 1] - (b_h[b:b + 1] * eg_h[b:b + 1]) * r[b][0:1] for b in seqs]
        upd = [_mm_tn(_pad_rows(k[b:b + 1], 16), _pad_rows(u[b], 16)) for b in seqs]
        for b in seqs:
            s_out_ref[b, h] = eg_h[b:b + 1] * s_old[b] + upd[b]
        o = jnp.concatenate([eg_h[b:b + 1] * r[b][1:2] + qk[b:b + 1] * u[b] for b in seqs], axis=0)
        o_ref[:, c0:c0 + GDN_DV] = (_rms(o, GDN_DV) * ng * _silu(gate[:, c0:c0 + GDN_DV])).astype(o_ref.dtype)


def _gdn_decode(proj, buf, s, cw, cb, hp, ng, bb):
    b = proj.shape[0]
    wq = GDN_KW
    kern = functools.partial(_gdn_decode_kernel, bb=bb)
    col = lambda blk: (lambda i: (i, blk))
    const = lambda i: (0, 0)
    return pl.pallas_call(
        kern,
        grid=(b // bb,),
        in_specs=[pl.BlockSpec((bb, wq), col(0)),
                  pl.BlockSpec((bb, wq), col(1)),
                  pl.BlockSpec((bb, wq), col(2)),
                  pl.BlockSpec((bb, wq), col(3)),
                  pl.BlockSpec((bb, LANES), col(E_OFF_SMALL // LANES)),
                  pl.BlockSpec((bb, 3 * GDN_CONV_DIM), col(0)),
                  pl.BlockSpec((bb, GDN_HEADS, GDN_DK, GDN_DV), lambda i: (i, 0, 0, 0)),
                  pl.BlockSpec((CONV_W, GDN_CONV_DIM), const),
                  pl.BlockSpec((1, GDN_CONV_DIM), const),
                  pl.BlockSpec((8, LANES), const),
                  pl.BlockSpec((1, GDN_DV), const)],
        out_specs=[pl.BlockSpec((bb, GDN_VW), col(0)),
                   pl.BlockSpec((bb, 3 * GDN_CONV_DIM), col(0)),
                   pl.BlockSpec((bb, GDN_HEADS, GDN_DK, GDN_DV), lambda i: (i, 0, 0, 0))],
        out_shape=[jax.ShapeDtypeStruct((b, GDN_VW), MM_DTYPE),
                   jax.ShapeDtypeStruct((b, 3 * GDN_CONV_DIM), f32),
                   jax.ShapeDtypeStruct((b, GDN_HEADS, GDN_DK, GDN_DV), f32)],
        compiler_params=pltpu.CompilerParams(dimension_semantics=("parallel",), vmem_limit_bytes=VMEM_LIMIT),
        name="gdn_decode",
    )(proj, proj, proj, proj, proj, buf, s, cw, cb, hp, ng)


def _gla_decode_kernel(q_ref, k_ref, v_ref, gate_ref, small_ref, s_ref, wlr_ref, blr_ref, ng_ref,
                       o_ref, s_out_ref, *, bb):
    z = _mm(small_ref[...], wlr_ref[...]) + blr_ref[...]
    la = -_softplus(-z) * (1.0 / GLA_TAU)
    dec = jnp.exp(la)
    qe = q_ref[...] * (GLA_DK ** -0.5) * dec
    kx = k_ref[...]
    ke = kx * jnp.exp(-la)
    vx = v_ref[...]
    ng = ng_ref[...]
    gate = gate_ref[...]
    for h in range(GLA_HEADS):
        k0 = h * GLA_DK
        v0 = h * GLA_DV
        p = jnp.sum(qe[:, k0:k0 + GLA_DK] * ke[:, k0:k0 + GLA_DK], axis=-1, keepdims=True)
        outs = []
        for b in range(bb):
            s_old = s_ref[b, h]
            vrow = vx[b:b + 1, v0:v0 + GLA_DV]
            outs.append(_mm(_pad_rows(qe[b:b + 1, k0:k0 + GLA_DK], 8), s_old)[0:1] + p[b:b + 1] * vrow)
            s_out_ref[b, h] = (_bcast_cols(dec[b:b + 1, k0:k0 + GLA_DK], GLA_DV) * s_old
                               + _mm_tn(_pad_rows(kx[b:b + 1, k0:k0 + GLA_DK], 16), _pad_rows(vrow, 16)))
        o = jnp.concatenate(outs, axis=0)
        o_ref[:, v0:v0 + GLA_DV] = (_rms(o, GLA_DV) * ng * _silu(gate[:, v0:v0 + GLA_DV])).astype(o_ref.dtype)


def _gla_decode(proj, s, wlr, blr, ng, bb):
    b = proj.shape[0]
    kern = functools.partial(_gla_decode_kernel, bb=bb)
    col = lambda blk: (lambda i: (i, blk))
    const = lambda i: (0, 0)
    return pl.pallas_call(
        kern,
        grid=(b // bb,),
        in_specs=[pl.BlockSpec((bb, GLA_KW), col(E_OFF_LQ // GLA_KW)),
                  pl.BlockSpec((bb, GLA_KW), col(E_OFF_LK // GLA_KW)),
                  pl.BlockSpec((bb, GLA_VW), col(E_OFF_LV // GLA_VW)),
                  pl.BlockSpec((bb, GLA_VW), col(E_OFF_LGATE // GLA_VW)),
                  pl.BlockSpec((bb, LANES), col(E_OFF_SMALL // LANES)),
                  pl.BlockSpec((bb, GLA_HEADS, GLA_DK, GLA_DV), lambda i: (i, 0, 0, 0)),
                  pl.BlockSpec((LANES, GLA_KW), const),
                  pl.BlockSpec((1, GLA_KW), const),
                  pl.BlockSpec((1, GLA_DV), const)],
        out_specs=[pl.BlockSpec((bb, GLA_VW), col(0)),
                   pl.BlockSpec((bb, GLA_HEADS, GLA_DK, GLA_DV), lambda i: (i, 0, 0, 0))],
        out_shape=[jax.ShapeDtypeStruct((b, GLA_VW), MM_DTYPE),
                   jax.ShapeDtypeStruct((b, GLA_HEADS, GLA_DK, GLA_DV), f32)],
        compiler_params=pltpu.CompilerParams(dimension_semantics=("parallel",), vmem_limit_bytes=VMEM_LIMIT),
        name="gla_decode",
    )(proj, proj, proj, proj, proj, s, wlr, blr, ng)


def _ssd_decode_kernel(x_ref, z_ref, bc_ref, dt_ref, buf_ref, h_ref, cw_ref, cb_ref, hp_ref, dskip_ref, ng_ref,
                       expand_ref, y_ref, new_buf_ref, h_out_ref, *, bb):
    xin = jnp.concatenate([x_ref[...], bc_ref[...]], axis=-1)
    xbc = _conv_step(buf_ref, xin, cw_ref, cb_ref, new_buf_ref)
    xs = xbc[:, :SSD_DI]
    dt = _softplus(dt_ref[...] + hp_ref[0:1, :])
    el = jnp.exp(dt * hp_ref[1:2, :])
    expand = expand_ref[...]
    dt_x = _sel_rows(dt, expand)
    el_x = _sel_rows(el, expand)
    xdt = xs * dt_x
    z = z_ref[...]
    for g in range(SSD_G):
        gs = slice(g * SSD_GW, (g + 1) * SSD_GW)
        bm = xbc[:, SSD_DI + g * SSD_N:SSD_DI + (g + 1) * SSD_N]
        cm = xbc[:, SSD_DI + SSD_BC + g * SSD_N:SSD_DI + SSD_BC + (g + 1) * SSD_N]
        cbs = jnp.sum(cm * bm, axis=-1, keepdims=True)
        outs = []
        for b in range(bb):
            hg = h_ref[b, gs, :]
            y_inter = _mm_nt(_pad_rows(cm[b:b + 1], 8), hg)[0:1]
            outs.append(el_x[b:b + 1, gs] * y_inter)
            h_out_ref[b, gs, :] = (_bcast_cols(el_x[b:b + 1, gs], SSD_N) * hg
                                   + _mm_tn(_pad_rows(xdt[b:b + 1, gs], 16), _pad_rows(bm[b:b + 1], 16)))
        y = jnp.concatenate(outs, axis=0) + cbs * xdt[:, gs] + dskip_ref[:, gs] * xs[:, gs]
        yg = y * _silu(z[:, gs])
        y_ref[:, gs] = (_rms(yg, SSD_GW) * ng_ref[:, gs]).astype(y_ref.dtype)


def _ssd_decode(proj, buf, hst, cw, cb, hp, dskip, ng, expand, bb):
    b = proj.shape[0]
    kern = functools.partial(_ssd_decode_kernel, bb=bb)
    col = lambda blk: (lambda i: (i, blk))
    const = lambda i: (0, 0)
    return pl.pallas_call(
        kern,
        grid=(b // bb,),
        in_specs=[pl.BlockSpec((bb, SSD_DI), col(0)),
                  pl.BlockSpec((bb, SSD_DI), col(O_OFF_Z // SSD_DI)),
                  pl.BlockSpec((bb, 2 * SSD_BC), col(O_OFF_BC // (2 * SSD_BC))),
                  pl.BlockSpec((bb, LANES), col(O_OFF_DT // LANES)),
                  pl.BlockSpec((bb, 3 * SSD_CONV_DIM), col(0)),
                  pl.BlockSpec((bb, SSD_HEADS * SSD_P, SSD_N), lambda i: (i, 0, 0)),
                  pl.BlockSpec((CONV_W, SSD_CONV_DIM), const),
                  pl.BlockSpec((1, SSD_CONV_DIM), const),
                  pl.BlockSpec((8, LANES), const),
                  pl.BlockSpec((1, SSD_DI), const),
                  pl.BlockSpec((1, SSD_DI), const),
                  pl.BlockSpec((LANES, SSD_DI), const)],
        out_specs=[pl.BlockSpec((bb, SSD_DI), col(0)),
                   pl.BlockSpec((bb, 3 * SSD_CONV_DIM), col(0)),
                   pl.BlockSpec((bb, SSD_HEADS * SSD_P, SSD_N), lambda i: (i, 0, 0))],
        out_shape=[jax.ShapeDtypeStruct((b, SSD_DI), MM_DTYPE),
                   jax.ShapeDtypeStruct((b, 3 * SSD_CONV_DIM), f32),
                   jax.ShapeDtypeStruct((b, SSD_HEADS * SSD_P, SSD_N), f32)],
        compiler_params=pltpu.CompilerParams(dimension_semantics=("parallel",), vmem_limit_bytes=VMEM_LIMIT),
        name="ssd_decode",
    )(proj, proj, proj, proj, buf, hst, cw, cb, hp, dskip, ng, expand)


def _pack_even_w_in(w):
    w = w.astype(MM_DTYPE)
    d = w.shape[0]
    o_beta = 4 * GDN_KW
    o_lq = o_beta + 2 * GDN_HEADS
    o_lr = o_lq + 2 * GLA_KW + 2 * GLA_VW
    small = jnp.concatenate([w[:, o_beta:o_lq], w[:, o_lr:o_lr + GLA_RANK],
                             jnp.zeros((d, LANES - 2 * GDN_HEADS - GLA_RANK), w.dtype)], axis=1)
    packed = jnp.concatenate([w[:, :o_beta], w[:, o_lq:o_lr], small], axis=1)
    return jnp.pad(packed, ((0, 0), (0, E_N - packed.shape[1])))


def _pack_odd_w_in(w):
    w = w.astype(MM_DTYPE)
    z = w[:, :SSD_DI]
    xbc = w[:, SSD_DI:SSD_DI + SSD_CONV_DIM]
    dt = w[:, SSD_DI + SSD_CONV_DIM:]
    packed = jnp.concatenate([xbc[:, :SSD_DI], z, xbc[:, SSD_DI:], dt], axis=1)
    return jnp.pad(packed, ((0, 0), (0, O_N - packed.shape[1])))


def _lane_row(vals, offset):
    return jnp.zeros((LANES,), f32).at[offset:offset + vals.shape[0]].set(vals.astype(f32))


def kernel(x_prompt, x_sample, state_gdn_conv, state_gdn, state_gla, state_ssd_conv, state_ssd, e_pre_g, e_post_g, e_w_in, e_conv_w, e_conv_b, gdn_a_log, gdn_dt_bias, gdn_norm_g, gla_w_lr, gla_b_lr, gla_norm_g, e_w_out, o_pre_g, o_post_g, o_w_in, ssd_conv_w, ssd_conv_b, ssd_dt_bias, ssd_a_log, ssd_d, ssd_norm_g, o_w_out):
    bp, lp, d = x_prompt.shape
    bs = x_sample.shape[0]
    tp = bp * lp
    tb = min(lp, 256)
    bb = min(bs, 8)

    w_in0 = _pack_even_w_in(e_w_in[0])
    w_out0 = e_w_out[0].astype(MM_DTYPE)
    pre0 = e_pre_g[0].reshape(1, d)
    post0 = e_post_g[0].reshape(1, d)
    cw0 = e_conv_w[0]
    cb0 = e_conv_b[0].reshape(1, GDN_CONV_DIM)
    hp0 = jnp.zeros((8, LANES), f32)
    hp0 = hp0.at[0].set(_lane_row(-jnp.exp(gdn_a_log[0].astype(f32)), SMALL_A))
    hp0 = hp0.at[1].set(_lane_row(gdn_dt_bias[0], SMALL_A))
    ng_gdn = gdn_norm_g[0].reshape(1, GDN_DV)
    wlr = jnp.zeros((LANES, GLA_KW), f32).at[SMALL_LR:SMALL_LR + GLA_RANK].set(gla_w_lr[0]).astype(MM_DTYPE)
    blr = gla_b_lr[0].reshape(1, GLA_KW)
    ng_gla = gla_norm_g[0].reshape(1, GLA_DV)

    hp_flat = x_prompt.reshape(tp, d)
    hs_flat = x_sample.reshape(bs, d)

    o1_p, gdn_conv_p, gdn_p = _gdn_prompt(x_prompt, pre0, w_in0, cw0, cb0, hp0, ng_gdn, tb)
    o2_p, gla_p = _gla_prompt(x_prompt, pre0, w_in0, wlr, blr, ng_gla, tb)
    hp_flat = _out_proj(o1_p.reshape(tp, GDN_VW), o2_p.reshape(tp, GLA_VW), w_out0, post0, hp_flat)

    proj_s = _norm_proj(hs_flat, pre0, w_in0, 1280)
    o1_s, gdn_conv_s, gdn_s = _gdn_decode(proj_s, state_gdn_conv[0].reshape(bs, 3 * GDN_CONV_DIM), state_gdn[0],
                                          cw0, cb0, hp0, ng_gdn, bb)
    o2_s, gla_s = _gla_decode(proj_s, state_gla[0], wlr, blr, ng_gla, bb)
    hs_flat = _out_proj(o1_s, o2_s, w_out0, post0, hs_flat)

    w_in1 = _pack_odd_w_in(o_w_in[0])
    w_out1 = o_w_out[0].astype(MM_DTYPE)
    pre1 = o_pre_g[0].reshape(1, d)
    post1 = o_post_g[0].reshape(1, d)
    cw1 = ssd_conv_w[0]
    cb1 = ssd_conv_b[0].reshape(1, SSD_CONV_DIM)
    hp1 = jnp.zeros((8, LANES), f32)
    hp1 = hp1.at[0].set(_lane_row(ssd_dt_bias[0], 0))
    hp1 = hp1.at[1].set(_lane_row(-jnp.exp(ssd_a_log[0].astype(f32)), 0))
    dskip = jnp.repeat(ssd_d[0].astype(f32), SSD_P).reshape(1, SSD_DI)
    ng_ssd = ssd_norm_g[0].reshape(1, SSD_DI)
    expand = (lax.broadcasted_iota(jnp.int32, (LANES, SSD_DI), 0)
              == lax.broadcasted_iota(jnp.int32, (LANES, SSD_DI), 1) // SSD_P).astype(MM_DTYPE)

    y_p, ssd_conv_p, ssd_p = _ssd_prompt(hp_flat.reshape(bp, lp, d), pre1, w_in1, cw1, cb1, hp1, dskip, ng_ssd,
                                         expand, tb)
    y_p = y_p.reshape(tp, SSD_DI)
    hp_flat = _out_proj(y_p, y_p, w_out1, post1, hp_flat)

    proj_s = _norm_proj(hs_flat, pre1, w_in1, 1792)
    y_s, ssd_conv_s, ssd_s = _ssd_decode(proj_s, state_ssd_conv[0].reshape(bs, 3 * SSD_CONV_DIM),
                                         state_ssd[0].reshape(bs, SSD_HEADS * SSD_P, SSD_N),
                                         cw1, cb1, hp1, dskip, ng_ssd, expand, bb)
    hs_flat = _out_proj(y_s, y_s, w_out1, post1, hs_flat)

    return (hp_flat.reshape(bp, lp, d), hs_flat.reshape(bs, 1, d),
            gdn_conv_p[None], gdn_p[None], gla_p[None],
            ssd_conv_p[None], ssd_p.reshape(bp, SSD_HEADS, SSD_P, SSD_N)[None],
            gdn_conv_s.reshape(bs, CONV_W - 1, GDN_CONV_DIM)[None], gdn_s[None], gla_s[None],
            ssd_conv_s.reshape(bs, CONV_W - 1, SSD_CONV_DIM)[None],
            ssd_s.reshape(bs, SSD_HEADS, SSD_P, SSD_N)[None])
```

```python
import functools

import jax
import jax.numpy as jnp
from jax import lax
from jax.experimental import pallas as pl
from jax.experimental.pallas import tpu as pltpu

f32 = jnp.float32
bf16 = jnp.bfloat16
MM_DTYPE = jnp.bfloat16

D_MODEL = 1024
CONV_W = 4
EPS = 1e-6
LANES = 128

GDN_HEADS = 8
GDN_DK = 128
GDN_DV = 128
GDN_KW = GDN_HEADS * GDN_DK
GDN_VW = GDN_HEADS * GDN_DV
GDN_CONV_DIM = 2 * GDN_KW + GDN_VW
GLA_HEADS = 4
GLA_DK = 128
GLA_DV = 256
GLA_KW = GLA_HEADS * GLA_DK
GLA_VW = GLA_HEADS * GLA_DV
GLA_RANK = 16
GLA_TAU = 16.0
SSD_DI = 2 * D_MODEL
SSD_P = 64
SSD_HEADS = SSD_DI // SSD_P
SSD_N = 128
SSD_G = 4
SSD_K = SSD_HEADS // SSD_G
SSD_GW = SSD_DI // SSD_G
SSD_BC = SSD_G * SSD_N
SSD_CONV_DIM = SSD_DI + 2 * SSD_BC

E_OFF_GATE = 3 * GDN_KW
E_OFF_LQ = 4 * GDN_KW
E_OFF_LK = E_OFF_LQ + GLA_KW
E_OFF_LV = E_OFF_LK + GLA_KW
E_OFF_LGATE = E_OFF_LV + GLA_VW
E_OFF_SMALL = E_OFF_LGATE + GLA_VW
E_N = 7680
SMALL_BETA = 0
SMALL_A = GDN_HEADS
SMALL_LR = 2 * GDN_HEADS
O_OFF_Z = SSD_DI
O_OFF_BC = 2 * SSD_DI
O_OFF_DT = 2 * SSD_DI + 2 * SSD_BC
O_N = 5376

CHUNK = 64
GDN_PREP_CHUNKS = 2
assert SSD_P == CHUNK and 2 * SSD_P == LANES
VMEM_LIMIT = 56 * 1024 * 1024


def _mm(a, b):
    return jnp.dot(a.astype(MM_DTYPE), b.astype(MM_DTYPE), preferred_element_type=f32)


def _mm_nt(a, b):
    return lax.dot_general(a.astype(MM_DTYPE), b.astype(MM_DTYPE), (((1,), (1,)), ((), ())),
                           preferred_element_type=f32)


def _mm_tn(a, b):
    return lax.dot_general(a.astype(MM_DTYPE), b.astype(MM_DTYPE), (((0,), (0,)), ((), ())),
                           preferred_element_type=f32)


def _split3(x):
    hi = x.astype(bf16)
    r = x - hi.astype(f32)
    mid = r.astype(bf16)
    lo = (r - mid.astype(f32)).astype(bf16)
    return hi, mid, lo


def _sel_mm(sel, x):
    hi, mid, lo = _split3(x)
    s = sel.astype(bf16)
    return (jnp.dot(s, hi, preferred_element_type=f32) + jnp.dot(s, mid, preferred_element_type=f32)
            + jnp.dot(s, lo, preferred_element_type=f32))


def _sel_rows(x, sel):
    hi, mid, lo = _split3(x)
    s = sel.astype(bf16)
    return (jnp.dot(hi, s, preferred_element_type=f32) + jnp.dot(mid, s, preferred_element_type=f32)
            + jnp.dot(lo, s, preferred_element_type=f32))


def _sigmoid(x):
    return 1.0 / (1.0 + jnp.exp(-x))


def _silu(x):
    return x * _sigmoid(x)


def _softplus(x):
    return jnp.maximum(x, 0.0) + jnp.log1p(jnp.exp(-jnp.abs(x)))


def _tril(n, strict=False):
    r = lax.broadcasted_iota(jnp.int32, (n, n), 0)
    c = lax.broadcasted_iota(jnp.int32, (n, n), 1)
    return (r > c) if strict else (r >= c)


def _masked_decay(col, row, incl):
    d = col - row
    return jnp.where(incl, jnp.exp(jnp.where(incl, d, 0.0)), 0.0)


def _run_interleaved(main, side, ratio):
    gens = [g for g in (main, side) if g is not None]
    steps = {id(main): ratio}
    while gens:
        for g in list(gens):
            for _ in range(steps.get(id(g), 1)):
                try:
                    next(g)
                except StopIteration:
                    gens.remove(g)
                    break


def _rms(x, width):
    return x * lax.rsqrt(jnp.sum(x * x, axis=-1, keepdims=True) * (1.0 / width) + EPS)


def _norm_proj_kernel(x_ref, g_ref, w_ref, o_ref, u_scr):
    @pl.when(pl.program_id(1) == 0)
    def _():
        u_scr[...] = (_rms(x_ref[...], D_MODEL) * g_ref[...]).astype(u_scr.dtype)

    o_ref[...] = jnp.dot(u_scr[...], w_ref[...], preferred_element_type=f32)


def _norm_proj(x, g, w, tn):
    t, d = x.shape
    n = w.shape[1]
    tm = min(t, 1024)
    return pl.pallas_call(
        _norm_proj_kernel,
        grid=(t // tm, n // tn),
        in_specs=[pl.BlockSpec((tm, d), lambda i, j: (i, 0)),
                  pl.BlockSpec((1, d), lambda i, j: (0, 0)),
                  pl.BlockSpec((d, tn), lambda i, j: (0, j))],
        out_specs=pl.BlockSpec((tm, tn), lambda i, j: (i, j)),
        out_shape=jax.ShapeDtypeStruct((t, n), f32),
        scratch_shapes=[pltpu.VMEM((tm, d), MM_DTYPE)],
        compiler_params=pltpu.CompilerParams(dimension_semantics=("parallel", "arbitrary"),
                                             vmem_limit_bytes=VMEM_LIMIT),
        name="norm_proj",
    )(x, g, w)


def _out_proj_kernel(oa_ref, ob_ref, wa_ref, wb_ref, g_ref, h_ref, y_ref):
    p = (jnp.dot(oa_ref[...], wa_ref[...], preferred_element_type=f32)
         + jnp.dot(ob_ref[...], wb_ref[...], preferred_element_type=f32))
    y_ref[...] = h_ref[...] + _rms(p, D_MODEL) * g_ref[...]


def _out_proj(oa, ob, w, g, h):
    t = oa.shape[0]
    e = w.shape[0] // 2
    assert oa.shape[1] == ob.shape[1] and oa.shape[1] in (e, 2 * e)
    col_b = 1 if ob.shape[1] == 2 * e else 0
    d = w.shape[1]
    tm = min(t, 512)
    return pl.pallas_call(
        _out_proj_kernel,
        grid=(t // tm,),
        in_specs=[pl.BlockSpec((tm, e), lambda i: (i, 0)),
                  pl.BlockSpec((tm, e), lambda i: (i, col_b)),
                  pl.BlockSpec((e, d), lambda i: (0, 0)),
                  pl.BlockSpec((e, d), lambda i: (1, 0)),
                  pl.BlockSpec((1, d), lambda i: (0, 0)),
                  pl.BlockSpec((tm, d), lambda i: (i, 0))],
        out_specs=pl.BlockSpec((tm, d), lambda i: (i, 0)),
        out_shape=jax.ShapeDtypeStruct((t, d), f32),
        compiler_params=pltpu.CompilerParams(dimension_semantics=("parallel",),
                                             vmem_limit_bytes=VMEM_LIMIT),
        name="out_proj",
    )(oa, ob, w, w, g, h)


def _normed(x_ref, g_ref):
    return (_rms(x_ref[...], D_MODEL) * g_ref[...]).astype(MM_DTYPE)


def _proj(u, w_ref, c0, c1):
    return jnp.dot(u, w_ref[:, c0:c1], preferred_element_type=f32)


def _conv_block(xp_scr, cw_ref, cb_ref, tb, first):
    @pl.when(first)
    def _():
        xp_scr[0:8, :] = jnp.zeros((8, xp_scr.shape[1]), f32)

    @pl.when(jnp.logical_not(first))
    def _():
        xp_scr[5:8, :] = xp_scr[tb + 5:tb + 8, :]

    def conv_cols(c0, c1, r0=0, nr=tb):
        acc = cb_ref[:, c0:c1] + xp_scr[8 + r0:8 + r0 + nr, c0:c1] * cw_ref[3:4, c0:c1]
        for s in range(CONV_W - 1):
            acc = acc + xp_scr[5 + s + r0:5 + s + r0 + nr, c0:c1] * cw_ref[s:s + 1, c0:c1]
        return _silu(acc)

    return conv_cols


def _gdn_prompt_kernel(x_ref, pg_ref, wm_ref, ws_ref, cw_ref, cb_ref, hp_ref, ng_ref,
                       o_ref, conv_out_ref, s_out_ref,
                       xp_scr, qkv_scr, gate_scr, beta_scr, lg_scr, s_scr, gl_scr, up_scr, w_scr, qg_scr, kd_scr,
                       p_scr, *, tb):
    i = pl.program_id(1)
    first = i == 0
    conv_cols = _conv_block(xp_scr, cw_ref, cb_ref, tb, first)

    @pl.when(first)
    def _():
        s_scr[...] = jnp.zeros(s_scr.shape, f32)

    u = _normed(x_ref, pg_ref)
    for c0 in range(0, GDN_CONV_DIM, GDN_KW):
        xp_scr[8:8 + tb, c0:c0 + GDN_KW] = _proj(u, wm_ref, c0, c0 + GDN_KW)
    gate_scr[...] = _proj(u, wm_ref, E_OFF_GATE, E_OFF_GATE + GDN_VW)
    small = _proj(u, ws_ref, 0, LANES)
    beta_scr[...] = _sigmoid(small)
    lg_scr[...] = hp_ref[0:1, :] * _softplus(small + hp_ref[1:2, :])

    span = GDN_PREP_CHUNKS * CHUNK

    def conv_norm(it):
        r0 = it * span
        rr = slice(r0, r0 + span)
        for h in range(GDN_HEADS):
            c0 = h * GDN_DK
            q = conv_cols(c0, c0 + GDN_DK, r0, span)
            q = q * lax.rsqrt(jnp.sum(q * q, axis=-1, keepdims=True) + EPS) * (GDN_DK ** -0.5)
            qkv_scr[rr, c0:c0 + GDN_DK] = q
            k = conv_cols(GDN_KW + c0, GDN_KW + c0 + GDN_DK, r0, span)
            k = k * lax.rsqrt(jnp.sum(k * k, axis=-1, keepdims=True) + EPS)
            qkv_scr[rr, GDN_KW + c0:GDN_KW + c0 + GDN_DK] = k
            qkv_scr[rr, 2 * GDN_KW + c0:2 * GDN_KW + c0 + GDN_DV] = conv_cols(
                2 * GDN_KW + c0, 2 * GDN_KW + c0 + GDN_DV, r0, span)
            yield

    incl = _tril(CHUNK)
    strict = _tril(CHUNK, strict=True)
    eye = jnp.logical_and(incl, jnp.logical_not(strict)).astype(f32)
    sr = lax.broadcasted_iota(jnp.int32, (span, span), 0)
    sc = lax.broadcasted_iota(jnp.int32, (span, span), 1)
    lmat = jnp.logical_and(sr >= sc, sr // CHUNK == sc // CHUNK).astype(f32)
    ng = ng_ref[...]
    heads = range(GDN_HEADS)
    probs = [(ci, h) for ci in range(GDN_PREP_CHUNKS) for h in heads]
    n = range(len(probs))

    def prep(c):
        r0 = c * span
        gam_all = _sel_mm(lmat, lg_scr[pl.ds(r0, span), :])
        gam_t = gam_all.T
        beta_all = beta_scr[pl.ds(r0, span), :]
        rows, gcol, grow, glast, bcol = [], [], [], [], []
        for ci, h in probs:
            lo, hi = ci * CHUNK, (ci + 1) * CHUNK
            rows.append(pl.ds(r0 + lo, CHUNK))
            gcol.append(gam_all[lo:hi, SMALL_A + h:SMALL_A + h + 1])
            grow.append(gam_t[SMALL_A + h:SMALL_A + h + 1, lo:hi])
            glast.append(gam_all[hi - 1:hi, SMALL_A + h:SMALL_A + h + 1])
            bcol.append(beta_all[lo:hi, SMALL_BETA + h:SMALL_BETA + h + 1])
        for ci in range(GDN_PREP_CHUNKS):
            gl_scr[pl.ds(c * GDN_PREP_CHUNKS + ci, 1), :] = jnp.exp(gam_all[(ci + 1) * CHUNK - 1:(ci + 1) * CHUNK, :])
        q = [qkv_scr[rows[j], h * GDN_DK:(h + 1) * GDN_DK] for j, (_, h) in enumerate(probs)]
        k = [qkv_scr[rows[j], GDN_KW + h * GDN_DK:GDN_KW + (h + 1) * GDN_DK] for j, (_, h) in enumerate(probs)]
        v = [qkv_scr[rows[j], 2 * GDN_KW + h * GDN_DV:2 * GDN_KW + (h + 1) * GDN_DV] for j, (_, h) in enumerate(probs)]
        kb = [k[j].astype(MM_DTYPE) for j in n]
        kk = [_mm_nt(kb[j], kb[j]) for j in n]
        yield
        qk = [_mm_nt(q[j], kb[j]) for j in n]
        yield
        decay = [_masked_decay(gcol[j], grow[j], incl) for j in n]
        eg = [jnp.exp(gcol[j]) for j in n]
        for j, (_, h) in enumerate(probs):
            c0 = h * GDN_DK
            qg_scr[rows[j], c0:c0 + GDN_DK] = (eg[j] * q[j]).astype(MM_DTYPE)
            kd_scr[rows[j], c0:c0 + GDN_DK] = (jnp.exp(glast[j] - gcol[j]) * k[j]).astype(MM_DTYPE)
            p_scr[h, rows[j], :] = (qk[j] * decay[j]).astype(MM_DTYPE)
        nmat = [jnp.where(strict, bcol[j] * (kk[j] * decay[j]), 0.0) for j in n]
        tinv = [eye - nmat[j] for j in n]
        npow = nmat
        for _ in range(5):
            npow = [_mm(npow[j], npow[j]) for j in n]
            yield
            tinv = [tinv[j] + _mm(tinv[j], npow[j]) for j in n]
            yield
        x = [_mm(tinv[j], jnp.concatenate([bcol[j] * v[j], (bcol[j] * eg[j]) * k[j]], axis=-1)) for j in n]
        for j, (_, h) in enumerate(probs):
            c0 = h * GDN_DK
            up_scr[rows[j], c0:c0 + GDN_DV] = x[j][:, :GDN_DV]
            w_scr[rows[j], c0:c0 + GDN_DK] = x[j][:, GDN_DV:].astype(MM_DTYPE)

    state = [s_scr[h] for h in heads]

    def scan(c):
        rows = pl.ds(c * CHUNK, CHUNK)
        gl = gl_scr[c:c + 1, :]
        r = [_mm(jnp.concatenate([w_scr[rows, h * GDN_DK:(h + 1) * GDN_DK],
                                  qg_scr[rows, h * GDN_DK:(h + 1) * GDN_DK]], axis=0), state[h]) for h in heads]
        yield
        us = [(up_scr[rows, h * GDN_DV:(h + 1) * GDN_DV] - r[h][:CHUNK]).astype(MM_DTYPE) for h in heads]
        o = [r[h][CHUNK:] + _mm(p_scr[h, rows, :], us[h]) for h in heads]
        yield
        upd = [_mm_tn(kd_scr[rows, h * GDN_DK:(h + 1) * GDN_DK], us[h]) for h in heads]
        yield
        for h in heads:
            c0 = h * GDN_DV
            state[h] = gl[:, SMALL_A + h:SMALL_A + h + 1] * state[h] + upd[h]
            gate = gate_scr[rows, c0:c0 + GDN_DV]
            o_ref[rows, c0:c0 + GDN_DV] = (_rms(o[h], GDN_DV) * ng * _silu(gate)).astype(o_ref.dtype)
        yield

    def scans(chunks):
        for c in chunks:
            yield from scan(c)

    def side_work(it):
        if it + 1 < n_prep:
            yield from conv_norm(it + 1)
        if it >= 1:
            yield from scans(range((it - 1) * GDN_PREP_CHUNKS, it * GDN_PREP_CHUNKS))

    n_prep = tb // span
    _run_interleaved(conv_norm(0), None, 1)
    for it in range(n_prep):
        _run_interleaved(prep(it), side_work(it), 1)
    _run_interleaved(scans(range((n_prep - 1) * GDN_PREP_CHUNKS, n_prep * GDN_PREP_CHUNKS)), None, 1)
    for h in heads:
        s_scr[h] = state[h]

    @pl.when(i == pl.num_programs(1) - 1)
    def _():
        conv_out_ref[...] = xp_scr[tb + 5:tb + 8, :]
        s_out_ref[...] = s_scr[...]


def _gdn_prompt(x, pre_g, w_in, cw, cb, hp, ng, tb):
    b, l, d = x.shape
    assert tb % (GDN_PREP_CHUNKS * CHUNK) == 0 and l % tb == 0
    kern = functools.partial(_gdn_prompt_kernel, tb=tb)
    const = lambda bi, i: (0, 0)
    return pl.pallas_call(
        kern,
        grid=(b, l // tb),
        in_specs=[pl.BlockSpec((None, tb, d), lambda bi, i: (bi, i, 0)),
                  pl.BlockSpec((1, d), const),
                  pl.BlockSpec((d, E_OFF_LQ), const),
                  pl.BlockSpec((d, LANES), lambda bi, i: (0, E_OFF_SMALL // LANES)),
                  pl.BlockSpec((CONV_W, GDN_CONV_DIM), const),
                  pl.BlockSpec((1, GDN_CONV_DIM), const),
                  pl.BlockSpec((8, LANES), const),
                  pl.BlockSpec((1, GDN_DV), const)],
        out_specs=[pl.BlockSpec((None, tb, GDN_VW), lambda bi, i: (bi, i, 0)),
                   pl.BlockSpec((None, CONV_W - 1, GDN_CONV_DIM), lambda bi, i: (bi, 0, 0)),
                   pl.BlockSpec((None, GDN_HEADS, GDN_DK, GDN_DV), lambda bi, i: (bi, 0, 0, 0))],
        out_shape=[jax.ShapeDtypeStruct((b, l, GDN_VW), MM_DTYPE),
                   jax.ShapeDtypeStruct((b, CONV_W - 1, GDN_CONV_DIM), f32),
                   jax.ShapeDtypeStruct((b, GDN_HEADS, GDN_DK, GDN_DV), f32)],
        scratch_shapes=[pltpu.VMEM((tb + 8, GDN_CONV_DIM), f32),
                        pltpu.VMEM((tb, GDN_CONV_DIM), f32),
                        pltpu.VMEM((tb, GDN_VW), f32),
                        pltpu.VMEM((tb, LANES), f32),
                        pltpu.VMEM((tb, LANES), f32),
                        pltpu.VMEM((GDN_HEADS, GDN_DK, GDN_DV), f32),
                        pltpu.VMEM((max(8, tb // CHUNK), LANES), f32),
                        pltpu.VMEM((tb, GDN_VW), f32),
                        pltpu.VMEM((tb, GDN_KW), MM_DTYPE),
                        pltpu.VMEM((tb, GDN_KW), MM_DTYPE),
                        pltpu.VMEM((tb, GDN_KW), MM_DTYPE),
                        pltpu.VMEM((GDN_HEADS, tb, CHUNK), MM_DTYPE)],
        compiler_params=pltpu.CompilerParams(dimension_semantics=("parallel", "arbitrary"),
                                             vmem_limit_bytes=VMEM_LIMIT),
        name="gdn_prompt",
    )(x, pre_g, w_in, w_in, cw, cb, hp, ng)


def _gla_prompt_kernel(x_ref, pg_ref, wq_ref, wk_ref, wv_ref, wg_ref, ws_ref, wlr_ref, blr_ref, ng_ref,
                       o_ref, s_out_ref, q_ref, k_ref, v_ref, gate_ref, la_scr, st_scr, *, tb):
    i = pl.program_id(1)

    @pl.when(i == 0)
    def _():
        st_scr[...] = jnp.zeros(st_scr.shape, f32)

    u = _normed(x_ref, pg_ref)
    q_ref[...] = _proj(u, wq_ref, 0, GLA_KW)
    k_ref[...] = _proj(u, wk_ref, 0, GLA_KW)
    v_ref[...] = _proj(u, wv_ref, 0, GLA_VW).astype(v_ref.dtype)
    gate_ref[...] = _proj(u, wg_ref, 0, GLA_VW)
    small = _proj(u, ws_ref, 0, LANES)
    z = _mm(small, wlr_ref[...]) + blr_ref[...]
    la_scr[...] = -_softplus(-z) * (1.0 / GLA_TAU)

    incl = _tril(CHUNK)
    lmat = incl.astype(f32)
    ng = ng_ref[...]

    heads = range(GLA_HEADS)
    n_chunks = tb // CHUNK
    staged = [None] * n_chunks
    state = [st_scr[h] for h in heads]

    def prep(c):
        rows = pl.ds(c * CHUNK, CHUNK)
        bcum = [_sel_mm(lmat, la_scr[rows, h * GLA_DK:(h + 1) * GLA_DK]) for h in heads]
        yield
        k = [k_ref[rows, h * GLA_DK:(h + 1) * GLA_DK] for h in heads]
        v = [v_ref[rows, h * GLA_DV:(h + 1) * GLA_DV].astype(MM_DTYPE) for h in heads]
        qe = [(q_ref[rows, h * GLA_DK:(h + 1) * GLA_DK] * (GLA_DK ** -0.5) * jnp.exp(bcum[h])).astype(MM_DTYPE)
              for h in heads]
        ke = [k[h] * jnp.exp(-bcum[h]) for h in heads]
        blast = [bcum[h][CHUNK - 1:CHUNK, :] for h in heads]
        kdec = [(k[h] * jnp.exp(blast[h] - bcum[h])).astype(MM_DTYPE) for h in heads]
        p = [jnp.where(incl, _mm_nt(qe[h], ke[h]), 0.0) for h in heads]
        yield
        o_intra = [_mm(p[h], v[h]) for h in heads]
        staged[c] = (qe, v, kdec, blast, o_intra)
        yield

    def scan(c):
        rows = pl.ds(c * CHUNK, CHUNK)
        qe, v, kdec, blast, o_intra = staged[c]
        o = [o_intra[h] + _mm_nt(qe[h], state[h]) for h in heads]
        upd = [_mm_tn(v[h], kdec[h]) for h in heads]
        yield
        for h in heads:
            v0 = h * GLA_DV
            state[h] = jnp.exp(blast[h]) * state[h] + upd[h]
            gate = gate_ref[rows, v0:v0 + GLA_DV]
            o_ref[rows, v0:v0 + GLA_DV] = (_rms(o[h], GLA_DV) * ng * _silu(gate)).astype(o_ref.dtype)
        yield

    _run_interleaved(prep(0), None, 1)
    for c in range(1, n_chunks):
        _run_interleaved(prep(c), scan(c - 1), 1)
    _run_interleaved(scan(n_chunks - 1), None, 1)
    for h in heads:
        st_scr[h] = state[h]

    @pl.when(i == pl.num_programs(1) - 1)
    def _():
        for h in range(GLA_HEADS):
            s_out_ref[h] = st_scr[h].T


def _gla_prompt(x, pre_g, w_in, wlr, blr, ng, tb):
    b, l, d = x.shape
    kern = functools.partial(_gla_prompt_kernel, tb=tb)
    wcol = lambda blk: (lambda bi, i: (0, blk))
    const = lambda bi, i: (0, 0)
    return pl.pallas_call(
        kern,
        grid=(b, l // tb),
        in_specs=[pl.BlockSpec((None, tb, d), lambda bi, i: (bi, i, 0)),
                  pl.BlockSpec((1, d), const),
                  pl.BlockSpec((d, GLA_KW), wcol(E_OFF_LQ // GLA_KW)),
                  pl.BlockSpec((d, GLA_KW), wcol(E_OFF_LK // GLA_KW)),
                  pl.BlockSpec((d, GLA_VW), wcol(E_OFF_LV // GLA_VW)),
                  pl.BlockSpec((d, GLA_VW), wcol(E_OFF_LGATE // GLA_VW)),
                  pl.BlockSpec((d, LANES), wcol(E_OFF_SMALL // LANES)),
                  pl.BlockSpec((LANES, GLA_KW), const),
                  pl.BlockSpec((1, GLA_KW), const),
                  pl.BlockSpec((1, GLA_DV), const)],
        out_specs=[pl.BlockSpec((None, tb, GLA_VW), lambda bi, i: (bi, i, 0)),
                   pl.BlockSpec((None, GLA_HEADS, GLA_DK, GLA_DV), lambda bi, i: (bi, 0, 0, 0))],
        out_shape=[jax.ShapeDtypeStruct((b, l, GLA_VW), MM_DTYPE),
                   jax.ShapeDtypeStruct((b, GLA_HEADS, GLA_DK, GLA_DV), f32)],
        scratch_shapes=[pltpu.VMEM((tb, GLA_KW), f32),
                        pltpu.VMEM((tb, GLA_KW), f32),
                        pltpu.VMEM((tb, GLA_VW), MM_DTYPE),
                        pltpu.VMEM((tb, GLA_VW), f32),
                        pltpu.VMEM((tb, GLA_KW), f32),
                        pltpu.VMEM((GLA_HEADS, GLA_DV, GLA_DK), f32)],
        compiler_params=pltpu.CompilerParams(dimension_semantics=("parallel", "arbitrary"),
                                             vmem_limit_bytes=VMEM_LIMIT),
        name="gla_prompt",
    )(x, pre_g, w_in, w_in, w_in, w_in, w_in, wlr, blr, ng)


def _ssd_prompt_kernel(x_ref, pg_ref, wx_ref, wz_ref, wbc_ref, wdt_ref, cw_ref, cb_ref, hp_ref, dskip_ref, ng_ref,
                       expand_ref, y_ref, conv_out_ref, h_out_ref,
                       xp_scr, xbc_scr, z_ref, lam_scr, lamx_scr, xdt_scr, h_scr, *, tb):
    i = pl.program_id(1)
    first = i == 0
    conv_cols = _conv_block(xp_scr, cw_ref, cb_ref, tb, first)

    @pl.when(first)
    def _():
        h_scr[...] = jnp.zeros(h_scr.shape, f32)

    u = _normed(x_ref, pg_ref)
    for c0 in range(0, SSD_DI, 1024):
        xp_scr[8:8 + tb, c0:c0 + 1024] = _proj(u, wx_ref, c0, c0 + 1024)
        z_ref[:, c0:c0 + 1024] = _proj(u, wz_ref, c0, c0 + 1024)
    xp_scr[8:8 + tb, SSD_DI:SSD_CONV_DIM] = _proj(u, wbc_ref, 0, 2 * SSD_BC)
    dt_raw = _proj(u, wdt_ref, 0, LANES)

    for c0 in range(0, SSD_CONV_DIM, 512):
        xbc_scr[:, c0:c0 + 512] = conv_cols(c0, c0 + 512)
    dt = _softplus(dt_raw + hp_ref[0:1, :])
    tr = lax.broadcasted_iota(jnp.int32, (tb, tb), 0)
    tc = lax.broadcasted_iota(jnp.int32, (tb, tb), 1)
    lam_all = _sel_mm(jnp.logical_and(tr >= tc, tr // CHUNK == tc // CHUNK).astype(f32), dt * hp_ref[1:2, :])
    lam_scr[...] = lam_all
    lo_blk = lax.broadcasted_iota(jnp.int32, (tb, LANES), 1) < SSD_P
    for p in range(SSD_HEADS // 2):
        cols = slice(p * LANES, (p + 1) * LANES)
        k0 = 2 * p
        xdt_scr[:, cols] = xbc_scr[:, cols] * jnp.where(lo_blk, dt[:, k0:k0 + 1], dt[:, k0 + 1:k0 + 2])
    for c0 in range(0, SSD_DI, 512):
        lamx_scr[:, c0:c0 + 512] = _sel_rows(lam_all, expand_ref[:, c0:c0 + 512])

    lane = lax.broadcasted_iota(jnp.int32, (CHUNK, LANES), 1)
    lo = lane < SSD_P
    incl2 = lax.broadcasted_iota(jnp.int32, (CHUNK, LANES), 0) >= lane % CHUNK
    top = lax.broadcasted_iota(jnp.int32, (2 * SSD_P, SSD_N), 0) < SSD_P
    n_pairs = SSD_HEADS // 2
    ppg = SSD_K // 2
    groups = range(SSD_G)

    def chunk_body(c):
        rows = pl.ds(c * CHUNK, CHUNK)
        lam = lam_scr[rows, :]
        lam_t = lam.T
        llast = lam[CHUNK - 1:CHUNK, :]
        bm = [xbc_scr[rows, SSD_DI + g * SSD_N:SSD_DI + (g + 1) * SSD_N].astype(MM_DTYPE) for g in groups]
        cm = [xbc_scr[rows, SSD_DI + SSD_BC + g * SSD_N:SSD_DI + SSD_BC + (g + 1) * SSD_N].astype(MM_DTYPE)
              for g in groups]
        h_old = [h_scr[g * SSD_GW:(g + 1) * SSD_GW, :] for g in groups]
        cb2 = [_mm_nt(cm[g], jnp.concatenate([bm[g], bm[g]], axis=0)) for g in groups]
        y_inter = [_mm_nt(cm[g], h_old[g]) for g in groups]
        xs, xdt, xdec, lcol = [], [], [], []
        for p in range(n_pairs):
            cols = slice(p * LANES, (p + 1) * LANES)
            xs.append(xbc_scr[rows, cols])
            lcol.append(lamx_scr[rows, cols])
            xdt.append(xdt_scr[rows, cols])
            xdec.append((xdt[p] * jnp.exp(lcol[p][CHUNK - 1:CHUNK, :] - lcol[p])).astype(MM_DTYPE))
        upd = [_mm_tn(jnp.concatenate(xdec[g * ppg:(g + 1) * ppg], axis=-1), bm[g]) for g in groups]
        y_intra = []
        for p in range(n_pairs):
            k0 = 2 * p
            lrow = jnp.concatenate([lam_t[k0:k0 + 1, :], lam_t[k0 + 1:k0 + 2, :]], axis=-1)
            m = cb2[p // ppg] * _masked_decay(lcol[p], lrow, incl2)
            blockdiag = jnp.concatenate([jnp.where(lo, xdt[p], 0.0), jnp.where(lo, 0.0, xdt[p])], axis=0)
            y_intra.append(_mm(m, blockdiag))
        for g in groups:
            ys = []
            for pp in range(ppg):
                p = g * ppg + pp
                k0 = 2 * p
                ys.append(y_intra[p] + jnp.exp(lcol[p]) * y_inter[g][:, pp * LANES:(pp + 1) * LANES]
                          + dskip_ref[:, p * LANES:(p + 1) * LANES] * xs[p])
                el = jnp.where(top, jnp.exp(llast[:, k0:k0 + 1]), jnp.exp(llast[:, k0 + 1:k0 + 2]))
                h_scr[p * LANES:(p + 1) * LANES, :] = (el * h_old[g][pp * LANES:(pp + 1) * LANES, :]
                                                        + upd[g][pp * LANES:(pp + 1) * LANES, :])
            yg = jnp.concatenate(ys, axis=-1) * _silu(z_ref[rows, g * SSD_GW:(g + 1) * SSD_GW])
            y_ref[rows, g * SSD_GW:(g + 1) * SSD_GW] = (
                _rms(yg, SSD_GW) * ng_ref[:, g * SSD_GW:(g + 1) * SSD_GW]).astype(y_ref.dtype)

    for c in range(tb // CHUNK):
        chunk_body(c)

    @pl.when(i == pl.num_programs(1) - 1)
    def _():
        conv_out_ref[...] = xp_scr[tb + 5:tb + 8, :]
        h_out_ref[...] = h_scr[...]


def _ssd_prompt(x, pre_g, w_in, cw, cb, hp, dskip, ng, expand, tb):
    b, l, d = x.shape
    kern = functools.partial(_ssd_prompt_kernel, tb=tb)
    wcol = lambda blk: (lambda bi, i: (0, blk))
    const = lambda bi, i: (0, 0)
    return pl.pallas_call(
        kern,
        grid=(b, l // tb),
        in_specs=[pl.BlockSpec((None, tb, d), lambda bi, i: (bi, i, 0)),
                  pl.BlockSpec((1, d), const),
                  pl.BlockSpec((d, SSD_DI), wcol(0)),
                  pl.BlockSpec((d, SSD_DI), wcol(O_OFF_Z // SSD_DI)),
                  pl.BlockSpec((d, 2 * SSD_BC), wcol(O_OFF_BC // (2 * SSD_BC))),
                  pl.BlockSpec((d, LANES), wcol(O_OFF_DT // LANES)),
                  pl.BlockSpec((CONV_W, SSD_CONV_DIM), const),
                  pl.BlockSpec((1, SSD_CONV_DIM), const),
                  pl.BlockSpec((8, LANES), const),
                  pl.BlockSpec((1, SSD_DI), const),
                  pl.BlockSpec((1, SSD_DI), const),
                  pl.BlockSpec((LANES, SSD_DI), const)],
        out_specs=[pl.BlockSpec((None, tb, SSD_DI), lambda bi, i: (bi, i, 0)),
                   pl.BlockSpec((None, CONV_W - 1, SSD_CONV_DIM), lambda bi, i: (bi, 0, 0)),
                   pl.BlockSpec((None, SSD_HEADS * SSD_P, SSD_N), lambda bi, i: (bi, 0, 0))],
        out_shape=[jax.ShapeDtypeStruct((b, l, SSD_DI), MM_DTYPE),
                   jax.ShapeDtypeStruct((b, CONV_W - 1, SSD_CONV_DIM), f32),
                   jax.ShapeDtypeStruct((b, SSD_HEADS * SSD_P, SSD_N), f32)],
        scratch_shapes=[pltpu.VMEM((tb + 8, SSD_CONV_DIM), f32),
                        pltpu.VMEM((tb, SSD_CONV_DIM), f32),
                        pltpu.VMEM((tb, SSD_DI), f32),
                        pltpu.VMEM((tb, LANES), f32),
                        pltpu.VMEM((tb, SSD_DI), f32),
                        pltpu.VMEM((tb, SSD_DI), f32),
                        pltpu.VMEM((SSD_HEADS * SSD_P, SSD_N), f32)],
        compiler_params=pltpu.CompilerParams(dimension_semantics=("parallel", "arbitrary"),
                                             vmem_limit_bytes=VMEM_LIMIT),
        name="ssd_prompt",
    )(x, pre_g, w_in, w_in, w_in, w_in, cw, cb, hp, dskip, ng, expand)


def _pad_rows(row, n):
    return jnp.concatenate([row, jnp.zeros((n - 1, row.shape[1]), row.dtype)], axis=0)


def _bcast_cols(row, width):
    hi, mid, lo = _split3(row)
    lhs = jnp.concatenate([hi, mid, lo, jnp.zeros((13, row.shape[1]), bf16)], axis=0)
    sel = (lax.broadcasted_iota(jnp.int32, (16, width), 0) < 3).astype(bf16)
    return lax.dot_general(lhs, sel, (((0,), (0,)), ((), ())), preferred_element_type=f32)


def _conv_step(buf_ref, x, cw_ref, cb_ref, new_buf_ref):
    bb = x.shape[0]
    prev = [buf_ref[b] for b in range(bb)]
    acc = cb_ref[...] + x * cw_ref[3:4, :]
    for s in range(CONV_W - 1):
        acc = acc + jnp.concatenate([prev[b][s:s + 1] for b in range(bb)], axis=0) * cw_ref[s:s + 1, :]
    for b in range(bb):
        new_buf_ref[b] = jnp.concatenate([prev[b][1:CONV_W - 1], x[b:b + 1]], axis=0)
    return _silu(acc)


def _gdn_decode_kernel(q_ref, k_ref, v_ref, gate_ref, small_ref, buf_ref, s_ref, cw_ref, cb_ref, hp_ref, ng_ref,
                       o_ref, new_buf_ref, s_out_ref, *, bb):
    x = jnp.concatenate([q_ref[...], k_ref[...], v_ref[...]], axis=-1)
    qkv = _conv_step(buf_ref, x, cw_ref, cb_ref, new_buf_ref)
    small = small_ref[...]
    beta = _sigmoid(small)
    eg = jnp.exp(hp_ref[0:1, :] * _softplus(small + hp_ref[1:2, :]))
    ng = ng_ref[...]
    gate = gate_ref[...]
    for h in range(GDN_HEADS):
        c0 = h * GDN_DK
        q = qkv[:, c0:c0 + GDN_DK]
        q = q * lax.rsqrt(jnp.sum(q * q, axis=-1, keepdims=True) + EPS) * (GDN_DK ** -0.5)
        k = qkv[:, GDN_KW + c0:GDN_KW + c0 + GDN_DK]
        k = k * lax.rsqrt(jnp.sum(k * k, axis=-1, keepdims=True) + EPS)
        v = qkv[:, 2 * GDN_KW + c0:2 * GDN_KW + c0 + GDN_DV]
        qk = jnp.sum(q * k, axis=-1, keepdims=True)
        b_h = beta[:, SMALL_BETA + h:SMALL_BETA + h + 1]
        eg_h = eg[:, SMALL_A + h:SMALL_A + h + 1]
        seqs = range(bb)
        s_old = [s_ref[b, h] for b in seqs]
        r = [_mm(jnp.concatenate([k[b:b + 1], q[b:b + 1], jnp.zeros((6, GDN_DK), f32)], axis=0), s_old[b])
             for b in seqs]
        u = [b_h[b:b + 1] * v[b:b + 1] - (b_h[b:b + 1] * eg_h[b:b + 1]) * r[b][0:1] for b in seqs]
        upd = [_mm_tn(_pad_rows(k[b:b + 1], 16), _pad_rows(u[b], 16)) for b in seqs]
        for b in seqs:
            s_out_ref[b, h] = eg_h[b:b + 1] * s_old[b] + upd[b]
        o = jnp.concatenate([eg_h[b:b + 1] * r[b][1:2] + qk[b:b + 1] * u[b] for b in seqs], axis=0)
        o_ref[:, c0:c0 + GDN_DV] = (_rms(o, GDN_DV) * ng * _silu(gate[:, c0:c0 + GDN_DV])).astype(o_ref.dtype)


def _gdn_decode(proj, buf, s, cw, cb, hp, ng, bb):
    b = proj.shape[0]
    wq = GDN_KW
    kern = functools.partial(_gdn_decode_kernel, bb=bb)
    col = lambda blk: (lambda i: (i, blk))
    const = lambda i: (0, 0)
    return pl.pallas_call(
        kern,
        grid=(b // bb,),
        in_specs=[pl.BlockSpec((bb, wq), col(0)),
                  pl.BlockSpec((bb, wq), col(1)),
                  pl.BlockSpec((bb, wq), col(2)),
                  pl.BlockSpec((bb, wq), col(3)),
                  pl.BlockSpec((bb, LANES), col(E_OFF_SMALL // LANES)),
                  pl.BlockSpec((bb, CONV_W - 1, GDN_CONV_DIM), lambda i: (i, 0, 0)),
                  pl.BlockSpec((bb, GDN_HEADS, GDN_DK, GDN_DV), lambda i: (i, 0, 0, 0)),
                  pl.BlockSpec((CONV_W, GDN_CONV_DIM), const),
                  pl.BlockSpec((1, GDN_CONV_DIM), const),
                  pl.BlockSpec((8, LANES), const),
                  pl.BlockSpec((1, GDN_DV), const)],
        out_specs=[pl.BlockSpec((bb, GDN_VW), col(0)),
                   pl.BlockSpec((bb, CONV_W - 1, GDN_CONV_DIM), lambda i: (i, 0, 0)),
                   pl.BlockSpec((bb, GDN_HEADS, GDN_DK, GDN_DV), lambda i: (i, 0, 0, 0))],
        out_shape=[jax.ShapeDtypeStruct((b, GDN_VW), MM_DTYPE),
                   jax.ShapeDtypeStruct((b, CONV_W - 1, GDN_CONV_DIM), f32),
                   jax.ShapeDtypeStruct((b, GDN_HEADS, GDN_DK, GDN_DV), f32)],
        compiler_params=pltpu.CompilerParams(dimension_semantics=("parallel",), vmem_limit_bytes=VMEM_LIMIT),
        name="gdn_decode",
    )(proj, proj, proj, proj, proj, buf, s, cw, cb, hp, ng)


def _gla_decode_kernel(q_ref, k_ref, v_ref, gate_ref, small_ref, s_ref, wlr_ref, blr_ref, ng_ref,
                       o_ref, s_out_ref, *, bb):
    z = _mm(small_ref[...], wlr_ref[...]) + blr_ref[...]
    la = -_softplus(-z) * (1.0 / GLA_TAU)
    dec = jnp.exp(la)
    qe = q_ref[...] * (GLA_DK ** -0.5) * dec
    kx = k_ref[...]
    ke = kx * jnp.exp(-la)
    vx = v_ref[...]
    ng = ng_ref[...]
    gate = gate_ref[...]
    for h in range(GLA_HEADS):
        k0 = h * GLA_DK
        v0 = h * GLA_DV
        p = jnp.sum(qe[:, k0:k0 + GLA_DK] * ke[:, k0:k0 + GLA_DK], axis=-1, keepdims=True)
        outs = []
        for b in range(bb):
            s_old = s_ref[b, h]
            vrow = vx[b:b + 1, v0:v0 + GLA_DV]
            outs.append(_mm(_pad_rows(qe[b:b + 1, k0:k0 + GLA_DK], 8), s_old)[0:1] + p[b:b + 1] * vrow)
            s_out_ref[b, h] = (_bcast_cols(dec[b:b + 1, k0:k0 + GLA_DK], GLA_DV) * s_old
                               + _mm_tn(_pad_rows(kx[b:b + 1, k0:k0 + GLA_DK], 16), _pad_rows(vrow, 16)))
        o = jnp.concatenate(outs, axis=0)
        o_ref[:, v0:v0 + GLA_DV] = (_rms(o, GLA_DV) * ng * _silu(gate[:, v0:v0 + GLA_DV])).astype(o_ref.dtype)


def _gla_decode(proj, s, wlr, blr, ng, bb):
    b = proj.shape[0]
    kern = functools.partial(_gla_decode_kernel, bb=bb)
    col = lambda blk: (lambda i: (i, blk))
    const = lambda i: (0, 0)
    return pl.pallas_call(
        kern,
        grid=(b // bb,),
        in_specs=[pl.BlockSpec((bb, GLA_KW), col(E_OFF_LQ // GLA_KW)),
                  pl.BlockSpec((bb, GLA_KW), col(E_OFF_LK // GLA_KW)),
                  pl.BlockSpec((bb, GLA_VW), col(E_OFF_LV // GLA_VW)),
                  pl.BlockSpec((bb, GLA_VW), col(E_OFF_LGATE // GLA_VW)),
                  pl.BlockSpec((bb, LANES), col(E_OFF_SMALL // LANES)),
                  pl.BlockSpec((bb, GLA_HEADS, GLA_DK, GLA_DV), lambda i: (i, 0, 0, 0)),
                  pl.BlockSpec((LANES, GLA_KW), const),
                  pl.BlockSpec((1, GLA_KW), const),
                  pl.BlockSpec((1, GLA_DV), const)],
        out_specs=[pl.BlockSpec((bb, GLA_VW), col(0)),
                   pl.BlockSpec((bb, GLA_HEADS, GLA_DK, GLA_DV), lambda i: (i, 0, 0, 0))],
        out_shape=[jax.ShapeDtypeStruct((b, GLA_VW), MM_DTYPE),
                   jax.ShapeDtypeStruct((b, GLA_HEADS, GLA_DK, GLA_DV), f32)],
        compiler_params=pltpu.CompilerParams(dimension_semantics=("parallel",), vmem_limit_bytes=VMEM_LIMIT),
        name="gla_decode",
    )(proj, proj, proj, proj, proj, s, wlr, blr, ng)


def _ssd_decode_kernel(x_ref, z_ref, bc_ref, dt_ref, buf_ref, h_ref, cw_ref, cb_ref, hp_ref, dskip_ref, ng_ref,
                       expand_ref, y_ref, new_buf_ref, h_out_ref, *, bb):
    xin = jnp.concatenate([x_ref[...], bc_ref[...]], axis=-1)
    xbc = _conv_step(buf_ref, xin, cw_ref, cb_ref, new_buf_ref)
    xs = xbc[:, :SSD_DI]
    dt = _softplus(dt_ref[...] + hp_ref[0:1, :])
    el = jnp.exp(dt * hp_ref[1:2, :])
    expand = expand_ref[...]
    dt_x = _sel_rows(dt, expand)
    el_x = _sel_rows(el, expand)
    xdt = xs * dt_x
    z = z_ref[...]
    for g in range(SSD_G):
        gs = slice(g * SSD_GW, (g + 1) * SSD_GW)
        bm = xbc[:, SSD_DI + g * SSD_N:SSD_DI + (g + 1) * SSD_N]
        cm = xbc[:, SSD_DI + SSD_BC + g * SSD_N:SSD_DI + SSD_BC + (g + 1) * SSD_N]
        cbs = jnp.sum(cm * bm, axis=-1, keepdims=True)
        outs = []
        for b in range(bb):
            hg = h_ref[b, gs, :]
            y_inter = _mm_nt(_pad_rows(cm[b:b + 1], 8), hg)[0:1]
            outs.append(el_x[b:b + 1, gs] * y_inter)
            h_out_ref[b, gs, :] = (_bcast_cols(el_x[b:b + 1, gs], SSD_N) * hg
                                   + _mm_tn(_pad_rows(xdt[b:b + 1, gs], 16), _pad_rows(bm[b:b + 1], 16)))
        y = jnp.concatenate(outs, axis=0) + cbs * xdt[:, gs] + dskip_ref[:, gs] * xs[:, gs]
        yg = y * _silu(z[:, gs])
        y_ref[:, gs] = (_rms(yg, SSD_GW) * ng_ref[:, gs]).astype(y_ref.dtype)


def _ssd_decode(proj, buf, hst, cw, cb, hp, dskip, ng, expand, bb):
    b = proj.shape[0]
    kern = functools.partial(_ssd_decode_kernel, bb=bb)
    col = lambda blk: (lambda i: (i, blk))
    const = lambda i: (0, 0)
    return pl.pallas_call(
        kern,
        grid=(b // bb,),
        in_specs=[pl.BlockSpec((bb, SSD_DI), col(0)),
                  pl.BlockSpec((bb, SSD_DI), col(O_OFF_Z // SSD_DI)),
                  pl.BlockSpec((bb, 2 * SSD_BC), col(O_OFF_BC // (2 * SSD_BC))),
                  pl.BlockSpec((bb, LANES), col(O_OFF_DT // LANES)),
                  pl.BlockSpec((bb, CONV_W - 1, SSD_CONV_DIM), lambda i: (i, 0, 0)),
                  pl.BlockSpec((bb, SSD_HEADS * SSD_P, SSD_N), lambda i: (i, 0, 0)),
                  pl.BlockSpec((CONV_W, SSD_CONV_DIM), const),
                  pl.BlockSpec((1, SSD_CONV_DIM), const),
                  pl.BlockSpec((8, LANES), const),
                  pl.BlockSpec((1, SSD_DI), const),
                  pl.BlockSpec((1, SSD_DI), const),
                  pl.BlockSpec((LANES, SSD_DI), const)],
        out_specs=[pl.BlockSpec((bb, SSD_DI), col(0)),
                   pl.BlockSpec((bb, CONV_W - 1, SSD_CONV_DIM), lambda i: (i, 0, 0)),
                   pl.BlockSpec((bb, SSD_HEADS * SSD_P, SSD_N), lambda i: (i, 0, 0))],
        out_shape=[jax.ShapeDtypeStruct((b, SSD_DI), MM_DTYPE),
                   jax.ShapeDtypeStruct((b, CONV_W - 1, SSD_CONV_DIM), f32),
                   jax.ShapeDtypeStruct((b, SSD_HEADS * SSD_P, SSD_N), f32)],
        compiler_params=pltpu.CompilerParams(dimension_semantics=("parallel",), vmem_limit_bytes=VMEM_LIMIT),
        name="ssd_decode",
    )(proj, proj, proj, proj, buf, hst, cw, cb, hp, dskip, ng, expand)


def _pack_even_w_in(w):
    w = w.astype(MM_DTYPE)
    d = w.shape[0]
    o_beta = 4 * GDN_KW
    o_lq = o_beta + 2 * GDN_HEADS
    o_lr = o_lq + 2 * GLA_KW + 2 * GLA_VW
    small = jnp.concatenate([w[:, o_beta:o_lq], w[:, o_lr:o_lr + GLA_RANK],
                             jnp.zeros((d, LANES - 2 * GDN_HEADS - GLA_RANK), w.dtype)], axis=1)
    packed = jnp.concatenate([w[:, :o_beta], w[:, o_lq:o_lr], small], axis=1)
    return jnp.pad(packed, ((0, 0), (0, E_N - packed.shape[1])))


def _pack_odd_w_in(w):
    w = w.astype(MM_DTYPE)
    z = w[:, :SSD_DI]
    xbc = w[:, SSD_DI:SSD_DI + SSD_CONV_DIM]
    dt = w[:, SSD_DI + SSD_CONV_DIM:]
    packed = jnp.concatenate([xbc[:, :SSD_DI], z, xbc[:, SSD_DI:], dt], axis=1)
    return jnp.pad(packed, ((0, 0), (0, O_N - packed.shape[1])))


def _lane_row(vals, offset):
    return jnp.zeros((LANES,), f32).at[offset:offset + vals.shape[0]].set(vals.astype(f32))


def kernel(x_prompt, x_sample, state_gdn_conv, state_gdn, state_gla, state_ssd_conv, state_ssd, e_pre_g, e_post_g, e_w_in, e_conv_w, e_conv_b, gdn_a_log, gdn_dt_bias, gdn_norm_g, gla_w_lr, gla_b_lr, gla_norm_g, e_w_out, o_pre_g, o_post_g, o_w_in, ssd_conv_w, ssd_conv_b, ssd_dt_bias, ssd_a_log, ssd_d, ssd_norm_g, o_w_out):
    bp, lp, d = x_prompt.shape
    bs = x_sample.shape[0]
    tp = bp * lp
    tb = min(lp, 256)
    bb = min(bs, 8)

    w_in0 = _pack_even_w_in(e_w_in[0])
    w_out0 = e_w_out[0].astype(MM_DTYPE)
    pre0 = e_pre_g[0].reshape(1, d)
    post0 = e_post_g[0].reshape(1, d)
    cw0 = e_conv_w[0]
    cb0 = e_conv_b[0].reshape(1, GDN_CONV_DIM)
    hp0 = jnp.zeros((8, LANES), f32)
    hp0 = hp0.at[0].set(_lane_row(-jnp.exp(gdn_a_log[0].astype(f32)), SMALL_A))
    hp0 = hp0.at[1].set(_lane_row(gdn_dt_bias[0], SMALL_A))
    ng_gdn = gdn_norm_g[0].reshape(1, GDN_DV)
    wlr = jnp.zeros((LANES, GLA_KW), f32).at[SMALL_LR:SMALL_LR + GLA_RANK].set(gla_w_lr[0]).astype(MM_DTYPE)
    blr = gla_b_lr[0].reshape(1, GLA_KW)
    ng_gla = gla_norm_g[0].reshape(1, GLA_DV)

    hp_flat = x_prompt.reshape(tp, d)
    hs_flat = x_sample.reshape(bs, d)

    o1_p, gdn_conv_p, gdn_p = _gdn_prompt(x_prompt, pre0, w_in0, cw0, cb0, hp0, ng_gdn, tb)
    o2_p, gla_p = _gla_prompt(x_prompt, pre0, w_in0, wlr, blr, ng_gla, tb)
    hp_flat = _out_proj(o1_p.reshape(tp, GDN_VW), o2_p.reshape(tp, GLA_VW), w_out0, post0, hp_flat)

    proj_s = _norm_proj(hs_flat, pre0, w_in0, 1280)
    o1_s, gdn_conv_s, gdn_s = _gdn_decode(proj_s, state_gdn_conv[0], state_gdn[0],
                                          cw0, cb0, hp0, ng_gdn, bb)
    o2_s, gla_s = _gla_decode(proj_s, state_gla[0], wlr, blr, ng_gla, bb)
    hs_flat = _out_proj(o1_s, o2_s, w_out0, post0, hs_flat)

    w_in1 = _pack_odd_w_in(o_w_in[0])
    w_out1 = o_w_out[0].astype(MM_DTYPE)
    pre1 = o_pre_g[0].reshape(1, d)
    post1 = o_post_g[0].reshape(1, d)
    cw1 = ssd_conv_w[0]
    cb1 = ssd_conv_b[0].reshape(1, SSD_CONV_DIM)
    hp1 = jnp.zeros((8, LANES), f32)
    hp1 = hp1.at[0].set(_lane_row(ssd_dt_bias[0], 0))
    hp1 = hp1.at[1].set(_lane_row(-jnp.exp(ssd_a_log[0].astype(f32)), 0))
    dskip = jnp.repeat(ssd_d[0].astype(f32), SSD_P).reshape(1, SSD_DI)
    ng_ssd = ssd_norm_g[0].reshape(1, SSD_DI)
    expand = (lax.broadcasted_iota(jnp.int32, (LANES, SSD_DI), 0)
              == lax.broadcasted_iota(jnp.int32, (LANES, SSD_DI), 1) // SSD_P).astype(MM_DTYPE)

    y_p, ssd_conv_p, ssd_p = _ssd_prompt(hp_flat.reshape(bp, lp, d), pre1, w_in1, cw1, cb1, hp1, dskip, ng_ssd,
                                         expand, tb)
    y_p = y_p.reshape(tp, SSD_DI)
    hp_flat = _out_proj(y_p, y_p, w_out1, post1, hp_flat)

    proj_s = _norm_proj(hs_flat, pre1, w_in1, 1792)
    y_s, ssd_conv_s, ssd_s = _ssd_decode(proj_s, state_ssd_conv[0],
                                         state_ssd[0].reshape(bs, SSD_HEADS * SSD_P, SSD_N),
                                         cw1, cb1, hp1, dskip, ng_ssd, expand, bb)
    hs_flat = _out_proj(y_s, y_s, w_out1, post1, hs_flat)

    return (hp_flat.reshape(bp, lp, d), hs_flat.reshape(bs, 1, d),
            gdn_conv_p[None], gdn_p[None], gla_p[None],
            ssd_conv_p[None], ssd_p.reshape(bp, SSD_HEADS, SSD_P, SSD_N)[None],
            gdn_conv_s[None], gdn_s[None], gla_s[None],
            ssd_conv_s[None],
            ssd_s.reshape(bs, SSD_HEADS, SSD_P, SSD_N)[None])
```

```python
import functools

import jax
import jax.numpy as jnp
from jax import lax
from jax.experimental import pallas as pl
from jax.experimental.pallas import tpu as pltpu

f32 = jnp.float32
bf16 = jnp.bfloat16
MM_DTYPE = jnp.bfloat16

D_MODEL = 1024
CONV_W = 4
EPS = 1e-6
LANES = 128

GDN_HEADS = 8
GDN_DK = 128
GDN_DV = 128
GDN_KW = GDN_HEADS * GDN_DK
GDN_VW = GDN_HEADS * GDN_DV
GDN_CONV_DIM = 2 * GDN_KW + GDN_VW
GLA_HEADS = 4
GLA_DK = 128
GLA_DV = 256
GLA_KW = GLA_HEADS * GLA_DK
GLA_VW = GLA_HEADS * GLA_DV
GLA_RANK = 16
GLA_TAU = 16.0
SSD_DI = 2 * D_MODEL
SSD_P = 64
SSD_HEADS = SSD_DI // SSD_P
SSD_N = 128
SSD_G = 4
SSD_K = SSD_HEADS // SSD_G
SSD_GW = SSD_DI // SSD_G
SSD_BC = SSD_G * SSD_N
SSD_CONV_DIM = SSD_DI + 2 * SSD_BC

E_OFF_GATE = 3 * GDN_KW
E_OFF_LQ = 4 * GDN_KW
E_OFF_LK = E_OFF_LQ + GLA_KW
E_OFF_LV = E_OFF_LK + GLA_KW
E_OFF_LGATE = E_OFF_LV + GLA_VW
E_OFF_SMALL = E_OFF_LGATE + GLA_VW
E_N = 7680
SMALL_BETA = 0
SMALL_A = GDN_HEADS
SMALL_LR = 2 * GDN_HEADS
O_OFF_Z = SSD_DI
O_OFF_BC = 2 * SSD_DI
O_OFF_DT = 2 * SSD_DI + 2 * SSD_BC
O_N = 5376

CHUNK = 64
GDN_PREP_CHUNKS = 2
assert SSD_P == CHUNK and 2 * SSD_P == LANES
VMEM_LIMIT = 56 * 1024 * 1024


def _mm(a, b):
    return jnp.dot(a.astype(MM_DTYPE), b.astype(MM_DTYPE), preferred_element_type=f32)


def _mm_nt(a, b):
    return lax.dot_general(a.astype(MM_DTYPE), b.astype(MM_DTYPE), (((1,), (1,)), ((), ())),
                           preferred_element_type=f32)


def _mm_tn(a, b):
    return lax.dot_general(a.astype(MM_DTYPE), b.astype(MM_DTYPE), (((0,), (0,)), ((), ())),
                           preferred_element_type=f32)


def _split3(x):
    hi = x.astype(bf16)
    r = x - hi.astype(f32)
    mid = r.astype(bf16)
    lo = (r - mid.astype(f32)).astype(bf16)
    return hi, mid, lo


def _sel_mm(sel, x):
    hi, mid, lo = _split3(x)
    s = sel.astype(bf16)
    return (jnp.dot(s, hi, preferred_element_type=f32) + jnp.dot(s, mid, preferred_element_type=f32)
            + jnp.dot(s, lo, preferred_element_type=f32))


def _sel_rows(x, sel):
    hi, mid, lo = _split3(x)
    s = sel.astype(bf16)
    return (jnp.dot(hi, s, preferred_element_type=f32) + jnp.dot(mid, s, preferred_element_type=f32)
            + jnp.dot(lo, s, preferred_element_type=f32))


def _sigmoid(x):
    return 1.0 / (1.0 + jnp.exp(-x))


def _silu(x):
    return x * _sigmoid(x)


def _softplus(x):
    return jnp.maximum(x, 0.0) + jnp.log1p(jnp.exp(-jnp.abs(x)))


def _tril(n, strict=False):
    r = lax.broadcasted_iota(jnp.int32, (n, n), 0)
    c = lax.broadcasted_iota(jnp.int32, (n, n), 1)
    return (r > c) if strict else (r >= c)


def _masked_decay(col, row, incl):
    return jnp.exp(jnp.where(incl, col - row, -jnp.inf))


def _interleave(main, side, ratio):
    gens = [main, side]
    steps = {id(main): ratio, id(side): 1}
    while gens:
        for g in list(gens):
            for _ in range(steps[id(g)]):
                try:
                    next(g)
                except StopIteration:
                    gens.remove(g)
                    break
                yield


def _rms(x, width):
    return x * lax.rsqrt(jnp.sum(x * x, axis=-1, keepdims=True) * (1.0 / width) + EPS)


def _norm_proj_kernel(x_ref, g_ref, w_ref, o_ref, u_scr):
    @pl.when(pl.program_id(1) == 0)
    def _():
        u_scr[...] = (_rms(x_ref[...], D_MODEL) * g_ref[...]).astype(u_scr.dtype)

    o_ref[...] = jnp.dot(u_scr[...], w_ref[...], preferred_element_type=f32)


def _norm_proj(x, g, w, tn):
    t, d = x.shape
    n = w.shape[1]
    tm = min(t, 1024)
    return pl.pallas_call(
        _norm_proj_kernel,
        grid=(t // tm, n // tn),
        in_specs=[pl.BlockSpec((tm, d), lambda i, j: (i, 0)),
                  pl.BlockSpec((1, d), lambda i, j: (0, 0)),
                  pl.BlockSpec((d, tn), lambda i, j: (0, j))],
        out_specs=pl.BlockSpec((tm, tn), lambda i, j: (i, j)),
        out_shape=jax.ShapeDtypeStruct((t, n), f32),
        scratch_shapes=[pltpu.VMEM((tm, d), MM_DTYPE)],
        compiler_params=pltpu.CompilerParams(dimension_semantics=("parallel", "arbitrary"),
                                             vmem_limit_bytes=VMEM_LIMIT),
        name="norm_proj",
    )(x, g, w)


def _out_proj_kernel(oa_ref, ob_ref, wa_ref, wb_ref, g_ref, h_ref, y_ref):
    p = (jnp.dot(oa_ref[...], wa_ref[...], preferred_element_type=f32)
         + jnp.dot(ob_ref[...], wb_ref[...], preferred_element_type=f32))
    y_ref[...] = h_ref[...] + _rms(p, D_MODEL) * g_ref[...]


def _out_proj(oa, ob, w, g, h):
    t = oa.shape[0]
    e = w.shape[0] // 2
    assert oa.shape[1] == ob.shape[1] and oa.shape[1] in (e, 2 * e)
    col_b = 1 if ob.shape[1] == 2 * e else 0
    d = w.shape[1]
    tm = min(t, 512)
    return pl.pallas_call(
        _out_proj_kernel,
        grid=(t // tm,),
        in_specs=[pl.BlockSpec((tm, e), lambda i: (i, 0)),
                  pl.BlockSpec((tm, e), lambda i: (i, col_b)),
                  pl.BlockSpec((e, d), lambda i: (0, 0)),
                  pl.BlockSpec((e, d), lambda i: (1, 0)),
                  pl.BlockSpec((1, d), lambda i: (0, 0)),
                  pl.BlockSpec((tm, d), lambda i: (i, 0))],
        out_specs=pl.BlockSpec((tm, d), lambda i: (i, 0)),
        out_shape=jax.ShapeDtypeStruct((t, d), f32),
        compiler_params=pltpu.CompilerParams(dimension_semantics=("parallel",),
                                             vmem_limit_bytes=VMEM_LIMIT),
        name="out_proj",
    )(oa, ob, w, w, g, h)


def _normed(x_ref, g_ref):
    return (_rms(x_ref[...], D_MODEL) * g_ref[...]).astype(MM_DTYPE)


def _proj(u, w_ref, c0, c1):
    return jnp.dot(u, w_ref[:, c0:c1], preferred_element_type=f32)


def _conv_block(xp_scr, cw_ref, cb_ref, tb, first):
    @pl.when(first)
    def _():
        xp_scr[0:8, :] = jnp.zeros((8, xp_scr.shape[1]), f32)

    @pl.when(jnp.logical_not(first))
    def _():
        xp_scr[5:8, :] = xp_scr[tb + 5:tb + 8, :]

    def conv_cols(c0, c1, r0=0, nr=tb):
        acc = cb_ref[:, c0:c1] + xp_scr[8 + r0:8 + r0 + nr, c0:c1] * cw_ref[3:4, c0:c1]
        for s in range(CONV_W - 1):
            acc = acc + xp_scr[5 + s + r0:5 + s + r0 + nr, c0:c1] * cw_ref[s:s + 1, c0:c1]
        return _silu(acc)

    return conv_cols


def _gdn_stages(u, first, last, wm_ref, ws_ref, cw_ref, cb_ref, hp_ref, ng_ref,
                o_ref, conv_out_ref, s_out_ref,
                xp_scr, qkv_scr, gate_scr, beta_scr, lg_scr, s_scr, gl_scr, up_scr, w_scr, qg_scr, kd_scr,
                p_scr, *, tb):
    conv_cols = _conv_block(xp_scr, cw_ref, cb_ref, tb, first)

    @pl.when(first)
    def _():
        s_scr[...] = jnp.zeros(s_scr.shape, f32)

    def proj_stages():
        for c0 in range(0, GDN_CONV_DIM, GDN_KW):
            xp_scr[8:8 + tb, c0:c0 + GDN_KW] = _proj(u, wm_ref, c0, c0 + GDN_KW)
            yield
        small = _proj(u, ws_ref, 0, LANES)
        beta_scr[...] = _sigmoid(small)
        lg_scr[...] = hp_ref[0:1, :] * _softplus(small + hp_ref[1:2, :])
        gate_scr[...] = _proj(u, wm_ref, E_OFF_GATE, E_OFF_GATE + GDN_VW)
        yield

    span = GDN_PREP_CHUNKS * CHUNK

    def conv_norm(it):
        r0 = it * span
        rr = slice(r0, r0 + span)
        for h in range(GDN_HEADS):
            c0 = h * GDN_DK
            q = conv_cols(c0, c0 + GDN_DK, r0, span)
            q = q * lax.rsqrt(jnp.sum(q * q, axis=-1, keepdims=True) + EPS) * (GDN_DK ** -0.5)
            qkv_scr[rr, c0:c0 + GDN_DK] = q
            k = conv_cols(GDN_KW + c0, GDN_KW + c0 + GDN_DK, r0, span)
            k = k * lax.rsqrt(jnp.sum(k * k, axis=-1, keepdims=True) + EPS)
            qkv_scr[rr, GDN_KW + c0:GDN_KW + c0 + GDN_DK] = k
            qkv_scr[rr, 2 * GDN_KW + c0:2 * GDN_KW + c0 + GDN_DV] = conv_cols(
                2 * GDN_KW + c0, 2 * GDN_KW + c0 + GDN_DV, r0, span)
            yield

    incl = _tril(CHUNK)
    strict = _tril(CHUNK, strict=True)
    eye = jnp.logical_and(incl, jnp.logical_not(strict)).astype(f32)
    sr = lax.broadcasted_iota(jnp.int32, (span, span), 0)
    sc = lax.broadcasted_iota(jnp.int32, (span, span), 1)
    lmat = jnp.logical_and(sr >= sc, sr // CHUNK == sc // CHUNK).astype(f32)
    ng = ng_ref[...]
    heads = range(GDN_HEADS)
    probs = [(ci, h) for ci in range(GDN_PREP_CHUNKS) for h in heads]
    n = range(len(probs))

    def prep(c):
        r0 = c * span
        gam_all = _sel_mm(lmat, lg_scr[pl.ds(r0, span), :])
        gam_t = gam_all.T
        beta_all = beta_scr[pl.ds(r0, span), :]
        rows, gcol, grow, glast, bcol = [], [], [], [], []
        for ci, h in probs:
            lo, hi = ci * CHUNK, (ci + 1) * CHUNK
            rows.append(pl.ds(r0 + lo, CHUNK))
            gcol.append(gam_all[lo:hi, SMALL_A + h:SMALL_A + h + 1])
            grow.append(gam_t[SMALL_A + h:SMALL_A + h + 1, lo:hi])
            glast.append(gam_all[hi - 1:hi, SMALL_A + h:SMALL_A + h + 1])
            bcol.append(beta_all[lo:hi, SMALL_BETA + h:SMALL_BETA + h + 1])
        for ci in range(GDN_PREP_CHUNKS):
            gl_scr[pl.ds(c * GDN_PREP_CHUNKS + ci, 1), :] = jnp.exp(gam_all[(ci + 1) * CHUNK - 1:(ci + 1) * CHUNK, :])
        q = [qkv_scr[rows[j], h * GDN_DK:(h + 1) * GDN_DK] for j, (_, h) in enumerate(probs)]
        k = [qkv_scr[rows[j], GDN_KW + h * GDN_DK:GDN_KW + (h + 1) * GDN_DK] for j, (_, h) in enumerate(probs)]
        v = [qkv_scr[rows[j], 2 * GDN_KW + h * GDN_DV:2 * GDN_KW + (h + 1) * GDN_DV] for j, (_, h) in enumerate(probs)]
        kb = [k[j].astype(MM_DTYPE) for j in n]
        kk = [_mm_nt(kb[j], kb[j]) for j in n]
        yield
        qk = [_mm_nt(q[j], kb[j]) for j in n]
        yield
        decay = [_masked_decay(gcol[j], grow[j], incl) for j in n]
        eg = [jnp.exp(gcol[j]) for j in n]
        for j, (_, h) in enumerate(probs):
            c0 = h * GDN_DK
            qg_scr[rows[j], c0:c0 + GDN_DK] = (eg[j] * q[j]).astype(MM_DTYPE)
            kd_scr[rows[j], c0:c0 + GDN_DK] = (jnp.exp(glast[j] - gcol[j]) * k[j]).astype(MM_DTYPE)
            p_scr[h, rows[j], :] = (qk[j] * decay[j]).astype(MM_DTYPE)
        nmat = [jnp.where(strict, bcol[j] * (kk[j] * decay[j]), 0.0) for j in n]
        tinv = [eye - nmat[j] for j in n]
        npow = nmat
        for _ in range(5):
            npow = [_mm(npow[j], npow[j]) for j in n]
            yield
            tinv = [tinv[j] + _mm(tinv[j], npow[j]) for j in n]
            yield
        x = [_mm(tinv[j], jnp.concatenate([bcol[j] * v[j], (bcol[j] * eg[j]) * k[j]], axis=-1)) for j in n]
        for j, (_, h) in enumerate(probs):
            c0 = h * GDN_DK
            up_scr[rows[j], c0:c0 + GDN_DV] = x[j][:, :GDN_DV]
            w_scr[rows[j], c0:c0 + GDN_DK] = x[j][:, GDN_DV:].astype(MM_DTYPE)

    state = [s_scr[h] for h in heads]

    def scan(c):
        rows = pl.ds(c * CHUNK, CHUNK)
        gl = gl_scr[c:c + 1, :]
        r = [_mm(jnp.concatenate([w_scr[rows, h * GDN_DK:(h + 1) * GDN_DK],
                                  qg_scr[rows, h * GDN_DK:(h + 1) * GDN_DK]], axis=0), state[h]) for h in heads]
        yield
        us = [(up_scr[rows, h * GDN_DV:(h + 1) * GDN_DV] - r[h][:CHUNK]).astype(MM_DTYPE) for h in heads]
        o = [r[h][CHUNK:] + _mm(p_scr[h, rows, :], us[h]) for h in heads]
        yield
        upd = [_mm_tn(kd_scr[rows, h * GDN_DK:(h + 1) * GDN_DK], us[h]) for h in heads]
        yield
        for h in heads:
            c0 = h * GDN_DV
            state[h] = gl[:, SMALL_A + h:SMALL_A + h + 1] * state[h] + upd[h]
            gate = gate_scr[rows, c0:c0 + GDN_DV]
            o_ref[rows, c0:c0 + GDN_DV] = (_rms(o[h], GDN_DV) * ng * _silu(gate)).astype(o_ref.dtype)
        yield

    def scans(chunks):
        for c in chunks:
            yield from scan(c)

    def side_work(it):
        if it + 1 < n_prep:
            yield from conv_norm(it + 1)
        if it >= 1:
            yield from scans(range((it - 1) * GDN_PREP_CHUNKS, it * GDN_PREP_CHUNKS))

    n_prep = tb // span

    def stages():
        yield from proj_stages()
        yield from conv_norm(0)
        for it in range(n_prep):
            yield from _interleave(prep(it), side_work(it), 1)
        yield from scans(range((n_prep - 1) * GDN_PREP_CHUNKS, n_prep * GDN_PREP_CHUNKS))

    def finish():
        for h in heads:
            s_scr[h] = state[h]

        @pl.when(last)
        def _():
            conv_out_ref[...] = xp_scr[tb + 5:tb + 8, :]
            s_out_ref[...] = s_scr[...]

    return stages(), finish


def _gla_stages(u, first, last, wq_ref, wk_ref, wv_ref, wg_ref, ws_ref, wlr_ref, blr_ref, ng_ref,
                o_ref, s_out_ref, q_ref, k_ref, v_ref, gate_ref, la_scr, st_scr, *, tb):
    @pl.when(first)
    def _():
        st_scr[...] = jnp.zeros(st_scr.shape, f32)

    def proj_stages():
        q_ref[...] = _proj(u, wq_ref, 0, GLA_KW)
        k_ref[...] = _proj(u, wk_ref, 0, GLA_KW)
        small = _proj(u, ws_ref, 0, LANES)
        z = _mm(small, wlr_ref[...]) + blr_ref[...]
        la_scr[...] = -_softplus(-z) * (1.0 / GLA_TAU)
        yield
        v_ref[...] = _proj(u, wv_ref, 0, GLA_VW).astype(v_ref.dtype)
        yield
        gate_ref[...] = _proj(u, wg_ref, 0, GLA_VW)
        yield

    incl = _tril(CHUNK)
    lmat = incl.astype(f32)
    ng = ng_ref[...]

    heads = range(GLA_HEADS)
    n_chunks = tb // CHUNK
    staged = [None] * n_chunks
    state = [st_scr[h] for h in heads]

    def prep(c):
        rows = pl.ds(c * CHUNK, CHUNK)
        bcum = [_sel_mm(lmat, la_scr[rows, h * GLA_DK:(h + 1) * GLA_DK]) for h in heads]
        yield
        k = [k_ref[rows, h * GLA_DK:(h + 1) * GLA_DK] for h in heads]
        v = [v_ref[rows, h * GLA_DV:(h + 1) * GLA_DV].astype(MM_DTYPE) for h in heads]
        qe = [(q_ref[rows, h * GLA_DK:(h + 1) * GLA_DK] * (GLA_DK ** -0.5) * jnp.exp(bcum[h])).astype(MM_DTYPE)
              for h in heads]
        ke = [k[h] * jnp.exp(-bcum[h]) for h in heads]
        blast = [bcum[h][CHUNK - 1:CHUNK, :] for h in heads]
        kdec = [(k[h] * jnp.exp(blast[h] - bcum[h])).astype(MM_DTYPE) for h in heads]
        p = [jnp.where(incl, _mm_nt(qe[h], ke[h]), 0.0) for h in heads]
        yield
        o_intra = [_mm(p[h], v[h]) for h in heads]
        staged[c] = (qe, v, kdec, blast, o_intra)
        yield

    def scan(c):
        rows = pl.ds(c * CHUNK, CHUNK)
        qe, v, kdec, blast, o_intra = staged[c]
        o = [o_intra[h] + _mm_nt(qe[h], state[h]) for h in heads]
        upd = [_mm_tn(v[h], kdec[h]) for h in heads]
        yield
        for h in heads:
            v0 = h * GLA_DV
            state[h] = jnp.exp(blast[h]) * state[h] + upd[h]
            gate = gate_ref[rows, v0:v0 + GLA_DV]
            o_ref[rows, GDN_VW + v0:GDN_VW + v0 + GLA_DV] = (
                _rms(o[h], GLA_DV) * ng * _silu(gate)).astype(o_ref.dtype)
        yield

    def stages():
        yield from proj_stages()
        yield from prep(0)
        for c in range(1, n_chunks):
            yield from _interleave(prep(c), scan(c - 1), 1)
        yield from scan(n_chunks - 1)

    def finish():
        for h in heads:
            st_scr[h] = state[h]

        @pl.when(last)
        def _():
            for h in heads:
                s_out_ref[h] = st_scr[h].T

    return stages(), finish


N_GDN_SCRATCH = 12
EVEN_STAGE_RATIO = 2


def _even_prompt_kernel(x_ref, pg_ref, wm_ref, ws_ref, wq_ref, wk_ref, wv_ref, wg_ref, cw_ref, cb_ref, hp_ref,
                        ngd_ref, wlr_ref, blr_ref, ngl_ref,
                        o_ref, conv_out_ref, sgdn_out_ref, sgla_out_ref, *scratch, tb):
    i = pl.program_id(1)
    first = i == 0
    last = i == pl.num_programs(1) - 1
    u = _normed(x_ref, pg_ref)
    gdn, gdn_finish = _gdn_stages(u, first, last, wm_ref, ws_ref, cw_ref, cb_ref, hp_ref, ngd_ref,
                                  o_ref, conv_out_ref, sgdn_out_ref, *scratch[:N_GDN_SCRATCH], tb=tb)
    gla, gla_finish = _gla_stages(u, first, last, wq_ref, wk_ref, wv_ref, wg_ref, ws_ref, wlr_ref, blr_ref, ngl_ref,
                                  o_ref, sgla_out_ref, *scratch[N_GDN_SCRATCH:], tb=tb)
    for _ in _interleave(gdn, gla, EVEN_STAGE_RATIO):
        pass
    gdn_finish()
    gla_finish()


def _even_prompt(x, pre_g, w_in, cw, cb, hp, ng_gdn, wlr, blr, ng_gla, tb):
    b, l, d = x.shape
    assert tb % (GDN_PREP_CHUNKS * CHUNK) == 0 and l % tb == 0
    kern = functools.partial(_even_prompt_kernel, tb=tb)
    wcol = lambda blk: (lambda bi, i: (0, blk))
    const = lambda bi, i: (0, 0)
    gdn_scratch = [pltpu.VMEM((tb + 8, GDN_CONV_DIM), f32),
                   pltpu.VMEM((tb, GDN_CONV_DIM), f32),
                   pltpu.VMEM((tb, GDN_VW), f32),
                   pltpu.VMEM((tb, LANES), f32),
                   pltpu.VMEM((tb, LANES), f32),
                   pltpu.VMEM((GDN_HEADS, GDN_DK, GDN_DV), f32),
                   pltpu.VMEM((max(8, tb // CHUNK), LANES), f32),
                   pltpu.VMEM((tb, GDN_VW), f32),
                   pltpu.VMEM((tb, GDN_KW), MM_DTYPE),
                   pltpu.VMEM((tb, GDN_KW), MM_DTYPE),
                   pltpu.VMEM((tb, GDN_KW), MM_DTYPE),
                   pltpu.VMEM((GDN_HEADS, tb, CHUNK), MM_DTYPE)]
    assert len(gdn_scratch) == N_GDN_SCRATCH
    gla_scratch = [pltpu.VMEM((tb, GLA_KW), f32),
                   pltpu.VMEM((tb, GLA_KW), f32),
                   pltpu.VMEM((tb, GLA_VW), MM_DTYPE),
                   pltpu.VMEM((tb, GLA_VW), f32),
                   pltpu.VMEM((tb, GLA_KW), f32),
                   pltpu.VMEM((GLA_HEADS, GLA_DV, GLA_DK), f32)]
    return pl.pallas_call(
        kern,
        grid=(b, l // tb),
        in_specs=[pl.BlockSpec((None, tb, d), lambda bi, i: (bi, i, 0)),
                  pl.BlockSpec((1, d), const),
                  pl.BlockSpec((d, E_OFF_LQ), const),
                  pl.BlockSpec((d, LANES), wcol(E_OFF_SMALL // LANES)),
                  pl.BlockSpec((d, GLA_KW), wcol(E_OFF_LQ // GLA_KW)),
                  pl.BlockSpec((d, GLA_KW), wcol(E_OFF_LK // GLA_KW)),
                  pl.BlockSpec((d, GLA_VW), wcol(E_OFF_LV // GLA_VW)),
                  pl.BlockSpec((d, GLA_VW), wcol(E_OFF_LGATE // GLA_VW)),
                  pl.BlockSpec((CONV_W, GDN_CONV_DIM), const),
                  pl.BlockSpec((1, GDN_CONV_DIM), const),
                  pl.BlockSpec((8, LANES), const),
                  pl.BlockSpec((1, GDN_DV), const),
                  pl.BlockSpec((LANES, GLA_KW), const),
                  pl.BlockSpec((1, GLA_KW), const),
                  pl.BlockSpec((1, GLA_DV), const)],
        out_specs=[pl.BlockSpec((None, tb, GDN_VW + GLA_VW), lambda bi, i: (bi, i, 0)),
                   pl.BlockSpec((None, CONV_W - 1, GDN_CONV_DIM), lambda bi, i: (bi, 0, 0)),
                   pl.BlockSpec((None, GDN_HEADS, GDN_DK, GDN_DV), lambda bi, i: (bi, 0, 0, 0)),
                   pl.BlockSpec((None, GLA_HEADS, GLA_DK, GLA_DV), lambda bi, i: (bi, 0, 0, 0))],
        out_shape=[jax.ShapeDtypeStruct((b, l, GDN_VW + GLA_VW), MM_DTYPE),
                   jax.ShapeDtypeStruct((b, CONV_W - 1, GDN_CONV_DIM), f32),
                   jax.ShapeDtypeStruct((b, GDN_HEADS, GDN_DK, GDN_DV), f32),
                   jax.ShapeDtypeStruct((b, GLA_HEADS, GLA_DK, GLA_DV), f32)],
        scratch_shapes=gdn_scratch + gla_scratch,
        compiler_params=pltpu.CompilerParams(dimension_semantics=("parallel", "arbitrary"),
                                             vmem_limit_bytes=VMEM_LIMIT),
        name="even_prompt",
    )(x, pre_g, w_in, w_in, w_in, w_in, w_in, w_in, cw, cb, hp, ng_gdn, wlr, blr, ng_gla)


def _ssd_prompt_kernel(x_ref, pg_ref, wx_ref, wz_ref, wbc_ref, wdt_ref, cw_ref, cb_ref, hp_ref, dskip_ref, ng_ref,
                       expand_ref, y_ref, conv_out_ref, h_out_ref,
                       xp_scr, xbc_scr, z_ref, lam_scr, lamx_scr, xdt_scr, h_scr, *, tb):
    i = pl.program_id(1)
    first = i == 0
    conv_cols = _conv_block(xp_scr, cw_ref, cb_ref, tb, first)

    @pl.when(first)
    def _():
        h_scr[...] = jnp.zeros(h_scr.shape, f32)

    u = _normed(x_ref, pg_ref)
    for c0 in range(0, SSD_DI, 1024):
        xp_scr[8:8 + tb, c0:c0 + 1024] = _proj(u, wx_ref, c0, c0 + 1024)
    xp_scr[8:8 + tb, SSD_DI:SSD_CONV_DIM] = _proj(u, wbc_ref, 0, 2 * SSD_BC)
    dt_raw = _proj(u, wdt_ref, 0, LANES)
    for c0 in range(0, SSD_DI, 1024):
        z_ref[:, c0:c0 + 1024] = _proj(u, wz_ref, c0, c0 + 1024)

    for c0 in range(0, SSD_CONV_DIM, 512):
        xbc_scr[:, c0:c0 + 512] = conv_cols(c0, c0 + 512)
    dt = _softplus(dt_raw + hp_ref[0:1, :])
    tr = lax.broadcasted_iota(jnp.int32, (tb, tb), 0)
    tc = lax.broadcasted_iota(jnp.int32, (tb, tb), 1)
    lam_all = _sel_mm(jnp.logical_and(tr >= tc, tr // CHUNK == tc // CHUNK).astype(f32), dt * hp_ref[1:2, :])
    lam_scr[...] = lam_all
    lo_blk = lax.broadcasted_iota(jnp.int32, (tb, LANES), 1) < SSD_P
    for p in range(SSD_HEADS // 2):
        cols = slice(p * LANES, (p + 1) * LANES)
        k0 = 2 * p
        xdt_scr[:, cols] = xbc_scr[:, cols] * jnp.where(lo_blk, dt[:, k0:k0 + 1], dt[:, k0 + 1:k0 + 2])
    for c0 in range(0, SSD_DI, 512):
        lamx_scr[:, c0:c0 + 512] = _sel_rows(lam_all, expand_ref[:, c0:c0 + 512])

    lane = lax.broadcasted_iota(jnp.int32, (CHUNK, LANES), 1)
    lo = lane < SSD_P
    incl2 = lax.broadcasted_iota(jnp.int32, (CHUNK, LANES), 0) >= lane % CHUNK
    top = lax.broadcasted_iota(jnp.int32, (2 * SSD_P, SSD_N), 0) < SSD_P
    n_pairs = SSD_HEADS // 2
    ppg = SSD_K // 2
    groups = range(SSD_G)

    def chunk_body(c):
        rows = pl.ds(c * CHUNK, CHUNK)
        lam = lam_scr[rows, :]
        lam_t = lam.T
        llast = lam[CHUNK - 1:CHUNK, :]
        bm = [xbc_scr[rows, SSD_DI + g * SSD_N:SSD_DI + (g + 1) * SSD_N].astype(MM_DTYPE) for g in groups]
        cm = [xbc_scr[rows, SSD_DI + SSD_BC + g * SSD_N:SSD_DI + SSD_BC + (g + 1) * SSD_N].astype(MM_DTYPE)
              for g in groups]
        h_old = [h_scr[g * SSD_GW:(g + 1) * SSD_GW, :] for g in groups]
        cb2 = [_mm_nt(cm[g], jnp.concatenate([bm[g], bm[g]], axis=0)) for g in groups]
        y_inter = [_mm_nt(cm[g], h_old[g]) for g in groups]
        xs, xdt, xdec, lcol = [], [], [], []
        for p in range(n_pairs):
            cols = slice(p * LANES, (p + 1) * LANES)
            xs.append(xbc_scr[rows, cols])
            lcol.append(lamx_scr[rows, cols])
            xdt.append(xdt_scr[rows, cols])
            xdec.append((xdt[p] * jnp.exp(lcol[p][CHUNK - 1:CHUNK, :] - lcol[p])).astype(MM_DTYPE))
        upd = [_mm_tn(jnp.concatenate(xdec[g * ppg:(g + 1) * ppg], axis=-1), bm[g]) for g in groups]
        y_intra = []
        for p in range(n_pairs):
            k0 = 2 * p
            lrow = jnp.concatenate([lam_t[k0:k0 + 1, :], lam_t[k0 + 1:k0 + 2, :]], axis=-1)
            m = cb2[p // ppg] * _masked_decay(lcol[p], lrow, incl2)
            blockdiag = jnp.concatenate([jnp.where(lo, xdt[p], 0.0), jnp.where(lo, 0.0, xdt[p])], axis=0)
            y_intra.append(_mm(m, blockdiag))
        for g in groups:
            ys = []
            for pp in range(ppg):
                p = g * ppg + pp
                k0 = 2 * p
                ys.append(y_intra[p] + jnp.exp(lcol[p]) * y_inter[g][:, pp * LANES:(pp + 1) * LANES]
                          + dskip_ref[:, p * LANES:(p + 1) * LANES] * xs[p])
                el = jnp.where(top, jnp.exp(llast[:, k0:k0 + 1]), jnp.exp(llast[:, k0 + 1:k0 + 2]))
                h_scr[p * LANES:(p + 1) * LANES, :] = (el * h_old[g][pp * LANES:(pp + 1) * LANES, :]
                                                        + upd[g][pp * LANES:(pp + 1) * LANES, :])
            yg = jnp.concatenate(ys, axis=-1) * _silu(z_ref[rows, g * SSD_GW:(g + 1) * SSD_GW])
            y_ref[rows, g * SSD_GW:(g + 1) * SSD_GW] = (
                _rms(yg, SSD_GW) * ng_ref[:, g * SSD_GW:(g + 1) * SSD_GW]).astype(y_ref.dtype)

    for c in range(tb // CHUNK):
        chunk_body(c)

    @pl.when(i == pl.num_programs(1) - 1)
    def _():
        conv_out_ref[...] = xp_scr[tb + 5:tb + 8, :]
        h_out_ref[...] = h_scr[...]


def _ssd_prompt(x, pre_g, w_in, cw, cb, hp, dskip, ng, expand, tb):
    b, l, d = x.shape
    kern = functools.partial(_ssd_prompt_kernel, tb=tb)
    wcol = lambda blk: (lambda bi, i: (0, blk))
    const = lambda bi, i: (0, 0)
    return pl.pallas_call(
        kern,
        grid=(b, l // tb),
        in_specs=[pl.BlockSpec((None, tb, d), lambda bi, i: (bi, i, 0)),
                  pl.BlockSpec((1, d), const),
                  pl.BlockSpec((d, SSD_DI), wcol(0)),
                  pl.BlockSpec((d, SSD_DI), wcol(O_OFF_Z // SSD_DI)),
                  pl.BlockSpec((d, 2 * SSD_BC), wcol(O_OFF_BC // (2 * SSD_BC))),
                  pl.BlockSpec((d, LANES), wcol(O_OFF_DT // LANES)),
                  pl.BlockSpec((CONV_W, SSD_CONV_DIM), const),
                  pl.BlockSpec((1, SSD_CONV_DIM), const),
                  pl.BlockSpec((8, LANES), const),
                  pl.BlockSpec((1, SSD_DI), const),
                  pl.BlockSpec((1, SSD_DI), const),
                  pl.BlockSpec((LANES, SSD_DI), const)],
        out_specs=[pl.BlockSpec((None, tb, SSD_DI), lambda bi, i: (bi, i, 0)),
                   pl.BlockSpec((None, CONV_W - 1, SSD_CONV_DIM), lambda bi, i: (bi, 0, 0)),
                   pl.BlockSpec((None, SSD_HEADS * SSD_P, SSD_N), lambda bi, i: (bi, 0, 0))],
        out_shape=[jax.ShapeDtypeStruct((b, l, SSD_DI), MM_DTYPE),
                   jax.ShapeDtypeStruct((b, CONV_W - 1, SSD_CONV_DIM), f32),
                   jax.ShapeDtypeStruct((b, SSD_HEADS * SSD_P, SSD_N), f32)],
        scratch_shapes=[pltpu.VMEM((tb + 8, SSD_CONV_DIM), f32),
                        pltpu.VMEM((tb, SSD_CONV_DIM), f32),
                        pltpu.VMEM((tb, SSD_DI), f32),
                        pltpu.VMEM((tb, LANES), f32),
                        pltpu.VMEM((tb, SSD_DI), f32),
                        pltpu.VMEM((tb, SSD_DI), f32),
                        pltpu.VMEM((SSD_HEADS * SSD_P, SSD_N), f32)],
        compiler_params=pltpu.CompilerParams(dimension_semantics=("parallel", "arbitrary"),
                                             vmem_limit_bytes=VMEM_LIMIT),
        name="ssd_prompt",
    )(x, pre_g, w_in, w_in, w_in, w_in, cw, cb, hp, dskip, ng, expand)


def _pad_rows(row, n):
    return jnp.concatenate([row, jnp.zeros((n - 1, row.shape[1]), row.dtype)], axis=0)


def _bcast_cols(row, width):
    hi, mid, lo = _split3(row)
    lhs = jnp.concatenate([hi, mid, lo, jnp.zeros((13, row.shape[1]), bf16)], axis=0)
    sel = (lax.broadcasted_iota(jnp.int32, (16, width), 0) < 3).astype(bf16)
    return lax.dot_general(lhs, sel, (((0,), (0,)), ((), ())), preferred_element_type=f32)


def _conv_step(buf_ref, x, cw_ref, cb_ref, new_buf_ref):
    acc = cb_ref[...] + x * cw_ref[3:4, :]
    for s in range(CONV_W - 1):
        acc = acc + buf_ref[s] * cw_ref[s:s + 1, :]
    for s in range(CONV_W - 2):
        new_buf_ref[s] = buf_ref[s + 1]
    new_buf_ref[CONV_W - 2] = x
    return _silu(acc)


def _gdn_decode_kernel(q_ref, k_ref, v_ref, gate_ref, small_ref, buf_ref, s_ref, cw_ref, cb_ref, hp_ref, ng_ref,
                       o_ref, new_buf_ref, s_out_ref, *, bb):
    x = jnp.concatenate([q_ref[...], k_ref[...], v_ref[...]], axis=-1)
    qkv = _conv_step(buf_ref, x, cw_ref, cb_ref, new_buf_ref)
    small = small_ref[...]
    beta = _sigmoid(small)
    eg = jnp.exp(hp_ref[0:1, :] * _softplus(small + hp_ref[1:2, :]))
    ng = ng_ref[...]
    gate = gate_ref[...]
    for h in range(GDN_HEADS):
        c0 = h * GDN_DK
        q = qkv[:, c0:c0 + GDN_DK]
        q = q * lax.rsqrt(jnp.sum(q * q, axis=-1, keepdims=True) + EPS) * (GDN_DK ** -0.5)
        k = qkv[:, GDN_KW + c0:GDN_KW + c0 + GDN_DK]
        k = k * lax.rsqrt(jnp.sum(k * k, axis=-1, keepdims=True) + EPS)
        v = qkv[:, 2 * GDN_KW + c0:2 * GDN_KW + c0 + GDN_DV]
        qk = jnp.sum(q * k, axis=-1, keepdims=True)
        b_h = beta[:, SMALL_BETA + h:SMALL_BETA + h + 1]
        eg_h = eg[:, SMALL_A + h:SMALL_A + h + 1]
        seqs = range(bb)
        s_old = [s_ref[b, h] for b in seqs]
        r = [_mm(jnp.concatenate([k[b:b + 1], q[b:b + 1], jnp.zeros((6, GDN_DK), f32)], axis=0), s_old[b])
             for b in seqs]
        u = [b_h[b:b + 1] * v[b:b + 1] - (b_h[b:b + 1] * eg_h[b:b + 1]) * r[b][0:1] for b in seqs]
        upd = [_mm_tn(_pad_rows(k[b:b + 1], 16), _pad_rows(u[b], 16)) for b in seqs]
        for b in seqs:
            s_out_ref[b, h] = eg_h[b:b + 1] * s_old[b] + upd[b]
        o = jnp.concatenate([eg_h[b:b + 1] * r[b][1:2] + qk[b:b + 1] * u[b] for b in seqs], axis=0)
        o_ref[:, c0:c0 + GDN_DV] = (_rms(o, GDN_DV) * ng * _silu(gate[:, c0:c0 + GDN_DV])).astype(o_ref.dtype)


def _gdn_decode(proj, buf, s, cw, cb, hp, ng, bb):
    b = proj.shape[0]
    wq = GDN_KW
    kern = functools.partial(_gdn_decode_kernel, bb=bb)
    col = lambda blk: (lambda i: (i, blk))
    const = lambda i: (0, 0)
    return pl.pallas_call(
        kern,
        grid=(b // bb,),
        in_specs=[pl.BlockSpec((bb, wq), col(0)),
                  pl.BlockSpec((bb, wq), col(1)),
                  pl.BlockSpec((bb, wq), col(2)),
                  pl.BlockSpec((bb, wq), col(3)),
                  pl.BlockSpec((bb, LANES), col(E_OFF_SMALL // LANES)),
                  pl.BlockSpec((CONV_W - 1, bb, GDN_CONV_DIM), lambda i: (0, i, 0)),
                  pl.BlockSpec((bb, GDN_HEADS, GDN_DK, GDN_DV), lambda i: (i, 0, 0, 0)),
                  pl.BlockSpec((CONV_W, GDN_CONV_DIM), const),
                  pl.BlockSpec((1, GDN_CONV_DIM), const),
                  pl.BlockSpec((8, LANES), const),
                  pl.BlockSpec((1, GDN_DV), const)],
        out_specs=[pl.BlockSpec((bb, GDN_VW), col(0)),
                   pl.BlockSpec((CONV_W - 1, bb, GDN_CONV_DIM), lambda i: (0, i, 0)),
                   pl.BlockSpec((bb, GDN_HEADS, GDN_DK, GDN_DV), lambda i: (i, 0, 0, 0))],
        out_shape=[jax.ShapeDtypeStruct((b, GDN_VW), MM_DTYPE),
                   jax.ShapeDtypeStruct((CONV_W - 1, b, GDN_CONV_DIM), f32),
                   jax.ShapeDtypeStruct((b, GDN_HEADS, GDN_DK, GDN_DV), f32)],
        compiler_params=pltpu.CompilerParams(dimension_semantics=("parallel",), vmem_limit_bytes=VMEM_LIMIT),
        name="gdn_decode",
    )(proj, proj, proj, proj, proj, buf, s, cw, cb, hp, ng)


def _gla_decode_kernel(q_ref, k_ref, v_ref, gate_ref, small_ref, s_ref, wlr_ref, blr_ref, ng_ref,
                       o_ref, s_out_ref, *, bb):
    z = _mm(small_ref[...], wlr_ref[...]) + blr_ref[...]
    la = -_softplus(-z) * (1.0 / GLA_TAU)
    dec = jnp.exp(la)
    qe = q_ref[...] * (GLA_DK ** -0.5) * dec
    kx = k_ref[...]
    ke = kx * jnp.exp(-la)
    vx = v_ref[...]
    ng = ng_ref[...]
    gate = gate_ref[...]
    for h in range(GLA_HEADS):
        k0 = h * GLA_DK
        v0 = h * GLA_DV
        p = jnp.sum(qe[:, k0:k0 + GLA_DK] * ke[:, k0:k0 + GLA_DK], axis=-1, keepdims=True)
        outs = []
        for b in range(bb):
            s_old = s_ref[b, h]
            vrow = vx[b:b + 1, v0:v0 + GLA_DV]
            outs.append(_mm(_pad_rows(qe[b:b + 1, k0:k0 + GLA_DK], 8), s_old)[0:1] + p[b:b + 1] * vrow)
            s_out_ref[b, h] = (_bcast_cols(dec[b:b + 1, k0:k0 + GLA_DK], GLA_DV) * s_old
                               + _mm_tn(_pad_rows(kx[b:b + 1, k0:k0 + GLA_DK], 16), _pad_rows(vrow, 16)))
        o = jnp.concatenate(outs, axis=0)
        o_ref[:, v0:v0 + GLA_DV] = (_rms(o, GLA_DV) * ng * _silu(gate[:, v0:v0 + GLA_DV])).astype(o_ref.dtype)


def _gla_decode(proj, s, wlr, blr, ng, bb):
    b = proj.shape[0]
    kern = functools.partial(_gla_decode_kernel, bb=bb)
    col = lambda blk: (lambda i: (i, blk))
    const = lambda i: (0, 0)
    return pl.pallas_call(
        kern,
        grid=(b // bb,),
        in_specs=[pl.BlockSpec((bb, GLA_KW), col(E_OFF_LQ // GLA_KW)),
                  pl.BlockSpec((bb, GLA_KW), col(E_OFF_LK // GLA_KW)),
                  pl.BlockSpec((bb, GLA_VW), col(E_OFF_LV // GLA_VW)),
                  pl.BlockSpec((bb, GLA_VW), col(E_OFF_LGATE // GLA_VW)),
                  pl.BlockSpec((bb, LANES), col(E_OFF_SMALL // LANES)),
                  pl.BlockSpec((bb, GLA_HEADS, GLA_DK, GLA_DV), lambda i: (i, 0, 0, 0)),
                  pl.BlockSpec((LANES, GLA_KW), const),
                  pl.BlockSpec((1, GLA_KW), const),
                  pl.BlockSpec((1, GLA_DV), const)],
        out_specs=[pl.BlockSpec((bb, GLA_VW), col(0)),
                   pl.BlockSpec((bb, GLA_HEADS, GLA_DK, GLA_DV), lambda i: (i, 0, 0, 0))],
        out_shape=[jax.ShapeDtypeStruct((b, GLA_VW), MM_DTYPE),
                   jax.ShapeDtypeStruct((b, GLA_HEADS, GLA_DK, GLA_DV), f32)],
        compiler_params=pltpu.CompilerParams(dimension_semantics=("parallel",), vmem_limit_bytes=VMEM_LIMIT),
        name="gla_decode",
    )(proj, proj, proj, proj, proj, s, wlr, blr, ng)


def _ssd_decode_kernel(x_ref, z_ref, bc_ref, dt_ref, buf_ref, h_ref, cw_ref, cb_ref, hp_ref, dskip_ref, ng_ref,
                       expand_ref, y_ref, new_buf_ref, h_out_ref, *, bb):
    xin = jnp.concatenate([x_ref[...], bc_ref[...]], axis=-1)
    xbc = _conv_step(buf_ref, xin, cw_ref, cb_ref, new_buf_ref)
    xs = xbc[:, :SSD_DI]
    dt = _softplus(dt_ref[...] + hp_ref[0:1, :])
    el = jnp.exp(dt * hp_ref[1:2, :])
    expand = expand_ref[...]
    dt_x = _sel_rows(dt, expand)
    el_x = _sel_rows(el, expand)
    xdt = xs * dt_x
    z = z_ref[...]
    for g in range(SSD_G):
        gs = slice(g * SSD_GW, (g + 1) * SSD_GW)
        bm = xbc[:, SSD_DI + g * SSD_N:SSD_DI + (g + 1) * SSD_N]
        cm = xbc[:, SSD_DI + SSD_BC + g * SSD_N:SSD_DI + SSD_BC + (g + 1) * SSD_N]
        cbs = jnp.sum(cm * bm, axis=-1, keepdims=True)
        outs = []
        for b in range(bb):
            hg = h_ref[b, gs, :]
            y_inter = _mm_nt(_pad_rows(cm[b:b + 1], 8), hg)[0:1]
            outs.append(el_x[b:b + 1, gs] * y_inter)
            h_out_ref[b, gs, :] = (_bcast_cols(el_x[b:b + 1, gs], SSD_N) * hg
                                   + _mm_tn(_pad_rows(xdt[b:b + 1, gs], 16), _pad_rows(bm[b:b + 1], 16)))
        y = jnp.concatenate(outs, axis=0) + cbs * xdt[:, gs] + dskip_ref[:, gs] * xs[:, gs]
        yg = y * _silu(z[:, gs])
        y_ref[:, gs] = (_rms(yg, SSD_GW) * ng_ref[:, gs]).astype(y_ref.dtype)


def _ssd_decode(proj, buf, hst, cw, cb, hp, dskip, ng, expand, bb):
    b = proj.shape[0]
    kern = functools.partial(_ssd_decode_kernel, bb=bb)
    col = lambda blk: (lambda i: (i, blk))
    const = lambda i: (0, 0)
    return pl.pallas_call(
        kern,
        grid=(b // bb,),
        in_specs=[pl.BlockSpec((bb, SSD_DI), col(0)),
                  pl.BlockSpec((bb, SSD_DI), col(O_OFF_Z // SSD_DI)),
                  pl.BlockSpec((bb, 2 * SSD_BC), col(O_OFF_BC // (2 * SSD_BC))),
                  pl.BlockSpec((bb, LANES), col(O_OFF_DT // LANES)),
                  pl.BlockSpec((CONV_W - 1, bb, SSD_CONV_DIM), lambda i: (0, i, 0)),
                  pl.BlockSpec((bb, SSD_HEADS * SSD_P, SSD_N), lambda i: (i, 0, 0)),
                  pl.BlockSpec((CONV_W, SSD_CONV_DIM), const),
                  pl.BlockSpec((1, SSD_CONV_DIM), const),
                  pl.BlockSpec((8, LANES), const),
                  pl.BlockSpec((1, SSD_DI), const),
                  pl.BlockSpec((1, SSD_DI), const),
                  pl.BlockSpec((LANES, SSD_DI), const)],
        out_specs=[pl.BlockSpec((bb, SSD_DI), col(0)),
                   pl.BlockSpec((CONV_W - 1, bb, SSD_CONV_DIM), lambda i: (0, i, 0)),
                   pl.BlockSpec((bb, SSD_HEADS * SSD_P, SSD_N), lambda i: (i, 0, 0))],
        out_shape=[jax.ShapeDtypeStruct((b, SSD_DI), MM_DTYPE),
                   jax.ShapeDtypeStruct((CONV_W - 1, b, SSD_CONV_DIM), f32),
                   jax.ShapeDtypeStruct((b, SSD_HEADS * SSD_P, SSD_N), f32)],
        compiler_params=pltpu.CompilerParams(dimension_semantics=("parallel",), vmem_limit_bytes=VMEM_LIMIT),
        name="ssd_decode",
    )(proj, proj, proj, proj, buf, hst, cw, cb, hp, dskip, ng, expand)


def _pack_even_w_in(w):
    w = w.astype(MM_DTYPE)
    d = w.shape[0]
    o_beta = 4 * GDN_KW
    o_lq = o_beta + 2 * GDN_HEADS
    o_lr = o_lq + 2 * GLA_KW + 2 * GLA_VW
    used = o_lr + GLA_RANK
    return jnp.concatenate([w[:, :o_beta], w[:, o_lq:o_lr], w[:, o_beta:o_lq], w[:, o_lr:used],
                            jnp.zeros((d, E_N - used), w.dtype)], axis=1)


def _pack_odd_w_in(w):
    w = w.astype(MM_DTYPE)
    z = w[:, :SSD_DI]
    xbc = w[:, SSD_DI:SSD_DI + SSD_CONV_DIM]
    dt = w[:, SSD_DI + SSD_CONV_DIM:]
    return jnp.concatenate([xbc[:, :SSD_DI], z, xbc[:, SSD_DI:], dt,
                            jnp.zeros((w.shape[0], O_N - w.shape[1]), w.dtype)], axis=1)


def _lane_row(vals, offset):
    return jnp.zeros((LANES,), f32).at[offset:offset + vals.shape[0]].set(vals.astype(f32))


def kernel(x_prompt, x_sample, state_gdn_conv, state_gdn, state_gla, state_ssd_conv, state_ssd, e_pre_g, e_post_g, e_w_in, e_conv_w, e_conv_b, gdn_a_log, gdn_dt_bias, gdn_norm_g, gla_w_lr, gla_b_lr, gla_norm_g, e_w_out, o_pre_g, o_post_g, o_w_in, ssd_conv_w, ssd_conv_b, ssd_dt_bias, ssd_a_log, ssd_d, ssd_norm_g, o_w_out):
    bp, lp, d = x_prompt.shape
    bs = x_sample.shape[0]
    tp = bp * lp
    tb = min(lp, 256)
    bb = min(bs, 8)

    w_in0 = _pack_even_w_in(e_w_in[0])
    w_out0 = e_w_out[0].astype(MM_DTYPE)
    pre0 = e_pre_g[0].reshape(1, d)
    post0 = e_post_g[0].reshape(1, d)
    cw0 = e_conv_w[0]
    cb0 = e_conv_b[0].reshape(1, GDN_CONV_DIM)
    hp0 = jnp.zeros((8, LANES), f32)
    hp0 = hp0.at[0].set(_lane_row(-jnp.exp(gdn_a_log[0].astype(f32)), SMALL_A))
    hp0 = hp0.at[1].set(_lane_row(gdn_dt_bias[0], SMALL_A))
    ng_gdn = gdn_norm_g[0].reshape(1, GDN_DV)
    wlr = jnp.zeros((LANES, GLA_KW), f32).at[SMALL_LR:SMALL_LR + GLA_RANK].set(gla_w_lr[0]).astype(MM_DTYPE)
    blr = gla_b_lr[0].reshape(1, GLA_KW)
    ng_gla = gla_norm_g[0].reshape(1, GLA_DV)

    hp_flat = x_prompt.reshape(tp, d)
    hs_flat = x_sample.reshape(bs, d)

    o_p, gdn_conv_p, gdn_p, gla_p = _even_prompt(x_prompt, pre0, w_in0, cw0, cb0, hp0, ng_gdn, wlr, blr, ng_gla, tb)
    o_p = o_p.reshape(tp, GDN_VW + GLA_VW)
    hp_flat = _out_proj(o_p, o_p, w_out0, post0, hp_flat)

    proj_s = _norm_proj(hs_flat, pre0, w_in0, 1280)
    o1_s, gdn_conv_s, gdn_s = _gdn_decode(proj_s, jnp.swapaxes(state_gdn_conv[0], 0, 1), state_gdn[0],
                                          cw0, cb0, hp0, ng_gdn, bb)
    o2_s, gla_s = _gla_decode(proj_s, state_gla[0], wlr, blr, ng_gla, bb)
    hs_flat = _out_proj(o1_s, o2_s, w_out0, post0, hs_flat)

    w_in1 = _pack_odd_w_in(o_w_in[0])
    w_out1 = o_w_out[0].astype(MM_DTYPE)
    pre1 = o_pre_g[0].reshape(1, d)
    post1 = o_post_g[0].reshape(1, d)
    cw1 = ssd_conv_w[0]
    cb1 = ssd_conv_b[0].reshape(1, SSD_CONV_DIM)
    hp1 = jnp.zeros((8, LANES), f32)
    hp1 = hp1.at[0].set(_lane_row(ssd_dt_bias[0], 0))
    hp1 = hp1.at[1].set(_lane_row(-jnp.exp(ssd_a_log[0].astype(f32)), 0))
    dskip = jnp.repeat(ssd_d[0].astype(f32), SSD_P).reshape(1, SSD_DI)
    ng_ssd = ssd_norm_g[0].reshape(1, SSD_DI)
    expand = (lax.broadcasted_iota(jnp.int32, (LANES, SSD_DI), 0)
              == lax.broadcasted_iota(jnp.int32, (LANES, SSD_DI), 1) // SSD_P).astype(MM_DTYPE)

    y_p, ssd_conv_p, ssd_p = _ssd_prompt(hp_flat.reshape(bp, lp, d), pre1, w_in1, cw1, cb1, hp1, dskip, ng_ssd,
                                         expand, tb)
    y_p = y_p.reshape(tp, SSD_DI)
    hp_flat = _out_proj(y_p, y_p, w_out1, post1, hp_flat)

    proj_s = _norm_proj(hs_flat, pre1, w_in1, 1792)
    y_s, ssd_conv_s, ssd_s = _ssd_decode(proj_s, jnp.swapaxes(state_ssd_conv[0], 0, 1),
                                         state_ssd[0].reshape(bs, SSD_HEADS * SSD_P, SSD_N),
                                         cw1, cb1, hp1, dskip, ng_ssd, expand, bb)
    hs_flat = _out_proj(y_s, y_s, w_out1, post1, hs_flat)

    return (hp_flat.reshape(bp, lp, d), hs_flat.reshape(bs, 1, d),
            gdn_conv_p[None], gdn_p[None], gla_p[None],
            ssd_conv_p[None], ssd_p.reshape(bp, SSD_HEADS, SSD_P, SSD_N)[None],
            jnp.swapaxes(gdn_conv_s, 0, 1)[None], gdn_s[None], gla_s[None],
            jnp.swapaxes(ssd_conv_s, 0, 1)[None],
            ssd_s.reshape(bs, SSD_HEADS, SSD_P, SSD_N)[None])
```

```python
import functools

import jax
import jax.numpy as jnp
from jax import lax
from jax.experimental import pallas as pl
from jax.experimental.pallas import tpu as pltpu

f32 = jnp.float32
bf16 = jnp.bfloat16
MM_DTYPE = jnp.bfloat16

D_MODEL = 1024
CONV_W = 4
EPS = 1e-6
LANES = 128

GDN_HEADS = 8
GDN_DK = 128
GDN_DV = 128
GDN_KW = GDN_HEADS * GDN_DK
GDN_VW = GDN_HEADS * GDN_DV
GDN_CONV_DIM = 2 * GDN_KW + GDN_VW
GLA_HEADS = 4
GLA_DK = 128
GLA_DV = 256
GLA_KW = GLA_HEADS * GLA_DK
GLA_VW = GLA_HEADS * GLA_DV
GLA_RANK = 16
GLA_TAU = 16.0
SSD_DI = 2 * D_MODEL
SSD_P = 64
SSD_HEADS = SSD_DI // SSD_P
SSD_N = 128
SSD_G = 4
SSD_K = SSD_HEADS // SSD_G
SSD_GW = SSD_DI // SSD_G
SSD_BC = SSD_G * SSD_N
SSD_CONV_DIM = SSD_DI + 2 * SSD_BC

E_OFF_GATE = 3 * GDN_KW
E_OFF_GSMALL = 4 * GDN_KW
E_GDN_N = E_OFF_GSMALL + LANES
E_OFF_GLA = E_OFF_GSMALL + 2 * GDN_HEADS
SMALL_BETA = 0
SMALL_A = GDN_HEADS
L_OFF_K = GLA_KW
L_OFF_V = 2 * GLA_KW
L_OFF_GATE = L_OFF_V + GLA_VW
L_OFF_SMALL = L_OFF_GATE + GLA_VW
L_N = L_OFF_SMALL + LANES
SMALL_LR = 0
O_OFF_X = SSD_DI
O_OFF_BC = 2 * SSD_DI
O_OFF_DT = 2 * SSD_DI + 2 * SSD_BC
O_PROJ_N = 6 * 1024

CHUNK = 64
GDN_PREP_CHUNKS = 2
assert SSD_P == CHUNK and 2 * SSD_P == LANES
VMEM_LIMIT = 56 * 1024 * 1024


def _mm(a, b):
    return jnp.dot(a.astype(MM_DTYPE), b.astype(MM_DTYPE), preferred_element_type=f32)


def _mm_nt(a, b):
    return lax.dot_general(a.astype(MM_DTYPE), b.astype(MM_DTYPE), (((1,), (1,)), ((), ())),
                           preferred_element_type=f32)


def _mm_tn(a, b):
    return lax.dot_general(a.astype(MM_DTYPE), b.astype(MM_DTYPE), (((0,), (0,)), ((), ())),
                           preferred_element_type=f32)


def _split3(x):
    hi = x.astype(bf16)
    r = x - hi.astype(f32)
    mid = r.astype(bf16)
    lo = (r - mid.astype(f32)).astype(bf16)
    return hi, mid, lo


def _sel_mm(sel, x):
    hi, mid, lo = _split3(x)
    s = sel.astype(bf16)
    return (jnp.dot(s, hi, preferred_element_type=f32) + jnp.dot(s, mid, preferred_element_type=f32)
            + jnp.dot(s, lo, preferred_element_type=f32))


def _sel_rows(x, sel):
    hi, mid, lo = _split3(x)
    s = sel.astype(bf16)
    return (jnp.dot(hi, s, preferred_element_type=f32) + jnp.dot(mid, s, preferred_element_type=f32)
            + jnp.dot(lo, s, preferred_element_type=f32))


def _sigmoid(x):
    return 1.0 / (1.0 + jnp.exp(-x))


def _silu(x):
    return x * _sigmoid(x)


def _softplus(x):
    return jnp.maximum(x, 0.0) + jnp.log1p(jnp.exp(-jnp.abs(x)))


def _tril(n, strict=False):
    r = lax.broadcasted_iota(jnp.int32, (n, n), 0)
    c = lax.broadcasted_iota(jnp.int32, (n, n), 1)
    return (r > c) if strict else (r >= c)


def _masked_decay(col, row, incl):
    return jnp.exp(jnp.where(incl, col - row, -jnp.inf))


def _interleave(main, side, ratio):
    gens = [main, side]
    steps = {id(main): ratio, id(side): 1}
    while gens:
        for g in list(gens):
            for _ in range(steps[id(g)]):
                try:
                    next(g)
                except StopIteration:
                    gens.remove(g)
                    break
                yield


def _rms(x, width):
    return x * lax.rsqrt(jnp.sum(x * x, axis=-1, keepdims=True) * (1.0 / width) + EPS)


def _norm_proj_kernel(x_ref, g_ref, w_ref, o_ref, u_scr):
    @pl.when(pl.program_id(1) == 0)
    def _():
        u_scr[...] = (_rms(x_ref[...], D_MODEL) * g_ref[...]).astype(u_scr.dtype)

    o_ref[...] = jnp.dot(u_scr[...], w_ref[...], preferred_element_type=f32)


def _norm_proj(x, g, w, tn, n):
    t, d = x.shape
    assert n % tn == 0 and n - tn < w.shape[1]
    tm = min(t, 1024)
    return pl.pallas_call(
        _norm_proj_kernel,
        grid=(t // tm, n // tn),
        in_specs=[pl.BlockSpec((tm, d), lambda i, j: (i, 0)),
                  pl.BlockSpec((1, d), lambda i, j: (0, 0)),
                  pl.BlockSpec((d, tn), lambda i, j: (0, j))],
        out_specs=pl.BlockSpec((tm, tn), lambda i, j: (i, j)),
        out_shape=jax.ShapeDtypeStruct((t, n), f32),
        scratch_shapes=[pltpu.VMEM((tm, d), MM_DTYPE)],
        compiler_params=pltpu.CompilerParams(dimension_semantics=("parallel", "arbitrary"),
                                             vmem_limit_bytes=VMEM_LIMIT),
        name="norm_proj",
    )(x, g, w)


def _out_proj_kernel(oa_ref, ob_ref, wa_ref, wb_ref, g_ref, h_ref, y_ref):
    p = (jnp.dot(oa_ref[...], wa_ref[...], preferred_element_type=f32)
         + jnp.dot(ob_ref[...], wb_ref[...], preferred_element_type=f32))
    y_ref[...] = h_ref[...] + _rms(p, D_MODEL) * g_ref[...]


def _out_proj(oa, ob, w, g, h):
    t = oa.shape[0]
    e = w.shape[0] // 2
    assert oa.shape[1] == ob.shape[1] and oa.shape[1] in (e, 2 * e)
    col_b = 1 if ob.shape[1] == 2 * e else 0
    d = w.shape[1]
    tm = min(t, 1024)
    return pl.pallas_call(
        _out_proj_kernel,
        grid=(t // tm,),
        in_specs=[pl.BlockSpec((tm, e), lambda i: (i, 0)),
                  pl.BlockSpec((tm, e), lambda i: (i, col_b)),
                  pl.BlockSpec((e, d), lambda i: (0, 0)),
                  pl.BlockSpec((e, d), lambda i: (1, 0)),
                  pl.BlockSpec((1, d), lambda i: (0, 0)),
                  pl.BlockSpec((tm, d), lambda i: (i, 0))],
        out_specs=pl.BlockSpec((tm, d), lambda i: (i, 0)),
        out_shape=jax.ShapeDtypeStruct((t, d), f32),
        compiler_params=pltpu.CompilerParams(dimension_semantics=("parallel",),
                                             vmem_limit_bytes=VMEM_LIMIT),
        name="out_proj",
    )(oa, ob, w, w, g, h)


def _normed(x_ref, g_ref):
    return (_rms(x_ref[...], D_MODEL) * g_ref[...]).astype(MM_DTYPE)


def _proj(u, w_ref, c0, c1):
    return jnp.dot(u, w_ref[:, c0:c1], preferred_element_type=f32)


def _conv_block(xp_scr, cw_ref, cb_ref, tb, first):
    @pl.when(first)
    def _():
        xp_scr[0:8, :] = jnp.zeros((8, xp_scr.shape[1]), f32)

    @pl.when(jnp.logical_not(first))
    def _():
        xp_scr[5:8, :] = xp_scr[tb + 5:tb + 8, :]

    def conv_cols(c0, c1, r0=0, nr=tb):
        acc = cb_ref[:, c0:c1] + xp_scr[8 + r0:8 + r0 + nr, c0:c1] * cw_ref[3:4, c0:c1]
        for s in range(CONV_W - 1):
            acc = acc + xp_scr[5 + s + r0:5 + s + r0 + nr, c0:c1] * cw_ref[s:s + 1, c0:c1]
        return _silu(acc)

    return conv_cols


def _gdn_stages(u, first, last, wm_ref, ws_ref, cw_ref, cb_ref, hp_ref, ng_ref,
                o_ref, conv_out_ref, s_out_ref,
                xp_scr, qkv_scr, gate_scr, beta_scr, lg_scr, s_scr, gl_scr, up_scr, w_scr, qg_scr, kd_scr,
                p_scr, *, tb):
    conv_cols = _conv_block(xp_scr, cw_ref, cb_ref, tb, first)

    @pl.when(first)
    def _():
        s_scr[...] = jnp.zeros(s_scr.shape, f32)

    def proj_stages():
        for c0 in range(0, GDN_CONV_DIM, GDN_KW):
            xp_scr[8:8 + tb, c0:c0 + GDN_KW] = _proj(u, wm_ref, c0, c0 + GDN_KW)
            yield
        small = _proj(u, ws_ref, 0, LANES)
        beta_scr[...] = _sigmoid(small)
        lg_scr[...] = hp_ref[0:1, :] * _softplus(small + hp_ref[1:2, :])
        gate_scr[...] = _proj(u, wm_ref, E_OFF_GATE, E_OFF_GATE + GDN_VW)
        yield

    span = GDN_PREP_CHUNKS * CHUNK

    def conv_norm(it):
        r0 = it * span
        rr = slice(r0, r0 + span)
        for h in range(GDN_HEADS):
            c0 = h * GDN_DK
            q = conv_cols(c0, c0 + GDN_DK, r0, span)
            q = q * lax.rsqrt(jnp.sum(q * q, axis=-1, keepdims=True) + EPS) * (GDN_DK ** -0.5)
            qkv_scr[rr, c0:c0 + GDN_DK] = q
            k = conv_cols(GDN_KW + c0, GDN_KW + c0 + GDN_DK, r0, span)
            k = k * lax.rsqrt(jnp.sum(k * k, axis=-1, keepdims=True) + EPS)
            qkv_scr[rr, GDN_KW + c0:GDN_KW + c0 + GDN_DK] = k
            qkv_scr[rr, 2 * GDN_KW + c0:2 * GDN_KW + c0 + GDN_DV] = conv_cols(
                2 * GDN_KW + c0, 2 * GDN_KW + c0 + GDN_DV, r0, span)
            yield

    incl = _tril(CHUNK)
    strict = _tril(CHUNK, strict=True)
    eye = jnp.logical_and(incl, jnp.logical_not(strict)).astype(f32)
    sr = lax.broadcasted_iota(jnp.int32, (span, span), 0)
    sc = lax.broadcasted_iota(jnp.int32, (span, span), 1)
    lmat = jnp.logical_and(sr >= sc, sr // CHUNK == sc // CHUNK).astype(f32)
    ng = ng_ref[...]
    heads = range(GDN_HEADS)
    probs = [(ci, h) for ci in range(GDN_PREP_CHUNKS) for h in heads]
    n = range(len(probs))

    def prep(c):
        r0 = c * span
        gam_all = _sel_mm(lmat, lg_scr[pl.ds(r0, span), :])
        gam_t = gam_all.T
        beta_all = beta_scr[pl.ds(r0, span), :]
        rows, gcol, grow, glast, bcol = [], [], [], [], []
        for ci, h in probs:
            lo, hi = ci * CHUNK, (ci + 1) * CHUNK
            rows.append(pl.ds(r0 + lo, CHUNK))
            gcol.append(gam_all[lo:hi, SMALL_A + h:SMALL_A + h + 1])
            grow.append(gam_t[SMALL_A + h:SMALL_A + h + 1, lo:hi])
            glast.append(gam_all[hi - 1:hi, SMALL_A + h:SMALL_A + h + 1])
            bcol.append(beta_all[lo:hi, SMALL_BETA + h:SMALL_BETA + h + 1])
        for ci in range(GDN_PREP_CHUNKS):
            gl_scr[pl.ds(c * GDN_PREP_CHUNKS + ci, 1), :] = jnp.exp(gam_all[(ci + 1) * CHUNK - 1:(ci + 1) * CHUNK, :])
        q = [qkv_scr[rows[j], h * GDN_DK:(h + 1) * GDN_DK] for j, (_, h) in enumerate(probs)]
        k = [qkv_scr[rows[j], GDN_KW + h * GDN_DK:GDN_KW + (h + 1) * GDN_DK] for j, (_, h) in enumerate(probs)]
        v = [qkv_scr[rows[j], 2 * GDN_KW + h * GDN_DV:2 * GDN_KW + (h + 1) * GDN_DV] for j, (_, h) in enumerate(probs)]
        kb = [k[j].astype(MM_DTYPE) for j in n]
        kk = [_mm_nt(kb[j], kb[j]) for j in n]
        yield
        qk = [_mm_nt(q[j], kb[j]) for j in n]
        yield
        decay = [_masked_decay(gcol[j], grow[j], incl) for j in n]
        eg = [jnp.exp(gcol[j]) for j in n]
        for j, (_, h) in enumerate(probs):
            c0 = h * GDN_DK
            qg_scr[rows[j], c0:c0 + GDN_DK] = (eg[j] * q[j]).astype(MM_DTYPE)
            kd_scr[rows[j], c0:c0 + GDN_DK] = (jnp.exp(glast[j] - gcol[j]) * k[j]).astype(MM_DTYPE)
            p_scr[h, rows[j], :] = (qk[j] * decay[j]).astype(MM_DTYPE)
        nmat = [jnp.where(strict, bcol[j] * (kk[j] * decay[j]), 0.0) for j in n]
        tinv = [eye - nmat[j] for j in n]
        npow = nmat
        for _ in range(5):
            npow = [_mm(npow[j], npow[j]) for j in n]
            yield
            tinv = [tinv[j] + _mm(tinv[j], npow[j]) for j in n]
            yield
        x = [_mm(tinv[j], jnp.concatenate([bcol[j] * v[j], (bcol[j] * eg[j]) * k[j]], axis=-1)) for j in n]
        for j, (_, h) in enumerate(probs):
            c0 = h * GDN_DK
            up_scr[rows[j], c0:c0 + GDN_DV] = x[j][:, :GDN_DV]
            w_scr[rows[j], c0:c0 + GDN_DK] = x[j][:, GDN_DV:].astype(MM_DTYPE)

    state = [s_scr[h] for h in heads]

    def scan(c):
        rows = pl.ds(c * CHUNK, CHUNK)
        gl = gl_scr[c:c + 1, :]
        r = [_mm(jnp.concatenate([w_scr[rows, h * GDN_DK:(h + 1) * GDN_DK],
                                  qg_scr[rows, h * GDN_DK:(h + 1) * GDN_DK]], axis=0), state[h]) for h in heads]
        yield
        us = [(up_scr[rows, h * GDN_DV:(h + 1) * GDN_DV] - r[h][:CHUNK]).astype(MM_DTYPE) for h in heads]
        o = [r[h][CHUNK:] + _mm(p_scr[h, rows, :], us[h]) for h in heads]
        yield
        upd = [_mm_tn(kd_scr[rows, h * GDN_DK:(h + 1) * GDN_DK], us[h]) for h in heads]
        yield
        for h in heads:
            c0 = h * GDN_DV
            state[h] = gl[:, SMALL_A + h:SMALL_A + h + 1] * state[h] + upd[h]
            gate = gate_scr[rows, c0:c0 + GDN_DV]
            o_ref[rows, c0:c0 + GDN_DV] = (_rms(o[h], GDN_DV) * ng * _silu(gate)).astype(o_ref.dtype)
        yield

    def scans(chunks):
        for c in chunks:
            yield from scan(c)

    def side_work(it):
        if it + 1 < n_prep:
            yield from conv_norm(it + 1)
        if it >= 1:
            yield from scans(range((it - 1) * GDN_PREP_CHUNKS, it * GDN_PREP_CHUNKS))

    n_prep = tb // span

    def stages():
        yield from proj_stages()
        yield from conv_norm(0)
        for it in range(n_prep):
            yield from _interleave(prep(it), side_work(it), 1)
        yield from scans(range((n_prep - 1) * GDN_PREP_CHUNKS, n_prep * GDN_PREP_CHUNKS))

    def finish():
        for h in heads:
            s_scr[h] = state[h]

        @pl.when(last)
        def _():
            conv_out_ref[...] = xp_scr[tb + 5:tb + 8, :]
            s_out_ref[...] = s_scr[...]

    return stages(), finish


def _gla_stages(u, first, last, wq_ref, wk_ref, wv_ref, wg_ref, ws_ref, wlr_ref, blr_ref, ng_ref,
                o_ref, s_out_ref, q_ref, k_ref, v_ref, gate_ref, la_scr, st_scr, *, tb):
    @pl.when(first)
    def _():
        st_scr[...] = jnp.zeros(st_scr.shape, f32)

    def proj_stages():
        q_ref[...] = _proj(u, wq_ref, 0, GLA_KW)
        k_ref[...] = _proj(u, wk_ref, 0, GLA_KW)
        small = _proj(u, ws_ref, 0, LANES)
        z = _mm(small, wlr_ref[...]) + blr_ref[...]
        la_scr[...] = -_softplus(-z) * (1.0 / GLA_TAU)
        yield
        v_ref[...] = _proj(u, wv_ref, 0, GLA_VW).astype(v_ref.dtype)
        yield
        gate_ref[...] = _proj(u, wg_ref, 0, GLA_VW)
        yield

    incl = _tril(CHUNK)
    lmat = incl.astype(f32)
    ng = ng_ref[...]

    heads = range(GLA_HEADS)
    n_chunks = tb // CHUNK
    staged = [None] * n_chunks
    state = [st_scr[h] for h in heads]

    def prep(c):
        rows = pl.ds(c * CHUNK, CHUNK)
        bcum = [_sel_mm(lmat, la_scr[rows, h * GLA_DK:(h + 1) * GLA_DK]) for h in heads]
        yield
        k = [k_ref[rows, h * GLA_DK:(h + 1) * GLA_DK] for h in heads]
        v = [v_ref[rows, h * GLA_DV:(h + 1) * GLA_DV].astype(MM_DTYPE) for h in heads]
        qe = [(q_ref[rows, h * GLA_DK:(h + 1) * GLA_DK] * (GLA_DK ** -0.5) * jnp.exp(bcum[h])).astype(MM_DTYPE)
              for h in heads]
        ke = [k[h] * jnp.exp(-bcum[h]) for h in heads]
        blast = [bcum[h][CHUNK - 1:CHUNK, :] for h in heads]
        kdec = [(k[h] * jnp.exp(blast[h] - bcum[h])).astype(MM_DTYPE) for h in heads]
        p = [jnp.where(incl, _mm_nt(qe[h], ke[h]), 0.0) for h in heads]
        yield
        o_intra = [_mm(p[h], v[h]) for h in heads]
        staged[c] = (qe, v, kdec, blast, o_intra)
        yield

    def scan(c):
        rows = pl.ds(c * CHUNK, CHUNK)
        qe, v, kdec, blast, o_intra = staged[c]
        o = [o_intra[h] + _mm_nt(qe[h], state[h]) for h in heads]
        upd = [_mm_tn(v[h], kdec[h]) for h in heads]
        yield
        for h in heads:
            v0 = h * GLA_DV
            state[h] = jnp.exp(blast[h]) * state[h] + upd[h]
            gate = gate_ref[rows, v0:v0 + GLA_DV]
            o_ref[rows, GDN_VW + v0:GDN_VW + v0 + GLA_DV] = (
                _rms(o[h], GLA_DV) * ng * _silu(gate)).astype(o_ref.dtype)
        yield

    def stages():
        yield from proj_stages()
        yield from prep(0)
        for c in range(1, n_chunks):
            yield from _interleave(prep(c), scan(c - 1), 1)
        yield from scan(n_chunks - 1)

    def finish():
        for h in heads:
            st_scr[h] = state[h]

        @pl.when(last)
        def _():
            for h in heads:
                s_out_ref[h] = st_scr[h].T

    return stages(), finish


N_GDN_SCRATCH = 12
EVEN_STAGE_RATIO = 2


def _even_prompt_kernel(x_ref, pg_ref, wm_ref, ws_ref, wq_ref, wk_ref, wv_ref, wg_ref, wsl_ref, cw_ref, cb_ref, hp_ref,
                        ngd_ref, wlr_ref, blr_ref, ngl_ref,
                        o_ref, conv_out_ref, sgdn_out_ref, sgla_out_ref, *scratch, tb):
    i = pl.program_id(1)
    first = i == 0
    last = i == pl.num_programs(1) - 1
    u = _normed(x_ref, pg_ref)
    gdn, gdn_finish = _gdn_stages(u, first, last, wm_ref, ws_ref, cw_ref, cb_ref, hp_ref, ngd_ref,
                                  o_ref, conv_out_ref, sgdn_out_ref, *scratch[:N_GDN_SCRATCH], tb=tb)
    gla, gla_finish = _gla_stages(u, first, last, wq_ref, wk_ref, wv_ref, wg_ref, wsl_ref, wlr_ref, blr_ref, ngl_ref,
                                  o_ref, sgla_out_ref, *scratch[N_GDN_SCRATCH:], tb=tb)
    for _ in _interleave(gdn, gla, EVEN_STAGE_RATIO):
        pass
    gdn_finish()
    gla_finish()


def _even_prompt(x, pre_g, w_in, w_gla, cw, cb, hp, ng_gdn, wlr, blr, ng_gla, tb):
    b, l, d = x.shape
    assert tb % (GDN_PREP_CHUNKS * CHUNK) == 0 and l % tb == 0
    kern = functools.partial(_even_prompt_kernel, tb=tb)
    wcol = lambda blk: (lambda bi, i: (0, blk))
    const = lambda bi, i: (0, 0)
    gdn_scratch = [pltpu.VMEM((tb + 8, GDN_CONV_DIM), f32),
                   pltpu.VMEM((tb, GDN_CONV_DIM), f32),
                   pltpu.VMEM((tb, GDN_VW), f32),
                   pltpu.VMEM((tb, LANES), f32),
                   pltpu.VMEM((tb, LANES), f32),
                   pltpu.VMEM((GDN_HEADS, GDN_DK, GDN_DV), f32),
                   pltpu.VMEM((max(8, tb // CHUNK), LANES), f32),
                   pltpu.VMEM((tb, GDN_VW), f32),
                   pltpu.VMEM((tb, GDN_KW), MM_DTYPE),
                   pltpu.VMEM((tb, GDN_KW), MM_DTYPE),
                   pltpu.VMEM((tb, GDN_KW), MM_DTYPE),
                   pltpu.VMEM((GDN_HEADS, tb, CHUNK), MM_DTYPE)]
    assert len(gdn_scratch) == N_GDN_SCRATCH
    gla_scratch = [pltpu.VMEM((tb, GLA_KW), f32),
                   pltpu.VMEM((tb, GLA_KW), f32),
                   pltpu.VMEM((tb, GLA_VW), MM_DTYPE),
                   pltpu.VMEM((tb, GLA_VW), f32),
                   pltpu.VMEM((tb, GLA_KW), f32),
                   pltpu.VMEM((GLA_HEADS, GLA_DV, GLA_DK), f32)]
    return pl.pallas_call(
        kern,
        grid=(b, l // tb),
        in_specs=[pl.BlockSpec((None, tb, d), lambda bi, i: (bi, i, 0)),
                  pl.BlockSpec((1, d), const),
                  pl.BlockSpec((d, E_OFF_GSMALL), const),
                  pl.BlockSpec((d, LANES), wcol(E_OFF_GSMALL // LANES)),
                  pl.BlockSpec((d, GLA_KW), wcol(0)),
                  pl.BlockSpec((d, GLA_KW), wcol(L_OFF_K // GLA_KW)),
                  pl.BlockSpec((d, GLA_VW), wcol(L_OFF_V // GLA_VW)),
                  pl.BlockSpec((d, GLA_VW), wcol(L_OFF_GATE // GLA_VW)),
                  pl.BlockSpec((d, LANES), wcol(L_OFF_SMALL // LANES)),
                  pl.BlockSpec((CONV_W, GDN_CONV_DIM), const),
                  pl.BlockSpec((1, GDN_CONV_DIM), const),
                  pl.BlockSpec((8, LANES), const),
                  pl.BlockSpec((1, GDN_DV), const),
                  pl.BlockSpec((LANES, GLA_KW), const),
                  pl.BlockSpec((1, GLA_KW), const),
                  pl.BlockSpec((1, GLA_DV), const)],
        out_specs=[pl.BlockSpec((None, tb, GDN_VW + GLA_VW), lambda bi, i: (bi, i, 0)),
                   pl.BlockSpec((None, CONV_W - 1, GDN_CONV_DIM), lambda bi, i: (bi, 0, 0)),
                   pl.BlockSpec((None, GDN_HEADS, GDN_DK, GDN_DV), lambda bi, i: (bi, 0, 0, 0)),
                   pl.BlockSpec((None, GLA_HEADS, GLA_DK, GLA_DV), lambda bi, i: (bi, 0, 0, 0))],
        out_shape=[jax.ShapeDtypeStruct((b, l, GDN_VW + GLA_VW), MM_DTYPE),
                   jax.ShapeDtypeStruct((b, CONV_W - 1, GDN_CONV_DIM), f32),
                   jax.ShapeDtypeStruct((b, GDN_HEADS, GDN_DK, GDN_DV), f32),
                   jax.ShapeDtypeStruct((b, GLA_HEADS, GLA_DK, GLA_DV), f32)],
        scratch_shapes=gdn_scratch + gla_scratch,
        compiler_params=pltpu.CompilerParams(dimension_semantics=("parallel", "arbitrary"),
                                             vmem_limit_bytes=VMEM_LIMIT),
        name="even_prompt",
    )(x, pre_g, w_in, w_in, w_gla, w_gla, w_gla, w_gla, w_gla, cw, cb, hp, ng_gdn, wlr, blr, ng_gla)


def _head_lanes(x):
    return jnp.where(lax.broadcasted_iota(jnp.int32, x.shape, 1) < SSD_HEADS, x, 0.0)


def _ssd_prompt_kernel(x_ref, pg_ref, wx_ref, wz_ref, wbc_ref, wdt_ref, cw_ref, cb_ref, hp_ref, dskip_ref, ng_ref,
                       expand_ref, y_ref, conv_out_ref, h_out_ref,
                       xp_scr, xbc_scr, z_ref, lam_scr, lamx_scr, xdt_scr, h_scr, *, tb):
    i = pl.program_id(1)
    first = i == 0
    conv_cols = _conv_block(xp_scr, cw_ref, cb_ref, tb, first)

    @pl.when(first)
    def _():
        h_scr[...] = jnp.zeros(h_scr.shape, f32)

    u = _normed(x_ref, pg_ref)
    for c0 in range(0, SSD_DI, 1024):
        xp_scr[8:8 + tb, c0:c0 + 1024] = _proj(u, wx_ref, c0, c0 + 1024)
    xp_scr[8:8 + tb, SSD_DI:SSD_CONV_DIM] = _proj(u, wbc_ref, 0, 2 * SSD_BC)
    dt_raw = _head_lanes(_proj(u, wdt_ref, 0, LANES))
    for c0 in range(0, SSD_DI, 1024):
        z_ref[:, c0:c0 + 1024] = _proj(u, wz_ref, c0, c0 + 1024)

    for c0 in range(0, SSD_CONV_DIM, 512):
        xbc_scr[:, c0:c0 + 512] = conv_cols(c0, c0 + 512)
    dt = _softplus(dt_raw + hp_ref[0:1, :])
    tr = lax.broadcasted_iota(jnp.int32, (tb, tb), 0)
    tc = lax.broadcasted_iota(jnp.int32, (tb, tb), 1)
    lam_all = _sel_mm(jnp.logical_and(tr >= tc, tr // CHUNK == tc // CHUNK).astype(f32), dt * hp_ref[1:2, :])
    lam_scr[...] = lam_all
    lo_blk = lax.broadcasted_iota(jnp.int32, (tb, LANES), 1) < SSD_P
    for p in range(SSD_HEADS // 2):
        cols = slice(p * LANES, (p + 1) * LANES)
        k0 = 2 * p
        xdt_scr[:, cols] = xbc_scr[:, cols] * jnp.where(lo_blk, dt[:, k0:k0 + 1], dt[:, k0 + 1:k0 + 2])
    for c0 in range(0, SSD_DI, 512):
        lamx_scr[:, c0:c0 + 512] = _sel_rows(lam_all, expand_ref[:, c0:c0 + 512])

    lane = lax.broadcasted_iota(jnp.int32, (CHUNK, LANES), 1)
    lo = lane < SSD_P
    incl2 = lax.broadcasted_iota(jnp.int32, (CHUNK, LANES), 0) >= lane % CHUNK
    top = lax.broadcasted_iota(jnp.int32, (2 * SSD_P, SSD_N), 0) < SSD_P
    n_pairs = SSD_HEADS // 2
    ppg = SSD_K // 2
    groups = range(SSD_G)

    def chunk_body(c):
        rows = pl.ds(c * CHUNK, CHUNK)
        lam = lam_scr[rows, :]
        lam_t = lam.T
        llast = lam[CHUNK - 1:CHUNK, :]
        bm = [xbc_scr[rows, SSD_DI + g * SSD_N:SSD_DI + (g + 1) * SSD_N].astype(MM_DTYPE) for g in groups]
        cm = [xbc_scr[rows, SSD_DI + SSD_BC + g * SSD_N:SSD_DI + SSD_BC + (g + 1) * SSD_N].astype(MM_DTYPE)
              for g in groups]
        h_old = [h_scr[g * SSD_GW:(g + 1) * SSD_GW, :] for g in groups]
        cb2 = [_mm_nt(cm[g], jnp.concatenate([bm[g], bm[g]], axis=0)) for g in groups]
        y_inter = [_mm_nt(cm[g], h_old[g]) for g in groups]
        xs, xdt, xdec, lcol = [], [], [], []
        for p in range(n_pairs):
            cols = slice(p * LANES, (p + 1) * LANES)
            xs.append(xbc_scr[rows, cols])
            lcol.append(lamx_scr[rows, cols])
            xdt.append(xdt_scr[rows, cols])
            xdec.append((xdt[p] * jnp.exp(lcol[p][CHUNK - 1:CHUNK, :] - lcol[p])).astype(MM_DTYPE))
        upd = [_mm_tn(jnp.concatenate(xdec[g * ppg:(g + 1) * ppg], axis=-1), bm[g]) for g in groups]
        y_intra = []
        for p in range(n_pairs):
            k0 = 2 * p
            lrow = jnp.concatenate([lam_t[k0:k0 + 1, :], lam_t[k0 + 1:k0 + 2, :]], axis=-1)
            m = cb2[p // ppg] * _masked_decay(lcol[p], lrow, incl2)
            blockdiag = jnp.concatenate([jnp.where(lo, xdt[p], 0.0), jnp.where(lo, 0.0, xdt[p])], axis=0)
            y_intra.append(_mm(m, blockdiag))
        for g in groups:
            ys = []
            for pp in range(ppg):
                p = g * ppg + pp
                k0 = 2 * p
                ys.append(y_intra[p] + jnp.exp(lcol[p]) * y_inter[g][:, pp * LANES:(pp + 1) * LANES]
                          + dskip_ref[:, p * LANES:(p + 1) * LANES] * xs[p])
                el = jnp.where(top, jnp.exp(llast[:, k0:k0 + 1]), jnp.exp(llast[:, k0 + 1:k0 + 2]))
                h_scr[p * LANES:(p + 1) * LANES, :] = (el * h_old[g][pp * LANES:(pp + 1) * LANES, :]
                                                        + upd[g][pp * LANES:(pp + 1) * LANES, :])
            yg = jnp.concatenate(ys, axis=-1) * _silu(z_ref[rows, g * SSD_GW:(g + 1) * SSD_GW])
            y_ref[rows, g * SSD_GW:(g + 1) * SSD_GW] = (
                _rms(yg, SSD_GW) * ng_ref[:, g * SSD_GW:(g + 1) * SSD_GW]).astype(y_ref.dtype)

    for c in range(tb // CHUNK):
        chunk_body(c)

    @pl.when(i == pl.num_programs(1) - 1)
    def _():
        conv_out_ref[...] = xp_scr[tb + 5:tb + 8, :]
        h_out_ref[...] = h_scr[...]


def _ssd_prompt(x, pre_g, w_in, cw, cb, hp, dskip, ng, expand, tb):
    b, l, d = x.shape
    kern = functools.partial(_ssd_prompt_kernel, tb=tb)
    wcol = lambda blk: (lambda bi, i: (0, blk))
    const = lambda bi, i: (0, 0)
    return pl.pallas_call(
        kern,
        grid=(b, l // tb),
        in_specs=[pl.BlockSpec((None, tb, d), lambda bi, i: (bi, i, 0)),
                  pl.BlockSpec((1, d), const),
                  pl.BlockSpec((d, SSD_DI), wcol(O_OFF_X // SSD_DI)),
                  pl.BlockSpec((d, SSD_DI), wcol(0)),
                  pl.BlockSpec((d, 2 * SSD_BC), wcol(O_OFF_BC // (2 * SSD_BC))),
                  pl.BlockSpec((d, LANES), wcol(O_OFF_DT // LANES)),
                  pl.BlockSpec((CONV_W, SSD_CONV_DIM), const),
                  pl.BlockSpec((1, SSD_CONV_DIM), const),
                  pl.BlockSpec((8, LANES), const),
                  pl.BlockSpec((1, SSD_DI), const),
                  pl.BlockSpec((1, SSD_DI), const),
                  pl.BlockSpec((LANES, SSD_DI), const)],
        out_specs=[pl.BlockSpec((None, tb, SSD_DI), lambda bi, i: (bi, i, 0)),
                   pl.BlockSpec((None, CONV_W - 1, SSD_CONV_DIM), lambda bi, i: (bi, 0, 0)),
                   pl.BlockSpec((None, SSD_HEADS * SSD_P, SSD_N), lambda bi, i: (bi, 0, 0))],
        out_shape=[jax.ShapeDtypeStruct((b, l, SSD_DI), MM_DTYPE),
                   jax.ShapeDtypeStruct((b, CONV_W - 1, SSD_CONV_DIM), f32),
                   jax.ShapeDtypeStruct((b, SSD_HEADS * SSD_P, SSD_N), f32)],
        scratch_shapes=[pltpu.VMEM((tb + 8, SSD_CONV_DIM), f32),
                        pltpu.VMEM((tb, SSD_CONV_DIM), f32),
                        pltpu.VMEM((tb, SSD_DI), f32),
                        pltpu.VMEM((tb, LANES), f32),
                        pltpu.VMEM((tb, SSD_DI), f32),
                        pltpu.VMEM((tb, SSD_DI), f32),
                        pltpu.VMEM((SSD_HEADS * SSD_P, SSD_N), f32)],
        compiler_params=pltpu.CompilerParams(dimension_semantics=("parallel", "arbitrary"),
                                             vmem_limit_bytes=VMEM_LIMIT),
        name="ssd_prompt",
    )(x, pre_g, w_in, w_in, w_in, w_in, cw, cb, hp, dskip, ng, expand)


def _pad_rows(row, n):
    return jnp.concatenate([row, jnp.zeros((n - 1, row.shape[1]), row.dtype)], axis=0)


def _bcast_cols(row, width):
    hi, mid, lo = _split3(row)
    lhs = jnp.concatenate([hi, mid, lo, jnp.zeros((13, row.shape[1]), bf16)], axis=0)
    sel = (lax.broadcasted_iota(jnp.int32, (16, width), 0) < 3).astype(bf16)
    return lax.dot_general(lhs, sel, (((0,), (0,)), ((), ())), preferred_element_type=f32)


def _conv_step(buf_ref, x, cw_ref, cb_ref, new_buf_ref):
    acc = cb_ref[...] + x * cw_ref[3:4, :]
    for s in range(CONV_W - 1):
        acc = acc + buf_ref[s] * cw_ref[s:s + 1, :]
    for s in range(CONV_W - 2):
        new_buf_ref[s] = buf_ref[s + 1]
    new_buf_ref[CONV_W - 2] = x
    return _silu(acc)


def _gdn_decode_kernel(q_ref, k_ref, v_ref, gate_ref, small_ref, buf_ref, s_ref, cw_ref, cb_ref, hp_ref, ng_ref,
                       o_ref, new_buf_ref, s_out_ref, *, bb):
    x = jnp.concatenate([q_ref[...], k_ref[...], v_ref[...]], axis=-1)
    qkv = _conv_step(buf_ref, x, cw_ref, cb_ref, new_buf_ref)
    small = small_ref[...]
    beta = _sigmoid(small)
    eg = jnp.exp(hp_ref[0:1, :] * _softplus(small + hp_ref[1:2, :]))
    ng = ng_ref[...]
    gate = gate_ref[...]
    for h in range(GDN_HEADS):
        c0 = h * GDN_DK
        q = qkv[:, c0:c0 + GDN_DK]
        q = q * lax.rsqrt(jnp.sum(q * q, axis=-1, keepdims=True) + EPS) * (GDN_DK ** -0.5)
        k = qkv[:, GDN_KW + c0:GDN_KW + c0 + GDN_DK]
        k = k * lax.rsqrt(jnp.sum(k * k, axis=-1, keepdims=True) + EPS)
        v = qkv[:, 2 * GDN_KW + c0:2 * GDN_KW + c0 + GDN_DV]
        qk = jnp.sum(q * k, axis=-1, keepdims=True)
        b_h = beta[:, SMALL_BETA + h:SMALL_BETA + h + 1]
        eg_h = eg[:, SMALL_A + h:SMALL_A + h + 1]
        seqs = range(bb)
        s_old = [s_ref[b, h] for b in seqs]
        r = [_mm(jnp.concatenate([k[b:b + 1], q[b:b + 1], jnp.zeros((6, GDN_DK), f32)], axis=0), s_old[b])
             for b in seqs]
        u = [b_h[b:b + 1] * v[b:b + 1] - (b_h[b:b + 1] * eg_h[b:b + 1]) * r[b][0:1] for b in seqs]
        upd = [_mm_tn(_pad_rows(k[b:b + 1], 16), _pad_rows(u[b], 16)) for b in seqs]
        for b in seqs:
            s_out_ref[b, h] = eg_h[b:b + 1] * s_old[b] + upd[b]
        o = jnp.concatenate([eg_h[b:b + 1] * r[b][1:2] + qk[b:b + 1] * u[b] for b in seqs], axis=0)
        o_ref[:, c0:c0 + GDN_DV] = (_rms(o, GDN_DV) * ng * _silu(gate[:, c0:c0 + GDN_DV])).astype(o_ref.dtype)


def _gdn_decode(proj, buf, s, cw, cb, hp, ng, bb):
    b = proj.shape[0]
    wq = GDN_KW
    kern = functools.partial(_gdn_decode_kernel, bb=bb)
    col = lambda blk: (lambda i: (i, blk))
    const = lambda i: (0, 0)
    return pl.pallas_call(
        kern,
        grid=(b // bb,),
        in_specs=[pl.BlockSpec((bb, wq), col(0)),
                  pl.BlockSpec((bb, wq), col(1)),
                  pl.BlockSpec((bb, wq), col(2)),
                  pl.BlockSpec((bb, wq), col(3)),
                  pl.BlockSpec((bb, LANES), col(E_OFF_GSMALL // LANES)),
                  pl.BlockSpec((CONV_W - 1, bb, GDN_CONV_DIM), lambda i: (0, i, 0)),
                  pl.BlockSpec((bb, GDN_HEADS, GDN_DK, GDN_DV), lambda i: (i, 0, 0, 0)),
                  pl.BlockSpec((CONV_W, GDN_CONV_DIM), const),
                  pl.BlockSpec((1, GDN_CONV_DIM), const),
                  pl.BlockSpec((8, LANES), const),
                  pl.BlockSpec((1, GDN_DV), const)],
        out_specs=[pl.BlockSpec((bb, GDN_VW), col(0)),
                   pl.BlockSpec((CONV_W - 1, bb, GDN_CONV_DIM), lambda i: (0, i, 0)),
                   pl.BlockSpec((bb, GDN_HEADS, GDN_DK, GDN_DV), lambda i: (i, 0, 0, 0))],
        out_shape=[jax.ShapeDtypeStruct((b, GDN_VW), MM_DTYPE),
                   jax.ShapeDtypeStruct((CONV_W - 1, b, GDN_CONV_DIM), f32),
                   jax.ShapeDtypeStruct((b, GDN_HEADS, GDN_DK, GDN_DV), f32)],
        compiler_params=pltpu.CompilerParams(dimension_semantics=("parallel",), vmem_limit_bytes=VMEM_LIMIT),
        name="gdn_decode",
    )(proj, proj, proj, proj, proj, buf, s, cw, cb, hp, ng)


def _gla_decode_kernel(q_ref, k_ref, v_ref, gate_ref, small_ref, s_ref, wlr_ref, blr_ref, ng_ref,
                       o_ref, s_out_ref, *, bb):
    z = _mm(small_ref[...], wlr_ref[...]) + blr_ref[...]
    la = -_softplus(-z) * (1.0 / GLA_TAU)
    dec = jnp.exp(la)
    qe = q_ref[...] * (GLA_DK ** -0.5) * dec
    kx = k_ref[...]
    ke = kx * jnp.exp(-la)
    vx = v_ref[...]
    ng = ng_ref[...]
    gate = gate_ref[...]
    for h in range(GLA_HEADS):
        k0 = h * GLA_DK
        v0 = h * GLA_DV
        p = jnp.sum(qe[:, k0:k0 + GLA_DK] * ke[:, k0:k0 + GLA_DK], axis=-1, keepdims=True)
        outs = []
        for b in range(bb):
            s_old = s_ref[b, h]
            vrow = vx[b:b + 1, v0:v0 + GLA_DV]
            outs.append(_mm(_pad_rows(qe[b:b + 1, k0:k0 + GLA_DK], 8), s_old)[0:1] + p[b:b + 1] * vrow)
            s_out_ref[b, h] = (_bcast_cols(dec[b:b + 1, k0:k0 + GLA_DK], GLA_DV) * s_old
                               + _mm_tn(_pad_rows(kx[b:b + 1, k0:k0 + GLA_DK], 16), _pad_rows(vrow, 16)))
        o = jnp.concatenate(outs, axis=0)
        o_ref[:, v0:v0 + GLA_DV] = (_rms(o, GLA_DV) * ng * _silu(gate[:, v0:v0 + GLA_DV])).astype(o_ref.dtype)


def _gla_decode(proj, s, wlr, blr, ng, bb):
    b = proj.shape[0]
    kern = functools.partial(_gla_decode_kernel, bb=bb)
    col = lambda blk: (lambda i: (i, blk))
    const = lambda i: (0, 0)
    return pl.pallas_call(
        kern,
        grid=(b // bb,),
        in_specs=[pl.BlockSpec((bb, GLA_KW), col(0)),
                  pl.BlockSpec((bb, GLA_KW), col(L_OFF_K // GLA_KW)),
                  pl.BlockSpec((bb, GLA_VW), col(L_OFF_V // GLA_VW)),
                  pl.BlockSpec((bb, GLA_VW), col(L_OFF_GATE // GLA_VW)),
                  pl.BlockSpec((bb, LANES), col(L_OFF_SMALL // LANES)),
                  pl.BlockSpec((bb, GLA_HEADS, GLA_DK, GLA_DV), lambda i: (i, 0, 0, 0)),
                  pl.BlockSpec((LANES, GLA_KW), const),
                  pl.BlockSpec((1, GLA_KW), const),
                  pl.BlockSpec((1, GLA_DV), const)],
        out_specs=[pl.BlockSpec((bb, GLA_VW), col(0)),
                   pl.BlockSpec((bb, GLA_HEADS, GLA_DK, GLA_DV), lambda i: (i, 0, 0, 0))],
        out_shape=[jax.ShapeDtypeStruct((b, GLA_VW), MM_DTYPE),
                   jax.ShapeDtypeStruct((b, GLA_HEADS, GLA_DK, GLA_DV), f32)],
        compiler_params=pltpu.CompilerParams(dimension_semantics=("parallel",), vmem_limit_bytes=VMEM_LIMIT),
        name="gla_decode",
    )(proj, proj, proj, proj, proj, s, wlr, blr, ng)


def _ssd_decode_kernel(x_ref, z_ref, bc_ref, dt_ref, buf_ref, h_ref, cw_ref, cb_ref, hp_ref, dskip_ref, ng_ref,
                       expand_ref, y_ref, new_buf_ref, h_out_ref, *, bb):
    xin = jnp.concatenate([x_ref[...], bc_ref[...]], axis=-1)
    xbc = _conv_step(buf_ref, xin, cw_ref, cb_ref, new_buf_ref)
    xs = xbc[:, :SSD_DI]
    dt = _softplus(_head_lanes(dt_ref[...]) + hp_ref[0:1, :])
    el = jnp.exp(dt * hp_ref[1:2, :])
    expand = expand_ref[...]
    dt_x = _sel_rows(dt, expand)
    el_x = _sel_rows(el, expand)
    xdt = xs * dt_x
    z = z_ref[...]
    for g in range(SSD_G):
        gs = slice(g * SSD_GW, (g + 1) * SSD_GW)
        bm = xbc[:, SSD_DI + g * SSD_N:SSD_DI + (g + 1) * SSD_N]
        cm = xbc[:, SSD_DI + SSD_BC + g * SSD_N:SSD_DI + SSD_BC + (g + 1) * SSD_N]
        cbs = jnp.sum(cm * bm, axis=-1, keepdims=True)
        outs = []
        for b in range(bb):
            hg = h_ref[b, gs, :]
            y_inter = _mm_nt(_pad_rows(cm[b:b + 1], 8), hg)[0:1]
            outs.append(el_x[b:b + 1, gs] * y_inter)
            h_out_ref[b, gs, :] = (_bcast_cols(el_x[b:b + 1, gs], SSD_N) * hg
                                   + _mm_tn(_pad_rows(xdt[b:b + 1, gs], 16), _pad_rows(bm[b:b + 1], 16)))
        y = jnp.concatenate(outs, axis=0) + cbs * xdt[:, gs] + dskip_ref[:, gs] * xs[:, gs]
        yg = y * _silu(z[:, gs])
        y_ref[:, gs] = (_rms(yg, SSD_GW) * ng_ref[:, gs]).astype(y_ref.dtype)


def _ssd_decode(proj, buf, hst, cw, cb, hp, dskip, ng, expand, bb):
    b = proj.shape[0]
    kern = functools.partial(_ssd_decode_kernel, bb=bb)
    col = lambda blk: (lambda i: (i, blk))
    const = lambda i: (0, 0)
    return pl.pallas_call(
        kern,
        grid=(b // bb,),
        in_specs=[pl.BlockSpec((bb, SSD_DI), col(O_OFF_X // SSD_DI)),
                  pl.BlockSpec((bb, SSD_DI), col(0)),
                  pl.BlockSpec((bb, 2 * SSD_BC), col(O_OFF_BC // (2 * SSD_BC))),
                  pl.BlockSpec((bb, LANES), col(O_OFF_DT // LANES)),
                  pl.BlockSpec((CONV_W - 1, bb, SSD_CONV_DIM), lambda i: (0, i, 0)),
                  pl.BlockSpec((bb, SSD_HEADS * SSD_P, SSD_N), lambda i: (i, 0, 0)),
                  pl.BlockSpec((CONV_W, SSD_CONV_DIM), const),
                  pl.BlockSpec((1, SSD_CONV_DIM), const),
                  pl.BlockSpec((8, LANES), const),
                  pl.BlockSpec((1, SSD_DI), const),
                  pl.BlockSpec((1, SSD_DI), const),
                  pl.BlockSpec((LANES, SSD_DI), const)],
        out_specs=[pl.BlockSpec((bb, SSD_DI), col(0)),
                   pl.BlockSpec((CONV_W - 1, bb, SSD_CONV_DIM), lambda i: (0, i, 0)),
                   pl.BlockSpec((bb, SSD_HEADS * SSD_P, SSD_N), lambda i: (i, 0, 0))],
        out_shape=[jax.ShapeDtypeStruct((b, SSD_DI), MM_DTYPE),
                   jax.ShapeDtypeStruct((CONV_W - 1, b, SSD_CONV_DIM), f32),
                   jax.ShapeDtypeStruct((b, SSD_HEADS * SSD_P, SSD_N), f32)],
        compiler_params=pltpu.CompilerParams(dimension_semantics=("parallel",), vmem_limit_bytes=VMEM_LIMIT),
        name="ssd_decode",
    )(proj, proj, proj, proj, buf, hst, cw, cb, hp, dskip, ng, expand)


def _pack_gla_w_in(w):
    gla = w[:, E_OFF_GLA:]
    return jnp.concatenate([gla, jnp.zeros((w.shape[0], L_N - gla.shape[1]), w.dtype)], axis=1)


def _lane_row(vals, offset):
    return jnp.zeros((LANES,), f32).at[offset:offset + vals.shape[0]].set(vals.astype(f32))


def kernel(x_prompt, x_sample, state_gdn_conv, state_gdn, state_gla, state_ssd_conv, state_ssd, e_pre_g, e_post_g, e_w_in, e_conv_w, e_conv_b, gdn_a_log, gdn_dt_bias, gdn_norm_g, gla_w_lr, gla_b_lr, gla_norm_g, e_w_out, o_pre_g, o_post_g, o_w_in, ssd_conv_w, ssd_conv_b, ssd_dt_bias, ssd_a_log, ssd_d, ssd_norm_g, o_w_out):
    bp, lp, d = x_prompt.shape
    bs = x_sample.shape[0]
    tp = bp * lp
    tb = min(lp, 256)
    bb = min(bs, 8)

    w_in0 = e_w_in[0].astype(MM_DTYPE)
    w_gla = _pack_gla_w_in(w_in0)
    w_out0 = e_w_out[0].astype(MM_DTYPE)
    pre0 = e_pre_g[0].reshape(1, d)
    post0 = e_post_g[0].reshape(1, d)
    cw0 = e_conv_w[0]
    cb0 = e_conv_b[0].reshape(1, GDN_CONV_DIM)
    hp0 = jnp.zeros((8, LANES), f32)
    hp0 = hp0.at[0].set(_lane_row(-jnp.exp(gdn_a_log[0].astype(f32)), SMALL_A))
    hp0 = hp0.at[1].set(_lane_row(gdn_dt_bias[0], SMALL_A))
    ng_gdn = gdn_norm_g[0].reshape(1, GDN_DV)
    wlr = jnp.zeros((LANES, GLA_KW), f32).at[SMALL_LR:SMALL_LR + GLA_RANK].set(gla_w_lr[0]).astype(MM_DTYPE)
    blr = gla_b_lr[0].reshape(1, GLA_KW)
    ng_gla = gla_norm_g[0].reshape(1, GLA_DV)

    hp_flat = x_prompt.reshape(tp, d)
    hs_flat = x_sample.reshape(bs, d)

    o_p, gdn_conv_p, gdn_p, gla_p = _even_prompt(x_prompt, pre0, w_in0, w_gla, cw0, cb0, hp0, ng_gdn, wlr, blr, ng_gla,
                                                 tb)
    o_p = o_p.reshape(tp, GDN_VW + GLA_VW)
    hp_flat = _out_proj(o_p, o_p, w_out0, post0, hp_flat)

    proj_gdn = _norm_proj(hs_flat, pre0, w_in0, E_GDN_N // 3, E_GDN_N)
    proj_gla = _norm_proj(hs_flat, pre0, w_gla, L_N // 5, L_N)
    o1_s, gdn_conv_s, gdn_s = _gdn_decode(proj_gdn, jnp.swapaxes(state_gdn_conv[0], 0, 1), state_gdn[0],
                                          cw0, cb0, hp0, ng_gdn, bb)
    o2_s, gla_s = _gla_decode(proj_gla, state_gla[0], wlr, blr, ng_gla, bb)
    hs_flat = _out_proj(o1_s, o2_s, w_out0, post0, hs_flat)

    w_in1 = o_w_in[0].astype(MM_DTYPE)
    w_out1 = o_w_out[0].astype(MM_DTYPE)
    pre1 = o_pre_g[0].reshape(1, d)
    post1 = o_post_g[0].reshape(1, d)
    cw1 = ssd_conv_w[0]
    cb1 = ssd_conv_b[0].reshape(1, SSD_CONV_DIM)
    hp1 = jnp.zeros((8, LANES), f32)
    hp1 = hp1.at[0].set(_lane_row(ssd_dt_bias[0], 0))
    hp1 = hp1.at[1].set(_lane_row(-jnp.exp(ssd_a_log[0].astype(f32)), 0))
    dskip = jnp.repeat(ssd_d[0].astype(f32), SSD_P).reshape(1, SSD_DI)
    ng_ssd = ssd_norm_g[0].reshape(1, SSD_DI)
    expand = (lax.broadcasted_iota(jnp.int32, (LANES, SSD_DI), 0)
              == lax.broadcasted_iota(jnp.int32, (LANES, SSD_DI), 1) // SSD_P).astype(MM_DTYPE)

    y_p, ssd_conv_p, ssd_p = _ssd_prompt(hp_flat.reshape(bp, lp, d), pre1, w_in1, cw1, cb1, hp1, dskip, ng_ssd,
                                         expand, tb)
    y_p = y_p.reshape(tp, SSD_DI)
    hp_flat = _out_proj(y_p, y_p, w_out1, post1, hp_flat)

    proj_s = _norm_proj(hs_flat, pre1, w_in1, O_PROJ_N // 6, O_PROJ_N)
    y_s, ssd_conv_s, ssd_s = _ssd_decode(proj_s, jnp.swapaxes(state_ssd_conv[0], 0, 1),
                                         state_ssd[0].reshape(bs, SSD_HEADS * SSD_P, SSD_N),
                                         cw1, cb1, hp1, dskip, ng_ssd, expand, bb)
    hs_flat = _out_proj(y_s, y_s, w_out1, post1, hs_flat)

    return (hp_flat.reshape(bp, lp, d), hs_flat.reshape(bs, 1, d),
            gdn_conv_p[None], gdn_p[None], gla_p[None],
            ssd_conv_p[None], ssd_p.reshape(bp, SSD_HEADS, SSD_P, SSD_N)[None],
            jnp.swapaxes(gdn_conv_s, 0, 1)[None], gdn_s[None], gla_s[None],
            jnp.swapaxes(ssd_conv_s, 0, 1)[None],
            ssd_s.reshape(bs, SSD_HEADS, SSD_P, SSD_N)[None])
```

```python
import functools

import jax
import jax.numpy as jnp
from jax import lax
from jax.experimental import pallas as pl
from jax.experimental.pallas import tpu as pltpu

f32 = jnp.float32
bf16 = jnp.bfloat16
MM_DTYPE = jnp.bfloat16

D_MODEL = 1024
CONV_W = 4
EPS = 1e-6
LANES = 128

GDN_HEADS = 8
GDN_DK = 128
GDN_DV = 128
GDN_KW = GDN_HEADS * GDN_DK
GDN_VW = GDN_HEADS * GDN_DV
GDN_CONV_DIM = 2 * GDN_KW + GDN_VW
GLA_HEADS = 4
GLA_DK = 128
GLA_DV = 256
GLA_KW = GLA_HEADS * GLA_DK
GLA_VW = GLA_HEADS * GLA_DV
GLA_RANK = 16
GLA_TAU = 16.0
SSD_DI = 2 * D_MODEL
SSD_P = 64
SSD_HEADS = SSD_DI // SSD_P
SSD_N = 128
SSD_G = 4
SSD_K = SSD_HEADS // SSD_G
SSD_GW = SSD_DI // SSD_G
SSD_BC = SSD_G * SSD_N
SSD_CONV_DIM = SSD_DI + 2 * SSD_BC

E_OFF_GATE = 3 * GDN_KW
E_OFF_GSMALL = 4 * GDN_KW
E_GDN_N = E_OFF_GSMALL + LANES
E_OFF_GLA = E_OFF_GSMALL + 2 * GDN_HEADS
SMALL_BETA = 0
SMALL_A = GDN_HEADS
L_OFF_K = GLA_KW
L_OFF_V = 2 * GLA_KW
L_OFF_GATE = L_OFF_V + GLA_VW
L_OFF_SMALL = L_OFF_GATE + GLA_VW
L_N = L_OFF_SMALL + LANES
SMALL_LR = 0
O_OFF_X = SSD_DI
O_OFF_BC = 2 * SSD_DI
O_OFF_DT = 2 * SSD_DI + 2 * SSD_BC
O_PROJ_N = 6 * 1024

CHUNK = 64
GDN_PREP_CHUNKS = 2
assert SSD_P == CHUNK and 2 * SSD_P == LANES
VMEM_LIMIT = 56 * 1024 * 1024


def _mm(a, b):
    return jnp.dot(a.astype(MM_DTYPE), b.astype(MM_DTYPE), preferred_element_type=f32)


def _mm_nt(a, b):
    return lax.dot_general(a.astype(MM_DTYPE), b.astype(MM_DTYPE), (((1,), (1,)), ((), ())),
                           preferred_element_type=f32)


def _mm_tn(a, b):
    return lax.dot_general(a.astype(MM_DTYPE), b.astype(MM_DTYPE), (((0,), (0,)), ((), ())),
                           preferred_element_type=f32)


def _split3(x):
    hi = x.astype(bf16)
    r = x - hi.astype(f32)
    mid = r.astype(bf16)
    lo = (r - mid.astype(f32)).astype(bf16)
    return hi, mid, lo


def _sel_mm(sel, x):
    hi, mid, lo = _split3(x)
    s = sel.astype(bf16)
    return (jnp.dot(s, hi, preferred_element_type=f32) + jnp.dot(s, mid, preferred_element_type=f32)
            + jnp.dot(s, lo, preferred_element_type=f32))


def _sel_rows(x, sel):
    hi, mid, lo = _split3(x)
    s = sel.astype(bf16)
    return (jnp.dot(hi, s, preferred_element_type=f32) + jnp.dot(mid, s, preferred_element_type=f32)
            + jnp.dot(lo, s, preferred_element_type=f32))


def _sigmoid(x):
    return 1.0 / (1.0 + jnp.exp(-x))


def _silu(x):
    return x * _sigmoid(x)


def _softplus(x):
    return jnp.maximum(x, 0.0) + jnp.log1p(jnp.exp(-jnp.abs(x)))


def _tril(n, strict=False):
    r = lax.broadcasted_iota(jnp.int32, (n, n), 0)
    c = lax.broadcasted_iota(jnp.int32, (n, n), 1)
    return (r > c) if strict else (r >= c)


def _masked_decay(col, row, incl):
    return jnp.exp(jnp.where(incl, col - row, -jnp.inf))


def _interleave(main, side, ratio):
    gens = [main, side]
    steps = {id(main): ratio, id(side): 1}
    while gens:
        for g in list(gens):
            for _ in range(steps[id(g)]):
                try:
                    next(g)
                except StopIteration:
                    gens.remove(g)
                    break
                yield


def _rms(x, width):
    return x * lax.rsqrt(jnp.sum(x * x, axis=-1, keepdims=True) * (1.0 / width) + EPS)


def _norm_proj_kernel(x_ref, g_ref, w_ref, o_ref, u_scr):
    @pl.when(pl.program_id(1) == 0)
    def _():
        u_scr[...] = (_rms(x_ref[...], D_MODEL) * g_ref[...]).astype(u_scr.dtype)

    o_ref[...] = jnp.dot(u_scr[...], w_ref[...], preferred_element_type=f32)


def _norm_proj(x, g, w, tn, n):
    t, d = x.shape
    assert n % tn == 0 and n - tn < w.shape[1]
    tm = min(t, 1024)
    return pl.pallas_call(
        _norm_proj_kernel,
        grid=(t // tm, n // tn),
        in_specs=[pl.BlockSpec((tm, d), lambda i, j: (i, 0)),
                  pl.BlockSpec((1, d), lambda i, j: (0, 0)),
                  pl.BlockSpec((d, tn), lambda i, j: (0, j))],
        out_specs=pl.BlockSpec((tm, tn), lambda i, j: (i, j)),
        out_shape=jax.ShapeDtypeStruct((t, n), f32),
        scratch_shapes=[pltpu.VMEM((tm, d), MM_DTYPE)],
        compiler_params=pltpu.CompilerParams(dimension_semantics=("parallel", "arbitrary"),
                                             vmem_limit_bytes=VMEM_LIMIT),
        name="norm_proj",
    )(x, g, w)


def _out_proj_kernel(oa_ref, ob_ref, wa_ref, wb_ref, g_ref, h_ref, y_ref):
    p = (jnp.dot(oa_ref[...], wa_ref[...], preferred_element_type=f32)
         + jnp.dot(ob_ref[...], wb_ref[...], preferred_element_type=f32))
    y_ref[...] = h_ref[...] + _rms(p, D_MODEL) * g_ref[...]


def _out_proj(oa, ob, w, g, h):
    t = oa.shape[0]
    e = w.shape[0] // 2
    assert oa.shape[1] == ob.shape[1] and oa.shape[1] in (e, 2 * e)
    col_b = 1 if ob.shape[1] == 2 * e else 0
    d = w.shape[1]
    tm = min(t, 1024)
    return pl.pallas_call(
        _out_proj_kernel,
        grid=(t // tm,),
        in_specs=[pl.BlockSpec((tm, e), lambda i: (i, 0)),
                  pl.BlockSpec((tm, e), lambda i: (i, col_b)),
                  pl.BlockSpec((e, d), lambda i: (0, 0)),
                  pl.BlockSpec((e, d), lambda i: (1, 0)),
                  pl.BlockSpec((1, d), lambda i: (0, 0)),
                  pl.BlockSpec((tm, d), lambda i: (i, 0))],
        out_specs=pl.BlockSpec((tm, d), lambda i: (i, 0)),
        out_shape=jax.ShapeDtypeStruct((t, d), f32),
        compiler_params=pltpu.CompilerParams(dimension_semantics=("parallel",),
                                             vmem_limit_bytes=VMEM_LIMIT),
        name="out_proj",
    )(oa, ob, w, w, g, h)


def _normed(x_ref, g_ref):
    return (_rms(x_ref[...], D_MODEL) * g_ref[...]).astype(MM_DTYPE)


def _proj(u, w_ref, c0, c1):
    return jnp.dot(u, w_ref[:, c0:c1], preferred_element_type=f32)


def _conv_block(xp_scr, cw_ref, cb_ref, tb, first):
    @pl.when(first)
    def _():
        xp_scr[0:8, :] = jnp.zeros((8, xp_scr.shape[1]), f32)

    @pl.when(jnp.logical_not(first))
    def _():
        xp_scr[5:8, :] = xp_scr[tb + 5:tb + 8, :]

    def conv_cols(c0, c1, r0=0, nr=tb):
        acc = cb_ref[:, c0:c1] + xp_scr[8 + r0:8 + r0 + nr, c0:c1] * cw_ref[3:4, c0:c1]
        for s in range(CONV_W - 1):
            acc = acc + xp_scr[5 + s + r0:5 + s + r0 + nr, c0:c1] * cw_ref[s:s + 1, c0:c1]
        return _silu(acc)

    return conv_cols


def _gdn_stages(u, first, last, wm_ref, ws_ref, cw_ref, cb_ref, hp_ref, ng_ref,
                o_ref, conv_out_ref, s_out_ref,
                xp_scr, qkv_scr, gate_scr, beta_scr, lg_scr, s_scr, gl_scr, up_scr, w_scr, qg_scr, kd_scr,
                p_scr, *, tb):
    conv_cols = _conv_block(xp_scr, cw_ref, cb_ref, tb, first)

    @pl.when(first)
    def _():
        s_scr[...] = jnp.zeros(s_scr.shape, f32)

    def proj_stages():
        for c0 in range(0, GDN_CONV_DIM, GDN_KW):
            xp_scr[8:8 + tb, c0:c0 + GDN_KW] = _proj(u, wm_ref, c0, c0 + GDN_KW)
            yield
        small = _proj(u, ws_ref, 0, LANES)
        beta_scr[...] = _sigmoid(small)
        lg_scr[...] = hp_ref[0:1, :] * _softplus(small + hp_ref[1:2, :])
        gate_scr[...] = _proj(u, wm_ref, E_OFF_GATE, E_OFF_GATE + GDN_VW)
        yield

    span = GDN_PREP_CHUNKS * CHUNK

    def conv_norm(it):
        r0 = it * span
        rr = slice(r0, r0 + span)
        for h in range(GDN_HEADS):
            c0 = h * GDN_DK
            q = conv_cols(c0, c0 + GDN_DK, r0, span)
            q = q * lax.rsqrt(jnp.sum(q * q, axis=-1, keepdims=True) + EPS) * (GDN_DK ** -0.5)
            qkv_scr[rr, c0:c0 + GDN_DK] = q
            k = conv_cols(GDN_KW + c0, GDN_KW + c0 + GDN_DK, r0, span)
            k = k * lax.rsqrt(jnp.sum(k * k, axis=-1, keepdims=True) + EPS)
            qkv_scr[rr, GDN_KW + c0:GDN_KW + c0 + GDN_DK] = k
            qkv_scr[rr, 2 * GDN_KW + c0:2 * GDN_KW + c0 + GDN_DV] = conv_cols(
                2 * GDN_KW + c0, 2 * GDN_KW + c0 + GDN_DV, r0, span)
            yield

    incl = _tril(CHUNK)
    strict = _tril(CHUNK, strict=True)
    eye = jnp.logical_and(incl, jnp.logical_not(strict)).astype(f32)
    sr = lax.broadcasted_iota(jnp.int32, (span, span), 0)
    sc = lax.broadcasted_iota(jnp.int32, (span, span), 1)
    lmat = jnp.logical_and(sr >= sc, sr // CHUNK == sc // CHUNK).astype(f32)
    ng = ng_ref[...]
    heads = range(GDN_HEADS)
    probs = [(ci, h) for ci in range(GDN_PREP_CHUNKS) for h in heads]
    n = range(len(probs))

    def prep(c):
        r0 = c * span
        gam_all = _sel_mm(lmat, lg_scr[pl.ds(r0, span), :])
        gam_t = gam_all.T
        beta_all = beta_scr[pl.ds(r0, span), :]
        rows, gcol, grow, glast, bcol = [], [], [], [], []
        for ci, h in probs:
            lo, hi = ci * CHUNK, (ci + 1) * CHUNK
            rows.append(pl.ds(r0 + lo, CHUNK))
            gcol.append(gam_all[lo:hi, SMALL_A + h:SMALL_A + h + 1])
            grow.append(gam_t[SMALL_A + h:SMALL_A + h + 1, lo:hi])
            glast.append(gam_all[hi - 1:hi, SMALL_A + h:SMALL_A + h + 1])
            bcol.append(beta_all[lo:hi, SMALL_BETA + h:SMALL_BETA + h + 1])
        for ci in range(GDN_PREP_CHUNKS):
            gl_scr[pl.ds(c * GDN_PREP_CHUNKS + ci, 1), :] = jnp.exp(gam_all[(ci + 1) * CHUNK - 1:(ci + 1) * CHUNK, :])
        q = [qkv_scr[rows[j], h * GDN_DK:(h + 1) * GDN_DK] for j, (_, h) in enumerate(probs)]
        k = [qkv_scr[rows[j], GDN_KW + h * GDN_DK:GDN_KW + (h + 1) * GDN_DK] for j, (_, h) in enumerate(probs)]
        v = [qkv_scr[rows[j], 2 * GDN_KW + h * GDN_DV:2 * GDN_KW + (h + 1) * GDN_DV] for j, (_, h) in enumerate(probs)]
        kb = [k[j].astype(MM_DTYPE) for j in n]
        kk = [_mm_nt(kb[j], kb[j]) for j in n]
        yield
        qk = [_mm_nt(q[j], kb[j]) for j in n]
        yield
        decay = [_masked_decay(gcol[j], grow[j], incl) for j in n]
        eg = [jnp.exp(gcol[j]) for j in n]
        for j, (_, h) in enumerate(probs):
            c0 = h * GDN_DK
            qg_scr[rows[j], c0:c0 + GDN_DK] = (eg[j] * q[j]).astype(MM_DTYPE)
            kd_scr[rows[j], c0:c0 + GDN_DK] = (jnp.exp(glast[j] - gcol[j]) * k[j]).astype(MM_DTYPE)
            p_scr[h, rows[j], :] = (qk[j] * decay[j]).astype(MM_DTYPE)
        nmat = [jnp.where(strict, bcol[j] * (kk[j] * decay[j]), 0.0) for j in n]
        tinv = [eye - nmat[j] for j in n]
        npow = nmat
        for _ in range(5):
            npow = [_mm(npow[j], npow[j]) for j in n]
            yield
            tinv = [tinv[j] + _mm(tinv[j], npow[j]) for j in n]
            yield
        x = [_mm(tinv[j], jnp.concatenate([bcol[j] * v[j], (bcol[j] * eg[j]) * k[j]], axis=-1)) for j in n]
        for j, (_, h) in enumerate(probs):
            c0 = h * GDN_DK
            up_scr[rows[j], c0:c0 + GDN_DV] = x[j][:, :GDN_DV]
            w_scr[rows[j], c0:c0 + GDN_DK] = x[j][:, GDN_DV:].astype(MM_DTYPE)

    state = [s_scr[h] for h in heads]

    def scan(c):
        rows = pl.ds(c * CHUNK, CHUNK)
        gl = gl_scr[c:c + 1, :]
        r = [_mm(jnp.concatenate([w_scr[rows, h * GDN_DK:(h + 1) * GDN_DK],
                                  qg_scr[rows, h * GDN_DK:(h + 1) * GDN_DK]], axis=0), state[h]) for h in heads]
        yield
        us = [(up_scr[rows, h * GDN_DV:(h + 1) * GDN_DV] - r[h][:CHUNK]).astype(MM_DTYPE) for h in heads]
        o = [r[h][CHUNK:] + _mm(p_scr[h, rows, :], us[h]) for h in heads]
        yield
        upd = [_mm_tn(kd_scr[rows, h * GDN_DK:(h + 1) * GDN_DK], us[h]) for h in heads]
        yield
        for h in heads:
            c0 = h * GDN_DV
            state[h] = gl[:, SMALL_A + h:SMALL_A + h + 1] * state[h] + upd[h]
            gate = gate_scr[rows, c0:c0 + GDN_DV]
            o_ref[rows, c0:c0 + GDN_DV] = (_rms(o[h], GDN_DV) * ng * _silu(gate)).astype(o_ref.dtype)
        yield

    def scans(chunks):
        for c in chunks:
            yield from scan(c)

    def side_work(it):
        if it + 1 < n_prep:
            yield from conv_norm(it + 1)
        if it >= 1:
            yield from scans(range((it - 1) * GDN_PREP_CHUNKS, it * GDN_PREP_CHUNKS))

    n_prep = tb // span

    def stages():
        yield from proj_stages()
        yield from conv_norm(0)
        for it in range(n_prep):
            yield from _interleave(prep(it), side_work(it), 1)
        yield from scans(range((n_prep - 1) * GDN_PREP_CHUNKS, n_prep * GDN_PREP_CHUNKS))

    def finish():
        for h in heads:
            s_scr[h] = state[h]

        @pl.when(last)
        def _():
            conv_out_ref[...] = xp_scr[tb + 5:tb + 8, :]
            s_out_ref[...] = s_scr[...]

    return stages(), finish


def _gla_stages(u, first, last, wq_ref, wk_ref, wv_ref, wg_ref, ws_ref, wlr_ref, blr_ref, ng_ref,
                o_ref, s_out_ref, q_ref, k_ref, v_ref, gate_ref, la_scr, st_scr, *, tb):
    @pl.when(first)
    def _():
        st_scr[...] = jnp.zeros(st_scr.shape, f32)

    def proj_stages():
        q_ref[...] = _proj(u, wq_ref, 0, GLA_KW)
        k_ref[...] = _proj(u, wk_ref, 0, GLA_KW)
        small = _proj(u, ws_ref, 0, LANES)
        z = _mm(small, wlr_ref[...]) + blr_ref[...]
        la_scr[...] = -_softplus(-z) * (1.0 / GLA_TAU)
        yield
        v_ref[...] = _proj(u, wv_ref, 0, GLA_VW).astype(v_ref.dtype)
        yield
        gate_ref[...] = _proj(u, wg_ref, 0, GLA_VW)
        yield

    incl = _tril(CHUNK)
    lmat = incl.astype(f32)
    ng = ng_ref[...]

    heads = range(GLA_HEADS)
    n_chunks = tb // CHUNK
    staged = [None] * n_chunks
    state = [st_scr[h] for h in heads]

    def prep(c):
        rows = pl.ds(c * CHUNK, CHUNK)
        bcum = [_sel_mm(lmat, la_scr[rows, h * GLA_DK:(h + 1) * GLA_DK]) for h in heads]
        yield
        k = [k_ref[rows, h * GLA_DK:(h + 1) * GLA_DK] for h in heads]
        v = [v_ref[rows, h * GLA_DV:(h + 1) * GLA_DV].astype(MM_DTYPE) for h in heads]
        qe = [(q_ref[rows, h * GLA_DK:(h + 1) * GLA_DK] * (GLA_DK ** -0.5) * jnp.exp(bcum[h])).astype(MM_DTYPE)
              for h in heads]
        ke = [k[h] * jnp.exp(-bcum[h]) for h in heads]
        blast = [bcum[h][CHUNK - 1:CHUNK, :] for h in heads]
        kdec = [(k[h] * jnp.exp(blast[h] - bcum[h])).astype(MM_DTYPE) for h in heads]
        p = [jnp.where(incl, _mm_nt(qe[h], ke[h]), 0.0) for h in heads]
        yield
        o_intra = [_mm(p[h], v[h]) for h in heads]
        staged[c] = (qe, v, kdec, blast, o_intra)
        yield

    def scan(c):
        rows = pl.ds(c * CHUNK, CHUNK)
        qe, v, kdec, blast, o_intra = staged[c]
        o = [o_intra[h] + _mm_nt(qe[h], state[h]) for h in heads]
        upd = [_mm_tn(v[h], kdec[h]) for h in heads]
        yield
        for h in heads:
            v0 = h * GLA_DV
            state[h] = jnp.exp(blast[h]) * state[h] + upd[h]
            gate = gate_ref[rows, v0:v0 + GLA_DV]
            o_ref[rows, GDN_VW + v0:GDN_VW + v0 + GLA_DV] = (
                _rms(o[h], GLA_DV) * ng * _silu(gate)).astype(o_ref.dtype)
        yield

    def stages():
        yield from proj_stages()
        yield from prep(0)
        for c in range(1, n_chunks):
            yield from _interleave(prep(c), scan(c - 1), 1)
        yield from scan(n_chunks - 1)

    def finish():
        for h in heads:
            st_scr[h] = state[h]

        @pl.when(last)
        def _():
            for h in heads:
                s_out_ref[h] = st_scr[h].T

    return stages(), finish


N_GDN_SCRATCH = 12
EVEN_STAGE_RATIO = 2


def _even_prompt_kernel(x_ref, pg_ref, wm_ref, ws_ref, wq_ref, wk_ref, wv_ref, wg_ref, wsl_ref, cw_ref, cb_ref, hp_ref,
                        ngd_ref, wlr_ref, blr_ref, ngl_ref,
                        o_ref, conv_out_ref, sgdn_out_ref, sgla_out_ref, *scratch, tb):
    i = pl.program_id(1)
    first = i == 0
    last = i == pl.num_programs(1) - 1
    u = _normed(x_ref, pg_ref)
    gdn, gdn_finish = _gdn_stages(u, first, last, wm_ref, ws_ref, cw_ref, cb_ref, hp_ref, ngd_ref,
                                  o_ref, conv_out_ref, sgdn_out_ref, *scratch[:N_GDN_SCRATCH], tb=tb)
    gla, gla_finish = _gla_stages(u, first, last, wq_ref, wk_ref, wv_ref, wg_ref, wsl_ref, wlr_ref, blr_ref, ngl_ref,
                                  o_ref, sgla_out_ref, *scratch[N_GDN_SCRATCH:], tb=tb)
    for _ in _interleave(gdn, gla, EVEN_STAGE_RATIO):
        pass
    gdn_finish()
    gla_finish()


def _even_prompt(x, pre_g, w_in, w_gla, cw, cb, hp, ng_gdn, wlr, blr, ng_gla, tb):
    b, l, d = x.shape
    assert tb % (GDN_PREP_CHUNKS * CHUNK) == 0 and l % tb == 0
    kern = functools.partial(_even_prompt_kernel, tb=tb)
    wcol = lambda blk: (lambda bi, i: (0, blk))
    const = lambda bi, i: (0, 0)
    gdn_scratch = [pltpu.VMEM((tb + 8, GDN_CONV_DIM), f32),
                   pltpu.VMEM((tb, GDN_CONV_DIM), f32),
                   pltpu.VMEM((tb, GDN_VW), f32),
                   pltpu.VMEM((tb, LANES), f32),
                   pltpu.VMEM((tb, LANES), f32),
                   pltpu.VMEM((GDN_HEADS, GDN_DK, GDN_DV), f32),
                   pltpu.VMEM((max(8, tb // CHUNK), LANES), f32),
                   pltpu.VMEM((tb, GDN_VW), f32),
                   pltpu.VMEM((tb, GDN_KW), MM_DTYPE),
                   pltpu.VMEM((tb, GDN_KW), MM_DTYPE),
                   pltpu.VMEM((tb, GDN_KW), MM_DTYPE),
                   pltpu.VMEM((GDN_HEADS, tb, CHUNK), MM_DTYPE)]
    assert len(gdn_scratch) == N_GDN_SCRATCH
    gla_scratch = [pltpu.VMEM((tb, GLA_KW), f32),
                   pltpu.VMEM((tb, GLA_KW), f32),
                   pltpu.VMEM((tb, GLA_VW), MM_DTYPE),
                   pltpu.VMEM((tb, GLA_VW), f32),
                   pltpu.VMEM((tb, GLA_KW), f32),
                   pltpu.VMEM((GLA_HEADS, GLA_DV, GLA_DK), f32)]
    return pl.pallas_call(
        kern,
        grid=(b, l // tb),
        in_specs=[pl.BlockSpec((None, tb, d), lambda bi, i: (bi, i, 0)),
                  pl.BlockSpec((1, d), const),
                  pl.BlockSpec((d, E_OFF_GSMALL), const),
                  pl.BlockSpec((d, LANES), wcol(E_OFF_GSMALL // LANES)),
                  pl.BlockSpec((d, GLA_KW), wcol(0)),
                  pl.BlockSpec((d, GLA_KW), wcol(L_OFF_K // GLA_KW)),
                  pl.BlockSpec((d, GLA_VW), wcol(L_OFF_V // GLA_VW)),
                  pl.BlockSpec((d, GLA_VW), wcol(L_OFF_GATE // GLA_VW)),
                  pl.BlockSpec((d, LANES), wcol(L_OFF_SMALL // LANES)),
                  pl.BlockSpec((CONV_W, GDN_CONV_DIM), const),
                  pl.BlockSpec((1, GDN_CONV_DIM), const),
                  pl.BlockSpec((8, LANES), const),
                  pl.BlockSpec((1, GDN_DV), const),
                  pl.BlockSpec((LANES, GLA_KW), const),
                  pl.BlockSpec((1, GLA_KW), const),
                  pl.BlockSpec((1, GLA_DV), const)],
        out_specs=[pl.BlockSpec((None, tb, GDN_VW + GLA_VW), lambda bi, i: (bi, i, 0)),
                   pl.BlockSpec((None, CONV_W - 1, GDN_CONV_DIM), lambda bi, i: (bi, 0, 0)),
                   pl.BlockSpec((None, GDN_HEADS, GDN_DK, GDN_DV), lambda bi, i: (bi, 0, 0, 0)),
                   pl.BlockSpec((None, GLA_HEADS, GLA_DK, GLA_DV), lambda bi, i: (bi, 0, 0, 0))],
        out_shape=[jax.ShapeDtypeStruct((b, l, GDN_VW + GLA_VW), MM_DTYPE),
                   jax.ShapeDtypeStruct((b, CONV_W - 1, GDN_CONV_DIM), f32),
                   jax.ShapeDtypeStruct((b, GDN_HEADS, GDN_DK, GDN_DV), f32),
                   jax.ShapeDtypeStruct((b, GLA_HEADS, GLA_DK, GLA_DV), f32)],
        scratch_shapes=gdn_scratch + gla_scratch,
        compiler_params=pltpu.CompilerParams(dimension_semantics=("parallel", "arbitrary"),
                                             vmem_limit_bytes=VMEM_LIMIT),
        name="even_prompt",
    )(x, pre_g, w_in, w_in, w_gla, w_gla, w_gla, w_gla, w_gla, cw, cb, hp, ng_gdn, wlr, blr, ng_gla)


def _head_lanes(x):
    return jnp.where(lax.broadcasted_iota(jnp.int32, x.shape, 1) < SSD_HEADS, x, 0.0)


def _ssd_prompt_kernel(x_ref, pg_ref, wx_ref, wz_ref, wbc_ref, wdt_ref, cw_ref, cb_ref, hp_ref, dskip_ref, ng_ref,
                       expand_ref, xs_ref, zs_ref, bcs_ref, dts_ref, bufs_ref, hs_ref,
                       y_ref, conv_out_ref, h_out_ref, ys_ref, new_bufs_ref, hs_out_ref,
                       xp_scr, xbc_scr, z_ref, lam_scr, lamx_scr, xdt_scr, h_scr, *, tb):
    i = pl.program_id(1)
    first = i == 0
    conv_cols = _conv_block(xp_scr, cw_ref, cb_ref, tb, first)

    @pl.when(first)
    def _():
        h_scr[...] = jnp.zeros(h_scr.shape, f32)

    _ssd_decode_kernel(xs_ref, zs_ref, bcs_ref, dts_ref, bufs_ref, hs_ref, cw_ref, cb_ref, hp_ref, dskip_ref, ng_ref,
                       expand_ref, ys_ref, new_bufs_ref, hs_out_ref, bb=xs_ref.shape[0])

    u = _normed(x_ref, pg_ref)
    for c0 in range(0, SSD_DI, 1024):
        xp_scr[8:8 + tb, c0:c0 + 1024] = _proj(u, wx_ref, c0, c0 + 1024)
    xp_scr[8:8 + tb, SSD_DI:SSD_CONV_DIM] = _proj(u, wbc_ref, 0, 2 * SSD_BC)
    dt_raw = _head_lanes(_proj(u, wdt_ref, 0, LANES))
    for c0 in range(0, SSD_DI, 1024):
        z_ref[:, c0:c0 + 1024] = _proj(u, wz_ref, c0, c0 + 1024)

    for c0 in range(0, SSD_CONV_DIM, 512):
        xbc_scr[:, c0:c0 + 512] = conv_cols(c0, c0 + 512)
    dt = _softplus(dt_raw + hp_ref[0:1, :])
    tr = lax.broadcasted_iota(jnp.int32, (tb, tb), 0)
    tc = lax.broadcasted_iota(jnp.int32, (tb, tb), 1)
    lam_all = _sel_mm(jnp.logical_and(tr >= tc, tr // CHUNK == tc // CHUNK).astype(f32), dt * hp_ref[1:2, :])
    lam_scr[...] = lam_all
    lo_blk = lax.broadcasted_iota(jnp.int32, (tb, LANES), 1) < SSD_P
    for p in range(SSD_HEADS // 2):
        cols = slice(p * LANES, (p + 1) * LANES)
        k0 = 2 * p
        xdt_scr[:, cols] = xbc_scr[:, cols] * jnp.where(lo_blk, dt[:, k0:k0 + 1], dt[:, k0 + 1:k0 + 2])
    for c0 in range(0, SSD_DI, 512):
        lamx_scr[:, c0:c0 + 512] = _sel_rows(lam_all, expand_ref[:, c0:c0 + 512])

    lane = lax.broadcasted_iota(jnp.int32, (CHUNK, LANES), 1)
    lo = lane < SSD_P
    incl2 = lax.broadcasted_iota(jnp.int32, (CHUNK, LANES), 0) >= lane % CHUNK
    top = lax.broadcasted_iota(jnp.int32, (2 * SSD_P, SSD_N), 0) < SSD_P
    n_pairs = SSD_HEADS // 2
    ppg = SSD_K // 2
    groups = range(SSD_G)

    def chunk_body(c):
        rows = pl.ds(c * CHUNK, CHUNK)
        lam = lam_scr[rows, :]
        lam_t = lam.T
        llast = lam[CHUNK - 1:CHUNK, :]
        bm = [xbc_scr[rows, SSD_DI + g * SSD_N:SSD_DI + (g + 1) * SSD_N].astype(MM_DTYPE) for g in groups]
        cm = [xbc_scr[rows, SSD_DI + SSD_BC + g * SSD_N:SSD_DI + SSD_BC + (g + 1) * SSD_N].astype(MM_DTYPE)
              for g in groups]
        h_old = [h_scr[g * SSD_GW:(g + 1) * SSD_GW, :] for g in groups]
        cb2 = [_mm_nt(cm[g], jnp.concatenate([bm[g], bm[g]], axis=0)) for g in groups]
        y_inter = [_mm_nt(cm[g], h_old[g]) for g in groups]
        xs, xdt, xdec, lcol = [], [], [], []
        for p in range(n_pairs):
            cols = slice(p * LANES, (p + 1) * LANES)
            xs.append(xbc_scr[rows, cols])
            lcol.append(lamx_scr[rows, cols])
            xdt.append(xdt_scr[rows, cols])
            xdec.append((xdt[p] * jnp.exp(lcol[p][CHUNK - 1:CHUNK, :] - lcol[p])).astype(MM_DTYPE))
        upd = [_mm_tn(jnp.concatenate(xdec[g * ppg:(g + 1) * ppg], axis=-1), bm[g]) for g in groups]
        y_intra = []
        for p in range(n_pairs):
            k0 = 2 * p
            lrow = jnp.concatenate([lam_t[k0:k0 + 1, :], lam_t[k0 + 1:k0 + 2, :]], axis=-1)
            m = cb2[p // ppg] * _masked_decay(lcol[p], lrow, incl2)
            blockdiag = jnp.concatenate([jnp.where(lo, xdt[p], 0.0), jnp.where(lo, 0.0, xdt[p])], axis=0)
            y_intra.append(_mm(m, blockdiag))
        for g in groups:
            ys = []
            for pp in range(ppg):
                p = g * ppg + pp
                k0 = 2 * p
                ys.append(y_intra[p] + jnp.exp(lcol[p]) * y_inter[g][:, pp * LANES:(pp + 1) * LANES]
                          + dskip_ref[:, p * LANES:(p + 1) * LANES] * xs[p])
                el = jnp.where(top, jnp.exp(llast[:, k0:k0 + 1]), jnp.exp(llast[:, k0 + 1:k0 + 2]))
                h_scr[p * LANES:(p + 1) * LANES, :] = (el * h_old[g][pp * LANES:(pp + 1) * LANES, :]
                                                        + upd[g][pp * LANES:(pp + 1) * LANES, :])
            yg = jnp.concatenate(ys, axis=-1) * _silu(z_ref[rows, g * SSD_GW:(g + 1) * SSD_GW])
            y_ref[rows, g * SSD_GW:(g + 1) * SSD_GW] = (
                _rms(yg, SSD_GW) * ng_ref[:, g * SSD_GW:(g + 1) * SSD_GW]).astype(y_ref.dtype)

    for c in range(tb // CHUNK):
        chunk_body(c)

    @pl.when(i == pl.num_programs(1) - 1)
    def _():
        conv_out_ref[...] = xp_scr[tb + 5:tb + 8, :]
        h_out_ref[...] = h_scr[...]


def _ssd_prompt(x, pre_g, w_in, cw, cb, hp, dskip, ng, expand, proj_s, buf_s, h_s, tb):
    b, l, d = x.shape
    nblk = l // tb
    n_steps = b * nblk
    s_total = proj_s.shape[0]
    sb = s_total // n_steps
    assert sb * n_steps == s_total
    kern = functools.partial(_ssd_prompt_kernel, tb=tb)
    wcol = lambda blk: (lambda bi, i: (0, blk))
    const = lambda bi, i: (0, 0)
    scol = lambda blk: (lambda bi, i: (bi * nblk + i, 0, blk))
    proj_s = proj_s.reshape(n_steps, sb, O_PROJ_N)
    buf_s = buf_s.reshape(CONV_W - 1, n_steps, sb, SSD_CONV_DIM)
    buf_spec = pl.BlockSpec((CONV_W - 1, None, sb, SSD_CONV_DIM), lambda bi, i: (0, bi * nblk + i, 0, 0))
    hs_spec = pl.BlockSpec((sb, SSD_HEADS * SSD_P, SSD_N), lambda bi, i: (bi * nblk + i, 0, 0))
    y_p, conv_p, h_p, y_s, new_buf_s, h_s_new = pl.pallas_call(
        kern,
        grid=(b, l // tb),
        in_specs=[pl.BlockSpec((None, tb, d), lambda bi, i: (bi, i, 0)),
                  pl.BlockSpec((1, d), const),
                  pl.BlockSpec((d, SSD_DI), wcol(O_OFF_X // SSD_DI)),
                  pl.BlockSpec((d, SSD_DI), wcol(0)),
                  pl.BlockSpec((d, 2 * SSD_BC), wcol(O_OFF_BC // (2 * SSD_BC))),
                  pl.BlockSpec((d, LANES), wcol(O_OFF_DT // LANES)),
                  pl.BlockSpec((CONV_W, SSD_CONV_DIM), const),
                  pl.BlockSpec((1, SSD_CONV_DIM), const),
                  pl.BlockSpec((8, LANES), const),
                  pl.BlockSpec((1, SSD_DI), const),
                  pl.BlockSpec((1, SSD_DI), const),
                  pl.BlockSpec((LANES, SSD_DI), const),
                  pl.BlockSpec((None, sb, SSD_DI), scol(O_OFF_X // SSD_DI)),
                  pl.BlockSpec((None, sb, SSD_DI), scol(0)),
                  pl.BlockSpec((None, sb, 2 * SSD_BC), scol(O_OFF_BC // (2 * SSD_BC))),
                  pl.BlockSpec((None, sb, LANES), scol(O_OFF_DT // LANES)),
                  buf_spec,
                  hs_spec],
        out_specs=[pl.BlockSpec((None, tb, SSD_DI), lambda bi, i: (bi, i, 0)),
                   pl.BlockSpec((None, CONV_W - 1, SSD_CONV_DIM), lambda bi, i: (bi, 0, 0)),
                   pl.BlockSpec((None, SSD_HEADS * SSD_P, SSD_N), lambda bi, i: (bi, 0, 0)),
                   pl.BlockSpec((None, sb, SSD_DI), scol(0)),
                   buf_spec,
                   hs_spec],
        out_shape=[jax.ShapeDtypeStruct((b, l, SSD_DI), MM_DTYPE),
                   jax.ShapeDtypeStruct((b, CONV_W - 1, SSD_CONV_DIM), f32),
                   jax.ShapeDtypeStruct((b, SSD_HEADS * SSD_P, SSD_N), f32),
                   jax.ShapeDtypeStruct((n_steps, sb, SSD_DI), MM_DTYPE),
                   jax.ShapeDtypeStruct((CONV_W - 1, n_steps, sb, SSD_CONV_DIM), f32),
                   jax.ShapeDtypeStruct((s_total, SSD_HEADS * SSD_P, SSD_N), f32)],
        scratch_shapes=[pltpu.VMEM((tb + 8, SSD_CONV_DIM), f32),
                        pltpu.VMEM((tb, SSD_CONV_DIM), f32),
                        pltpu.VMEM((tb, SSD_DI), f32),
                        pltpu.VMEM((tb, LANES), f32),
                        pltpu.VMEM((tb, SSD_DI), f32),
                        pltpu.VMEM((tb, SSD_DI), f32),
                        pltpu.VMEM((SSD_HEADS * SSD_P, SSD_N), f32)],
        compiler_params=pltpu.CompilerParams(dimension_semantics=("parallel", "arbitrary"),
                                             vmem_limit_bytes=VMEM_LIMIT),
        name="ssd_prompt",
    )(x, pre_g, w_in, w_in, w_in, w_in, cw, cb, hp, dskip, ng, expand, proj_s, proj_s, proj_s, proj_s, buf_s, h_s)
    return (y_p, conv_p, h_p, y_s.reshape(s_total, SSD_DI),
            new_buf_s.reshape(CONV_W - 1, s_total, SSD_CONV_DIM), h_s_new)


def _pad_rows(row, n):
    return jnp.concatenate([row, jnp.zeros((n - 1, row.shape[1]), row.dtype)], axis=0)


def _bcast_cols(row, width):
    hi, mid, lo = _split3(row)
    lhs = jnp.concatenate([hi, mid, lo, jnp.zeros((13, row.shape[1]), bf16)], axis=0)
    sel = (lax.broadcasted_iota(jnp.int32, (16, width), 0) < 3).astype(bf16)
    return lax.dot_general(lhs, sel, (((0,), (0,)), ((), ())), preferred_element_type=f32)


def _conv_step(buf_ref, x, cw_ref, cb_ref, new_buf_ref):
    acc = cb_ref[...] + x * cw_ref[3:4, :]
    for s in range(CONV_W - 1):
        acc = acc + buf_ref[s] * cw_ref[s:s + 1, :]
    for s in range(CONV_W - 2):
        new_buf_ref[s] = buf_ref[s + 1]
    new_buf_ref[CONV_W - 2] = x
    return _silu(acc)


def _gdn_decode_kernel(q_ref, k_ref, v_ref, gate_ref, small_ref, buf_ref, s_ref, cw_ref, cb_ref, hp_ref, ng_ref,
                       o_ref, new_buf_ref, s_out_ref, *, bb):
    x = jnp.concatenate([q_ref[...], k_ref[...], v_ref[...]], axis=-1)
    qkv = _conv_step(buf_ref, x, cw_ref, cb_ref, new_buf_ref)
    small = small_ref[...]
    beta = _sigmoid(small)
    eg = jnp.exp(hp_ref[0:1, :] * _softplus(small + hp_ref[1:2, :]))
    ng = ng_ref[...]
    gate = gate_ref[...]
    for h in range(GDN_HEADS):
        c0 = h * GDN_DK
        q = qkv[:, c0:c0 + GDN_DK]
        q = q * lax.rsqrt(jnp.sum(q * q, axis=-1, keepdims=True) + EPS) * (GDN_DK ** -0.5)
        k = qkv[:, GDN_KW + c0:GDN_KW + c0 + GDN_DK]
        k = k * lax.rsqrt(jnp.sum(k * k, axis=-1, keepdims=True) + EPS)
        v = qkv[:, 2 * GDN_KW + c0:2 * GDN_KW + c0 + GDN_DV]
        qk = jnp.sum(q * k, axis=-1, keepdims=True)
        b_h = beta[:, SMALL_BETA + h:SMALL_BETA + h + 1]
        eg_h = eg[:, SMALL_A + h:SMALL_A + h + 1]
        seqs = range(bb)
        s_old = [s_ref[b, h] for b in seqs]
        r = [_mm(jnp.concatenate([k[b:b + 1], q[b:b + 1], jnp.zeros((6, GDN_DK), f32)], axis=0), s_old[b])
             for b in seqs]
        u = [b_h[b:b + 1] * v[b:b + 1] - (b_h[b:b + 1] * eg_h[b:b + 1]) * r[b][0:1] for b in seqs]
        upd = [_mm_tn(_pad_rows(k[b:b + 1], 16), _pad_rows(u[b], 16)) for b in seqs]
        for b in seqs:
            s_out_ref[b, h] = eg_h[b:b + 1] * s_old[b] + upd[b]
        o = jnp.concatenate([eg_h[b:b + 1] * r[b][1:2] + qk[b:b + 1] * u[b] for b in seqs], axis=0)
        o_ref[:, c0:c0 + GDN_DV] = (_rms(o, GDN_DV) * ng * _silu(gate[:, c0:c0 + GDN_DV])).astype(o_ref.dtype)


def _gdn_decode(proj, buf, s, cw, cb, hp, ng, bb):
    b = proj.shape[0]
    wq = GDN_KW
    kern = functools.partial(_gdn_decode_kernel, bb=bb)
    col = lambda blk: (lambda i: (i, blk))
    const = lambda i: (0, 0)
    return pl.pallas_call(
        kern,
        grid=(b // bb,),
        in_specs=[pl.BlockSpec((bb, wq), col(0)),
                  pl.BlockSpec((bb, wq), col(1)),
                  pl.BlockSpec((bb, wq), col(2)),
                  pl.BlockSpec((bb, wq), col(3)),
                  pl.BlockSpec((bb, LANES), col(E_OFF_GSMALL // LANES)),
                  pl.BlockSpec((CONV_W - 1, bb, GDN_CONV_DIM), lambda i: (0, i, 0)),
                  pl.BlockSpec((bb, GDN_HEADS, GDN_DK, GDN_DV), lambda i: (i, 0, 0, 0)),
                  pl.BlockSpec((CONV_W, GDN_CONV_DIM), const),
                  pl.BlockSpec((1, GDN_CONV_DIM), const),
                  pl.BlockSpec((8, LANES), const),
                  pl.BlockSpec((1, GDN_DV), const)],
        out_specs=[pl.BlockSpec((bb, GDN_VW), col(0)),
                   pl.BlockSpec((CONV_W - 1, bb, GDN_CONV_DIM), lambda i: (0, i, 0)),
                   pl.BlockSpec((bb, GDN_HEADS, GDN_DK, GDN_DV), lambda i: (i, 0, 0, 0))],
        out_shape=[jax.ShapeDtypeStruct((b, GDN_VW), MM_DTYPE),
                   jax.ShapeDtypeStruct((CONV_W - 1, b, GDN_CONV_DIM), f32),
                   jax.ShapeDtypeStruct((b, GDN_HEADS, GDN_DK, GDN_DV), f32)],
        compiler_params=pltpu.CompilerParams(dimension_semantics=("parallel",), vmem_limit_bytes=VMEM_LIMIT),
        name="gdn_decode",
    )(proj, proj, proj, proj, proj, buf, s, cw, cb, hp, ng)


def _gla_decode_kernel(q_ref, k_ref, v_ref, gate_ref, small_ref, s_ref, wlr_ref, blr_ref, ng_ref,
                       o_ref, s_out_ref, *, bb):
    z = _mm(small_ref[...], wlr_ref[...]) + blr_ref[...]
    la = -_softplus(-z) * (1.0 / GLA_TAU)
    dec = jnp.exp(la)
    qe = q_ref[...] * (GLA_DK ** -0.5) * dec
    kx = k_ref[...]
    ke = kx * jnp.exp(-la)
    vx = v_ref[...]
    ng = ng_ref[...]
    gate = gate_ref[...]
    for h in range(GLA_HEADS):
        k0 = h * GLA_DK
        v0 = h * GLA_DV
        p = jnp.sum(qe[:, k0:k0 + GLA_DK] * ke[:, k0:k0 + GLA_DK], axis=-1, keepdims=True)
        outs = []
        for b in range(bb):
            s_old = s_ref[b, h]
            vrow = vx[b:b + 1, v0:v0 + GLA_DV]
            outs.append(_mm(_pad_rows(qe[b:b + 1, k0:k0 + GLA_DK], 8), s_old)[0:1] + p[b:b + 1] * vrow)
            s_out_ref[b, h] = (_bcast_cols(dec[b:b + 1, k0:k0 + GLA_DK], GLA_DV) * s_old
                               + _mm_tn(_pad_rows(kx[b:b + 1, k0:k0 + GLA_DK], 16), _pad_rows(vrow, 16)))
        o = jnp.concatenate(outs, axis=0)
        o_ref[:, v0:v0 + GLA_DV] = (_rms(o, GLA_DV) * ng * _silu(gate[:, v0:v0 + GLA_DV])).astype(o_ref.dtype)


def _gla_decode(proj, s, wlr, blr, ng, bb):
    b = proj.shape[0]
    kern = functools.partial(_gla_decode_kernel, bb=bb)
    col = lambda blk: (lambda i: (i, blk))
    const = lambda i: (0, 0)
    return pl.pallas_call(
        kern,
        grid=(b // bb,),
        in_specs=[pl.BlockSpec((bb, GLA_KW), col(0)),
                  pl.BlockSpec((bb, GLA_KW), col(L_OFF_K // GLA_KW)),
                  pl.BlockSpec((bb, GLA_VW), col(L_OFF_V // GLA_VW)),
                  pl.BlockSpec((bb, GLA_VW), col(L_OFF_GATE // GLA_VW)),
                  pl.BlockSpec((bb, LANES), col(L_OFF_SMALL // LANES)),
                  pl.BlockSpec((bb, GLA_HEADS, GLA_DK, GLA_DV), lambda i: (i, 0, 0, 0)),
                  pl.BlockSpec((LANES, GLA_KW), const),
                  pl.BlockSpec((1, GLA_KW), const),
                  pl.BlockSpec((1, GLA_DV), const)],
        out_specs=[pl.BlockSpec((bb, GLA_VW), col(0)),
                   pl.BlockSpec((bb, GLA_HEADS, GLA_DK, GLA_DV), lambda i: (i, 0, 0, 0))],
        out_shape=[jax.ShapeDtypeStruct((b, GLA_VW), MM_DTYPE),
                   jax.ShapeDtypeStruct((b, GLA_HEADS, GLA_DK, GLA_DV), f32)],
        compiler_params=pltpu.CompilerParams(dimension_semantics=("parallel",), vmem_limit_bytes=VMEM_LIMIT),
        name="gla_decode",
    )(proj, proj, proj, proj, proj, s, wlr, blr, ng)


def _ssd_decode_kernel(x_ref, z_ref, bc_ref, dt_ref, buf_ref, h_ref, cw_ref, cb_ref, hp_ref, dskip_ref, ng_ref,
                       expand_ref, y_ref, new_buf_ref, h_out_ref, *, bb):
    xin = jnp.concatenate([x_ref[...], bc_ref[...]], axis=-1)
    xbc = _conv_step(buf_ref, xin, cw_ref, cb_ref, new_buf_ref)
    xs = xbc[:, :SSD_DI]
    dt = _softplus(_head_lanes(dt_ref[...]) + hp_ref[0:1, :])
    el = jnp.exp(dt * hp_ref[1:2, :])
    expand = expand_ref[...]
    dt_x = _sel_rows(dt, expand)
    el_x = _sel_rows(el, expand)
    xdt = xs * dt_x
    z = z_ref[...]
    for g in range(SSD_G):
        gs = slice(g * SSD_GW, (g + 1) * SSD_GW)
        bm = xbc[:, SSD_DI + g * SSD_N:SSD_DI + (g + 1) * SSD_N]
        cm = xbc[:, SSD_DI + SSD_BC + g * SSD_N:SSD_DI + SSD_BC + (g + 1) * SSD_N]
        cbs = jnp.sum(cm * bm, axis=-1, keepdims=True)
        outs = []
        for b in range(bb):
            hg = h_ref[b, gs, :]
            y_inter = _mm_nt(_pad_rows(cm[b:b + 1], 8), hg)[0:1]
            outs.append(el_x[b:b + 1, gs] * y_inter)
            h_out_ref[b, gs, :] = (_bcast_cols(el_x[b:b + 1, gs], SSD_N) * hg
                                   + _mm_tn(_pad_rows(xdt[b:b + 1, gs], 16), _pad_rows(bm[b:b + 1], 16)))
        y = jnp.concatenate(outs, axis=0) + cbs * xdt[:, gs] + dskip_ref[:, gs] * xs[:, gs]
        yg = y * _silu(z[:, gs])
        y_ref[:, gs] = (_rms(yg, SSD_GW) * ng_ref[:, gs]).astype(y_ref.dtype)


def _ssd_decode(proj, buf, hst, cw, cb, hp, dskip, ng, expand, bb):
    b = proj.shape[0]
    kern = functools.partial(_ssd_decode_kernel, bb=bb)
    col = lambda blk: (lambda i: (i, blk))
    const = lambda i: (0, 0)
    return pl.pallas_call(
        kern,
        grid=(b // bb,),
        in_specs=[pl.BlockSpec((bb, SSD_DI), col(O_OFF_X // SSD_DI)),
                  pl.BlockSpec((bb, SSD_DI), col(0)),
                  pl.BlockSpec((bb, 2 * SSD_BC), col(O_OFF_BC // (2 * SSD_BC))),
                  pl.BlockSpec((bb, LANES), col(O_OFF_DT // LANES)),
                  pl.BlockSpec((CONV_W - 1, bb, SSD_CONV_DIM), lambda i: (0, i, 0)),
                  pl.BlockSpec((bb, SSD_HEADS * SSD_P, SSD_N), lambda i: (i, 0, 0)),
                  pl.BlockSpec((CONV_W, SSD_CONV_DIM), const),
                  pl.BlockSpec((1, SSD_CONV_DIM), const),
                  pl.BlockSpec((8, LANES), const),
                  pl.BlockSpec((1, SSD_DI), const),
                  pl.BlockSpec((1, SSD_DI), const),
                  pl.BlockSpec((LANES, SSD_DI), const)],
        out_specs=[pl.BlockSpec((bb, SSD_DI), col(0)),
                   pl.BlockSpec((CONV_W - 1, bb, SSD_CONV_DIM), lambda i: (0, i, 0)),
                   pl.BlockSpec((bb, SSD_HEADS * SSD_P, SSD_N), lambda i: (i, 0, 0))],
        out_shape=[jax.ShapeDtypeStruct((b, SSD_DI), MM_DTYPE),
                   jax.ShapeDtypeStruct((CONV_W - 1, b, SSD_CONV_DIM), f32),
                   jax.ShapeDtypeStruct((b, SSD_HEADS * SSD_P, SSD_N), f32)],
        compiler_params=pltpu.CompilerParams(dimension_semantics=("parallel",), vmem_limit_bytes=VMEM_LIMIT),
        name="ssd_decode",
    )(proj, proj, proj, proj, buf, hst, cw, cb, hp, dskip, ng, expand)


def _pack_gla_w_in(w):
    gla = w[:, E_OFF_GLA:]
    return jnp.concatenate([gla, jnp.zeros((w.shape[0], L_N - gla.shape[1]), w.dtype)], axis=1)


def _lane_row(vals, offset):
    return jnp.zeros((LANES,), f32).at[offset:offset + vals.shape[0]].set(vals.astype(f32))


def kernel(x_prompt, x_sample, state_gdn_conv, state_gdn, state_gla, state_ssd_conv, state_ssd, e_pre_g, e_post_g, e_w_in, e_conv_w, e_conv_b, gdn_a_log, gdn_dt_bias, gdn_norm_g, gla_w_lr, gla_b_lr, gla_norm_g, e_w_out, o_pre_g, o_post_g, o_w_in, ssd_conv_w, ssd_conv_b, ssd_dt_bias, ssd_a_log, ssd_d, ssd_norm_g, o_w_out):
    bp, lp, d = x_prompt.shape
    bs = x_sample.shape[0]
    tp = bp * lp
    tb = min(lp, 256)
    bb = min(bs, 8)

    w_in0 = e_w_in[0].astype(MM_DTYPE)
    w_gla = _pack_gla_w_in(w_in0)
    w_out0 = e_w_out[0].astype(MM_DTYPE)
    pre0 = e_pre_g[0].reshape(1, d)
    post0 = e_post_g[0].reshape(1, d)
    cw0 = e_conv_w[0]
    cb0 = e_conv_b[0].reshape(1, GDN_CONV_DIM)
    hp0 = jnp.zeros((8, LANES), f32)
    hp0 = hp0.at[0].set(_lane_row(-jnp.exp(gdn_a_log[0].astype(f32)), SMALL_A))
    hp0 = hp0.at[1].set(_lane_row(gdn_dt_bias[0], SMALL_A))
    ng_gdn = gdn_norm_g[0].reshape(1, GDN_DV)
    wlr = jnp.zeros((LANES, GLA_KW), f32).at[SMALL_LR:SMALL_LR + GLA_RANK].set(gla_w_lr[0]).astype(MM_DTYPE)
    blr = gla_b_lr[0].reshape(1, GLA_KW)
    ng_gla = gla_norm_g[0].reshape(1, GLA_DV)

    hp_flat = x_prompt.reshape(tp, d)
    hs_flat = x_sample.reshape(bs, d)

    o_p, gdn_conv_p, gdn_p, gla_p = _even_prompt(x_prompt, pre0, w_in0, w_gla, cw0, cb0, hp0, ng_gdn, wlr, blr, ng_gla,
                                                 tb)
    o_p = o_p.reshape(tp, GDN_VW + GLA_VW)
    hp_flat = _out_proj(o_p, o_p, w_out0, post0, hp_flat)

    proj_gdn = _norm_proj(hs_flat, pre0, w_in0, E_GDN_N // 3, E_GDN_N)
    proj_gla = _norm_proj(hs_flat, pre0, w_gla, L_N // 5, L_N)
    o1_s, gdn_conv_s, gdn_s = _gdn_decode(proj_gdn, jnp.swapaxes(state_gdn_conv[0], 0, 1), state_gdn[0],
                                          cw0, cb0, hp0, ng_gdn, bb)
    o2_s, gla_s = _gla_decode(proj_gla, state_gla[0], wlr, blr, ng_gla, bb)
    hs_flat = _out_proj(o1_s, o2_s, w_out0, post0, hs_flat)

    w_in1 = o_w_in[0].astype(MM_DTYPE)
    w_out1 = o_w_out[0].astype(MM_DTYPE)
    pre1 = o_pre_g[0].reshape(1, d)
    post1 = o_post_g[0].reshape(1, d)
    cw1 = ssd_conv_w[0]
    cb1 = ssd_conv_b[0].reshape(1, SSD_CONV_DIM)
    hp1 = jnp.zeros((8, LANES), f32)
    hp1 = hp1.at[0].set(_lane_row(ssd_dt_bias[0], 0))
    hp1 = hp1.at[1].set(_lane_row(-jnp.exp(ssd_a_log[0].astype(f32)), 0))
    dskip = jnp.repeat(ssd_d[0].astype(f32), SSD_P).reshape(1, SSD_DI)
    ng_ssd = ssd_norm_g[0].reshape(1, SSD_DI)
    expand = (lax.broadcasted_iota(jnp.int32, (LANES, SSD_DI), 0)
              == lax.broadcasted_iota(jnp.int32, (LANES, SSD_DI), 1) // SSD_P).astype(MM_DTYPE)

    proj_s = _norm_proj(hs_flat, pre1, w_in1, O_PROJ_N // 6, O_PROJ_N)
    y_p, ssd_conv_p, ssd_p, y_s, ssd_conv_s, ssd_s = _ssd_prompt(
        hp_flat.reshape(bp, lp, d), pre1, w_in1, cw1, cb1, hp1, dskip, ng_ssd, expand,
        proj_s, jnp.swapaxes(state_ssd_conv[0], 0, 1), state_ssd[0].reshape(bs, SSD_HEADS * SSD_P, SSD_N), tb)
    y_p = y_p.reshape(tp, SSD_DI)
    hp_flat = _out_proj(y_p, y_p, w_out1, post1, hp_flat)
    hs_flat = _out_proj(y_s, y_s, w_out1, post1, hs_flat)

    return (hp_flat.reshape(bp, lp, d), hs_flat.reshape(bs, 1, d),
            gdn_conv_p[None], gdn_p[None], gla_p[None],
            ssd_conv_p[None], ssd_p.reshape(bp, SSD_HEADS, SSD_P, SSD_N)[None],
            jnp.swapaxes(gdn_conv_s, 0, 1)[None], gdn_s[None], gla_s[None],
            jnp.swapaxes(ssd_conv_s, 0, 1)[None],
            ssd_s.reshape(bs, SSD_HEADS, SSD_P, SSD_N)[None])
```

```python
import functools

import jax
import jax.numpy as jnp
from jax import lax
from jax.experimental import pallas as pl
from jax.experimental.pallas import tpu as pltpu

f32 = jnp.float32
bf16 = jnp.bfloat16
MM_DTYPE = jnp.bfloat16

D_MODEL = 1024
CONV_W = 4
EPS = 1e-6
LANES = 128

GDN_HEADS = 8
GDN_DK = 128
GDN_DV = 128
GDN_KW = GDN_HEADS * GDN_DK
GDN_VW = GDN_HEADS * GDN_DV
GDN_CONV_DIM = 2 * GDN_KW + GDN_VW
GLA_HEADS = 4
GLA_DK = 128
GLA_DV = 256
GLA_KW = GLA_HEADS * GLA_DK
GLA_VW = GLA_HEADS * GLA_DV
GLA_RANK = 16
GLA_TAU = 16.0
SSD_DI = 2 * D_MODEL
SSD_P = 64
SSD_HEADS = SSD_DI // SSD_P
SSD_N = 128
SSD_G = 4
SSD_K = SSD_HEADS // SSD_G
SSD_GW = SSD_DI // SSD_G
SSD_BC = SSD_G * SSD_N
SSD_CONV_DIM = SSD_DI + 2 * SSD_BC

E_OFF_GATE = 3 * GDN_KW
E_OFF_GSMALL = 4 * GDN_KW
E_GDN_N = E_OFF_GSMALL + LANES
E_OFF_GLA = E_OFF_GSMALL + 2 * GDN_HEADS
SMALL_BETA = 0
SMALL_A = GDN_HEADS
L_OFF_K = GLA_KW
L_OFF_V = 2 * GLA_KW
L_OFF_GATE = L_OFF_V + GLA_VW
L_OFF_SMALL = L_OFF_GATE + GLA_VW
L_N = L_OFF_SMALL + LANES
SMALL_LR = 0
O_OFF_X = SSD_DI
O_OFF_BC = 2 * SSD_DI
O_OFF_DT = 2 * SSD_DI + 2 * SSD_BC
O_PROJ_N = 6 * 1024

CHUNK = 64
GDN_PREP_CHUNKS = 2
assert SSD_P == CHUNK and 2 * SSD_P == LANES
VMEM_LIMIT = 56 * 1024 * 1024


def _mm(a, b):
    return jnp.dot(a.astype(MM_DTYPE), b.astype(MM_DTYPE), preferred_element_type=f32)


def _mm_nt(a, b):
    return lax.dot_general(a.astype(MM_DTYPE), b.astype(MM_DTYPE), (((1,), (1,)), ((), ())),
                           preferred_element_type=f32)


def _mm_tn(a, b):
    return lax.dot_general(a.astype(MM_DTYPE), b.astype(MM_DTYPE), (((0,), (0,)), ((), ())),
                           preferred_element_type=f32)


def _split3(x):
    hi = x.astype(bf16)
    r = x - hi.astype(f32)
    mid = r.astype(bf16)
    lo = (r - mid.astype(f32)).astype(bf16)
    return hi, mid, lo


def _sel_mm(sel, x):
    hi, mid, lo = _split3(x)
    s = sel.astype(bf16)
    return (jnp.dot(s, hi, preferred_element_type=f32) + jnp.dot(s, mid, preferred_element_type=f32)
            + jnp.dot(s, lo, preferred_element_type=f32))


def _sel_rows(x, sel):
    hi, mid, lo = _split3(x)
    s = sel.astype(bf16)
    return (jnp.dot(hi, s, preferred_element_type=f32) + jnp.dot(mid, s, preferred_element_type=f32)
            + jnp.dot(lo, s, preferred_element_type=f32))


def _sigmoid(x):
    return 1.0 / (1.0 + jnp.exp(-x))


def _silu(x):
    return x * _sigmoid(x)


def _softplus(x):
    return jnp.maximum(x, 0.0) + jnp.log1p(jnp.exp(-jnp.abs(x)))


def _tril(n, strict=False):
    r = lax.broadcasted_iota(jnp.int32, (n, n), 0)
    c = lax.broadcasted_iota(jnp.int32, (n, n), 1)
    return (r > c) if strict else (r >= c)


def _masked_decay(col, row, incl):
    return jnp.exp(jnp.where(incl, col - row, -jnp.inf))


def _interleave(main, side, ratio):
    gens = [main, side]
    steps = {id(main): ratio, id(side): 1}
    while gens:
        for g in list(gens):
            for _ in range(steps[id(g)]):
                try:
                    next(g)
                except StopIteration:
                    gens.remove(g)
                    break
                yield


def _rms(x, width):
    return x * lax.rsqrt(jnp.sum(x * x, axis=-1, keepdims=True) * (1.0 / width) + EPS)


def _norm_proj_kernel(x_ref, g_ref, w_ref, o_ref, u_scr):
    @pl.when(pl.program_id(1) == 0)
    def _():
        u_scr[...] = (_rms(x_ref[...], D_MODEL) * g_ref[...]).astype(u_scr.dtype)

    o_ref[...] = jnp.dot(u_scr[...], w_ref[...], preferred_element_type=f32)


def _norm_proj(x, g, w, tn, n):
    t, d = x.shape
    assert n % tn == 0 and n - tn < w.shape[1]
    tm = min(t, 1024)
    return pl.pallas_call(
        _norm_proj_kernel,
        grid=(t // tm, n // tn),
        in_specs=[pl.BlockSpec((tm, d), lambda i, j: (i, 0)),
                  pl.BlockSpec((1, d), lambda i, j: (0, 0)),
                  pl.BlockSpec((d, tn), lambda i, j: (0, j))],
        out_specs=pl.BlockSpec((tm, tn), lambda i, j: (i, j)),
        out_shape=jax.ShapeDtypeStruct((t, n), f32),
        scratch_shapes=[pltpu.VMEM((tm, d), MM_DTYPE)],
        compiler_params=pltpu.CompilerParams(dimension_semantics=("parallel", "arbitrary"),
                                             vmem_limit_bytes=VMEM_LIMIT),
        name="norm_proj",
    )(x, g, w)


def _out_proj_kernel(oa_ref, ob_ref, wa_ref, wb_ref, g_ref, h_ref, y_ref):
    p = (jnp.dot(oa_ref[...], wa_ref[...], preferred_element_type=f32)
         + jnp.dot(ob_ref[...], wb_ref[...], preferred_element_type=f32))
    y_ref[...] = h_ref[...] + _rms(p, D_MODEL) * g_ref[...]


def _out_proj(oa, ob, w, g, h):
    t = oa.shape[0]
    e = w.shape[0] // 2
    assert oa.shape[1] == ob.shape[1] and oa.shape[1] in (e, 2 * e)
    col_b = 1 if ob.shape[1] == 2 * e else 0
    d = w.shape[1]
    tm = min(t, 1024)
    return pl.pallas_call(
        _out_proj_kernel,
        grid=(t // tm,),
        in_specs=[pl.BlockSpec((tm, e), lambda i: (i, 0)),
                  pl.BlockSpec((tm, e), lambda i: (i, col_b)),
                  pl.BlockSpec((e, d), lambda i: (0, 0)),
                  pl.BlockSpec((e, d), lambda i: (1, 0)),
                  pl.BlockSpec((1, d), lambda i: (0, 0)),
                  pl.BlockSpec((tm, d), lambda i: (i, 0))],
        out_specs=pl.BlockSpec((tm, d), lambda i: (i, 0)),
        out_shape=jax.ShapeDtypeStruct((t, d), f32),
        compiler_params=pltpu.CompilerParams(dimension_semantics=("parallel",),
                                             vmem_limit_bytes=VMEM_LIMIT),
        name="out_proj",
    )(oa, ob, w, w, g, h)


def _normed(x_ref, g_ref):
    return (_rms(x_ref[...], D_MODEL) * g_ref[...]).astype(MM_DTYPE)


def _proj(u, w_ref, c0, c1):
    return jnp.dot(u, w_ref[:, c0:c1], preferred_element_type=f32)


def _conv_block(xp_scr, cw_ref, cb_ref, tb, first):
    @pl.when(first)
    def _():
        xp_scr[0:8, :] = jnp.zeros((8, xp_scr.shape[1]), f32)

    @pl.when(jnp.logical_not(first))
    def _():
        xp_scr[5:8, :] = xp_scr[tb + 5:tb + 8, :]

    def conv_cols(c0, c1, r0=0, nr=tb):
        acc = cb_ref[:, c0:c1] + xp_scr[8 + r0:8 + r0 + nr, c0:c1] * cw_ref[3:4, c0:c1]
        for s in range(CONV_W - 1):
            acc = acc + xp_scr[5 + s + r0:5 + s + r0 + nr, c0:c1] * cw_ref[s:s + 1, c0:c1]
        return _silu(acc)

    return conv_cols


def _gdn_stages(u, first, last, wm_ref, ws_ref, cw_ref, cb_ref, hp_ref, ng_ref,
                o_ref, conv_out_ref, s_out_ref,
                xp_scr, qkv_scr, gate_scr, beta_scr, lg_scr, s_scr, gl_scr, up_scr, w_scr, qg_scr, kd_scr,
                p_scr, *, tb):
    conv_cols = _conv_block(xp_scr, cw_ref, cb_ref, tb, first)

    @pl.when(first)
    def _():
        s_scr[...] = jnp.zeros(s_scr.shape, f32)

    def proj_stages():
        for c0 in range(0, GDN_CONV_DIM, GDN_KW):
            xp_scr[8:8 + tb, c0:c0 + GDN_KW] = _proj(u, wm_ref, c0, c0 + GDN_KW)
            yield
        small = _proj(u, ws_ref, 0, LANES)
        beta_scr[...] = _sigmoid(small)
        lg_scr[...] = hp_ref[0:1, :] * _softplus(small + hp_ref[1:2, :])
        gate_scr[...] = _proj(u, wm_ref, E_OFF_GATE, E_OFF_GATE + GDN_VW)
        yield

    span = GDN_PREP_CHUNKS * CHUNK

    def conv_norm(it):
        r0 = it * span
        rr = slice(r0, r0 + span)
        for h in range(GDN_HEADS):
            c0 = h * GDN_DK
            q = conv_cols(c0, c0 + GDN_DK, r0, span)
            q = q * lax.rsqrt(jnp.sum(q * q, axis=-1, keepdims=True) + EPS) * (GDN_DK ** -0.5)
            qkv_scr[rr, c0:c0 + GDN_DK] = q
            k = conv_cols(GDN_KW + c0, GDN_KW + c0 + GDN_DK, r0, span)
            k = k * lax.rsqrt(jnp.sum(k * k, axis=-1, keepdims=True) + EPS)
            qkv_scr[rr, GDN_KW + c0:GDN_KW + c0 + GDN_DK] = k
            qkv_scr[rr, 2 * GDN_KW + c0:2 * GDN_KW + c0 + GDN_DV] = conv_cols(
                2 * GDN_KW + c0, 2 * GDN_KW + c0 + GDN_DV, r0, span)
            yield

    incl = _tril(CHUNK)
    strict = _tril(CHUNK, strict=True)
    eye = jnp.logical_and(incl, jnp.logical_not(strict)).astype(f32)
    sr = lax.broadcasted_iota(jnp.int32, (span, span), 0)
    sc = lax.broadcasted_iota(jnp.int32, (span, span), 1)
    lmat = jnp.logical_and(sr >= sc, sr // CHUNK == sc // CHUNK).astype(f32)
    ng = ng_ref[...]
    heads = range(GDN_HEADS)
    probs = [(ci, h) for ci in range(GDN_PREP_CHUNKS) for h in heads]
    n = range(len(probs))

    def prep(c):
        r0 = c * span
        gam_all = _sel_mm(lmat, lg_scr[pl.ds(r0, span), :])
        gam_t = gam_all.T
        beta_all = beta_scr[pl.ds(r0, span), :]
        rows, gcol, grow, glast, bcol = [], [], [], [], []
        for ci, h in probs:
            lo, hi = ci * CHUNK, (ci + 1) * CHUNK
            rows.append(pl.ds(r0 + lo, CHUNK))
            gcol.append(gam_all[lo:hi, SMALL_A + h:SMALL_A + h + 1])
            grow.append(gam_t[SMALL_A + h:SMALL_A + h + 1, lo:hi])
            glast.append(gam_all[hi - 1:hi, SMALL_A + h:SMALL_A + h + 1])
            bcol.append(beta_all[lo:hi, SMALL_BETA + h:SMALL_BETA + h + 1])
        for ci in range(GDN_PREP_CHUNKS):
            gl_scr[pl.ds(c * GDN_PREP_CHUNKS + ci, 1), :] = jnp.exp(gam_all[(ci + 1) * CHUNK - 1:(ci + 1) * CHUNK, :])
        q = [qkv_scr[rows[j], h * GDN_DK:(h + 1) * GDN_DK] for j, (_, h) in enumerate(probs)]
        k = [qkv_scr[rows[j], GDN_KW + h * GDN_DK:GDN_KW + (h + 1) * GDN_DK] for j, (_, h) in enumerate(probs)]
        v = [qkv_scr[rows[j], 2 * GDN_KW + h * GDN_DV:2 * GDN_KW + (h + 1) * GDN_DV] for j, (_, h) in enumerate(probs)]
        kb = [k[j].astype(MM_DTYPE) for j in n]
        kk = [_mm_nt(kb[j], kb[j]) for j in n]
        yield
        qk = [_mm_nt(q[j], kb[j]) for j in n]
        yield
        decay = [_masked_decay(gcol[j], grow[j], incl) for j in n]
        eg = [jnp.exp(gcol[j]) for j in n]
        for j, (_, h) in enumerate(probs):
            c0 = h * GDN_DK
            qg_scr[rows[j], c0:c0 + GDN_DK] = (eg[j] * q[j]).astype(MM_DTYPE)
            kd_scr[rows[j], c0:c0 + GDN_DK] = (jnp.exp(glast[j] - gcol[j]) * k[j]).astype(MM_DTYPE)
            p_scr[h, rows[j], :] = (qk[j] * decay[j]).astype(MM_DTYPE)
        nmat = [jnp.where(strict, bcol[j] * (kk[j] * decay[j]), 0.0) for j in n]
        tinv = [eye - nmat[j] for j in n]
        npow = nmat
        for _ in range(5):
            npow = [_mm(npow[j], npow[j]) for j in n]
            yield
            tinv = [tinv[j] + _mm(tinv[j], npow[j]) for j in n]
            yield
        x = [_mm(tinv[j], jnp.concatenate([bcol[j] * v[j], (bcol[j] * eg[j]) * k[j]], axis=-1)) for j in n]
        for j, (_, h) in enumerate(probs):
            c0 = h * GDN_DK
            up_scr[rows[j], c0:c0 + GDN_DV] = x[j][:, :GDN_DV]
            w_scr[rows[j], c0:c0 + GDN_DK] = x[j][:, GDN_DV:].astype(MM_DTYPE)

    state = [s_scr[h] for h in heads]

    def scan(c):
        rows = pl.ds(c * CHUNK, CHUNK)
        gl = gl_scr[c:c + 1, :]
        r = [_mm(jnp.concatenate([w_scr[rows, h * GDN_DK:(h + 1) * GDN_DK],
                                  qg_scr[rows, h * GDN_DK:(h + 1) * GDN_DK]], axis=0), state[h]) for h in heads]
        yield
        us = [(up_scr[rows, h * GDN_DV:(h + 1) * GDN_DV] - r[h][:CHUNK]).astype(MM_DTYPE) for h in heads]
        o = [r[h][CHUNK:] + _mm(p_scr[h, rows, :], us[h]) for h in heads]
        yield
        upd = [_mm_tn(kd_scr[rows, h * GDN_DK:(h + 1) * GDN_DK], us[h]) for h in heads]
        yield
        for h in heads:
            c0 = h * GDN_DV
            state[h] = gl[:, SMALL_A + h:SMALL_A + h + 1] * state[h] + upd[h]
            gate = gate_scr[rows, c0:c0 + GDN_DV]
            o_ref[rows, c0:c0 + GDN_DV] = (_rms(o[h], GDN_DV) * ng * _silu(gate)).astype(o_ref.dtype)
        yield

    def scans(chunks):
        for c in chunks:
            yield from scan(c)

    def side_work(it):
        if it + 1 < n_prep:
            yield from conv_norm(it + 1)
        if it >= 1:
            yield from scans(range((it - 1) * GDN_PREP_CHUNKS, it * GDN_PREP_CHUNKS))

    n_prep = tb // span

    def stages():
        yield from proj_stages()
        yield from conv_norm(0)
        for it in range(n_prep):
            yield from _interleave(prep(it), side_work(it), 1)
        yield from scans(range((n_prep - 1) * GDN_PREP_CHUNKS, n_prep * GDN_PREP_CHUNKS))

    def finish():
        for h in heads:
            s_scr[h] = state[h]

        @pl.when(last)
        def _():
            conv_out_ref[...] = xp_scr[tb + 5:tb + 8, :]
            s_out_ref[...] = s_scr[...]

    return stages(), finish


def _gla_stages(u, first, last, wq_ref, wk_ref, wv_ref, wg_ref, ws_ref, wlr_ref, blr_ref, ng_ref,
                o_ref, s_out_ref, q_ref, k_ref, v_ref, gate_ref, la_scr, st_scr, *, tb):
    @pl.when(first)
    def _():
        st_scr[...] = jnp.zeros(st_scr.shape, f32)

    def proj_stages():
        q_ref[...] = _proj(u, wq_ref, 0, GLA_KW)
        k_ref[...] = _proj(u, wk_ref, 0, GLA_KW)
        small = _proj(u, ws_ref, 0, LANES)
        z = _mm(small, wlr_ref[...]) + blr_ref[...]
        la_scr[...] = -_softplus(-z) * (1.0 / GLA_TAU)
        yield
        v_ref[...] = _proj(u, wv_ref, 0, GLA_VW).astype(v_ref.dtype)
        yield
        gate_ref[...] = _proj(u, wg_ref, 0, GLA_VW)
        yield

    incl = _tril(CHUNK)
    lmat = incl.astype(f32)
    ng = ng_ref[...]

    heads = range(GLA_HEADS)
    n_chunks = tb // CHUNK
    staged = [None] * n_chunks
    state = [st_scr[h] for h in heads]

    def prep(c):
        rows = pl.ds(c * CHUNK, CHUNK)
        bcum = [_sel_mm(lmat, la_scr[rows, h * GLA_DK:(h + 1) * GLA_DK]) for h in heads]
        yield
        k = [k_ref[rows, h * GLA_DK:(h + 1) * GLA_DK] for h in heads]
        v = [v_ref[rows, h * GLA_DV:(h + 1) * GLA_DV].astype(MM_DTYPE) for h in heads]
        qe = [(q_ref[rows, h * GLA_DK:(h + 1) * GLA_DK] * (GLA_DK ** -0.5) * jnp.exp(bcum[h])).astype(MM_DTYPE)
              for h in heads]
        ke = [k[h] * jnp.exp(-bcum[h]) for h in heads]
        blast = [bcum[h][CHUNK - 1:CHUNK, :] for h in heads]
        kdec = [(k[h] * jnp.exp(blast[h] - bcum[h])).astype(MM_DTYPE) for h in heads]
        p = [jnp.where(incl, _mm_nt(qe[h], ke[h]), 0.0) for h in heads]
        yield
        o_intra = [_mm(p[h], v[h]) for h in heads]
        staged[c] = (qe, v, kdec, blast, o_intra)
        yield

    def scan(c):
        rows = pl.ds(c * CHUNK, CHUNK)
        qe, v, kdec, blast, o_intra = staged[c]
        o = [o_intra[h] + _mm_nt(qe[h], state[h]) for h in heads]
        upd = [_mm_tn(v[h], kdec[h]) for h in heads]
        yield
        for h in heads:
            v0 = h * GLA_DV
            state[h] = jnp.exp(blast[h]) * state[h] + upd[h]
            gate = gate_ref[rows, v0:v0 + GLA_DV]
            o_ref[rows, GDN_VW + v0:GDN_VW + v0 + GLA_DV] = (
                _rms(o[h], GLA_DV) * ng * _silu(gate)).astype(o_ref.dtype)
        yield

    def stages():
        yield from proj_stages()
        yield from prep(0)
        for c in range(1, n_chunks):
            yield from _interleave(prep(c), scan(c - 1), 1)
        yield from scan(n_chunks - 1)

    def finish():
        for h in heads:
            st_scr[h] = state[h]

        @pl.when(last)
        def _():
            for h in heads:
                s_out_ref[h] = st_scr[h].T

    return stages(), finish


N_GDN_SCRATCH = 12
EVEN_STAGE_RATIO = 2


def _even_prompt_kernel(x_ref, pg_ref, wm_ref, ws_ref, wq_ref, wk_ref, wv_ref, wg_ref, wsl_ref, cw_ref, cb_ref, hp_ref,
                        ngd_ref, wlr_ref, blr_ref, ngl_ref,
                        o_ref, conv_out_ref, sgdn_out_ref, sgla_out_ref, *scratch, tb):
    i = pl.program_id(1)
    first = i == 0
    last = i == pl.num_programs(1) - 1
    u = _normed(x_ref, pg_ref)
    gdn, gdn_finish = _gdn_stages(u, first, last, wm_ref, ws_ref, cw_ref, cb_ref, hp_ref, ngd_ref,
                                  o_ref, conv_out_ref, sgdn_out_ref, *scratch[:N_GDN_SCRATCH], tb=tb)
    gla, gla_finish = _gla_stages(u, first, last, wq_ref, wk_ref, wv_ref, wg_ref, wsl_ref, wlr_ref, blr_ref, ngl_ref,
                                  o_ref, sgla_out_ref, *scratch[N_GDN_SCRATCH:], tb=tb)
    for _ in _interleave(gdn, gla, EVEN_STAGE_RATIO):
        pass
    gdn_finish()
    gla_finish()


def _even_prompt(x, pre_g, w_in, w_gla, cw, cb, hp, ng_gdn, wlr, blr, ng_gla, tb):
    b, l, d = x.shape
    assert tb % (GDN_PREP_CHUNKS * CHUNK) == 0 and l % tb == 0
    kern = functools.partial(_even_prompt_kernel, tb=tb)
    wcol = lambda blk: (lambda bi, i: (0, blk))
    const = lambda bi, i: (0, 0)
    gdn_scratch = [pltpu.VMEM((tb + 8, GDN_CONV_DIM), f32),
                   pltpu.VMEM((tb, GDN_CONV_DIM), f32),
                   pltpu.VMEM((tb, GDN_VW), f32),
                   pltpu.VMEM((tb, LANES), f32),
                   pltpu.VMEM((tb, LANES), f32),
                   pltpu.VMEM((GDN_HEADS, GDN_DK, GDN_DV), f32),
                   pltpu.VMEM((max(8, tb // CHUNK), LANES), f32),
                   pltpu.VMEM((tb, GDN_VW), f32),
                   pltpu.VMEM((tb, GDN_KW), MM_DTYPE),
                   pltpu.VMEM((tb, GDN_KW), MM_DTYPE),
                   pltpu.VMEM((tb, GDN_KW), MM_DTYPE),
                   pltpu.VMEM((GDN_HEADS, tb, CHUNK), MM_DTYPE)]
    assert len(gdn_scratch) == N_GDN_SCRATCH
    gla_scratch = [pltpu.VMEM((tb, GLA_KW), f32),
                   pltpu.VMEM((tb, GLA_KW), f32),
                   pltpu.VMEM((tb, GLA_VW), MM_DTYPE),
                   pltpu.VMEM((tb, GLA_VW), f32),
                   pltpu.VMEM((tb, GLA_KW), f32),
                   pltpu.VMEM((GLA_HEADS, GLA_DV, GLA_DK), f32)]
    return pl.pallas_call(
        kern,
        grid=(b, l // tb),
        in_specs=[pl.BlockSpec((None, tb, d), lambda bi, i: (bi, i, 0)),
                  pl.BlockSpec((1, d), const),
                  pl.BlockSpec((d, E_OFF_GSMALL), const),
                  pl.BlockSpec((d, LANES), wcol(E_OFF_GSMALL // LANES)),
                  pl.BlockSpec((d, GLA_KW), wcol(0)),
                  pl.BlockSpec((d, GLA_KW), wcol(L_OFF_K // GLA_KW)),
                  pl.BlockSpec((d, GLA_VW), wcol(L_OFF_V // GLA_VW)),
                  pl.BlockSpec((d, GLA_VW), wcol(L_OFF_GATE // GLA_VW)),
                  pl.BlockSpec((d, LANES), wcol(L_OFF_SMALL // LANES)),
                  pl.BlockSpec((CONV_W, GDN_CONV_DIM), const),
                  pl.BlockSpec((1, GDN_CONV_DIM), const),
                  pl.BlockSpec((8, LANES), const),
                  pl.BlockSpec((1, GDN_DV), const),
                  pl.BlockSpec((LANES, GLA_KW), const),
                  pl.BlockSpec((1, GLA_KW), const),
                  pl.BlockSpec((1, GLA_DV), const)],
        out_specs=[pl.BlockSpec((None, tb, GDN_VW + GLA_VW), lambda bi, i: (bi, i, 0)),
                   pl.BlockSpec((None, CONV_W - 1, GDN_CONV_DIM), lambda bi, i: (bi, 0, 0)),
                   pl.BlockSpec((None, GDN_HEADS, GDN_DK, GDN_DV), lambda bi, i: (bi, 0, 0, 0)),
                   pl.BlockSpec((None, GLA_HEADS, GLA_DK, GLA_DV), lambda bi, i: (bi, 0, 0, 0))],
        out_shape=[jax.ShapeDtypeStruct((b, l, GDN_VW + GLA_VW), MM_DTYPE),
                   jax.ShapeDtypeStruct((b, CONV_W - 1, GDN_CONV_DIM), f32),
                   jax.ShapeDtypeStruct((b, GDN_HEADS, GDN_DK, GDN_DV), f32),
                   jax.ShapeDtypeStruct((b, GLA_HEADS, GLA_DK, GLA_DV), f32)],
        scratch_shapes=gdn_scratch + gla_scratch,
        compiler_params=pltpu.CompilerParams(dimension_semantics=("parallel", "arbitrary"),
                                             vmem_limit_bytes=VMEM_LIMIT),
        name="even_prompt",
    )(x, pre_g, w_in, w_in, w_gla, w_gla, w_gla, w_gla, w_gla, cw, cb, hp, ng_gdn, wlr, blr, ng_gla)


def _head_lanes(x):
    return jnp.where(lax.broadcasted_iota(jnp.int32, x.shape, 1) < SSD_HEADS, x, 0.0)


def _ssd_prompt_kernel(x_ref, pg_ref, wx_ref, wz_ref, wbc_ref, wdt_ref, cw_ref, cb_ref, hp_ref, dskip_ref, ng_ref,
                       expand_ref, y_ref, conv_out_ref, h_out_ref,
                       xp_scr, xbc_scr, z_ref, lam_scr, lamx_scr, xdt_scr, h_scr, *, tb):
    i = pl.program_id(1)
    first = i == 0
    conv_cols = _conv_block(xp_scr, cw_ref, cb_ref, tb, first)

    @pl.when(first)
    def _():
        h_scr[...] = jnp.zeros(h_scr.shape, f32)

    u = _normed(x_ref, pg_ref)
    for c0 in range(0, SSD_DI, 1024):
        xp_scr[8:8 + tb, c0:c0 + 1024] = _proj(u, wx_ref, c0, c0 + 1024)
    xp_scr[8:8 + tb, SSD_DI:SSD_CONV_DIM] = _proj(u, wbc_ref, 0, 2 * SSD_BC)
    dt_raw = _head_lanes(_proj(u, wdt_ref, 0, LANES))
    for c0 in range(0, SSD_DI, 1024):
        z_ref[:, c0:c0 + 1024] = _proj(u, wz_ref, c0, c0 + 1024)

    for c0 in range(0, SSD_CONV_DIM, 512):
        xbc_scr[:, c0:c0 + 512] = conv_cols(c0, c0 + 512)
    dt = _softplus(dt_raw + hp_ref[0:1, :])
    tr = lax.broadcasted_iota(jnp.int32, (tb, tb), 0)
    tc = lax.broadcasted_iota(jnp.int32, (tb, tb), 1)
    lam_all = _sel_mm(jnp.logical_and(tr >= tc, tr // CHUNK == tc // CHUNK).astype(f32), dt * hp_ref[1:2, :])
    lam_scr[...] = lam_all
    lo_blk = lax.broadcasted_iota(jnp.int32, (tb, LANES), 1) < SSD_P
    for p in range(SSD_HEADS // 2):
        cols = slice(p * LANES, (p + 1) * LANES)
        k0 = 2 * p
        xdt_scr[:, cols] = xbc_scr[:, cols] * jnp.where(lo_blk, dt[:, k0:k0 + 1], dt[:, k0 + 1:k0 + 2])
    for c0 in range(0, SSD_DI, 512):
        lamx_scr[:, c0:c0 + 512] = _sel_rows(lam_all, expand_ref[:, c0:c0 + 512])

    lane = lax.broadcasted_iota(jnp.int32, (CHUNK, LANES), 1)
    lo = lane < SSD_P
    incl2 = lax.broadcasted_iota(jnp.int32, (CHUNK, LANES), 0) >= lane % CHUNK
    top = lax.broadcasted_iota(jnp.int32, (2 * SSD_P, SSD_N), 0) < SSD_P
    n_pairs = SSD_HEADS // 2
    ppg = SSD_K // 2
    groups = range(SSD_G)

    def chunk_body(c):
        rows = pl.ds(c * CHUNK, CHUNK)
        lam = lam_scr[rows, :]
        lam_t = lam.T
        llast = lam[CHUNK - 1:CHUNK, :]
        bm = [xbc_scr[rows, SSD_DI + g * SSD_N:SSD_DI + (g + 1) * SSD_N].astype(MM_DTYPE) for g in groups]
        cm = [xbc_scr[rows, SSD_DI + SSD_BC + g * SSD_N:SSD_DI + SSD_BC + (g + 1) * SSD_N].astype(MM_DTYPE)
              for g in groups]
        h_old = [h_scr[g * SSD_GW:(g + 1) * SSD_GW, :] for g in groups]
        cb2 = [_mm_nt(cm[g], jnp.concatenate([bm[g], bm[g]], axis=0)) for g in groups]
        y_inter = [_mm_nt(cm[g], h_old[g]) for g in groups]
        xs, xdt, xdec, lcol = [], [], [], []
        for p in range(n_pairs):
            cols = slice(p * LANES, (p + 1) * LANES)
            xs.append(xbc_scr[rows, cols])
            lcol.append(lamx_scr[rows, cols])
            xdt.append(xdt_scr[rows, cols])
            xdec.append((xdt[p] * jnp.exp(lcol[p][CHUNK - 1:CHUNK, :] - lcol[p])).astype(MM_DTYPE))
        upd = [_mm_tn(jnp.concatenate(xdec[g * ppg:(g + 1) * ppg], axis=-1), bm[g]) for g in groups]
        y_intra = []
        for p in range(n_pairs):
            k0 = 2 * p
            lrow = jnp.concatenate([lam_t[k0:k0 + 1, :], lam_t[k0 + 1:k0 + 2, :]], axis=-1)
            m = cb2[p // ppg] * _masked_decay(lcol[p], lrow, incl2)
            blockdiag = jnp.concatenate([jnp.where(lo, xdt[p], 0.0), jnp.where(lo, 0.0, xdt[p])], axis=0)
            y_intra.append(_mm(m, blockdiag))
        for g in groups:
            ys = []
            for pp in range(ppg):
                p = g * ppg + pp
                k0 = 2 * p
                ys.append(y_intra[p] + jnp.exp(lcol[p]) * y_inter[g][:, pp * LANES:(pp + 1) * LANES]
                          + dskip_ref[:, p * LANES:(p + 1) * LANES] * xs[p])
                el = jnp.where(top, jnp.exp(llast[:, k0:k0 + 1]), jnp.exp(llast[:, k0 + 1:k0 + 2]))
                h_scr[p * LANES:(p + 1) * LANES, :] = (el * h_old[g][pp * LANES:(pp + 1) * LANES, :]
                                                        + upd[g][pp * LANES:(pp + 1) * LANES, :])
            yg = jnp.concatenate(ys, axis=-1) * _silu(z_ref[rows, g * SSD_GW:(g + 1) * SSD_GW])
            y_ref[rows, g * SSD_GW:(g + 1) * SSD_GW] = (
                _rms(yg, SSD_GW) * ng_ref[:, g * SSD_GW:(g + 1) * SSD_GW]).astype(y_ref.dtype)

    for c in range(tb // CHUNK):
        chunk_body(c)

    @pl.when(i == pl.num_programs(1) - 1)
    def _():
        conv_out_ref[...] = xp_scr[tb + 5:tb + 8, :]
        h_out_ref[...] = h_scr[...]


def _ssd_prompt(x, pre_g, w_in, cw, cb, hp, dskip, ng, expand, tb):
    b, l, d = x.shape
    kern = functools.partial(_ssd_prompt_kernel, tb=tb)
    wcol = lambda blk: (lambda bi, i: (0, blk))
    const = lambda bi, i: (0, 0)
    return pl.pallas_call(
        kern,
        grid=(b, l // tb),
        in_specs=[pl.BlockSpec((None, tb, d), lambda bi, i: (bi, i, 0)),
                  pl.BlockSpec((1, d), const),
                  pl.BlockSpec((d, SSD_DI), wcol(O_OFF_X // SSD_DI)),
                  pl.BlockSpec((d, SSD_DI), wcol(0)),
                  pl.BlockSpec((d, 2 * SSD_BC), wcol(O_OFF_BC // (2 * SSD_BC))),
                  pl.BlockSpec((d, LANES), wcol(O_OFF_DT // LANES)),
                  pl.BlockSpec((CONV_W, SSD_CONV_DIM), const),
                  pl.BlockSpec((1, SSD_CONV_DIM), const),
                  pl.BlockSpec((8, LANES), const),
                  pl.BlockSpec((1, SSD_DI), const),
                  pl.BlockSpec((1, SSD_DI), const),
                  pl.BlockSpec((LANES, SSD_DI), const)],
        out_specs=[pl.BlockSpec((None, tb, SSD_DI), lambda bi, i: (bi, i, 0)),
                   pl.BlockSpec((None, CONV_W - 1, SSD_CONV_DIM), lambda bi, i: (bi, 0, 0)),
                   pl.BlockSpec((None, SSD_HEADS * SSD_P, SSD_N), lambda bi, i: (bi, 0, 0))],
        out_shape=[jax.ShapeDtypeStruct((b, l, SSD_DI), MM_DTYPE),
                   jax.ShapeDtypeStruct((b, CONV_W - 1, SSD_CONV_DIM), f32),
                   jax.ShapeDtypeStruct((b, SSD_HEADS * SSD_P, SSD_N), f32)],
        scratch_shapes=[pltpu.VMEM((tb + 8, SSD_CONV_DIM), f32),
                        pltpu.VMEM((tb, SSD_CONV_DIM), f32),
                        pltpu.VMEM((tb, SSD_DI), f32),
                        pltpu.VMEM((tb, LANES), f32),
                        pltpu.VMEM((tb, SSD_DI), f32),
                        pltpu.VMEM((tb, SSD_DI), f32),
                        pltpu.VMEM((SSD_HEADS * SSD_P, SSD_N), f32)],
        compiler_params=pltpu.CompilerParams(dimension_semantics=("parallel", "arbitrary"),
                                             vmem_limit_bytes=VMEM_LIMIT),
        name="ssd_prompt",
    )(x, pre_g, w_in, w_in, w_in, w_in, cw, cb, hp, dskip, ng, expand)


def _pad_rows(row, n):
    return jnp.concatenate([row, jnp.zeros((n - 1, row.shape[1]), row.dtype)], axis=0)


def _bcast_cols(row, width):
    hi, mid, lo = _split3(row)
    lhs = jnp.concatenate([hi, mid, lo, jnp.zeros((13, row.shape[1]), bf16)], axis=0)
    sel = (lax.broadcasted_iota(jnp.int32, (16, width), 0) < 3).astype(bf16)
    return lax.dot_general(lhs, sel, (((0,), (0,)), ((), ())), preferred_element_type=f32)


def _conv_step(buf_ref, x, cw_ref, cb_ref, new_buf_ref):
    acc = cb_ref[...] + x * cw_ref[3:4, :]
    for s in range(CONV_W - 1):
        acc = acc + buf_ref[s] * cw_ref[s:s + 1, :]
    for s in range(CONV_W - 2):
        new_buf_ref[s] = buf_ref[s + 1]
    new_buf_ref[CONV_W - 2] = x
    return _silu(acc)


def _gdn_decode_kernel(q_ref, k_ref, v_ref, gate_ref, small_ref, buf_ref, s_ref, cw_ref, cb_ref, hp_ref, ng_ref,
                       o_ref, new_buf_ref, s_out_ref, *, bb):
    x = jnp.concatenate([q_ref[...], k_ref[...], v_ref[...]], axis=-1)
    qkv = _conv_step(buf_ref, x, cw_ref, cb_ref, new_buf_ref)
    small = small_ref[...]
    beta = _sigmoid(small)
    eg = jnp.exp(hp_ref[0:1, :] * _softplus(small + hp_ref[1:2, :]))
    ng = ng_ref[...]
    gate = gate_ref[...]
    for h in range(GDN_HEADS):
        c0 = h * GDN_DK
        q = qkv[:, c0:c0 + GDN_DK]
        q = q * lax.rsqrt(jnp.sum(q * q, axis=-1, keepdims=True) + EPS) * (GDN_DK ** -0.5)
        k = qkv[:, GDN_KW + c0:GDN_KW + c0 + GDN_DK]
        k = k * lax.rsqrt(jnp.sum(k * k, axis=-1, keepdims=True) + EPS)
        v = qkv[:, 2 * GDN_KW + c0:2 * GDN_KW + c0 + GDN_DV]
        qk = jnp.sum(q * k, axis=-1, keepdims=True)
        b_h = beta[:, SMALL_BETA + h:SMALL_BETA + h + 1]
        eg_h = eg[:, SMALL_A + h:SMALL_A + h + 1]
        seqs = range(bb)
        s_old = [s_ref[b, h] for b in seqs]
        r = [_mm(jnp.concatenate([k[b:b + 1], q[b:b + 1], jnp.zeros((6, GDN_DK), f32)], axis=0), s_old[b])
             for b in seqs]
        u = [b_h[b:b + 1] * v[b:b + 1] - (b_h[b:b + 1] * eg_h[b:b + 1]) * r[b][0:1] for b in seqs]
        upd = [_mm_tn(_pad_rows(k[b:b + 1], 16), _pad_rows(u[b], 16)) for b in seqs]
        for b in seqs:
            s_out_ref[b, h] = eg_h[b:b + 1] * s_old[b] + upd[b]
        o = jnp.concatenate([eg_h[b:b + 1] * r[b][1:2] + qk[b:b + 1] * u[b] for b in seqs], axis=0)
        o_ref[:, c0:c0 + GDN_DV] = (_rms(o, GDN_DV) * ng * _silu(gate[:, c0:c0 + GDN_DV])).astype(o_ref.dtype)


def _gdn_decode(proj, buf, s, cw, cb, hp, ng, bb):
    b = proj.shape[0]
    wq = GDN_KW
    kern = functools.partial(_gdn_decode_kernel, bb=bb)
    col = lambda blk: (lambda i: (i, blk))
    const = lambda i: (0, 0)
    return pl.pallas_call(
        kern,
        grid=(b // bb,),
        in_specs=[pl.BlockSpec((bb, wq), col(0)),
                  pl.BlockSpec((bb, wq), col(1)),
                  pl.BlockSpec((bb, wq), col(2)),
                  pl.BlockSpec((bb, wq), col(3)),
                  pl.BlockSpec((bb, LANES), col(E_OFF_GSMALL // LANES)),
                  pl.BlockSpec((CONV_W - 1, bb, GDN_CONV_DIM), lambda i: (0, i, 0)),
                  pl.BlockSpec((bb, GDN_HEADS, GDN_DK, GDN_DV), lambda i: (i, 0, 0, 0)),
                  pl.BlockSpec((CONV_W, GDN_CONV_DIM), const),
                  pl.BlockSpec((1, GDN_CONV_DIM), const),
                  pl.BlockSpec((8, LANES), const),
                  pl.BlockSpec((1, GDN_DV), const)],
        out_specs=[pl.BlockSpec((bb, GDN_VW), col(0)),
                   pl.BlockSpec((CONV_W - 1, bb, GDN_CONV_DIM), lambda i: (0, i, 0)),
                   pl.BlockSpec((bb, GDN_HEADS, GDN_DK, GDN_DV), lambda i: (i, 0, 0, 0))],
        out_shape=[jax.ShapeDtypeStruct((b, GDN_VW), MM_DTYPE),
                   jax.ShapeDtypeStruct((CONV_W - 1, b, GDN_CONV_DIM), f32),
                   jax.ShapeDtypeStruct((b, GDN_HEADS, GDN_DK, GDN_DV), f32)],
        compiler_params=pltpu.CompilerParams(dimension_semantics=("parallel",), vmem_limit_bytes=VMEM_LIMIT),
        name="gdn_decode",
    )(proj, proj, proj, proj, proj, buf, s, cw, cb, hp, ng)


def _gla_decode_kernel(q_ref, k_ref, v_ref, gate_ref, small_ref, s_ref, wlr_ref, blr_ref, ng_ref,
                       o_ref, s_out_ref, *, bb):
    z = _mm(small_ref[...], wlr_ref[...]) + blr_ref[...]
    la = -_softplus(-z) * (1.0 / GLA_TAU)
    dec = jnp.exp(la)
    qe = q_ref[...] * (GLA_DK ** -0.5) * dec
    kx = k_ref[...]
    ke = kx * jnp.exp(-la)
    vx = v_ref[...]
    ng = ng_ref[...]
    gate = gate_ref[...]
    for h in range(GLA_HEADS):
        k0 = h * GLA_DK
        v0 = h * GLA_DV
        p = jnp.sum(qe[:, k0:k0 + GLA_DK] * ke[:, k0:k0 + GLA_DK], axis=-1, keepdims=True)
        outs = []
        for b in range(bb):
            s_old = s_ref[b, h]
            vrow = vx[b:b + 1, v0:v0 + GLA_DV]
            outs.append(_mm(_pad_rows(qe[b:b + 1, k0:k0 + GLA_DK], 8), s_old)[0:1] + p[b:b + 1] * vrow)
            s_out_ref[b, h] = (_bcast_cols(dec[b:b + 1, k0:k0 + GLA_DK], GLA_DV) * s_old
                               + _mm_tn(_pad_rows(kx[b:b + 1, k0:k0 + GLA_DK], 16), _pad_rows(vrow, 16)))
        o = jnp.concatenate(outs, axis=0)
        o_ref[:, v0:v0 + GLA_DV] = (_rms(o, GLA_DV) * ng * _silu(gate[:, v0:v0 + GLA_DV])).astype(o_ref.dtype)


def _gla_decode(proj, s, wlr, blr, ng, bb):
    b = proj.shape[0]
    kern = functools.partial(_gla_decode_kernel, bb=bb)
    col = lambda blk: (lambda i: (i, blk))
    const = lambda i: (0, 0)
    return pl.pallas_call(
        kern,
        grid=(b // bb,),
        in_specs=[pl.BlockSpec((bb, GLA_KW), col(0)),
                  pl.BlockSpec((bb, GLA_KW), col(L_OFF_K // GLA_KW)),
                  pl.BlockSpec((bb, GLA_VW), col(L_OFF_V // GLA_VW)),
                  pl.BlockSpec((bb, GLA_VW), col(L_OFF_GATE // GLA_VW)),
                  pl.BlockSpec((bb, LANES), col(L_OFF_SMALL // LANES)),
                  pl.BlockSpec((bb, GLA_HEADS, GLA_DK, GLA_DV), lambda i: (i, 0, 0, 0)),
                  pl.BlockSpec((LANES, GLA_KW), const),
                  pl.BlockSpec((1, GLA_KW), const),
                  pl.BlockSpec((1, GLA_DV), const)],
        out_specs=[pl.BlockSpec((bb, GLA_VW), col(0)),
                   pl.BlockSpec((bb, GLA_HEADS, GLA_DK, GLA_DV), lambda i: (i, 0, 0, 0))],
        out_shape=[jax.ShapeDtypeStruct((b, GLA_VW), MM_DTYPE),
                   jax.ShapeDtypeStruct((b, GLA_HEADS, GLA_DK, GLA_DV), f32)],
        compiler_params=pltpu.CompilerParams(dimension_semantics=("parallel",), vmem_limit_bytes=VMEM_LIMIT),
        name="gla_decode",
    )(proj, proj, proj, proj, proj, s, wlr, blr, ng)


def _ssd_decode_kernel(x_ref, z_ref, bc_ref, dt_ref, buf_ref, h_ref, cw_ref, cb_ref, hp_ref, dskip_ref, ng_ref,
                       expand_ref, y_ref, new_buf_ref, h_out_ref, *, bb):
    xin = jnp.concatenate([x_ref[...], bc_ref[...]], axis=-1)
    xbc = _conv_step(buf_ref, xin, cw_ref, cb_ref, new_buf_ref)
    xs = xbc[:, :SSD_DI]
    dt = _softplus(_head_lanes(dt_ref[...]) + hp_ref[0:1, :])
    el = jnp.exp(dt * hp_ref[1:2, :])
    expand = expand_ref[...]
    dt_x = _sel_rows(dt, expand)
    el_x = _sel_rows(el, expand)
    xdt = xs * dt_x
    z = z_ref[...]
    for g in range(SSD_G):
        gs = slice(g * SSD_GW, (g + 1) * SSD_GW)
        bm = xbc[:, SSD_DI + g * SSD_N:SSD_DI + (g + 1) * SSD_N]
        cm = xbc[:, SSD_DI + SSD_BC + g * SSD_N:SSD_DI + SSD_BC + (g + 1) * SSD_N]
        cbs = jnp.sum(cm * bm, axis=-1, keepdims=True)
        outs = []
        for b in range(bb):
            hg = h_ref[b, gs, :]
            y_inter = _mm_nt(_pad_rows(cm[b:b + 1], 8), hg)[0:1]
            outs.append(el_x[b:b + 1, gs] * y_inter)
            h_out_ref[b, gs, :] = (_bcast_cols(el_x[b:b + 1, gs], SSD_N) * hg
                                   + _mm_tn(_pad_rows(xdt[b:b + 1, gs], 16), _pad_rows(bm[b:b + 1], 16)))
        y = jnp.concatenate(outs, axis=0) + cbs * xdt[:, gs] + dskip_ref[:, gs] * xs[:, gs]
        yg = y * _silu(z[:, gs])
        y_ref[:, gs] = (_rms(yg, SSD_GW) * ng_ref[:, gs]).astype(y_ref.dtype)


def _ssd_decode(proj, buf, hst, cw, cb, hp, dskip, ng, expand, bb):
    b = proj.shape[0]
    kern = functools.partial(_ssd_decode_kernel, bb=bb)
    col = lambda blk: (lambda i: (i, blk))
    const = lambda i: (0, 0)
    return pl.pallas_call(
        kern,
        grid=(b // bb,),
        in_specs=[pl.BlockSpec((bb, SSD_DI), col(O_OFF_X // SSD_DI)),
                  pl.BlockSpec((bb, SSD_DI), col(0)),
                  pl.BlockSpec((bb, 2 * SSD_BC), col(O_OFF_BC // (2 * SSD_BC))),
                  pl.BlockSpec((bb, LANES), col(O_OFF_DT // LANES)),
                  pl.BlockSpec((CONV_W - 1, bb, SSD_CONV_DIM), lambda i: (0, i, 0)),
                  pl.BlockSpec((bb, SSD_HEADS * SSD_P, SSD_N), lambda i: (i, 0, 0)),
                  pl.BlockSpec((CONV_W, SSD_CONV_DIM), const),
                  pl.BlockSpec((1, SSD_CONV_DIM), const),
                  pl.BlockSpec((8, LANES), const),
                  pl.BlockSpec((1, SSD_DI), const),
                  pl.BlockSpec((1, SSD_DI), const),
                  pl.BlockSpec((LANES, SSD_DI), const)],
        out_specs=[pl.BlockSpec((bb, SSD_DI), col(0)),
                   pl.BlockSpec((CONV_W - 1, bb, SSD_CONV_DIM), lambda i: (0, i, 0)),
                   pl.BlockSpec((bb, SSD_HEADS * SSD_P, SSD_N), lambda i: (i, 0, 0))],
        out_shape=[jax.ShapeDtypeStruct((b, SSD_DI), MM_DTYPE),
                   jax.ShapeDtypeStruct((CONV_W - 1, b, SSD_CONV_DIM), f32),
                   jax.ShapeDtypeStruct((b, SSD_HEADS * SSD_P, SSD_N), f32)],
        compiler_params=pltpu.CompilerParams(dimension_semantics=("parallel",), vmem_limit_bytes=VMEM_LIMIT),
        name="ssd_decode",
    )(proj, proj, proj, proj, buf, hst, cw, cb, hp, dskip, ng, expand)


def _pack_gla_w_in(w):
    gla = w[:, E_OFF_GLA:]
    return jnp.concatenate([gla, jnp.zeros((w.shape[0], L_N - gla.shape[1]), w.dtype)], axis=1)


def _lane_row(vals, offset):
    return jnp.zeros((LANES,), f32).at[offset:offset + vals.shape[0]].set(vals.astype(f32))


def kernel(x_prompt, x_sample, state_gdn_conv, state_gdn, state_gla, state_ssd_conv, state_ssd, e_pre_g, e_post_g, e_w_in, e_conv_w, e_conv_b, gdn_a_log, gdn_dt_bias, gdn_norm_g, gla_w_lr, gla_b_lr, gla_norm_g, e_w_out, o_pre_g, o_post_g, o_w_in, ssd_conv_w, ssd_conv_b, ssd_dt_bias, ssd_a_log, ssd_d, ssd_norm_g, o_w_out):
    bp, lp, d = x_prompt.shape
    bs = x_sample.shape[0]
    tp = bp * lp
    tb = min(lp, 512)
    bb = min(bs, 8)

    w_in0 = e_w_in[0].astype(MM_DTYPE)
    w_gla = _pack_gla_w_in(w_in0)
    w_out0 = e_w_out[0].astype(MM_DTYPE)
    pre0 = e_pre_g[0].reshape(1, d)
    post0 = e_post_g[0].reshape(1, d)
    cw0 = e_conv_w[0]
    cb0 = e_conv_b[0].reshape(1, GDN_CONV_DIM)
    hp0 = jnp.zeros((8, LANES), f32)
    hp0 = hp0.at[0].set(_lane_row(-jnp.exp(gdn_a_log[0].astype(f32)), SMALL_A))
    hp0 = hp0.at[1].set(_lane_row(gdn_dt_bias[0], SMALL_A))
    ng_gdn = gdn_norm_g[0].reshape(1, GDN_DV)
    wlr = jnp.zeros((LANES, GLA_KW), f32).at[SMALL_LR:SMALL_LR + GLA_RANK].set(gla_w_lr[0]).astype(MM_DTYPE)
    blr = gla_b_lr[0].reshape(1, GLA_KW)
    ng_gla = gla_norm_g[0].reshape(1, GLA_DV)

    hp_flat = x_prompt.reshape(tp, d)
    hs_flat = x_sample.reshape(bs, d)

    o_p, gdn_conv_p, gdn_p, gla_p = _even_prompt(x_prompt, pre0, w_in0, w_gla, cw0, cb0, hp0, ng_gdn, wlr, blr, ng_gla,
                                                 tb)
    o_p = o_p.reshape(tp, GDN_VW + GLA_VW)
    hp_flat = _out_proj(o_p, o_p, w_out0, post0, hp_flat)

    proj_gdn = _norm_proj(hs_flat, pre0, w_in0, E_GDN_N // 3, E_GDN_N)
    proj_gla = _norm_proj(hs_flat, pre0, w_gla, L_N // 5, L_N)
    o1_s, gdn_conv_s, gdn_s = _gdn_decode(proj_gdn, jnp.swapaxes(state_gdn_conv[0], 0, 1), state_gdn[0],
                                          cw0, cb0, hp0, ng_gdn, bb)
    o2_s, gla_s = _gla_decode(proj_gla, state_gla[0], wlr, blr, ng_gla, bb)
    hs_flat = _out_proj(o1_s, o2_s, w_out0, post0, hs_flat)

    w_in1 = o_w_in[0].astype(MM_DTYPE)
    w_out1 = o_w_out[0].astype(MM_DTYPE)
    pre1 = o_pre_g[0].reshape(1, d)
    post1 = o_post_g[0].reshape(1, d)
    cw1 = ssd_conv_w[0]
    cb1 = ssd_conv_b[0].reshape(1, SSD_CONV_DIM)
    hp1 = jnp.zeros((8, LANES), f32)
    hp1 = hp1.at[0].set(_lane_row(ssd_dt_bias[0], 0))
    hp1 = hp1.at[1].set(_lane_row(-jnp.exp(ssd_a_log[0].astype(f32)), 0))
    dskip = jnp.repeat(ssd_d[0].astype(f32), SSD_P).reshape(1, SSD_DI)
    ng_ssd = ssd_norm_g[0].reshape(1, SSD_DI)
    expand = (lax.broadcasted_iota(jnp.int32, (LANES, SSD_DI), 0)
              == lax.broadcasted_iota(jnp.int32, (LANES, SSD_DI), 1) // SSD_P).astype(MM_DTYPE)

    y_p, ssd_conv_p, ssd_p = _ssd_prompt(hp_flat.reshape(bp, lp, d), pre1, w_in1, cw1, cb1, hp1, dskip, ng_ssd,
                                         expand, tb)
    y_p = y_p.reshape(tp, SSD_DI)
    hp_flat = _out_proj(y_p, y_p, w_out1, post1, hp_flat)

    proj_s = _norm_proj(hs_flat, pre1, w_in1, O_PROJ_N // 6, O_PROJ_N)
    y_s, ssd_conv_s, ssd_s = _ssd_decode(proj_s, jnp.swapaxes(state_ssd_conv[0], 0, 1),
                                         state_ssd[0].reshape(bs, SSD_HEADS * SSD_P, SSD_N),
                                         cw1, cb1, hp1, dskip, ng_ssd, expand, bb)
    hs_flat = _out_proj(y_s, y_s, w_out1, post1, hs_flat)

    return (hp_flat.reshape(bp, lp, d), hs_flat.reshape(bs, 1, d),
            gdn_conv_p[None], gdn_p[None], gla_p[None],
            ssd_conv_p[None], ssd_p.reshape(bp, SSD_HEADS, SSD_P, SSD_N)[None],
            jnp.swapaxes(gdn_conv_s, 0, 1)[None], gdn_s[None], gla_s[None],
            jnp.swapaxes(ssd_conv_s, 0, 1)[None],
            ssd_s.reshape(bs, SSD_HEADS, SSD_P, SSD_N)[None])
```

```python
import functools

import jax
import jax.numpy as jnp
from jax import lax
from jax.experimental import pallas as pl
from jax.experimental.pallas import tpu as pltpu

f32 = jnp.float32
bf16 = jnp.bfloat16
MM_DTYPE = jnp.bfloat16

D_MODEL = 1024
CONV_W = 4
EPS = 1e-6
LANES = 128

GDN_HEADS = 8
GDN_DK = 128
GDN_DV = 128
GDN_KW = GDN_HEADS * GDN_DK
GDN_VW = GDN_HEADS * GDN_DV
GDN_CONV_DIM = 2 * GDN_KW + GDN_VW
GLA_HEADS = 4
GLA_DK = 128
GLA_DV = 256
GLA_KW = GLA_HEADS * GLA_DK
GLA_VW = GLA_HEADS * GLA_DV
GLA_RANK = 16
GLA_TAU = 16.0
SSD_DI = 2 * D_MODEL
SSD_P = 64
SSD_HEADS = SSD_DI // SSD_P
SSD_N = 128
SSD_G = 4
SSD_K = SSD_HEADS // SSD_G
SSD_GW = SSD_DI // SSD_G
SSD_BC = SSD_G * SSD_N
SSD_CONV_DIM = SSD_DI + 2 * SSD_BC

E_OFF_GATE = 3 * GDN_KW
E_OFF_GSMALL = 4 * GDN_KW
E_GDN_N = E_OFF_GSMALL + LANES
E_OFF_GLA = E_OFF_GSMALL + 2 * GDN_HEADS
SMALL_BETA = 0
SMALL_A = GDN_HEADS
L_OFF_K = GLA_KW
L_OFF_V = 2 * GLA_KW
L_OFF_GATE = L_OFF_V + GLA_VW
L_OFF_SMALL = L_OFF_GATE + GLA_VW
L_N = L_OFF_SMALL + LANES
SMALL_LR = 0
O_OFF_X = SSD_DI
O_OFF_BC = 2 * SSD_DI
O_OFF_DT = 2 * SSD_DI + 2 * SSD_BC
O_PROJ_N = 6 * 1024

CHUNK = 64
GDN_PREP_CHUNKS = 2
assert SSD_P == CHUNK and 2 * SSD_P == LANES
VMEM_LIMIT = 56 * 1024 * 1024


def _mm(a, b):
    return jnp.dot(a.astype(MM_DTYPE), b.astype(MM_DTYPE), preferred_element_type=f32)


def _mm_nt(a, b):
    return lax.dot_general(a.astype(MM_DTYPE), b.astype(MM_DTYPE), (((1,), (1,)), ((), ())),
                           preferred_element_type=f32)


def _mm_tn(a, b):
    return lax.dot_general(a.astype(MM_DTYPE), b.astype(MM_DTYPE), (((0,), (0,)), ((), ())),
                           preferred_element_type=f32)


def _split3(x):
    hi = x.astype(bf16)
    r = x - hi.astype(f32)
    mid = r.astype(bf16)
    lo = (r - mid.astype(f32)).astype(bf16)
    return hi, mid, lo


def _sel_mm(sel, x):
    hi, mid, lo = _split3(x)
    s = sel.astype(bf16)
    return (jnp.dot(s, hi, preferred_element_type=f32) + jnp.dot(s, mid, preferred_element_type=f32)
            + jnp.dot(s, lo, preferred_element_type=f32))


def _sel_rows(x, sel):
    hi, mid, lo = _split3(x)
    s = sel.astype(bf16)
    return (jnp.dot(hi, s, preferred_element_type=f32) + jnp.dot(mid, s, preferred_element_type=f32)
            + jnp.dot(lo, s, preferred_element_type=f32))


def _sigmoid(x):
    return 1.0 / (1.0 + jnp.exp(-x))


def _silu(x):
    return x * _sigmoid(x)


def _softplus(x):
    return jnp.maximum(x, 0.0) + jnp.log1p(jnp.exp(-jnp.abs(x)))


def _tril(n, strict=False):
    r = lax.broadcasted_iota(jnp.int32, (n, n), 0)
    c = lax.broadcasted_iota(jnp.int32, (n, n), 1)
    return (r > c) if strict else (r >= c)


def _masked_decay(col, row, incl):
    return jnp.exp(jnp.where(incl, col - row, -jnp.inf))


def _interleave(main, side, ratio):
    gens = [main, side]
    steps = {id(main): ratio, id(side): 1}
    while gens:
        for g in list(gens):
            for _ in range(steps[id(g)]):
                try:
                    next(g)
                except StopIteration:
                    gens.remove(g)
                    break
                yield


def _round_robin(gens):
    gens = list(gens)
    while gens:
        for g in list(gens):
            try:
                next(g)
            except StopIteration:
                gens.remove(g)
        yield


def _rms(x, width):
    return x * lax.rsqrt(jnp.sum(x * x, axis=-1, keepdims=True) * (1.0 / width) + EPS)


def _norm_proj_kernel(x_ref, g_ref, w_ref, o_ref, u_scr):
    @pl.when(pl.program_id(1) == 0)
    def _():
        u_scr[...] = (_rms(x_ref[...], D_MODEL) * g_ref[...]).astype(u_scr.dtype)

    o_ref[...] = jnp.dot(u_scr[...], w_ref[...], preferred_element_type=f32)


def _norm_proj(x, g, w, tn, n):
    t, d = x.shape
    assert n % tn == 0 and n - tn < w.shape[1]
    tm = min(t, 1024)
    return pl.pallas_call(
        _norm_proj_kernel,
        grid=(t // tm, n // tn),
        in_specs=[pl.BlockSpec((tm, d), lambda i, j: (i, 0)),
                  pl.BlockSpec((1, d), lambda i, j: (0, 0)),
                  pl.BlockSpec((d, tn), lambda i, j: (0, j))],
        out_specs=pl.BlockSpec((tm, tn), lambda i, j: (i, j)),
        out_shape=jax.ShapeDtypeStruct((t, n), f32),
        scratch_shapes=[pltpu.VMEM((tm, d), MM_DTYPE)],
        compiler_params=pltpu.CompilerParams(dimension_semantics=("parallel", "arbitrary"),
                                             vmem_limit_bytes=VMEM_LIMIT),
        name="norm_proj",
    )(x, g, w)


def _out_proj_kernel(oa_ref, ob_ref, wa_ref, wb_ref, g_ref, h_ref, y_ref):
    p = (jnp.dot(oa_ref[...], wa_ref[...], preferred_element_type=f32)
         + jnp.dot(ob_ref[...], wb_ref[...], preferred_element_type=f32))
    y_ref[...] = h_ref[...] + _rms(p, D_MODEL) * g_ref[...]


def _out_proj(oa, ob, w, g, h):
    t = oa.shape[0]
    e = w.shape[0] // 2
    assert oa.shape[1] == ob.shape[1] and oa.shape[1] in (e, 2 * e)
    col_b = 1 if ob.shape[1] == 2 * e else 0
    d = w.shape[1]
    tm = min(t, 1024)
    return pl.pallas_call(
        _out_proj_kernel,
        grid=(t // tm,),
        in_specs=[pl.BlockSpec((tm, e), lambda i: (i, 0)),
                  pl.BlockSpec((tm, e), lambda i: (i, col_b)),
                  pl.BlockSpec((e, d), lambda i: (0, 0)),
                  pl.BlockSpec((e, d), lambda i: (1, 0)),
                  pl.BlockSpec((1, d), lambda i: (0, 0)),
                  pl.BlockSpec((tm, d), lambda i: (i, 0))],
        out_specs=pl.BlockSpec((tm, d), lambda i: (i, 0)),
        out_shape=jax.ShapeDtypeStruct((t, d), f32),
        compiler_params=pltpu.CompilerParams(dimension_semantics=("parallel",),
                                             vmem_limit_bytes=VMEM_LIMIT),
        name="out_proj",
    )(oa, ob, w, w, g, h)


def _normed(x_ref, g_ref):
    return (_rms(x_ref[...], D_MODEL) * g_ref[...]).astype(MM_DTYPE)


def _proj(u, w_ref, c0, c1):
    return jnp.dot(u, w_ref[:, c0:c1], preferred_element_type=f32)


def _conv_block(xp_scr, cw_ref, cb_ref, tb, first):
    @pl.when(first)
    def _():
        xp_scr[0:8, :] = jnp.zeros((8, xp_scr.shape[1]), f32)

    @pl.when(jnp.logical_not(first))
    def _():
        xp_scr[5:8, :] = xp_scr[tb + 5:tb + 8, :]

    def conv_cols(c0, c1, r0=0, nr=tb):
        acc = cb_ref[:, c0:c1] + xp_scr[8 + r0:8 + r0 + nr, c0:c1] * cw_ref[3:4, c0:c1]
        for s in range(CONV_W - 1):
            acc = acc + xp_scr[5 + s + r0:5 + s + r0 + nr, c0:c1] * cw_ref[s:s + 1, c0:c1]
        return _silu(acc)

    return conv_cols


def _gdn_stages(u, first, last, wm_ref, ws_ref, cw_ref, cb_ref, hp_ref, ng_ref,
                o_ref, conv_out_ref, s_out_ref,
                xp_scr, qkv_scr, gate_scr, beta_scr, lg_scr, s_scr, gl_scr, up_scr, w_scr, qg_scr, kd_scr,
                p_scr, *, tb):
    conv_cols = _conv_block(xp_scr, cw_ref, cb_ref, tb, first)

    @pl.when(first)
    def _():
        s_scr[...] = jnp.zeros(s_scr.shape, f32)

    def proj_stages():
        for c0 in range(0, GDN_CONV_DIM, GDN_KW):
            xp_scr[8:8 + tb, c0:c0 + GDN_KW] = _proj(u, wm_ref, c0, c0 + GDN_KW)
            yield
        small = _proj(u, ws_ref, 0, LANES)
        beta_scr[...] = _sigmoid(small)
        lg_scr[...] = hp_ref[0:1, :] * _softplus(small + hp_ref[1:2, :])
        gate_scr[...] = _proj(u, wm_ref, E_OFF_GATE, E_OFF_GATE + GDN_VW)
        yield

    span = GDN_PREP_CHUNKS * CHUNK

    def conv_norm(it):
        r0 = it * span
        rr = slice(r0, r0 + span)
        for h in range(GDN_HEADS):
            c0 = h * GDN_DK
            q = conv_cols(c0, c0 + GDN_DK, r0, span)
            q = q * lax.rsqrt(jnp.sum(q * q, axis=-1, keepdims=True) + EPS) * (GDN_DK ** -0.5)
            qkv_scr[rr, c0:c0 + GDN_DK] = q
            k = conv_cols(GDN_KW + c0, GDN_KW + c0 + GDN_DK, r0, span)
            k = k * lax.rsqrt(jnp.sum(k * k, axis=-1, keepdims=True) + EPS)
            qkv_scr[rr, GDN_KW + c0:GDN_KW + c0 + GDN_DK] = k
            qkv_scr[rr, 2 * GDN_KW + c0:2 * GDN_KW + c0 + GDN_DV] = conv_cols(
                2 * GDN_KW + c0, 2 * GDN_KW + c0 + GDN_DV, r0, span)
            yield

    incl = _tril(CHUNK)
    strict = _tril(CHUNK, strict=True)
    eye = jnp.logical_and(incl, jnp.logical_not(strict)).astype(f32)
    sr = lax.broadcasted_iota(jnp.int32, (span, span), 0)
    sc = lax.broadcasted_iota(jnp.int32, (span, span), 1)
    lmat = jnp.logical_and(sr >= sc, sr // CHUNK == sc // CHUNK).astype(f32)
    ng = ng_ref[...]
    heads = range(GDN_HEADS)
    probs = [(ci, h) for ci in range(GDN_PREP_CHUNKS) for h in heads]
    n = range(len(probs))

    def prep(c):
        r0 = c * span
        gam_all = _sel_mm(lmat, lg_scr[pl.ds(r0, span), :])
        gam_t = gam_all.T
        beta_all = beta_scr[pl.ds(r0, span), :]
        rows, gcol, grow, glast, bcol = [], [], [], [], []
        for ci, h in probs:
            lo, hi = ci * CHUNK, (ci + 1) * CHUNK
            rows.append(pl.ds(r0 + lo, CHUNK))
            gcol.append(gam_all[lo:hi, SMALL_A + h:SMALL_A + h + 1])
            grow.append(gam_t[SMALL_A + h:SMALL_A + h + 1, lo:hi])
            glast.append(gam_all[hi - 1:hi, SMALL_A + h:SMALL_A + h + 1])
            bcol.append(beta_all[lo:hi, SMALL_BETA + h:SMALL_BETA + h + 1])
        for ci in range(GDN_PREP_CHUNKS):
            gl_scr[pl.ds(c * GDN_PREP_CHUNKS + ci, 1), :] = jnp.exp(gam_all[(ci + 1) * CHUNK - 1:(ci + 1) * CHUNK, :])
        q = [qkv_scr[rows[j], h * GDN_DK:(h + 1) * GDN_DK] for j, (_, h) in enumerate(probs)]
        k = [qkv_scr[rows[j], GDN_KW + h * GDN_DK:GDN_KW + (h + 1) * GDN_DK] for j, (_, h) in enumerate(probs)]
        v = [qkv_scr[rows[j], 2 * GDN_KW + h * GDN_DV:2 * GDN_KW + (h + 1) * GDN_DV] for j, (_, h) in enumerate(probs)]
        kb = [k[j].astype(MM_DTYPE) for j in n]
        kk = [_mm_nt(kb[j], kb[j]) for j in n]
        yield
        qk = [_mm_nt(q[j], kb[j]) for j in n]
        yield
        decay = [_masked_decay(gcol[j], grow[j], incl) for j in n]
        eg = [jnp.exp(gcol[j]) for j in n]
        for j, (_, h) in enumerate(probs):
            c0 = h * GDN_DK
            qg_scr[rows[j], c0:c0 + GDN_DK] = (eg[j] * q[j]).astype(MM_DTYPE)
            kd_scr[rows[j], c0:c0 + GDN_DK] = (jnp.exp(glast[j] - gcol[j]) * k[j]).astype(MM_DTYPE)
            p_scr[h, rows[j], :] = (qk[j] * decay[j]).astype(MM_DTYPE)
        nmat = [jnp.where(strict, bcol[j] * (kk[j] * decay[j]), 0.0) for j in n]
        tinv = [eye - nmat[j] for j in n]
        npow = nmat
        for _ in range(5):
            npow = [_mm(npow[j], npow[j]) for j in n]
            yield
            tinv = [tinv[j] + _mm(tinv[j], npow[j]) for j in n]
            yield
        x = [_mm(tinv[j], jnp.concatenate([bcol[j] * v[j], (bcol[j] * eg[j]) * k[j]], axis=-1)) for j in n]
        for j, (_, h) in enumerate(probs):
            c0 = h * GDN_DK
            up_scr[rows[j], c0:c0 + GDN_DV] = x[j][:, :GDN_DV]
            w_scr[rows[j], c0:c0 + GDN_DK] = x[j][:, GDN_DV:].astype(MM_DTYPE)

    state = [s_scr[h] for h in heads]

    def scan(c):
        rows = pl.ds(c * CHUNK, CHUNK)
        gl = gl_scr[c:c + 1, :]
        r = [_mm(jnp.concatenate([w_scr[rows, h * GDN_DK:(h + 1) * GDN_DK],
                                  qg_scr[rows, h * GDN_DK:(h + 1) * GDN_DK]], axis=0), state[h]) for h in heads]
        yield
        us = [(up_scr[rows, h * GDN_DV:(h + 1) * GDN_DV] - r[h][:CHUNK]).astype(MM_DTYPE) for h in heads]
        o = [r[h][CHUNK:] + _mm(p_scr[h, rows, :], us[h]) for h in heads]
        yield
        upd = [_mm_tn(kd_scr[rows, h * GDN_DK:(h + 1) * GDN_DK], us[h]) for h in heads]
        yield
        for h in heads:
            c0 = h * GDN_DV
            state[h] = gl[:, SMALL_A + h:SMALL_A + h + 1] * state[h] + upd[h]
            gate = gate_scr[rows, c0:c0 + GDN_DV]
            o_ref[rows, c0:c0 + GDN_DV] = (_rms(o[h], GDN_DV) * ng * _silu(gate)).astype(o_ref.dtype)
        yield

    def scans(chunks):
        for c in chunks:
            yield from scan(c)

    def side_work(it):
        if it + 1 < n_prep:
            yield from conv_norm(it + 1)
        if it >= 1:
            yield from scans(range((it - 1) * GDN_PREP_CHUNKS, it * GDN_PREP_CHUNKS))

    n_prep = tb // span

    def stages():
        yield from proj_stages()
        yield from conv_norm(0)
        for it in range(n_prep):
            yield from _interleave(prep(it), side_work(it), 1)
        yield from scans(range((n_prep - 1) * GDN_PREP_CHUNKS, n_prep * GDN_PREP_CHUNKS))

    def finish():
        for h in heads:
            s_scr[h] = state[h]

        @pl.when(last)
        def _():
            conv_out_ref[...] = xp_scr[tb + 5:tb + 8, :]
            s_out_ref[...] = s_scr[...]

    return stages(), finish


def _gla_stages(u, first, last, wq_ref, wk_ref, wv_ref, wg_ref, ws_ref, wlr_ref, blr_ref, ng_ref,
                o_ref, s_out_ref, q_ref, k_ref, v_ref, gate_ref, la_scr, st_scr, *, tb):
    @pl.when(first)
    def _():
        st_scr[...] = jnp.zeros(st_scr.shape, f32)

    def proj_stages():
        q_ref[...] = _proj(u, wq_ref, 0, GLA_KW)
        k_ref[...] = _proj(u, wk_ref, 0, GLA_KW)
        small = _proj(u, ws_ref, 0, LANES)
        z = _mm(small, wlr_ref[...]) + blr_ref[...]
        la_scr[...] = -_softplus(-z) * (1.0 / GLA_TAU)
        yield
        v_ref[...] = _proj(u, wv_ref, 0, GLA_VW).astype(v_ref.dtype)
        yield
        gate_ref[...] = _proj(u, wg_ref, 0, GLA_VW)
        yield

    incl = _tril(CHUNK)
    lmat = incl.astype(f32)
    ng = ng_ref[...]

    heads = range(GLA_HEADS)
    n_chunks = tb // CHUNK
    staged = [None] * n_chunks
    state = [st_scr[h] for h in heads]

    def prep(c):
        rows = pl.ds(c * CHUNK, CHUNK)
        bcum = [_sel_mm(lmat, la_scr[rows, h * GLA_DK:(h + 1) * GLA_DK]) for h in heads]
        yield
        k = [k_ref[rows, h * GLA_DK:(h + 1) * GLA_DK] for h in heads]
        v = [v_ref[rows, h * GLA_DV:(h + 1) * GLA_DV].astype(MM_DTYPE) for h in heads]
        qe = [(q_ref[rows, h * GLA_DK:(h + 1) * GLA_DK] * (GLA_DK ** -0.5) * jnp.exp(bcum[h])).astype(MM_DTYPE)
              for h in heads]
        ke = [k[h] * jnp.exp(-bcum[h]) for h in heads]
        blast = [bcum[h][CHUNK - 1:CHUNK, :] for h in heads]
        kdec = [(k[h] * jnp.exp(blast[h] - bcum[h])).astype(MM_DTYPE) for h in heads]
        p = [jnp.where(incl, _mm_nt(qe[h], ke[h]), 0.0) for h in heads]
        yield
        o_intra = [_mm(p[h], v[h]) for h in heads]
        staged[c] = (qe, v, kdec, blast, o_intra)
        yield

    def scan(c):
        rows = pl.ds(c * CHUNK, CHUNK)
        qe, v, kdec, blast, o_intra = staged[c]
        o = [o_intra[h] + _mm_nt(qe[h], state[h]) for h in heads]
        upd = [_mm_tn(v[h], kdec[h]) for h in heads]
        yield
        for h in heads:
            v0 = h * GLA_DV
            state[h] = jnp.exp(blast[h]) * state[h] + upd[h]
            gate = gate_ref[rows, v0:v0 + GLA_DV]
            o_ref[rows, GDN_VW + v0:GDN_VW + v0 + GLA_DV] = (
                _rms(o[h], GLA_DV) * ng * _silu(gate)).astype(o_ref.dtype)
        yield

    def stages():
        yield from proj_stages()
        yield from prep(0)
        for c in range(1, n_chunks):
            yield from _interleave(prep(c), scan(c - 1), 1)
        yield from scan(n_chunks - 1)

    def finish():
        for h in heads:
            st_scr[h] = state[h]

        @pl.when(last)
        def _():
            for h in heads:
                s_out_ref[h] = st_scr[h].T

    return stages(), finish


N_GDN_SCRATCH = 12
EVEN_STAGE_RATIO = 2


def _even_prompt_kernel(x_ref, pg_ref, wm_ref, ws_ref, wq_ref, wk_ref, wv_ref, wg_ref, wsl_ref, cw_ref, cb_ref, hp_ref,
                        ngd_ref, wlr_ref, blr_ref, ngl_ref,
                        o_ref, conv_out_ref, sgdn_out_ref, sgla_out_ref, *scratch, tb):
    i = pl.program_id(1)
    first = i == 0
    last = i == pl.num_programs(1) - 1
    u = _normed(x_ref, pg_ref)
    gdn, gdn_finish = _gdn_stages(u, first, last, wm_ref, ws_ref, cw_ref, cb_ref, hp_ref, ngd_ref,
                                  o_ref, conv_out_ref, sgdn_out_ref, *scratch[:N_GDN_SCRATCH], tb=tb)
    gla, gla_finish = _gla_stages(u, first, last, wq_ref, wk_ref, wv_ref, wg_ref, wsl_ref, wlr_ref, blr_ref, ngl_ref,
                                  o_ref, sgla_out_ref, *scratch[N_GDN_SCRATCH:], tb=tb)
    for _ in _interleave(gdn, gla, EVEN_STAGE_RATIO):
        pass
    gdn_finish()
    gla_finish()


def _even_prompt(x, pre_g, w_in, w_gla, cw, cb, hp, ng_gdn, wlr, blr, ng_gla, tb):
    b, l, d = x.shape
    assert tb % (GDN_PREP_CHUNKS * CHUNK) == 0 and l % tb == 0
    kern = functools.partial(_even_prompt_kernel, tb=tb)
    wcol = lambda blk: (lambda bi, i: (0, blk))
    const = lambda bi, i: (0, 0)
    gdn_scratch = [pltpu.VMEM((tb + 8, GDN_CONV_DIM), f32),
                   pltpu.VMEM((tb, GDN_CONV_DIM), f32),
                   pltpu.VMEM((tb, GDN_VW), f32),
                   pltpu.VMEM((tb, LANES), f32),
                   pltpu.VMEM((tb, LANES), f32),
                   pltpu.VMEM((GDN_HEADS, GDN_DK, GDN_DV), f32),
                   pltpu.VMEM((max(8, tb // CHUNK), LANES), f32),
                   pltpu.VMEM((tb, GDN_VW), f32),
                   pltpu.VMEM((tb, GDN_KW), MM_DTYPE),
                   pltpu.VMEM((tb, GDN_KW), MM_DTYPE),
                   pltpu.VMEM((tb, GDN_KW), MM_DTYPE),
                   pltpu.VMEM((GDN_HEADS, tb, CHUNK), MM_DTYPE)]
    assert len(gdn_scratch) == N_GDN_SCRATCH
    gla_scratch = [pltpu.VMEM((tb, GLA_KW), f32),
                   pltpu.VMEM((tb, GLA_KW), f32),
                   pltpu.VMEM((tb, GLA_VW), MM_DTYPE),
                   pltpu.VMEM((tb, GLA_VW), f32),
                   pltpu.VMEM((tb, GLA_KW), f32),
                   pltpu.VMEM((GLA_HEADS, GLA_DV, GLA_DK), f32)]
    return pl.pallas_call(
        kern,
        grid=(b, l // tb),
        in_specs=[pl.BlockSpec((None, tb, d), lambda bi, i: (bi, i, 0)),
                  pl.BlockSpec((1, d), const),
                  pl.BlockSpec((d, E_OFF_GSMALL), const),
                  pl.BlockSpec((d, LANES), wcol(E_OFF_GSMALL // LANES)),
                  pl.BlockSpec((d, GLA_KW), wcol(0)),
                  pl.BlockSpec((d, GLA_KW), wcol(L_OFF_K // GLA_KW)),
                  pl.BlockSpec((d, GLA_VW), wcol(L_OFF_V // GLA_VW)),
                  pl.BlockSpec((d, GLA_VW), wcol(L_OFF_GATE // GLA_VW)),
                  pl.BlockSpec((d, LANES), wcol(L_OFF_SMALL // LANES)),
                  pl.BlockSpec((CONV_W, GDN_CONV_DIM), const),
                  pl.BlockSpec((1, GDN_CONV_DIM), const),
                  pl.BlockSpec((8, LANES), const),
                  pl.BlockSpec((1, GDN_DV), const),
                  pl.BlockSpec((LANES, GLA_KW), const),
                  pl.BlockSpec((1, GLA_KW), const),
                  pl.BlockSpec((1, GLA_DV), const)],
        out_specs=[pl.BlockSpec((None, tb, GDN_VW + GLA_VW), lambda bi, i: (bi, i, 0)),
                   pl.BlockSpec((None, CONV_W - 1, GDN_CONV_DIM), lambda bi, i: (bi, 0, 0)),
                   pl.BlockSpec((None, GDN_HEADS, GDN_DK, GDN_DV), lambda bi, i: (bi, 0, 0, 0)),
                   pl.BlockSpec((None, GLA_HEADS, GLA_DK, GLA_DV), lambda bi, i: (bi, 0, 0, 0))],
        out_shape=[jax.ShapeDtypeStruct((b, l, GDN_VW + GLA_VW), MM_DTYPE),
                   jax.ShapeDtypeStruct((b, CONV_W - 1, GDN_CONV_DIM), f32),
                   jax.ShapeDtypeStruct((b, GDN_HEADS, GDN_DK, GDN_DV), f32),
                   jax.ShapeDtypeStruct((b, GLA_HEADS, GLA_DK, GLA_DV), f32)],
        scratch_shapes=gdn_scratch + gla_scratch,
        compiler_params=pltpu.CompilerParams(dimension_semantics=("parallel", "arbitrary"),
                                             vmem_limit_bytes=VMEM_LIMIT),
        name="even_prompt",
    )(x, pre_g, w_in, w_in, w_gla, w_gla, w_gla, w_gla, w_gla, cw, cb, hp, ng_gdn, wlr, blr, ng_gla)


def _head_lanes(x):
    return jnp.where(lax.broadcasted_iota(jnp.int32, x.shape, 1) < SSD_HEADS, x, 0.0)


def _ssd_prompt_kernel(x_ref, pg_ref, wx_ref, wz_ref, wbc_ref, wdt_ref, cw_ref, cb_ref, hp_ref, dskip_ref, ng_ref,
                       expand_ref, y_ref, conv_out_ref, h_out_ref,
                       xp_scr, xbc_scr, z_ref, lam_scr, lamx_scr, xdt_scr, h_scr, *, tb):
    i = pl.program_id(1)
    first = i == 0
    conv_cols = _conv_block(xp_scr, cw_ref, cb_ref, tb, first)

    @pl.when(first)
    def _():
        h_scr[...] = jnp.zeros(h_scr.shape, f32)

    u = _normed(x_ref, pg_ref)
    for c0 in range(0, SSD_DI, 1024):
        xp_scr[8:8 + tb, c0:c0 + 1024] = _proj(u, wx_ref, c0, c0 + 1024)
    xp_scr[8:8 + tb, SSD_DI:SSD_CONV_DIM] = _proj(u, wbc_ref, 0, 2 * SSD_BC)
    dt_raw = _head_lanes(_proj(u, wdt_ref, 0, LANES))
    for c0 in range(0, SSD_DI, 1024):
        z_ref[:, c0:c0 + 1024] = _proj(u, wz_ref, c0, c0 + 1024)

    for c0 in range(0, SSD_CONV_DIM, 512):
        xbc_scr[:, c0:c0 + 512] = conv_cols(c0, c0 + 512)
    dt = _softplus(dt_raw + hp_ref[0:1, :])
    tr = lax.broadcasted_iota(jnp.int32, (tb, tb), 0)
    tc = lax.broadcasted_iota(jnp.int32, (tb, tb), 1)
    lam_all = _sel_mm(jnp.logical_and(tr >= tc, tr // CHUNK == tc // CHUNK).astype(f32), dt * hp_ref[1:2, :])
    lam_scr[...] = lam_all
    lo_blk = lax.broadcasted_iota(jnp.int32, (tb, LANES), 1) < SSD_P
    for p in range(SSD_HEADS // 2):
        cols = slice(p * LANES, (p + 1) * LANES)
        k0 = 2 * p
        xdt_scr[:, cols] = xbc_scr[:, cols] * jnp.where(lo_blk, dt[:, k0:k0 + 1], dt[:, k0 + 1:k0 + 2])
    for c0 in range(0, SSD_DI, 512):
        lamx_scr[:, c0:c0 + 512] = _sel_rows(lam_all, expand_ref[:, c0:c0 + 512])

    lane = lax.broadcasted_iota(jnp.int32, (CHUNK, LANES), 1)
    lo = lane < SSD_P
    incl2 = lax.broadcasted_iota(jnp.int32, (CHUNK, LANES), 0) >= lane % CHUNK
    top = lax.broadcasted_iota(jnp.int32, (2 * SSD_P, SSD_N), 0) < SSD_P
    n_pairs = SSD_HEADS // 2
    ppg = SSD_K // 2
    groups = range(SSD_G)

    def chunk_body(c):
        rows = pl.ds(c * CHUNK, CHUNK)
        lam = lam_scr[rows, :]
        lam_t = lam.T
        llast = lam[CHUNK - 1:CHUNK, :]
        bm = [xbc_scr[rows, SSD_DI + g * SSD_N:SSD_DI + (g + 1) * SSD_N].astype(MM_DTYPE) for g in groups]
        cm = [xbc_scr[rows, SSD_DI + SSD_BC + g * SSD_N:SSD_DI + SSD_BC + (g + 1) * SSD_N].astype(MM_DTYPE)
              for g in groups]
        h_old = [h_scr[g * SSD_GW:(g + 1) * SSD_GW, :] for g in groups]
        cb2 = [_mm_nt(cm[g], jnp.concatenate([bm[g], bm[g]], axis=0)) for g in groups]
        y_inter = [_mm_nt(cm[g], h_old[g]) for g in groups]
        xs, xdt, xdec, lcol = [], [], [], []
        for p in range(n_pairs):
            cols = slice(p * LANES, (p + 1) * LANES)
            xs.append(xbc_scr[rows, cols])
            lcol.append(lamx_scr[rows, cols])
            xdt.append(xdt_scr[rows, cols])
            xdec.append((xdt[p] * jnp.exp(lcol[p][CHUNK - 1:CHUNK, :] - lcol[p])).astype(MM_DTYPE))
        upd = [_mm_tn(jnp.concatenate(xdec[g * ppg:(g + 1) * ppg], axis=-1), bm[g]) for g in groups]
        y_intra = []
        for p in range(n_pairs):
            k0 = 2 * p
            lrow = jnp.concatenate([lam_t[k0:k0 + 1, :], lam_t[k0 + 1:k0 + 2, :]], axis=-1)
            m = cb2[p // ppg] * _masked_decay(lcol[p], lrow, incl2)
            blockdiag = jnp.concatenate([jnp.where(lo, xdt[p], 0.0), jnp.where(lo, 0.0, xdt[p])], axis=0)
            y_intra.append(_mm(m, blockdiag))
        for g in groups:
            ys = []
            for pp in range(ppg):
                p = g * ppg + pp
                k0 = 2 * p
                ys.append(y_intra[p] + jnp.exp(lcol[p]) * y_inter[g][:, pp * LANES:(pp + 1) * LANES]
                          + dskip_ref[:, p * LANES:(p + 1) * LANES] * xs[p])
                el = jnp.where(top, jnp.exp(llast[:, k0:k0 + 1]), jnp.exp(llast[:, k0 + 1:k0 + 2]))
                h_scr[p * LANES:(p + 1) * LANES, :] = (el * h_old[g][pp * LANES:(pp + 1) * LANES, :]
                                                        + upd[g][pp * LANES:(pp + 1) * LANES, :])
            yg = jnp.concatenate(ys, axis=-1) * _silu(z_ref[rows, g * SSD_GW:(g + 1) * SSD_GW])
            y_ref[rows, g * SSD_GW:(g + 1) * SSD_GW] = (
                _rms(yg, SSD_GW) * ng_ref[:, g * SSD_GW:(g + 1) * SSD_GW]).astype(y_ref.dtype)

    for c in range(tb // CHUNK):
        chunk_body(c)

    @pl.when(i == pl.num_programs(1) - 1)
    def _():
        conv_out_ref[...] = xp_scr[tb + 5:tb + 8, :]
        h_out_ref[...] = h_scr[...]


def _ssd_prompt(x, pre_g, w_in, cw, cb, hp, dskip, ng, expand, tb):
    b, l, d = x.shape
    kern = functools.partial(_ssd_prompt_kernel, tb=tb)
    wcol = lambda blk: (lambda bi, i: (0, blk))
    const = lambda bi, i: (0, 0)
    return pl.pallas_call(
        kern,
        grid=(b, l // tb),
        in_specs=[pl.BlockSpec((None, tb, d), lambda bi, i: (bi, i, 0)),
                  pl.BlockSpec((1, d), const),
                  pl.BlockSpec((d, SSD_DI), wcol(O_OFF_X // SSD_DI)),
                  pl.BlockSpec((d, SSD_DI), wcol(0)),
                  pl.BlockSpec((d, 2 * SSD_BC), wcol(O_OFF_BC // (2 * SSD_BC))),
                  pl.BlockSpec((d, LANES), wcol(O_OFF_DT // LANES)),
                  pl.BlockSpec((CONV_W, SSD_CONV_DIM), const),
                  pl.BlockSpec((1, SSD_CONV_DIM), const),
                  pl.BlockSpec((8, LANES), const),
                  pl.BlockSpec((1, SSD_DI), const),
                  pl.BlockSpec((1, SSD_DI), const),
                  pl.BlockSpec((LANES, SSD_DI), const)],
        out_specs=[pl.BlockSpec((None, tb, SSD_DI), lambda bi, i: (bi, i, 0)),
                   pl.BlockSpec((None, CONV_W - 1, SSD_CONV_DIM), lambda bi, i: (bi, 0, 0)),
                   pl.BlockSpec((None, SSD_HEADS * SSD_P, SSD_N), lambda bi, i: (bi, 0, 0))],
        out_shape=[jax.ShapeDtypeStruct((b, l, SSD_DI), MM_DTYPE),
                   jax.ShapeDtypeStruct((b, CONV_W - 1, SSD_CONV_DIM), f32),
                   jax.ShapeDtypeStruct((b, SSD_HEADS * SSD_P, SSD_N), f32)],
        scratch_shapes=[pltpu.VMEM((tb + 8, SSD_CONV_DIM), f32),
                        pltpu.VMEM((tb, SSD_CONV_DIM), f32),
                        pltpu.VMEM((tb, SSD_DI), f32),
                        pltpu.VMEM((tb, LANES), f32),
                        pltpu.VMEM((tb, SSD_DI), f32),
                        pltpu.VMEM((tb, SSD_DI), f32),
                        pltpu.VMEM((SSD_HEADS * SSD_P, SSD_N), f32)],
        compiler_params=pltpu.CompilerParams(dimension_semantics=("parallel", "arbitrary"),
                                             vmem_limit_bytes=VMEM_LIMIT),
        name="ssd_prompt",
    )(x, pre_g, w_in, w_in, w_in, w_in, cw, cb, hp, dskip, ng, expand)


def _pad_rows(row, n):
    return jnp.concatenate([row, jnp.zeros((n - 1, row.shape[1]), row.dtype)], axis=0)


def _bcast_cols(row, width):
    hi, mid, lo = _split3(row)
    lhs = jnp.concatenate([hi, mid, lo, jnp.zeros((13, row.shape[1]), bf16)], axis=0)
    sel = (lax.broadcasted_iota(jnp.int32, (16, width), 0) < 3).astype(bf16)
    return lax.dot_general(lhs, sel, (((0,), (0,)), ((), ())), preferred_element_type=f32)


def _conv_step(buf_ref, x, cw_ref, cb_ref, new_buf_ref):
    acc = cb_ref[...] + x * cw_ref[3:4, :]
    for s in range(CONV_W - 1):
        acc = acc + buf_ref[s] * cw_ref[s:s + 1, :]
    for s in range(CONV_W - 2):
        new_buf_ref[s] = buf_ref[s + 1]
    new_buf_ref[CONV_W - 2] = x
    return _silu(acc)


def _gdn_decode_kernel(q_ref, k_ref, v_ref, gate_ref, small_ref, buf_ref, s_ref, cw_ref, cb_ref, hp_ref, ng_ref,
                       o_ref, new_buf_ref, s_out_ref, *, bb):
    x = jnp.concatenate([q_ref[...], k_ref[...], v_ref[...]], axis=-1)
    qkv = _conv_step(buf_ref, x, cw_ref, cb_ref, new_buf_ref)
    small = small_ref[...]
    beta = _sigmoid(small)
    eg = jnp.exp(hp_ref[0:1, :] * _softplus(small + hp_ref[1:2, :]))
    ng = ng_ref[...]
    gate = gate_ref[...]

    def head(h):
        c0 = h * GDN_DK
        q = qkv[:, c0:c0 + GDN_DK]
        q = q * lax.rsqrt(jnp.sum(q * q, axis=-1, keepdims=True) + EPS) * (GDN_DK ** -0.5)
        k = qkv[:, GDN_KW + c0:GDN_KW + c0 + GDN_DK]
        k = k * lax.rsqrt(jnp.sum(k * k, axis=-1, keepdims=True) + EPS)
        v = qkv[:, 2 * GDN_KW + c0:2 * GDN_KW + c0 + GDN_DV]
        qk = jnp.sum(q * k, axis=-1, keepdims=True)
        b_h = beta[:, SMALL_BETA + h:SMALL_BETA + h + 1]
        eg_h = eg[:, SMALL_A + h:SMALL_A + h + 1]
        seqs = range(bb)
        s_old = [s_ref[b, h] for b in seqs]
        r = [_mm(jnp.concatenate([k[b:b + 1], q[b:b + 1], jnp.zeros((6, GDN_DK), f32)], axis=0), s_old[b])
             for b in seqs]
        yield
        u = [b_h[b:b + 1] * v[b:b + 1] - (b_h[b:b + 1] * eg_h[b:b + 1]) * r[b][0:1] for b in seqs]
        upd = [_mm_tn(_pad_rows(k[b:b + 1], 16), _pad_rows(u[b], 16)) for b in seqs]
        yield
        for b in seqs:
            s_out_ref[b, h] = eg_h[b:b + 1] * s_old[b] + upd[b]
        o = jnp.concatenate([eg_h[b:b + 1] * r[b][1:2] + qk[b:b + 1] * u[b] for b in seqs], axis=0)
        o_ref[:, c0:c0 + GDN_DV] = (_rms(o, GDN_DV) * ng * _silu(gate[:, c0:c0 + GDN_DV])).astype(o_ref.dtype)

    for _ in _round_robin([head(h) for h in range(GDN_HEADS)]):
        pass


def _gdn_decode(proj, buf, s, cw, cb, hp, ng, bb):
    b = proj.shape[0]
    wq = GDN_KW
    kern = functools.partial(_gdn_decode_kernel, bb=bb)
    col = lambda blk: (lambda i: (i, blk))
    const = lambda i: (0, 0)
    return pl.pallas_call(
        kern,
        grid=(b // bb,),
        in_specs=[pl.BlockSpec((bb, wq), col(0)),
                  pl.BlockSpec((bb, wq), col(1)),
                  pl.BlockSpec((bb, wq), col(2)),
                  pl.BlockSpec((bb, wq), col(3)),
                  pl.BlockSpec((bb, LANES), col(E_OFF_GSMALL // LANES)),
                  pl.BlockSpec((CONV_W - 1, bb, GDN_CONV_DIM), lambda i: (0, i, 0)),
                  pl.BlockSpec((bb, GDN_HEADS, GDN_DK, GDN_DV), lambda i: (i, 0, 0, 0)),
                  pl.BlockSpec((CONV_W, GDN_CONV_DIM), const),
                  pl.BlockSpec((1, GDN_CONV_DIM), const),
                  pl.BlockSpec((8, LANES), const),
                  pl.BlockSpec((1, GDN_DV), const)],
        out_specs=[pl.BlockSpec((bb, GDN_VW), col(0)),
                   pl.BlockSpec((CONV_W - 1, bb, GDN_CONV_DIM), lambda i: (0, i, 0)),
                   pl.BlockSpec((bb, GDN_HEADS, GDN_DK, GDN_DV), lambda i: (i, 0, 0, 0))],
        out_shape=[jax.ShapeDtypeStruct((b, GDN_VW), MM_DTYPE),
                   jax.ShapeDtypeStruct((CONV_W - 1, b, GDN_CONV_DIM), f32),
                   jax.ShapeDtypeStruct((b, GDN_HEADS, GDN_DK, GDN_DV), f32)],
        compiler_params=pltpu.CompilerParams(dimension_semantics=("parallel",), vmem_limit_bytes=VMEM_LIMIT),
        name="gdn_decode",
    )(proj, proj, proj, proj, proj, buf, s, cw, cb, hp, ng)


def _gla_decode_kernel(q_ref, k_ref, v_ref, gate_ref, small_ref, s_ref, wlr_ref, blr_ref, ng_ref,
                       o_ref, s_out_ref, *, bb):
    z = _mm(small_ref[...], wlr_ref[...]) + blr_ref[...]
    la = -_softplus(-z) * (1.0 / GLA_TAU)
    dec = jnp.exp(la)
    qe = q_ref[...] * (GLA_DK ** -0.5) * dec
    kx = k_ref[...]
    ke = kx * jnp.exp(-la)
    vx = v_ref[...]
    ng = ng_ref[...]
    gate = gate_ref[...]
    for h in range(GLA_HEADS):
        k0 = h * GLA_DK
        v0 = h * GLA_DV
        p = jnp.sum(qe[:, k0:k0 + GLA_DK] * ke[:, k0:k0 + GLA_DK], axis=-1, keepdims=True)
        outs = []
        for b in range(bb):
            s_old = s_ref[b, h]
            vrow = vx[b:b + 1, v0:v0 + GLA_DV]
            outs.append(_mm(_pad_rows(qe[b:b + 1, k0:k0 + GLA_DK], 8), s_old)[0:1] + p[b:b + 1] * vrow)
            s_out_ref[b, h] = (_bcast_cols(dec[b:b + 1, k0:k0 + GLA_DK], GLA_DV) * s_old
                               + _mm_tn(_pad_rows(kx[b:b + 1, k0:k0 + GLA_DK], 16), _pad_rows(vrow, 16)))
        o = jnp.concatenate(outs, axis=0)
        o_ref[:, v0:v0 + GLA_DV] = (_rms(o, GLA_DV) * ng * _silu(gate[:, v0:v0 + GLA_DV])).astype(o_ref.dtype)


def _gla_decode(proj, s, wlr, blr, ng, bb):
    b = proj.shape[0]
    kern = functools.partial(_gla_decode_kernel, bb=bb)
    col = lambda blk: (lambda i: (i, blk))
    const = lambda i: (0, 0)
    return pl.pallas_call(
        kern,
        grid=(b // bb,),
        in_specs=[pl.BlockSpec((bb, GLA_KW), col(0)),
                  pl.BlockSpec((bb, GLA_KW), col(L_OFF_K // GLA_KW)),
                  pl.BlockSpec((bb, GLA_VW), col(L_OFF_V // GLA_VW)),
                  pl.BlockSpec((bb, GLA_VW), col(L_OFF_GATE // GLA_VW)),
                  pl.BlockSpec((bb, LANES), col(L_OFF_SMALL // LANES)),
                  pl.BlockSpec((bb, GLA_HEADS, GLA_DK, GLA_DV), lambda i: (i, 0, 0, 0)),
                  pl.BlockSpec((LANES, GLA_KW), const),
                  pl.BlockSpec((1, GLA_KW), const),
                  pl.BlockSpec((1, GLA_DV), const)],
        out_specs=[pl.BlockSpec((bb, GLA_VW), col(0)),
                   pl.BlockSpec((bb, GLA_HEADS, GLA_DK, GLA_DV), lambda i: (i, 0, 0, 0))],
        out_shape=[jax.ShapeDtypeStruct((b, GLA_VW), MM_DTYPE),
                   jax.ShapeDtypeStruct((b, GLA_HEADS, GLA_DK, GLA_DV), f32)],
        compiler_params=pltpu.CompilerParams(dimension_semantics=("parallel",), vmem_limit_bytes=VMEM_LIMIT),
        name="gla_decode",
    )(proj, proj, proj, proj, proj, s, wlr, blr, ng)


def _ssd_decode_kernel(x_ref, z_ref, bc_ref, dt_ref, buf_ref, h_ref, cw_ref, cb_ref, hp_ref, dskip_ref, ng_ref,
                       expand_ref, y_ref, new_buf_ref, h_out_ref, *, bb):
    xin = jnp.concatenate([x_ref[...], bc_ref[...]], axis=-1)
    xbc = _conv_step(buf_ref, xin, cw_ref, cb_ref, new_buf_ref)
    xs = xbc[:, :SSD_DI]
    dt = _softplus(_head_lanes(dt_ref[...]) + hp_ref[0:1, :])
    el = jnp.exp(dt * hp_ref[1:2, :])
    expand = expand_ref[...]
    dt_x = _sel_rows(dt, expand)
    el_x = _sel_rows(el, expand)
    xdt = xs * dt_x
    z = z_ref[...]
    for g in range(SSD_G):
        gs = slice(g * SSD_GW, (g + 1) * SSD_GW)
        bm = xbc[:, SSD_DI + g * SSD_N:SSD_DI + (g + 1) * SSD_N]
        cm = xbc[:, SSD_DI + SSD_BC + g * SSD_N:SSD_DI + SSD_BC + (g + 1) * SSD_N]
        cbs = jnp.sum(cm * bm, axis=-1, keepdims=True)
        outs = []
        for b in range(bb):
            hg = h_ref[b, gs, :]
            y_inter = _mm_nt(_pad_rows(cm[b:b + 1], 8), hg)[0:1]
            outs.append(el_x[b:b + 1, gs] * y_inter)
            h_out_ref[b, gs, :] = (_bcast_cols(el_x[b:b + 1, gs], SSD_N) * hg
                                   + _mm_tn(_pad_rows(xdt[b:b + 1, gs], 16), _pad_rows(bm[b:b + 1], 16)))
        y = jnp.concatenate(outs, axis=0) + cbs * xdt[:, gs] + dskip_ref[:, gs] * xs[:, gs]
        yg = y * _silu(z[:, gs])
        y_ref[:, gs] = (_rms(yg, SSD_GW) * ng_ref[:, gs]).astype(y_ref.dtype)


def _ssd_decode(proj, buf, hst, cw, cb, hp, dskip, ng, expand, bb):
    b = proj.shape[0]
    kern = functools.partial(_ssd_decode_kernel, bb=bb)
    col = lambda blk: (lambda i: (i, blk))
    const = lambda i: (0, 0)
    return pl.pallas_call(
        kern,
        grid=(b // bb,),
        in_specs=[pl.BlockSpec((bb, SSD_DI), col(O_OFF_X // SSD_DI)),
                  pl.BlockSpec((bb, SSD_DI), col(0)),
                  pl.BlockSpec((bb, 2 * SSD_BC), col(O_OFF_BC // (2 * SSD_BC))),
                  pl.BlockSpec((bb, LANES), col(O_OFF_DT // LANES)),
                  pl.BlockSpec((CONV_W - 1, bb, SSD_CONV_DIM), lambda i: (0, i, 0)),
                  pl.BlockSpec((bb, SSD_HEADS * SSD_P, SSD_N), lambda i: (i, 0, 0)),
                  pl.BlockSpec((CONV_W, SSD_CONV_DIM), const),
                  pl.BlockSpec((1, SSD_CONV_DIM), const),
                  pl.BlockSpec((8, LANES), const),
                  pl.BlockSpec((1, SSD_DI), const),
                  pl.BlockSpec((1, SSD_DI), const),
                  pl.BlockSpec((LANES, SSD_DI), const)],
        out_specs=[pl.BlockSpec((bb, SSD_DI), col(0)),
                   pl.BlockSpec((CONV_W - 1, bb, SSD_CONV_DIM), lambda i: (0, i, 0)),
                   pl.BlockSpec((bb, SSD_HEADS * SSD_P, SSD_N), lambda i: (i, 0, 0))],
        out_shape=[jax.ShapeDtypeStruct((b, SSD_DI), MM_DTYPE),
                   jax.ShapeDtypeStruct((CONV_W - 1, b, SSD_CONV_DIM), f32),
                   jax.ShapeDtypeStruct((b, SSD_HEADS * SSD_P, SSD_N), f32)],
        compiler_params=pltpu.CompilerParams(dimension_semantics=("parallel",), vmem_limit_bytes=VMEM_LIMIT),
        name="ssd_decode",
    )(proj, proj, proj, proj, buf, hst, cw, cb, hp, dskip, ng, expand)


def _pack_gla_w_in(w):
    gla = w[:, E_OFF_GLA:]
    return jnp.concatenate([gla, jnp.zeros((w.shape[0], L_N - gla.shape[1]), w.dtype)], axis=1)


def _lane_row(vals, offset):
    return jnp.zeros((LANES,), f32).at[offset:offset + vals.shape[0]].set(vals.astype(f32))


def kernel(x_prompt, x_sample, state_gdn_conv, state_gdn, state_gla, state_ssd_conv, state_ssd, e_pre_g, e_post_g, e_w_in, e_conv_w, e_conv_b, gdn_a_log, gdn_dt_bias, gdn_norm_g, gla_w_lr, gla_b_lr, gla_norm_g, e_w_out, o_pre_g, o_post_g, o_w_in, ssd_conv_w, ssd_conv_b, ssd_dt_bias, ssd_a_log, ssd_d, ssd_norm_g, o_w_out):
    bp, lp, d = x_prompt.shape
    bs = x_sample.shape[0]
    tp = bp * lp
    tb = min(lp, 512)
    bb = min(bs, 8)

    w_in0 = e_w_in[0].astype(MM_DTYPE)
    w_gla = _pack_gla_w_in(w_in0)
    w_out0 = e_w_out[0].astype(MM_DTYPE)
    pre0 = e_pre_g[0].reshape(1, d)
    post0 = e_post_g[0].reshape(1, d)
    cw0 = e_conv_w[0]
    cb0 = e_conv_b[0].reshape(1, GDN_CONV_DIM)
    hp0 = jnp.zeros((8, LANES), f32)
    hp0 = hp0.at[0].set(_lane_row(-jnp.exp(gdn_a_log[0].astype(f32)), SMALL_A))
    hp0 = hp0.at[1].set(_lane_row(gdn_dt_bias[0], SMALL_A))
    ng_gdn = gdn_norm_g[0].reshape(1, GDN_DV)
    wlr = jnp.zeros((LANES, GLA_KW), f32).at[SMALL_LR:SMALL_LR + GLA_RANK].set(gla_w_lr[0]).astype(MM_DTYPE)
    blr = gla_b_lr[0].reshape(1, GLA_KW)
    ng_gla = gla_norm_g[0].reshape(1, GLA_DV)

    hp_flat = x_prompt.reshape(tp, d)
    hs_flat = x_sample.reshape(bs, d)

    o_p, gdn_conv_p, gdn_p, gla_p = _even_prompt(x_prompt, pre0, w_in0, w_gla, cw0, cb0, hp0, ng_gdn, wlr, blr, ng_gla,
                                                 tb)
    o_p = o_p.reshape(tp, GDN_VW + GLA_VW)
    hp_flat = _out_proj(o_p, o_p, w_out0, post0, hp_flat)

    proj_gdn = _norm_proj(hs_flat, pre0, w_in0, E_GDN_N // 3, E_GDN_N)
    proj_gla = _norm_proj(hs_flat, pre0, w_gla, L_N // 5, L_N)
    o1_s, gdn_conv_s, gdn_s = _gdn_decode(proj_gdn, jnp.swapaxes(state_gdn_conv[0], 0, 1), state_gdn[0],
                                          cw0, cb0, hp0, ng_gdn, bb)
    o2_s, gla_s = _gla_decode(proj_gla, state_gla[0], wlr, blr, ng_gla, bb)
    hs_flat = _out_proj(o1_s, o2_s, w_out0, post0, hs_flat)

    w_in1 = o_w_in[0].astype(MM_DTYPE)
    w_out1 = o_w_out[0].astype(MM_DTYPE)
    pre1 = o_pre_g[0].reshape(1, d)
    post1 = o_post_g[0].reshape(1, d)
    cw1 = ssd_conv_w[0]
    cb1 = ssd_conv_b[0].reshape(1, SSD_CONV_DIM)
    hp1 = jnp.zeros((8, LANES), f32)
    hp1 = hp1.at[0].set(_lane_row(ssd_dt_bias[0], 0))
    hp1 = hp1.at[1].set(_lane_row(-jnp.exp(ssd_a_log[0].astype(f32)), 0))
    dskip = jnp.repeat(ssd_d[0].astype(f32), SSD_P).reshape(1, SSD_DI)
    ng_ssd = ssd_norm_g[0].reshape(1, SSD_DI)
    expand = (lax.broadcasted_iota(jnp.int32, (LANES, SSD_DI), 0)
              == lax.broadcasted_iota(jnp.int32, (LANES, SSD_DI), 1) // SSD_P).astype(MM_DTYPE)

    y_p, ssd_conv_p, ssd_p = _ssd_prompt(hp_flat.reshape(bp, lp, d), pre1, w_in1, cw1, cb1, hp1, dskip, ng_ssd,
                                         expand, tb)
    y_p = y_p.reshape(tp, SSD_DI)
    hp_flat = _out_proj(y_p, y_p, w_out1, post1, hp_flat)

    proj_s = _norm_proj(hs_flat, pre1, w_in1, O_PROJ_N // 6, O_PROJ_N)
    y_s, ssd_conv_s, ssd_s = _ssd_decode(proj_s, jnp.swapaxes(state_ssd_conv[0], 0, 1),
                                         state_ssd[0].reshape(bs, SSD_HEADS * SSD_P, SSD_N),
                                         cw1, cb1, hp1, dskip, ng_ssd, expand, bb)
    hs_flat = _out_proj(y_s, y_s, w_out1, post1, hs_flat)

    return (hp_flat.reshape(bp, lp, d), hs_flat.reshape(bs, 1, d),
            gdn_conv_p[None], gdn_p[None], gla_p[None],
            ssd_conv_p[None], ssd_p.reshape(bp, SSD_HEADS, SSD_P, SSD_N)[None],
            jnp.swapaxes(gdn_conv_s, 0, 1)[None], gdn_s[None], gla_s[None],
            jnp.swapaxes(ssd_conv_s, 0, 1)[None],
            ssd_s.reshape(bs, SSD_HEADS, SSD_P, SSD_N)[None])
```

```python
import functools

import jax
import jax.numpy as jnp
from jax import lax
from jax.experimental import pallas as pl
from jax.experimental.pallas import tpu as pltpu

f32 = jnp.float32
bf16 = jnp.bfloat16
MM_DTYPE = jnp.bfloat16

D_MODEL = 1024
CONV_W = 4
EPS = 1e-6
LANES = 128

GDN_HEADS = 8
GDN_DK = 128
GDN_DV = 128
GDN_KW = GDN_HEADS * GDN_DK
GDN_VW = GDN_HEADS * GDN_DV
GDN_CONV_DIM = 2 * GDN_KW + GDN_VW
GLA_HEADS = 4
GLA_DK = 128
GLA_DV = 256
GLA_KW = GLA_HEADS * GLA_DK
GLA_VW = GLA_HEADS * GLA_DV
GLA_RANK = 16
GLA_TAU = 16.0
SSD_DI = 2 * D_MODEL
SSD_P = 64
SSD_HEADS = SSD_DI // SSD_P
SSD_N = 128
SSD_G = 4
SSD_K = SSD_HEADS // SSD_G
SSD_GW = SSD_DI // SSD_G
SSD_BC = SSD_G * SSD_N
SSD_CONV_DIM = SSD_DI + 2 * SSD_BC

E_OFF_GATE = 3 * GDN_KW
E_OFF_GSMALL = 4 * GDN_KW
E_GDN_N = E_OFF_GSMALL + LANES
E_OFF_GLA = E_OFF_GSMALL + 2 * GDN_HEADS
SMALL_BETA = 0
SMALL_A = GDN_HEADS
L_OFF_K = GLA_KW
L_OFF_V = 2 * GLA_KW
L_OFF_GATE = L_OFF_V + GLA_VW
L_OFF_SMALL = L_OFF_GATE + GLA_VW
L_N = L_OFF_SMALL + LANES
SMALL_LR = 0
O_OFF_X = SSD_DI
O_OFF_BC = 2 * SSD_DI
O_OFF_DT = 2 * SSD_DI + 2 * SSD_BC
O_PROJ_N = 6 * 1024

CHUNK = 64
GDN_PREP_CHUNKS = 2
assert SSD_P == CHUNK and 2 * SSD_P == LANES
PROMPT_BLOCK = 512
SAMPLE_BLOCK = 8
ROW_TILE = 1024
V7X_VMEM_BYTES = 64 * 1024 * 1024
VMEM_LIMIT = V7X_VMEM_BYTES - 8 * 1024 * 1024


def _mm(a, b):
    return jnp.dot(a.astype(MM_DTYPE), b.astype(MM_DTYPE), preferred_element_type=f32)


def _mm_nt(a, b):
    return lax.dot_general(a.astype(MM_DTYPE), b.astype(MM_DTYPE), (((1,), (1,)), ((), ())),
                           preferred_element_type=f32)


def _mm_tn(a, b):
    return lax.dot_general(a.astype(MM_DTYPE), b.astype(MM_DTYPE), (((0,), (0,)), ((), ())),
                           preferred_element_type=f32)


def _split3(x):
    hi = x.astype(bf16)
    r = x - hi.astype(f32)
    mid = r.astype(bf16)
    lo = (r - mid.astype(f32)).astype(bf16)
    return hi, mid, lo


def _sel_mm(sel, x):
    hi, mid, lo = _split3(x)
    s = sel.astype(bf16)
    return (jnp.dot(s, hi, preferred_element_type=f32) + jnp.dot(s, mid, preferred_element_type=f32)
            + jnp.dot(s, lo, preferred_element_type=f32))


def _sel_rows(x, sel):
    hi, mid, lo = _split3(x)
    s = sel.astype(bf16)
    return (jnp.dot(hi, s, preferred_element_type=f32) + jnp.dot(mid, s, preferred_element_type=f32)
            + jnp.dot(lo, s, preferred_element_type=f32))


def _sigmoid(x):
    return 1.0 / (1.0 + jnp.exp(-x))


def _silu(x):
    return x * _sigmoid(x)


def _softplus(x):
    return jnp.maximum(x, 0.0) + jnp.log1p(jnp.exp(-jnp.abs(x)))


def _tril(n, strict=False):
    r = lax.broadcasted_iota(jnp.int32, (n, n), 0)
    c = lax.broadcasted_iota(jnp.int32, (n, n), 1)
    return (r > c) if strict else (r >= c)


def _masked_decay(col, row, incl):
    return jnp.exp(jnp.where(incl, col - row, -jnp.inf))


def _interleave(main, side, ratio):
    gens = [main, side]
    steps = {id(main): ratio, id(side): 1}
    while gens:
        for g in list(gens):
            for _ in range(steps[id(g)]):
                try:
                    next(g)
                except StopIteration:
                    gens.remove(g)
                    break
                yield


def _round_robin(gens):
    gens = list(gens)
    while gens:
        for g in list(gens):
            try:
                next(g)
            except StopIteration:
                gens.remove(g)
        yield


def _rms(x, width):
    return x * lax.rsqrt(jnp.sum(x * x, axis=-1, keepdims=True) * (1.0 / width) + EPS)


def _norm_proj_kernel(x_ref, g_ref, w_ref, o_ref, u_scr):
    @pl.when(pl.program_id(1) == 0)
    def _():
        u_scr[...] = (_rms(x_ref[...], D_MODEL) * g_ref[...]).astype(u_scr.dtype)

    o_ref[...] = jnp.dot(u_scr[...], w_ref[...], preferred_element_type=f32)


def _norm_proj(x, g, w, tn, n):
    t, d = x.shape
    assert n % tn == 0 and n - tn < w.shape[1]
    tm = min(t, ROW_TILE)
    return pl.pallas_call(
        _norm_proj_kernel,
        grid=(t // tm, n // tn),
        in_specs=[pl.BlockSpec((tm, d), lambda i, j: (i, 0)),
                  pl.BlockSpec((1, d), lambda i, j: (0, 0)),
                  pl.BlockSpec((d, tn), lambda i, j: (0, j))],
        out_specs=pl.BlockSpec((tm, tn), lambda i, j: (i, j)),
        out_shape=jax.ShapeDtypeStruct((t, n), f32),
        scratch_shapes=[pltpu.VMEM((tm, d), MM_DTYPE)],
        compiler_params=pltpu.CompilerParams(dimension_semantics=("parallel", "arbitrary"),
                                             vmem_limit_bytes=VMEM_LIMIT),
        name="norm_proj",
    )(x, g, w)


def _out_proj_kernel(oa_ref, ob_ref, wa_ref, wb_ref, g_ref, h_ref, y_ref):
    p = (jnp.dot(oa_ref[...], wa_ref[...], preferred_element_type=f32)
         + jnp.dot(ob_ref[...], wb_ref[...], preferred_element_type=f32))
    y_ref[...] = h_ref[...] + _rms(p, D_MODEL) * g_ref[...]


def _out_proj(oa, ob, w, g, h):
    t = oa.shape[0]
    e = w.shape[0] // 2
    assert oa.shape[1] == ob.shape[1] and oa.shape[1] in (e, 2 * e)
    col_b = 1 if ob.shape[1] == 2 * e else 0
    d = w.shape[1]
    tm = min(t, ROW_TILE)
    return pl.pallas_call(
        _out_proj_kernel,
        grid=(t // tm,),
        in_specs=[pl.BlockSpec((tm, e), lambda i: (i, 0)),
                  pl.BlockSpec((tm, e), lambda i: (i, col_b)),
                  pl.BlockSpec((e, d), lambda i: (0, 0)),
                  pl.BlockSpec((e, d), lambda i: (1, 0)),
                  pl.BlockSpec((1, d), lambda i: (0, 0)),
                  pl.BlockSpec((tm, d), lambda i: (i, 0))],
        out_specs=pl.BlockSpec((tm, d), lambda i: (i, 0)),
        out_shape=jax.ShapeDtypeStruct((t, d), f32),
        compiler_params=pltpu.CompilerParams(dimension_semantics=("parallel",),
                                             vmem_limit_bytes=VMEM_LIMIT),
        name="out_proj",
    )(oa, ob, w, w, g, h)


def _normed(x_ref, g_ref):
    return (_rms(x_ref[...], D_MODEL) * g_ref[...]).astype(MM_DTYPE)


def _proj(u, w_ref, c0, c1):
    return jnp.dot(u, w_ref[:, c0:c1], preferred_element_type=f32)


def _conv_block(xp_scr, cw_ref, cb_ref, tb, first):
    @pl.when(first)
    def _():
        xp_scr[0:8, :] = jnp.zeros((8, xp_scr.shape[1]), f32)

    @pl.when(jnp.logical_not(first))
    def _():
        xp_scr[5:8, :] = xp_scr[tb + 5:tb + 8, :]

    def conv_cols(c0, c1, r0=0, nr=tb):
        acc = cb_ref[:, c0:c1] + xp_scr[8 + r0:8 + r0 + nr, c0:c1] * cw_ref[3:4, c0:c1]
        for s in range(CONV_W - 1):
            acc = acc + xp_scr[5 + s + r0:5 + s + r0 + nr, c0:c1] * cw_ref[s:s + 1, c0:c1]
        return _silu(acc)

    return conv_cols


def _gdn_stages(u, first, last, wm_ref, ws_ref, cw_ref, cb_ref, hp_ref, ng_ref,
                o_ref, conv_out_ref, s_out_ref,
                xp_scr, qkv_scr, gate_scr, beta_scr, lg_scr, s_scr, gl_scr, up_scr, w_scr, qg_scr, kd_scr,
                p_scr, *, tb):
    conv_cols = _conv_block(xp_scr, cw_ref, cb_ref, tb, first)

    @pl.when(first)
    def _():
        s_scr[...] = jnp.zeros(s_scr.shape, f32)

    def proj_stages():
        for c0 in range(0, GDN_CONV_DIM, GDN_KW):
            xp_scr[8:8 + tb, c0:c0 + GDN_KW] = _proj(u, wm_ref, c0, c0 + GDN_KW)
            yield
        small = _proj(u, ws_ref, 0, LANES)
        beta_scr[...] = _sigmoid(small)
        lg_scr[...] = hp_ref[0:1, :] * _softplus(small + hp_ref[1:2, :])
        gate_scr[...] = _proj(u, wm_ref, E_OFF_GATE, E_OFF_GATE + GDN_VW)
        yield

    span = GDN_PREP_CHUNKS * CHUNK

    def conv_norm(it):
        r0 = it * span
        rr = slice(r0, r0 + span)
        for h in range(GDN_HEADS):
            c0 = h * GDN_DK
            q = conv_cols(c0, c0 + GDN_DK, r0, span)
            q = q * lax.rsqrt(jnp.sum(q * q, axis=-1, keepdims=True) + EPS) * (GDN_DK ** -0.5)
            qkv_scr[rr, c0:c0 + GDN_DK] = q
            k = conv_cols(GDN_KW + c0, GDN_KW + c0 + GDN_DK, r0, span)
            k = k * lax.rsqrt(jnp.sum(k * k, axis=-1, keepdims=True) + EPS)
            qkv_scr[rr, GDN_KW + c0:GDN_KW + c0 + GDN_DK] = k
            qkv_scr[rr, 2 * GDN_KW + c0:2 * GDN_KW + c0 + GDN_DV] = conv_cols(
                2 * GDN_KW + c0, 2 * GDN_KW + c0 + GDN_DV, r0, span)
            yield

    incl = _tril(CHUNK)
    strict = _tril(CHUNK, strict=True)
    eye = jnp.logical_and(incl, jnp.logical_not(strict)).astype(f32)
    sr = lax.broadcasted_iota(jnp.int32, (span, span), 0)
    sc = lax.broadcasted_iota(jnp.int32, (span, span), 1)
    lmat = jnp.logical_and(sr >= sc, sr // CHUNK == sc // CHUNK).astype(f32)
    ng = ng_ref[...]
    heads = range(GDN_HEADS)
    probs = [(ci, h) for ci in range(GDN_PREP_CHUNKS) for h in heads]
    n = range(len(probs))

    def prep(c):
        r0 = c * span
        gam_all = _sel_mm(lmat, lg_scr[pl.ds(r0, span), :])
        gam_t = gam_all.T
        beta_all = beta_scr[pl.ds(r0, span), :]
        rows, gcol, grow, glast, bcol = [], [], [], [], []
        for ci, h in probs:
            lo, hi = ci * CHUNK, (ci + 1) * CHUNK
            rows.append(pl.ds(r0 + lo, CHUNK))
            gcol.append(gam_all[lo:hi, SMALL_A + h:SMALL_A + h + 1])
            grow.append(gam_t[SMALL_A + h:SMALL_A + h + 1, lo:hi])
            glast.append(gam_all[hi - 1:hi, SMALL_A + h:SMALL_A + h + 1])
            bcol.append(beta_all[lo:hi, SMALL_BETA + h:SMALL_BETA + h + 1])
        for ci in range(GDN_PREP_CHUNKS):
            gl_scr[pl.ds(c * GDN_PREP_CHUNKS + ci, 1), :] = jnp.exp(gam_all[(ci + 1) * CHUNK - 1:(ci + 1) * CHUNK, :])
        q = [qkv_scr[rows[j], h * GDN_DK:(h + 1) * GDN_DK] for j, (_, h) in enumerate(probs)]
        k = [qkv_scr[rows[j], GDN_KW + h * GDN_DK:GDN_KW + (h + 1) * GDN_DK] for j, (_, h) in enumerate(probs)]
        v = [qkv_scr[rows[j], 2 * GDN_KW + h * GDN_DV:2 * GDN_KW + (h + 1) * GDN_DV] for j, (_, h) in enumerate(probs)]
        kb = [k[j].astype(MM_DTYPE) for j in n]
        kk = [_mm_nt(kb[j], kb[j]) for j in n]
        yield
        qk = [_mm_nt(q[j], kb[j]) for j in n]
        yield
        decay = [_masked_decay(gcol[j], grow[j], incl) for j in n]
        eg = [jnp.exp(gcol[j]) for j in n]
        for j, (_, h) in enumerate(probs):
            c0 = h * GDN_DK
            qg_scr[rows[j], c0:c0 + GDN_DK] = (eg[j] * q[j]).astype(MM_DTYPE)
            kd_scr[rows[j], c0:c0 + GDN_DK] = (jnp.exp(glast[j] - gcol[j]) * k[j]).astype(MM_DTYPE)
            p_scr[h, rows[j], :] = (qk[j] * decay[j]).astype(MM_DTYPE)
        nmat = [jnp.where(strict, bcol[j] * (kk[j] * decay[j]), 0.0) for j in n]
        tinv = [eye - nmat[j] for j in n]
        npow = nmat
        for _ in range(5):
            npow = [_mm(npow[j], npow[j]) for j in n]
            yield
            tinv = [tinv[j] + _mm(tinv[j], npow[j]) for j in n]
            yield
        x = [_mm(tinv[j], jnp.concatenate([bcol[j] * v[j], (bcol[j] * eg[j]) * k[j]], axis=-1)) for j in n]
        for j, (_, h) in enumerate(probs):
            c0 = h * GDN_DK
            up_scr[rows[j], c0:c0 + GDN_DV] = x[j][:, :GDN_DV]
            w_scr[rows[j], c0:c0 + GDN_DK] = x[j][:, GDN_DV:].astype(MM_DTYPE)

    state = [s_scr[h] for h in heads]

    def scan(c):
        rows = pl.ds(c * CHUNK, CHUNK)
        gl = gl_scr[c:c + 1, :]
        r = [_mm(jnp.concatenate([w_scr[rows, h * GDN_DK:(h + 1) * GDN_DK],
                                  qg_scr[rows, h * GDN_DK:(h + 1) * GDN_DK]], axis=0), state[h]) for h in heads]
        yield
        us = [(up_scr[rows, h * GDN_DV:(h + 1) * GDN_DV] - r[h][:CHUNK]).astype(MM_DTYPE) for h in heads]
        o = [r[h][CHUNK:] + _mm(p_scr[h, rows, :], us[h]) for h in heads]
        yield
        upd = [_mm_tn(kd_scr[rows, h * GDN_DK:(h + 1) * GDN_DK], us[h]) for h in heads]
        yield
        for h in heads:
            c0 = h * GDN_DV
            state[h] = gl[:, SMALL_A + h:SMALL_A + h + 1] * state[h] + upd[h]
            gate = gate_scr[rows, c0:c0 + GDN_DV]
            o_ref[rows, c0:c0 + GDN_DV] = (_rms(o[h], GDN_DV) * ng * _silu(gate)).astype(o_ref.dtype)
        yield

    def scans(chunks):
        for c in chunks:
            yield from scan(c)

    def side_work(it):
        if it + 1 < n_prep:
            yield from conv_norm(it + 1)
        if it >= 1:
            yield from scans(range((it - 1) * GDN_PREP_CHUNKS, it * GDN_PREP_CHUNKS))

    n_prep = tb // span

    def stages():
        yield from proj_stages()
        yield from conv_norm(0)
        for it in range(n_prep):
            yield from _interleave(prep(it), side_work(it), 1)
        yield from scans(range((n_prep - 1) * GDN_PREP_CHUNKS, n_prep * GDN_PREP_CHUNKS))

    def finish():
        for h in heads:
            s_scr[h] = state[h]

        @pl.when(last)
        def _():
            conv_out_ref[...] = xp_scr[tb + 5:tb + 8, :]
            s_out_ref[...] = s_scr[...]

    return stages(), finish


def _gla_stages(u, first, last, wq_ref, wk_ref, wv_ref, wg_ref, ws_ref, wlr_ref, blr_ref, ng_ref,
                o_ref, s_out_ref, q_ref, k_ref, v_ref, gate_ref, la_scr, st_scr, *, tb):
    @pl.when(first)
    def _():
        st_scr[...] = jnp.zeros(st_scr.shape, f32)

    def proj_stages():
        q_ref[...] = _proj(u, wq_ref, 0, GLA_KW)
        k_ref[...] = _proj(u, wk_ref, 0, GLA_KW)
        small = _proj(u, ws_ref, 0, LANES)
        z = _mm(small, wlr_ref[...]) + blr_ref[...]
        la_scr[...] = -_softplus(-z) * (1.0 / GLA_TAU)
        yield
        v_ref[...] = _proj(u, wv_ref, 0, GLA_VW).astype(v_ref.dtype)
        yield
        gate_ref[...] = _proj(u, wg_ref, 0, GLA_VW)
        yield

    incl = _tril(CHUNK)
    lmat = incl.astype(f32)
    ng = ng_ref[...]

    heads = range(GLA_HEADS)
    n_chunks = tb // CHUNK
    staged = [None] * n_chunks
    state = [st_scr[h] for h in heads]

    def prep(c):
        rows = pl.ds(c * CHUNK, CHUNK)
        bcum = [_sel_mm(lmat, la_scr[rows, h * GLA_DK:(h + 1) * GLA_DK]) for h in heads]
        yield
        k = [k_ref[rows, h * GLA_DK:(h + 1) * GLA_DK] for h in heads]
        v = [v_ref[rows, h * GLA_DV:(h + 1) * GLA_DV].astype(MM_DTYPE) for h in heads]
        qe = [(q_ref[rows, h * GLA_DK:(h + 1) * GLA_DK] * (GLA_DK ** -0.5) * jnp.exp(bcum[h])).astype(MM_DTYPE)
              for h in heads]
        ke = [k[h] * jnp.exp(-bcum[h]) for h in heads]
        blast = [bcum[h][CHUNK - 1:CHUNK, :] for h in heads]
        kdec = [(k[h] * jnp.exp(blast[h] - bcum[h])).astype(MM_DTYPE) for h in heads]
        p = [jnp.where(incl, _mm_nt(qe[h], ke[h]), 0.0) for h in heads]
        yield
        o_intra = [_mm(p[h], v[h]) for h in heads]
        staged[c] = (qe, v, kdec, blast, o_intra)
        yield

    def scan(c):
        rows = pl.ds(c * CHUNK, CHUNK)
        qe, v, kdec, blast, o_intra = staged[c]
        o = [o_intra[h] + _mm_nt(qe[h], state[h]) for h in heads]
        upd = [_mm_tn(v[h], kdec[h]) for h in heads]
        yield
        for h in heads:
            v0 = h * GLA_DV
            state[h] = jnp.exp(blast[h]) * state[h] + upd[h]
            gate = gate_ref[rows, v0:v0 + GLA_DV]
            o_ref[rows, GDN_VW + v0:GDN_VW + v0 + GLA_DV] = (
                _rms(o[h], GLA_DV) * ng * _silu(gate)).astype(o_ref.dtype)
        yield

    def stages():
        yield from proj_stages()
        yield from prep(0)
        for c in range(1, n_chunks):
            yield from _interleave(prep(c), scan(c - 1), 1)
        yield from scan(n_chunks - 1)

    def finish():
        for h in heads:
            st_scr[h] = state[h]

        @pl.when(last)
        def _():
            for h in heads:
                s_out_ref[h] = st_scr[h].T

    return stages(), finish


N_GDN_SCRATCH = 12
EVEN_STAGE_RATIO = 2


def _even_prompt_kernel(x_ref, pg_ref, wm_ref, ws_ref, wq_ref, wk_ref, wv_ref, wg_ref, wsl_ref, cw_ref, cb_ref, hp_ref,
                        ngd_ref, wlr_ref, blr_ref, ngl_ref,
                        o_ref, conv_out_ref, sgdn_out_ref, sgla_out_ref, *scratch, tb):
    i = pl.program_id(1)
    first = i == 0
    last = i == pl.num_programs(1) - 1
    u = _normed(x_ref, pg_ref)
    gdn, gdn_finish = _gdn_stages(u, first, last, wm_ref, ws_ref, cw_ref, cb_ref, hp_ref, ngd_ref,
                                  o_ref, conv_out_ref, sgdn_out_ref, *scratch[:N_GDN_SCRATCH], tb=tb)
    gla, gla_finish = _gla_stages(u, first, last, wq_ref, wk_ref, wv_ref, wg_ref, wsl_ref, wlr_ref, blr_ref, ngl_ref,
                                  o_ref, sgla_out_ref, *scratch[N_GDN_SCRATCH:], tb=tb)
    for _ in _interleave(gdn, gla, EVEN_STAGE_RATIO):
        pass
    gdn_finish()
    gla_finish()


def _even_prompt(x, pre_g, w_in, w_gla, cw, cb, hp, ng_gdn, wlr, blr, ng_gla, tb):
    b, l, d = x.shape
    assert tb % (GDN_PREP_CHUNKS * CHUNK) == 0 and l % tb == 0
    kern = functools.partial(_even_prompt_kernel, tb=tb)
    wcol = lambda blk: (lambda bi, i: (0, blk))
    const = lambda bi, i: (0, 0)
    gdn_scratch = [pltpu.VMEM((tb + 8, GDN_CONV_DIM), f32),
                   pltpu.VMEM((tb, GDN_CONV_DIM), f32),
                   pltpu.VMEM((tb, GDN_VW), f32),
                   pltpu.VMEM((tb, LANES), f32),
                   pltpu.VMEM((tb, LANES), f32),
                   pltpu.VMEM((GDN_HEADS, GDN_DK, GDN_DV), f32),
                   pltpu.VMEM((max(8, tb // CHUNK), LANES), f32),
                   pltpu.VMEM((tb, GDN_VW), f32),
                   pltpu.VMEM((tb, GDN_KW), MM_DTYPE),
                   pltpu.VMEM((tb, GDN_KW), MM_DTYPE),
                   pltpu.VMEM((tb, GDN_KW), MM_DTYPE),
                   pltpu.VMEM((GDN_HEADS, tb, CHUNK), MM_DTYPE)]
    assert len(gdn_scratch) == N_GDN_SCRATCH
    gla_scratch = [pltpu.VMEM((tb, GLA_KW), f32),
                   pltpu.VMEM((tb, GLA_KW), f32),
                   pltpu.VMEM((tb, GLA_VW), MM_DTYPE),
                   pltpu.VMEM((tb, GLA_VW), f32),
                   pltpu.VMEM((tb, GLA_KW), f32),
                   pltpu.VMEM((GLA_HEADS, GLA_DV, GLA_DK), f32)]
    return pl.pallas_call(
        kern,
        grid=(b, l // tb),
        in_specs=[pl.BlockSpec((None, tb, d), lambda bi, i: (bi, i, 0)),
                  pl.BlockSpec((1, d), const),
                  pl.BlockSpec((d, E_OFF_GSMALL), const),
                  pl.BlockSpec((d, LANES), wcol(E_OFF_GSMALL // LANES)),
                  pl.BlockSpec((d, GLA_KW), wcol(0)),
                  pl.BlockSpec((d, GLA_KW), wcol(L_OFF_K // GLA_KW)),
                  pl.BlockSpec((d, GLA_VW), wcol(L_OFF_V // GLA_VW)),
                  pl.BlockSpec((d, GLA_VW), wcol(L_OFF_GATE // GLA_VW)),
                  pl.BlockSpec((d, LANES), wcol(L_OFF_SMALL // LANES)),
                  pl.BlockSpec((CONV_W, GDN_CONV_DIM), const),
                  pl.BlockSpec((1, GDN_CONV_DIM), const),
                  pl.BlockSpec((8, LANES), const),
                  pl.BlockSpec((1, GDN_DV), const),
                  pl.BlockSpec((LANES, GLA_KW), const),
                  pl.BlockSpec((1, GLA_KW), const),
                  pl.BlockSpec((1, GLA_DV), const)],
        out_specs=[pl.BlockSpec((None, tb, GDN_VW + GLA_VW), lambda bi, i: (bi, i, 0)),
                   pl.BlockSpec((None, CONV_W - 1, GDN_CONV_DIM), lambda bi, i: (bi, 0, 0)),
                   pl.BlockSpec((None, GDN_HEADS, GDN_DK, GDN_DV), lambda bi, i: (bi, 0, 0, 0)),
                   pl.BlockSpec((None, GLA_HEADS, GLA_DK, GLA_DV), lambda bi, i: (bi, 0, 0, 0))],
        out_shape=[jax.ShapeDtypeStruct((b, l, GDN_VW + GLA_VW), MM_DTYPE),
                   jax.ShapeDtypeStruct((b, CONV_W - 1, GDN_CONV_DIM), f32),
                   jax.ShapeDtypeStruct((b, GDN_HEADS, GDN_DK, GDN_DV), f32),
                   jax.ShapeDtypeStruct((b, GLA_HEADS, GLA_DK, GLA_DV), f32)],
        scratch_shapes=gdn_scratch + gla_scratch,
        compiler_params=pltpu.CompilerParams(dimension_semantics=("parallel", "arbitrary"),
                                             vmem_limit_bytes=VMEM_LIMIT),
        name="even_prompt",
    )(x, pre_g, w_in, w_in, w_gla, w_gla, w_gla, w_gla, w_gla, cw, cb, hp, ng_gdn, wlr, blr, ng_gla)


def _head_lanes(x):
    return jnp.where(lax.broadcasted_iota(jnp.int32, x.shape, 1) < SSD_HEADS, x, 0.0)


def _ssd_prompt_kernel(x_ref, pg_ref, wx_ref, wz_ref, wbc_ref, wdt_ref, cw_ref, cb_ref, hp_ref, dskip_ref, ng_ref,
                       expand_ref, y_ref, conv_out_ref, h_out_ref,
                       xp_scr, xbc_scr, z_ref, lam_scr, lamx_scr, xdt_scr, h_scr, *, tb):
    i = pl.program_id(1)
    first = i == 0
    conv_cols = _conv_block(xp_scr, cw_ref, cb_ref, tb, first)

    @pl.when(first)
    def _():
        h_scr[...] = jnp.zeros(h_scr.shape, f32)

    u = _normed(x_ref, pg_ref)
    for c0 in range(0, SSD_DI, 1024):
        xp_scr[8:8 + tb, c0:c0 + 1024] = _proj(u, wx_ref, c0, c0 + 1024)
    xp_scr[8:8 + tb, SSD_DI:SSD_CONV_DIM] = _proj(u, wbc_ref, 0, 2 * SSD_BC)
    dt_raw = _head_lanes(_proj(u, wdt_ref, 0, LANES))
    for c0 in range(0, SSD_DI, 1024):
        z_ref[:, c0:c0 + 1024] = _proj(u, wz_ref, c0, c0 + 1024)

    for c0 in range(0, SSD_CONV_DIM, 512):
        xbc_scr[:, c0:c0 + 512] = conv_cols(c0, c0 + 512)
    dt = _softplus(dt_raw + hp_ref[0:1, :])
    tr = lax.broadcasted_iota(jnp.int32, (tb, tb), 0)
    tc = lax.broadcasted_iota(jnp.int32, (tb, tb), 1)
    lam_all = _sel_mm(jnp.logical_and(tr >= tc, tr // CHUNK == tc // CHUNK).astype(f32), dt * hp_ref[1:2, :])
    lam_scr[...] = lam_all
    lo_blk = lax.broadcasted_iota(jnp.int32, (tb, LANES), 1) < SSD_P
    for p in range(SSD_HEADS // 2):
        cols = slice(p * LANES, (p + 1) * LANES)
        k0 = 2 * p
        xdt_scr[:, cols] = xbc_scr[:, cols] * jnp.where(lo_blk, dt[:, k0:k0 + 1], dt[:, k0 + 1:k0 + 2])
    for c0 in range(0, SSD_DI, 512):
        lamx_scr[:, c0:c0 + 512] = _sel_rows(lam_all, expand_ref[:, c0:c0 + 512])

    lane = lax.broadcasted_iota(jnp.int32, (CHUNK, LANES), 1)
    lo = lane < SSD_P
    incl2 = lax.broadcasted_iota(jnp.int32, (CHUNK, LANES), 0) >= lane % CHUNK
    top = lax.broadcasted_iota(jnp.int32, (2 * SSD_P, SSD_N), 0) < SSD_P
    n_pairs = SSD_HEADS // 2
    ppg = SSD_K // 2
    groups = range(SSD_G)

    def chunk_body(c):
        rows = pl.ds(c * CHUNK, CHUNK)
        lam = lam_scr[rows, :]
        lam_t = lam.T
        llast = lam[CHUNK - 1:CHUNK, :]
        bm = [xbc_scr[rows, SSD_DI + g * SSD_N:SSD_DI + (g + 1) * SSD_N].astype(MM_DTYPE) for g in groups]
        cm = [xbc_scr[rows, SSD_DI + SSD_BC + g * SSD_N:SSD_DI + SSD_BC + (g + 1) * SSD_N].astype(MM_DTYPE)
              for g in groups]
        h_old = [h_scr[g * SSD_GW:(g + 1) * SSD_GW, :] for g in groups]
        cb2 = [_mm_nt(cm[g], jnp.concatenate([bm[g], bm[g]], axis=0)) for g in groups]
        y_inter = [_mm_nt(cm[g], h_old[g]) for g in groups]
        xs, xdt, xdec, lcol = [], [], [], []
        for p in range(n_pairs):
            cols = slice(p * LANES, (p + 1) * LANES)
            xs.append(xbc_scr[rows, cols])
            lcol.append(lamx_scr[rows, cols])
            xdt.append(xdt_scr[rows, cols])
            xdec.append((xdt[p] * jnp.exp(lcol[p][CHUNK - 1:CHUNK, :] - lcol[p])).astype(MM_DTYPE))
        upd = [_mm_tn(jnp.concatenate(xdec[g * ppg:(g + 1) * ppg], axis=-1), bm[g]) for g in groups]
        y_intra = []
        for p in range(n_pairs):
            k0 = 2 * p
            lrow = jnp.concatenate([lam_t[k0:k0 + 1, :], lam_t[k0 + 1:k0 + 2, :]], axis=-1)
            m = cb2[p // ppg] * _masked_decay(lcol[p], lrow, incl2)
            blockdiag = jnp.concatenate([jnp.where(lo, xdt[p], 0.0), jnp.where(lo, 0.0, xdt[p])], axis=0)
            y_intra.append(_mm(m, blockdiag))
        for g in groups:
            ys = []
            for pp in range(ppg):
                p = g * ppg + pp
                k0 = 2 * p
                ys.append(y_intra[p] + jnp.exp(lcol[p]) * y_inter[g][:, pp * LANES:(pp + 1) * LANES]
                          + dskip_ref[:, p * LANES:(p + 1) * LANES] * xs[p])
                el = jnp.where(top, jnp.exp(llast[:, k0:k0 + 1]), jnp.exp(llast[:, k0 + 1:k0 + 2]))
                h_scr[p * LANES:(p + 1) * LANES, :] = (el * h_old[g][pp * LANES:(pp + 1) * LANES, :]
                                                        + upd[g][pp * LANES:(pp + 1) * LANES, :])
            yg = jnp.concatenate(ys, axis=-1) * _silu(z_ref[rows, g * SSD_GW:(g + 1) * SSD_GW])
            y_ref[rows, g * SSD_GW:(g + 1) * SSD_GW] = (
                _rms(yg, SSD_GW) * ng_ref[:, g * SSD_GW:(g + 1) * SSD_GW]).astype(y_ref.dtype)

    for c in range(tb // CHUNK):
        chunk_body(c)

    @pl.when(i == pl.num_programs(1) - 1)
    def _():
        conv_out_ref[...] = xp_scr[tb + 5:tb + 8, :]
        h_out_ref[...] = h_scr[...]


def _ssd_prompt(x, pre_g, w_in, cw, cb, hp, dskip, ng, expand, tb):
    b, l, d = x.shape
    kern = functools.partial(_ssd_prompt_kernel, tb=tb)
    wcol = lambda blk: (lambda bi, i: (0, blk))
    const = lambda bi, i: (0, 0)
    return pl.pallas_call(
        kern,
        grid=(b, l // tb),
        in_specs=[pl.BlockSpec((None, tb, d), lambda bi, i: (bi, i, 0)),
                  pl.BlockSpec((1, d), const),
                  pl.BlockSpec((d, SSD_DI), wcol(O_OFF_X // SSD_DI)),
                  pl.BlockSpec((d, SSD_DI), wcol(0)),
                  pl.BlockSpec((d, 2 * SSD_BC), wcol(O_OFF_BC // (2 * SSD_BC))),
                  pl.BlockSpec((d, LANES), wcol(O_OFF_DT // LANES)),
                  pl.BlockSpec((CONV_W, SSD_CONV_DIM), const),
                  pl.BlockSpec((1, SSD_CONV_DIM), const),
                  pl.BlockSpec((8, LANES), const),
                  pl.BlockSpec((1, SSD_DI), const),
                  pl.BlockSpec((1, SSD_DI), const),
                  pl.BlockSpec((LANES, SSD_DI), const)],
        out_specs=[pl.BlockSpec((None, tb, SSD_DI), lambda bi, i: (bi, i, 0)),
                   pl.BlockSpec((None, CONV_W - 1, SSD_CONV_DIM), lambda bi, i: (bi, 0, 0)),
                   pl.BlockSpec((None, SSD_HEADS * SSD_P, SSD_N), lambda bi, i: (bi, 0, 0))],
        out_shape=[jax.ShapeDtypeStruct((b, l, SSD_DI), MM_DTYPE),
                   jax.ShapeDtypeStruct((b, CONV_W - 1, SSD_CONV_DIM), f32),
                   jax.ShapeDtypeStruct((b, SSD_HEADS * SSD_P, SSD_N), f32)],
        scratch_shapes=[pltpu.VMEM((tb + 8, SSD_CONV_DIM), f32),
                        pltpu.VMEM((tb, SSD_CONV_DIM), f32),
                        pltpu.VMEM((tb, SSD_DI), f32),
                        pltpu.VMEM((tb, LANES), f32),
                        pltpu.VMEM((tb, SSD_DI), f32),
                        pltpu.VMEM((tb, SSD_DI), f32),
                        pltpu.VMEM((SSD_HEADS * SSD_P, SSD_N), f32)],
        compiler_params=pltpu.CompilerParams(dimension_semantics=("parallel", "arbitrary"),
                                             vmem_limit_bytes=VMEM_LIMIT),
        name="ssd_prompt",
    )(x, pre_g, w_in, w_in, w_in, w_in, cw, cb, hp, dskip, ng, expand)


def _pad_rows(row, n):
    return jnp.concatenate([row, jnp.zeros((n - 1, row.shape[1]), row.dtype)], axis=0)


def _bcast_cols(row, width):
    hi, mid, lo = _split3(row)
    lhs = jnp.concatenate([hi, mid, lo, jnp.zeros((13, row.shape[1]), bf16)], axis=0)
    sel = (lax.broadcasted_iota(jnp.int32, (16, width), 0) < 3).astype(bf16)
    return lax.dot_general(lhs, sel, (((0,), (0,)), ((), ())), preferred_element_type=f32)


def _conv_step(buf_ref, x, cw_ref, cb_ref, new_buf_ref):
    acc = cb_ref[...] + x * cw_ref[3:4, :]
    for s in range(CONV_W - 1):
        acc = acc + buf_ref[s] * cw_ref[s:s + 1, :]
    for s in range(CONV_W - 2):
        new_buf_ref[s] = buf_ref[s + 1]
    new_buf_ref[CONV_W - 2] = x
    return _silu(acc)


def _gdn_decode_kernel(q_ref, k_ref, v_ref, gate_ref, small_ref, buf_ref, s_ref, cw_ref, cb_ref, hp_ref, ng_ref,
                       o_ref, new_buf_ref, s_out_ref, *, bb):
    x = jnp.concatenate([q_ref[...], k_ref[...], v_ref[...]], axis=-1)
    qkv = _conv_step(buf_ref, x, cw_ref, cb_ref, new_buf_ref)
    small = small_ref[...]
    beta = _sigmoid(small)
    eg = jnp.exp(hp_ref[0:1, :] * _softplus(small + hp_ref[1:2, :]))
    ng = ng_ref[...]
    gate = gate_ref[...]

    def head(h):
        c0 = h * GDN_DK
        q = qkv[:, c0:c0 + GDN_DK]
        q = q * lax.rsqrt(jnp.sum(q * q, axis=-1, keepdims=True) + EPS) * (GDN_DK ** -0.5)
        k = qkv[:, GDN_KW + c0:GDN_KW + c0 + GDN_DK]
        k = k * lax.rsqrt(jnp.sum(k * k, axis=-1, keepdims=True) + EPS)
        v = qkv[:, 2 * GDN_KW + c0:2 * GDN_KW + c0 + GDN_DV]
        qk = jnp.sum(q * k, axis=-1, keepdims=True)
        b_h = beta[:, SMALL_BETA + h:SMALL_BETA + h + 1]
        eg_h = eg[:, SMALL_A + h:SMALL_A + h + 1]
        seqs = range(bb)
        s_old = [s_ref[b, h] for b in seqs]
        r = [_mm(jnp.concatenate([k[b:b + 1], q[b:b + 1], jnp.zeros((6, GDN_DK), f32)], axis=0), s_old[b])
             for b in seqs]
        yield
        u = [b_h[b:b + 1] * v[b:b + 1] - (b_h[b:b + 1] * eg_h[b:b + 1]) * r[b][0:1] for b in seqs]
        upd = [_mm_tn(_pad_rows(k[b:b + 1], 16), _pad_rows(u[b], 16)) for b in seqs]
        yield
        for b in seqs:
            s_out_ref[b, h] = eg_h[b:b + 1] * s_old[b] + upd[b]
        o = jnp.concatenate([eg_h[b:b + 1] * r[b][1:2] + qk[b:b + 1] * u[b] for b in seqs], axis=0)
        o_ref[:, c0:c0 + GDN_DV] = (_rms(o, GDN_DV) * ng * _silu(gate[:, c0:c0 + GDN_DV])).astype(o_ref.dtype)

    for _ in _round_robin([head(h) for h in range(GDN_HEADS)]):
        pass


def _gdn_decode(proj, buf, s, cw, cb, hp, ng, bb):
    b = proj.shape[0]
    wq = GDN_KW
    kern = functools.partial(_gdn_decode_kernel, bb=bb)
    col = lambda blk: (lambda i: (i, blk))
    const = lambda i: (0, 0)
    return pl.pallas_call(
        kern,
        grid=(b // bb,),
        in_specs=[pl.BlockSpec((bb, wq), col(0)),
                  pl.BlockSpec((bb, wq), col(1)),
                  pl.BlockSpec((bb, wq), col(2)),
                  pl.BlockSpec((bb, wq), col(3)),
                  pl.BlockSpec((bb, LANES), col(E_OFF_GSMALL // LANES)),
                  pl.BlockSpec((CONV_W - 1, bb, GDN_CONV_DIM), lambda i: (0, i, 0)),
                  pl.BlockSpec((bb, GDN_HEADS, GDN_DK, GDN_DV), lambda i: (i, 0, 0, 0)),
                  pl.BlockSpec((CONV_W, GDN_CONV_DIM), const),
                  pl.BlockSpec((1, GDN_CONV_DIM), const),
                  pl.BlockSpec((8, LANES), const),
                  pl.BlockSpec((1, GDN_DV), const)],
        out_specs=[pl.BlockSpec((bb, GDN_VW), col(0)),
                   pl.BlockSpec((CONV_W - 1, bb, GDN_CONV_DIM), lambda i: (0, i, 0)),
                   pl.BlockSpec((bb, GDN_HEADS, GDN_DK, GDN_DV), lambda i: (i, 0, 0, 0))],
        out_shape=[jax.ShapeDtypeStruct((b, GDN_VW), MM_DTYPE),
                   jax.ShapeDtypeStruct((CONV_W - 1, b, GDN_CONV_DIM), f32),
                   jax.ShapeDtypeStruct((b, GDN_HEADS, GDN_DK, GDN_DV), f32)],
        compiler_params=pltpu.CompilerParams(dimension_semantics=("parallel",), vmem_limit_bytes=VMEM_LIMIT),
        name="gdn_decode",
    )(proj, proj, proj, proj, proj, buf, s, cw, cb, hp, ng)


def _gla_decode_kernel(q_ref, k_ref, v_ref, gate_ref, small_ref, s_ref, wlr_ref, blr_ref, ng_ref,
                       o_ref, s_out_ref, *, bb):
    z = _mm(small_ref[...], wlr_ref[...]) + blr_ref[...]
    la = -_softplus(-z) * (1.0 / GLA_TAU)
    dec = jnp.exp(la)
    qe = q_ref[...] * (GLA_DK ** -0.5) * dec
    kx = k_ref[...]
    ke = kx * jnp.exp(-la)
    vx = v_ref[...]
    ng = ng_ref[...]
    gate = gate_ref[...]
    for h in range(GLA_HEADS):
        k0 = h * GLA_DK
        v0 = h * GLA_DV
        p = jnp.sum(qe[:, k0:k0 + GLA_DK] * ke[:, k0:k0 + GLA_DK], axis=-1, keepdims=True)
        outs = []
        for b in range(bb):
            s_old = s_ref[b, h]
            vrow = vx[b:b + 1, v0:v0 + GLA_DV]
            outs.append(_mm(_pad_rows(qe[b:b + 1, k0:k0 + GLA_DK], 8), s_old)[0:1] + p[b:b + 1] * vrow)
            s_out_ref[b, h] = (_bcast_cols(dec[b:b + 1, k0:k0 + GLA_DK], GLA_DV) * s_old
                               + _mm_tn(_pad_rows(kx[b:b + 1, k0:k0 + GLA_DK], 16), _pad_rows(vrow, 16)))
        o = jnp.concatenate(outs, axis=0)
        o_ref[:, v0:v0 + GLA_DV] = (_rms(o, GLA_DV) * ng * _silu(gate[:, v0:v0 + GLA_DV])).astype(o_ref.dtype)


def _gla_decode(proj, s, wlr, blr, ng, bb):
    b = proj.shape[0]
    kern = functools.partial(_gla_decode_kernel, bb=bb)
    col = lambda blk: (lambda i: (i, blk))
    const = lambda i: (0, 0)
    return pl.pallas_call(
        kern,
        grid=(b // bb,),
        in_specs=[pl.BlockSpec((bb, GLA_KW), col(0)),
                  pl.BlockSpec((bb, GLA_KW), col(L_OFF_K // GLA_KW)),
                  pl.BlockSpec((bb, GLA_VW), col(L_OFF_V // GLA_VW)),
                  pl.BlockSpec((bb, GLA_VW), col(L_OFF_GATE // GLA_VW)),
                  pl.BlockSpec((bb, LANES), col(L_OFF_SMALL // LANES)),
                  pl.BlockSpec((bb, GLA_HEADS, GLA_DK, GLA_DV), lambda i: (i, 0, 0, 0)),
                  pl.BlockSpec((LANES, GLA_KW), const),
                  pl.BlockSpec((1, GLA_KW), const),
                  pl.BlockSpec((1, GLA_DV), const)],
        out_specs=[pl.BlockSpec((bb, GLA_VW), col(0)),
                   pl.BlockSpec((bb, GLA_HEADS, GLA_DK, GLA_DV), lambda i: (i, 0, 0, 0))],
        out_shape=[jax.ShapeDtypeStruct((b, GLA_VW), MM_DTYPE),
                   jax.ShapeDtypeStruct((b, GLA_HEADS, GLA_DK, GLA_DV), f32)],
        compiler_params=pltpu.CompilerParams(dimension_semantics=("parallel",), vmem_limit_bytes=VMEM_LIMIT),
        name="gla_decode",
    )(proj, proj, proj, proj, proj, s, wlr, blr, ng)


def _ssd_decode_kernel(x_ref, z_ref, bc_ref, dt_ref, buf_ref, h_ref, cw_ref, cb_ref, hp_ref, dskip_ref, ng_ref,
                       expand_ref, y_ref, new_buf_ref, h_out_ref, *, bb):
    xin = jnp.concatenate([x_ref[...], bc_ref[...]], axis=-1)
    xbc = _conv_step(buf_ref, xin, cw_ref, cb_ref, new_buf_ref)
    xs = xbc[:, :SSD_DI]
    dt = _softplus(_head_lanes(dt_ref[...]) + hp_ref[0:1, :])
    el = jnp.exp(dt * hp_ref[1:2, :])
    expand = expand_ref[...]
    dt_x = _sel_rows(dt, expand)
    el_x = _sel_rows(el, expand)
    xdt = xs * dt_x
    z = z_ref[...]
    for g in range(SSD_G):
        gs = slice(g * SSD_GW, (g + 1) * SSD_GW)
        bm = xbc[:, SSD_DI + g * SSD_N:SSD_DI + (g + 1) * SSD_N]
        cm = xbc[:, SSD_DI + SSD_BC + g * SSD_N:SSD_DI + SSD_BC + (g + 1) * SSD_N]
        cbs = jnp.sum(cm * bm, axis=-1, keepdims=True)
        outs = []
        for b in range(bb):
            hg = h_ref[b, gs, :]
            y_inter = _mm_nt(_pad_rows(cm[b:b + 1], 8), hg)[0:1]
            outs.append(el_x[b:b + 1, gs] * y_inter)
            h_out_ref[b, gs, :] = (_bcast_cols(el_x[b:b + 1, gs], SSD_N) * hg
                                   + _mm_tn(_pad_rows(xdt[b:b + 1, gs], 16), _pad_rows(bm[b:b + 1], 16)))
        y = jnp.concatenate(outs, axis=0) + cbs * xdt[:, gs] + dskip_ref[:, gs] * xs[:, gs]
        yg = y * _silu(z[:, gs])
        y_ref[:, gs] = (_rms(yg, SSD_GW) * ng_ref[:, gs]).astype(y_ref.dtype)


def _ssd_decode(proj, buf, hst, cw, cb, hp, dskip, ng, expand, bb):
    b = proj.shape[0]
    kern = functools.partial(_ssd_decode_kernel, bb=bb)
    col = lambda blk: (lambda i: (i, blk))
    const = lambda i: (0, 0)
    return pl.pallas_call(
        kern,
        grid=(b // bb,),
        in_specs=[pl.BlockSpec((bb, SSD_DI), col(O_OFF_X // SSD_DI)),
                  pl.BlockSpec((bb, SSD_DI), col(0)),
                  pl.BlockSpec((bb, 2 * SSD_BC), col(O_OFF_BC // (2 * SSD_BC))),
                  pl.BlockSpec((bb, LANES), col(O_OFF_DT // LANES)),
                  pl.BlockSpec((CONV_W - 1, bb, SSD_CONV_DIM), lambda i: (0, i, 0)),
                  pl.BlockSpec((bb, SSD_HEADS * SSD_P, SSD_N), lambda i: (i, 0, 0)),
                  pl.BlockSpec((CONV_W, SSD_CONV_DIM), const),
                  pl.BlockSpec((1, SSD_CONV_DIM), const),
                  pl.BlockSpec((8, LANES), const),
                  pl.BlockSpec((1, SSD_DI), const),
                  pl.BlockSpec((1, SSD_DI), const),
                  pl.BlockSpec((LANES, SSD_DI), const)],
        out_specs=[pl.BlockSpec((bb, SSD_DI), col(0)),
                   pl.BlockSpec((CONV_W - 1, bb, SSD_CONV_DIM), lambda i: (0, i, 0)),
                   pl.BlockSpec((bb, SSD_HEADS * SSD_P, SSD_N), lambda i: (i, 0, 0))],
        out_shape=[jax.ShapeDtypeStruct((b, SSD_DI), MM_DTYPE),
                   jax.ShapeDtypeStruct((CONV_W - 1, b, SSD_CONV_DIM), f32),
                   jax.ShapeDtypeStruct((b, SSD_HEADS * SSD_P, SSD_N), f32)],
        compiler_params=pltpu.CompilerParams(dimension_semantics=("parallel",), vmem_limit_bytes=VMEM_LIMIT),
        name="ssd_decode",
    )(proj, proj, proj, proj, buf, hst, cw, cb, hp, dskip, ng, expand)


def _pack_gla_w_in(w):
    gla = w[:, E_OFF_GLA:]
    return jnp.concatenate([gla, jnp.zeros((w.shape[0], L_N - gla.shape[1]), w.dtype)], axis=1)


def _lane_row(vals, offset):
    return jnp.zeros((LANES,), f32).at[offset:offset + vals.shape[0]].set(vals.astype(f32))


def kernel(x_prompt, x_sample, state_gdn_conv, state_gdn, state_gla, state_ssd_conv, state_ssd, e_pre_g, e_post_g, e_w_in, e_conv_w, e_conv_b, gdn_a_log, gdn_dt_bias, gdn_norm_g, gla_w_lr, gla_b_lr, gla_norm_g, e_w_out, o_pre_g, o_post_g, o_w_in, ssd_conv_w, ssd_conv_b, ssd_dt_bias, ssd_a_log, ssd_d, ssd_norm_g, o_w_out):
    bp, lp, d = x_prompt.shape
    bs = x_sample.shape[0]
    tp = bp * lp
    tb = min(lp, PROMPT_BLOCK)
    bb = min(bs, SAMPLE_BLOCK)

    w_in0 = e_w_in[0].astype(MM_DTYPE)
    w_gla = _pack_gla_w_in(w_in0)
    w_out0 = e_w_out[0].astype(MM_DTYPE)
    pre0 = e_pre_g[0].reshape(1, d)
    post0 = e_post_g[0].reshape(1, d)
    cw0 = e_conv_w[0]
    cb0 = e_conv_b[0].reshape(1, GDN_CONV_DIM)
    hp0 = jnp.zeros((8, LANES), f32)
    hp0 = hp0.at[0].set(_lane_row(-jnp.exp(gdn_a_log[0].astype(f32)), SMALL_A))
    hp0 = hp0.at[1].set(_lane_row(gdn_dt_bias[0], SMALL_A))
    ng_gdn = gdn_norm_g[0].reshape(1, GDN_DV)
    wlr = jnp.zeros((LANES, GLA_KW), f32).at[SMALL_LR:SMALL_LR + GLA_RANK].set(gla_w_lr[0]).astype(MM_DTYPE)
    blr = gla_b_lr[0].reshape(1, GLA_KW)
    ng_gla = gla_norm_g[0].reshape(1, GLA_DV)

    hp_flat = x_prompt.reshape(tp, d)
    hs_flat = x_sample.reshape(bs, d)

    o_p, gdn_conv_p, gdn_p, gla_p = _even_prompt(x_prompt, pre0, w_in0, w_gla, cw0, cb0, hp0, ng_gdn, wlr, blr, ng_gla,
                                                 tb)
    o_p = o_p.reshape(tp, GDN_VW + GLA_VW)
    hp_flat = _out_proj(o_p, o_p, w_out0, post0, hp_flat)

    proj_gdn = _norm_proj(hs_flat, pre0, w_in0, E_GDN_N // 3, E_GDN_N)
    proj_gla = _norm_proj(hs_flat, pre0, w_gla, L_N // 5, L_N)
    o1_s, gdn_conv_s, gdn_s = _gdn_decode(proj_gdn, jnp.swapaxes(state_gdn_conv[0], 0, 1), state_gdn[0],
                                          cw0, cb0, hp0, ng_gdn, bb)
    o2_s, gla_s = _gla_decode(proj_gla, state_gla[0], wlr, blr, ng_gla, bb)
    hs_flat = _out_proj(o1_s, o2_s, w_out0, post0, hs_flat)

    w_in1 = o_w_in[0].astype(MM_DTYPE)
    w_out1 = o_w_out[0].astype(MM_DTYPE)
    pre1 = o_pre_g[0].reshape(1, d)
    post1 = o_post_g[0].reshape(1, d)
    cw1 = ssd_conv_w[0]
    cb1 = ssd_conv_b[0].reshape(1, SSD_CONV_DIM)
    hp1 = jnp.zeros((8, LANES), f32)
    hp1 = hp1.at[0].set(_lane_row(ssd_dt_bias[0], 0))
    hp1 = hp1.at[1].set(_lane_row(-jnp.exp(ssd_a_log[0].astype(f32)), 0))
    dskip = jnp.repeat(ssd_d[0].astype(f32), SSD_P).reshape(1, SSD_DI)
    ng_ssd = ssd_norm_g[0].reshape(1, SSD_DI)
    expand = (lax.broadcasted_iota(jnp.int32, (LANES, SSD_DI), 0)
              == lax.broadcasted_iota(jnp.int32, (LANES, SSD_DI), 1) // SSD_P).astype(MM_DTYPE)

    y_p, ssd_conv_p, ssd_p = _ssd_prompt(hp_flat.reshape(bp, lp, d), pre1, w_in1, cw1, cb1, hp1, dskip, ng_ssd,
                                         expand, tb)
    y_p = y_p.reshape(tp, SSD_DI)
    hp_flat = _out_proj(y_p, y_p, w_out1, post1, hp_flat)

    proj_s = _norm_proj(hs_flat, pre1, w_in1, O_PROJ_N // 6, O_PROJ_N)
    y_s, ssd_conv_s, ssd_s = _ssd_decode(proj_s, jnp.swapaxes(state_ssd_conv[0], 0, 1),
                                         state_ssd[0].reshape(bs, SSD_HEADS * SSD_P, SSD_N),
                                         cw1, cb1, hp1, dskip, ng_ssd, expand, bb)
    hs_flat = _out_proj(y_s, y_s, w_out1, post1, hs_flat)

    return (hp_flat.reshape(bp, lp, d), hs_flat.reshape(bs, 1, d),
            gdn_conv_p[None], gdn_p[None], gla_p[None],
            ssd_conv_p[None], ssd_p.reshape(bp, SSD_HEADS, SSD_P, SSD_N)[None],
            jnp.swapaxes(gdn_conv_s, 0, 1)[None], gdn_s[None], gla_s[None],
            jnp.swapaxes(ssd_conv_s, 0, 1)[None],
            ssd_s.reshape(bs, SSD_HEADS, SSD_P, SSD_N)[None])
```

```python
import functools

import jax
import jax.numpy as jnp
from jax import lax
from jax.experimental import pallas as pl
from jax.experimental.pallas import tpu as pltpu

f32 = jnp.float32
bf16 = jnp.bfloat16
MM_DTYPE = jnp.bfloat16

D_MODEL = 1024
CONV_W = 4
EPS = 1e-6
LANES = 128

GDN_HEADS = 8
GDN_DK = 128
GDN_DV = 128
GDN_KW = GDN_HEADS * GDN_DK
GDN_VW = GDN_HEADS * GDN_DV
GDN_CONV_DIM = 2 * GDN_KW + GDN_VW
GLA_HEADS = 4
GLA_DK = 128
GLA_DV = 256
GLA_KW = GLA_HEADS * GLA_DK
GLA_VW = GLA_HEADS * GLA_DV
GLA_RANK = 16
GLA_TAU = 16.0
SSD_DI = 2 * D_MODEL
SSD_P = 64
SSD_HEADS = SSD_DI // SSD_P
SSD_N = 128
SSD_G = 4
SSD_K = SSD_HEADS // SSD_G
SSD_GW = SSD_DI // SSD_G
SSD_BC = SSD_G * SSD_N
SSD_CONV_DIM = SSD_DI + 2 * SSD_BC

E_OFF_GATE = 3 * GDN_KW
E_OFF_GSMALL = 4 * GDN_KW
E_GDN_N = E_OFF_GSMALL + LANES
E_OFF_GLA = E_OFF_GSMALL + 2 * GDN_HEADS
SMALL_BETA = 0
SMALL_A = GDN_HEADS
L_OFF_K = GLA_KW
L_OFF_V = 2 * GLA_KW
L_OFF_GATE = L_OFF_V + GLA_VW
L_OFF_SMALL = L_OFF_GATE + GLA_VW
L_N = L_OFF_SMALL + LANES
SMALL_LR = 0
O_OFF_X = SSD_DI
O_OFF_BC = 2 * SSD_DI
O_OFF_DT = 2 * SSD_DI + 2 * SSD_BC
O_PROJ_N = 6 * 1024

CHUNK = 64
GDN_PREP_CHUNKS = 2
assert SSD_P == CHUNK and 2 * SSD_P == LANES
PROMPT_BLOCK = 512
SAMPLE_BLOCK = 8
SAMPLE_BLOCK_EVEN = 16
ROW_TILE = 1024
V7X_VMEM_BYTES = 64 * 1024 * 1024
VMEM_LIMIT = V7X_VMEM_BYTES - 8 * 1024 * 1024


def _mm(a, b):
    return jnp.dot(a.astype(MM_DTYPE), b.astype(MM_DTYPE), preferred_element_type=f32)


def _mm_nt(a, b):
    return lax.dot_general(a.astype(MM_DTYPE), b.astype(MM_DTYPE), (((1,), (1,)), ((), ())),
                           preferred_element_type=f32)


def _mm_tn(a, b):
    return lax.dot_general(a.astype(MM_DTYPE), b.astype(MM_DTYPE), (((0,), (0,)), ((), ())),
                           preferred_element_type=f32)


def _split3(x):
    hi = x.astype(bf16)
    r = x - hi.astype(f32)
    mid = r.astype(bf16)
    lo = (r - mid.astype(f32)).astype(bf16)
    return hi, mid, lo


def _sel_mm(sel, x):
    hi, mid, lo = _split3(x)
    s = sel.astype(bf16)
    return (jnp.dot(s, hi, preferred_element_type=f32) + jnp.dot(s, mid, preferred_element_type=f32)
            + jnp.dot(s, lo, preferred_element_type=f32))


def _sel_rows(x, sel):
    hi, mid, lo = _split3(x)
    s = sel.astype(bf16)
    return (jnp.dot(hi, s, preferred_element_type=f32) + jnp.dot(mid, s, preferred_element_type=f32)
            + jnp.dot(lo, s, preferred_element_type=f32))


def _sigmoid(x):
    return 1.0 / (1.0 + jnp.exp(-x))


def _silu(x):
    return x * _sigmoid(x)


def _softplus(x):
    return jnp.maximum(x, 0.0) + jnp.log1p(jnp.exp(-jnp.abs(x)))


def _tril(n, strict=False):
    r = lax.broadcasted_iota(jnp.int32, (n, n), 0)
    c = lax.broadcasted_iota(jnp.int32, (n, n), 1)
    return (r > c) if strict else (r >= c)


def _masked_decay(col, row, incl):
    return jnp.exp(jnp.where(incl, col - row, -jnp.inf))


def _interleave(main, side, ratio):
    gens = [main, side]
    steps = {id(main): ratio, id(side): 1}
    while gens:
        for g in list(gens):
            for _ in range(steps[id(g)]):
                try:
                    next(g)
                except StopIteration:
                    gens.remove(g)
                    break
                yield


def _round_robin(gens):
    gens = list(gens)
    while gens:
        for g in list(gens):
            try:
                next(g)
            except StopIteration:
                gens.remove(g)
        yield


def _rms(x, width):
    return x * lax.rsqrt(jnp.sum(x * x, axis=-1, keepdims=True) * (1.0 / width) + EPS)


def _norm_proj_kernel(x_ref, g_ref, w_ref, o_ref, u_scr):
    @pl.when(pl.program_id(1) == 0)
    def _():
        u_scr[...] = (_rms(x_ref[...], D_MODEL) * g_ref[...]).astype(u_scr.dtype)

    o_ref[...] = jnp.dot(u_scr[...], w_ref[...], preferred_element_type=f32)


def _norm_proj(x, g, w, tn, n):
    t, d = x.shape
    assert n % tn == 0 and n - tn < w.shape[1]
    tm = min(t, ROW_TILE)
    return pl.pallas_call(
        _norm_proj_kernel,
        grid=(t // tm, n // tn),
        in_specs=[pl.BlockSpec((tm, d), lambda i, j: (i, 0)),
                  pl.BlockSpec((1, d), lambda i, j: (0, 0)),
                  pl.BlockSpec((d, tn), lambda i, j: (0, j))],
        out_specs=pl.BlockSpec((tm, tn), lambda i, j: (i, j)),
        out_shape=jax.ShapeDtypeStruct((t, n), f32),
        scratch_shapes=[pltpu.VMEM((tm, d), MM_DTYPE)],
        compiler_params=pltpu.CompilerParams(dimension_semantics=("parallel", "arbitrary"),
                                             vmem_limit_bytes=VMEM_LIMIT),
        name="norm_proj",
    )(x, g, w)


def _out_proj_kernel(oa_ref, ob_ref, wa_ref, wb_ref, g_ref, h_ref, y_ref):
    p = (jnp.dot(oa_ref[...], wa_ref[...], preferred_element_type=f32)
         + jnp.dot(ob_ref[...], wb_ref[...], preferred_element_type=f32))
    y_ref[...] = h_ref[...] + _rms(p, D_MODEL) * g_ref[...]


def _out_proj(oa, ob, w, g, h):
    t = oa.shape[0]
    e = w.shape[0] // 2
    assert oa.shape[1] == ob.shape[1] and oa.shape[1] in (e, 2 * e)
    col_b = 1 if ob.shape[1] == 2 * e else 0
    d = w.shape[1]
    tm = min(t, ROW_TILE)
    return pl.pallas_call(
        _out_proj_kernel,
        grid=(t // tm,),
        in_specs=[pl.BlockSpec((tm, e), lambda i: (i, 0)),
                  pl.BlockSpec((tm, e), lambda i: (i, col_b)),
                  pl.BlockSpec((e, d), lambda i: (0, 0)),
                  pl.BlockSpec((e, d), lambda i: (1, 0)),
                  pl.BlockSpec((1, d), lambda i: (0, 0)),
                  pl.BlockSpec((tm, d), lambda i: (i, 0))],
        out_specs=pl.BlockSpec((tm, d), lambda i: (i, 0)),
        out_shape=jax.ShapeDtypeStruct((t, d), f32),
        compiler_params=pltpu.CompilerParams(dimension_semantics=("parallel",),
                                             vmem_limit_bytes=VMEM_LIMIT),
        name="out_proj",
    )(oa, ob, w, w, g, h)


def _normed(x_ref, g_ref):
    return (_rms(x_ref[...], D_MODEL) * g_ref[...]).astype(MM_DTYPE)


def _proj(u, w_ref, c0, c1):
    return jnp.dot(u, w_ref[:, c0:c1], preferred_element_type=f32)


def _conv_block(xp_scr, cw_ref, cb_ref, tb, first):
    @pl.when(first)
    def _():
        xp_scr[0:8, :] = jnp.zeros((8, xp_scr.shape[1]), f32)

    @pl.when(jnp.logical_not(first))
    def _():
        xp_scr[5:8, :] = xp_scr[tb + 5:tb + 8, :]

    def conv_cols(c0, c1, r0=0, nr=tb):
        acc = cb_ref[:, c0:c1] + xp_scr[8 + r0:8 + r0 + nr, c0:c1] * cw_ref[3:4, c0:c1]
        for s in range(CONV_W - 1):
            acc = acc + xp_scr[5 + s + r0:5 + s + r0 + nr, c0:c1] * cw_ref[s:s + 1, c0:c1]
        return _silu(acc)

    return conv_cols


def _gdn_stages(u, first, last, wm_ref, ws_ref, cw_ref, cb_ref, hp_ref, ng_ref,
                o_ref, conv_out_ref, s_out_ref,
                xp_scr, qkv_scr, gate_scr, beta_scr, lg_scr, s_scr, gl_scr, up_scr, w_scr, qg_scr, kd_scr,
                p_scr, *, tb):
    conv_cols = _conv_block(xp_scr, cw_ref, cb_ref, tb, first)

    @pl.when(first)
    def _():
        s_scr[...] = jnp.zeros(s_scr.shape, f32)

    def proj_stages():
        for c0 in range(0, GDN_CONV_DIM, GDN_KW):
            xp_scr[8:8 + tb, c0:c0 + GDN_KW] = _proj(u, wm_ref, c0, c0 + GDN_KW)
            yield
        small = _proj(u, ws_ref, 0, LANES)
        beta_scr[...] = _sigmoid(small)
        lg_scr[...] = hp_ref[0:1, :] * _softplus(small + hp_ref[1:2, :])
        gate_scr[...] = _proj(u, wm_ref, E_OFF_GATE, E_OFF_GATE + GDN_VW)
        yield

    span = GDN_PREP_CHUNKS * CHUNK

    def conv_norm(it):
        r0 = it * span
        rr = slice(r0, r0 + span)
        for h in range(GDN_HEADS):
            c0 = h * GDN_DK
            q = conv_cols(c0, c0 + GDN_DK, r0, span)
            q = q * lax.rsqrt(jnp.sum(q * q, axis=-1, keepdims=True) + EPS) * (GDN_DK ** -0.5)
            qkv_scr[rr, c0:c0 + GDN_DK] = q
            k = conv_cols(GDN_KW + c0, GDN_KW + c0 + GDN_DK, r0, span)
            k = k * lax.rsqrt(jnp.sum(k * k, axis=-1, keepdims=True) + EPS)
            qkv_scr[rr, GDN_KW + c0:GDN_KW + c0 + GDN_DK] = k
            qkv_scr[rr, 2 * GDN_KW + c0:2 * GDN_KW + c0 + GDN_DV] = conv_cols(
                2 * GDN_KW + c0, 2 * GDN_KW + c0 + GDN_DV, r0, span)
            yield

    incl = _tril(CHUNK)
    strict = _tril(CHUNK, strict=True)
    eye = jnp.logical_and(incl, jnp.logical_not(strict)).astype(f32)
    sr = lax.broadcasted_iota(jnp.int32, (span, span), 0)
    sc = lax.broadcasted_iota(jnp.int32, (span, span), 1)
    lmat = jnp.logical_and(sr >= sc, sr // CHUNK == sc // CHUNK).astype(f32)
    ng = ng_ref[...]
    heads = range(GDN_HEADS)
    probs = [(ci, h) for ci in range(GDN_PREP_CHUNKS) for h in heads]
    n = range(len(probs))

    def prep(c):
        r0 = c * span
        gam_all = _sel_mm(lmat, lg_scr[pl.ds(r0, span), :])
        gam_t = gam_all.T
        beta_all = beta_scr[pl.ds(r0, span), :]
        rows, gcol, grow, glast, bcol = [], [], [], [], []
        for ci, h in probs:
            lo, hi = ci * CHUNK, (ci + 1) * CHUNK
            rows.append(pl.ds(r0 + lo, CHUNK))
            gcol.append(gam_all[lo:hi, SMALL_A + h:SMALL_A + h + 1])
            grow.append(gam_t[SMALL_A + h:SMALL_A + h + 1, lo:hi])
            glast.append(gam_all[hi - 1:hi, SMALL_A + h:SMALL_A + h + 1])
            bcol.append(beta_all[lo:hi, SMALL_BETA + h:SMALL_BETA + h + 1])
        for ci in range(GDN_PREP_CHUNKS):
            gl_scr[pl.ds(c * GDN_PREP_CHUNKS + ci, 1), :] = jnp.exp(gam_all[(ci + 1) * CHUNK - 1:(ci + 1) * CHUNK, :])
        q = [qkv_scr[rows[j], h * GDN_DK:(h + 1) * GDN_DK] for j, (_, h) in enumerate(probs)]
        k = [qkv_scr[rows[j], GDN_KW + h * GDN_DK:GDN_KW + (h + 1) * GDN_DK] for j, (_, h) in enumerate(probs)]
        v = [qkv_scr[rows[j], 2 * GDN_KW + h * GDN_DV:2 * GDN_KW + (h + 1) * GDN_DV] for j, (_, h) in enumerate(probs)]
        kb = [k[j].astype(MM_DTYPE) for j in n]
        kk = [_mm_nt(kb[j], kb[j]) for j in n]
        yield
        qk = [_mm_nt(q[j], kb[j]) for j in n]
        yield
        decay = [_masked_decay(gcol[j], grow[j], incl) for j in n]
        eg = [jnp.exp(gcol[j]) for j in n]
        for j, (_, h) in enumerate(probs):
            c0 = h * GDN_DK
            qg_scr[rows[j], c0:c0 + GDN_DK] = (eg[j] * q[j]).astype(MM_DTYPE)
            kd_scr[rows[j], c0:c0 + GDN_DK] = (jnp.exp(glast[j] - gcol[j]) * k[j]).astype(MM_DTYPE)
            p_scr[h, rows[j], :] = (qk[j] * decay[j]).astype(MM_DTYPE)
        nmat = [jnp.where(strict, bcol[j] * (kk[j] * decay[j]), 0.0) for j in n]
        tinv = [eye - nmat[j] for j in n]
        npow = nmat
        for _ in range(5):
            npow = [_mm(npow[j], npow[j]) for j in n]
            yield
            tinv = [tinv[j] + _mm(tinv[j], npow[j]) for j in n]
            yield
        x = [_mm(tinv[j], jnp.concatenate([bcol[j] * v[j], (bcol[j] * eg[j]) * k[j]], axis=-1)) for j in n]
        for j, (_, h) in enumerate(probs):
            c0 = h * GDN_DK
            up_scr[rows[j], c0:c0 + GDN_DV] = x[j][:, :GDN_DV]
            w_scr[rows[j], c0:c0 + GDN_DK] = x[j][:, GDN_DV:].astype(MM_DTYPE)

    state = [s_scr[h] for h in heads]

    def scan(c):
        rows = pl.ds(c * CHUNK, CHUNK)
        gl = gl_scr[c:c + 1, :]
        r = [_mm(jnp.concatenate([w_scr[rows, h * GDN_DK:(h + 1) * GDN_DK],
                                  qg_scr[rows, h * GDN_DK:(h + 1) * GDN_DK]], axis=0), state[h]) for h in heads]
        yield
        us = [(up_scr[rows, h * GDN_DV:(h + 1) * GDN_DV] - r[h][:CHUNK]).astype(MM_DTYPE) for h in heads]
        o = [r[h][CHUNK:] + _mm(p_scr[h, rows, :], us[h]) for h in heads]
        yield
        upd = [_mm_tn(kd_scr[rows, h * GDN_DK:(h + 1) * GDN_DK], us[h]) for h in heads]
        yield
        for h in heads:
            c0 = h * GDN_DV
            state[h] = gl[:, SMALL_A + h:SMALL_A + h + 1] * state[h] + upd[h]
            gate = gate_scr[rows, c0:c0 + GDN_DV]
            o_ref[rows, c0:c0 + GDN_DV] = (_rms(o[h], GDN_DV) * ng * _silu(gate)).astype(o_ref.dtype)
        yield

    def scans(chunks):
        for c in chunks:
            yield from scan(c)

    def side_work(it):
        if it + 1 < n_prep:
            yield from conv_norm(it + 1)
        if it >= 1:
            yield from scans(range((it - 1) * GDN_PREP_CHUNKS, it * GDN_PREP_CHUNKS))

    n_prep = tb // span

    def stages():
        yield from proj_stages()
        yield from conv_norm(0)
        for it in range(n_prep):
            yield from _interleave(prep(it), side_work(it), 1)
        yield from scans(range((n_prep - 1) * GDN_PREP_CHUNKS, n_prep * GDN_PREP_CHUNKS))

    def finish():
        for h in heads:
            s_scr[h] = state[h]

        @pl.when(last)
        def _():
            conv_out_ref[...] = xp_scr[tb + 5:tb + 8, :]
            s_out_ref[...] = s_scr[...]

    return stages(), finish


def _gla_stages(u, first, last, wq_ref, wk_ref, wv_ref, wg_ref, ws_ref, wlr_ref, blr_ref, ng_ref,
                o_ref, s_out_ref, q_ref, k_ref, v_ref, gate_ref, la_scr, st_scr, *, tb):
    @pl.when(first)
    def _():
        st_scr[...] = jnp.zeros(st_scr.shape, f32)

    def proj_stages():
        q_ref[...] = _proj(u, wq_ref, 0, GLA_KW)
        k_ref[...] = _proj(u, wk_ref, 0, GLA_KW)
        small = _proj(u, ws_ref, 0, LANES)
        z = _mm(small, wlr_ref[...]) + blr_ref[...]
        la_scr[...] = -_softplus(-z) * (1.0 / GLA_TAU)
        yield
        v_ref[...] = _proj(u, wv_ref, 0, GLA_VW).astype(v_ref.dtype)
        yield
        gate_ref[...] = _proj(u, wg_ref, 0, GLA_VW)
        yield

    incl = _tril(CHUNK)
    lmat = incl.astype(f32)
    ng = ng_ref[...]

    heads = range(GLA_HEADS)
    n_chunks = tb // CHUNK
    staged = [None] * n_chunks
    state = [st_scr[h] for h in heads]

    def prep(c):
        rows = pl.ds(c * CHUNK, CHUNK)
        bcum = [_sel_mm(lmat, la_scr[rows, h * GLA_DK:(h + 1) * GLA_DK]) for h in heads]
        yield
        k = [k_ref[rows, h * GLA_DK:(h + 1) * GLA_DK] for h in heads]
        v = [v_ref[rows, h * GLA_DV:(h + 1) * GLA_DV].astype(MM_DTYPE) for h in heads]
        qe = [(q_ref[rows, h * GLA_DK:(h + 1) * GLA_DK] * (GLA_DK ** -0.5) * jnp.exp(bcum[h])).astype(MM_DTYPE)
              for h in heads]
        ke = [k[h] * jnp.exp(-bcum[h]) for h in heads]
        blast = [bcum[h][CHUNK - 1:CHUNK, :] for h in heads]
        kdec = [(k[h] * jnp.exp(blast[h] - bcum[h])).astype(MM_DTYPE) for h in heads]
        p = [jnp.where(incl, _mm_nt(qe[h], ke[h]), 0.0) for h in heads]
        yield
        o_intra = [_mm(p[h], v[h]) for h in heads]
        staged[c] = (qe, v, kdec, blast, o_intra)
        yield

    def scan(c):
        rows = pl.ds(c * CHUNK, CHUNK)
        qe, v, kdec, blast, o_intra = staged[c]
        o = [o_intra[h] + _mm_nt(qe[h], state[h]) for h in heads]
        upd = [_mm_tn(v[h], kdec[h]) for h in heads]
        yield
        for h in heads:
            v0 = h * GLA_DV
            state[h] = jnp.exp(blast[h]) * state[h] + upd[h]
            gate = gate_ref[rows, v0:v0 + GLA_DV]
            o_ref[rows, GDN_VW + v0:GDN_VW + v0 + GLA_DV] = (
                _rms(o[h], GLA_DV) * ng * _silu(gate)).astype(o_ref.dtype)
        yield

    def stages():
        yield from proj_stages()
        yield from prep(0)
        for c in range(1, n_chunks):
            yield from _interleave(prep(c), scan(c - 1), 1)
        yield from scan(n_chunks - 1)

    def finish():
        for h in heads:
            st_scr[h] = state[h]

        @pl.when(last)
        def _():
            for h in heads:
                s_out_ref[h] = st_scr[h].T

    return stages(), finish


N_GDN_SCRATCH = 12
EVEN_STAGE_RATIO = 2


def _even_prompt_kernel(x_ref, pg_ref, wm_ref, ws_ref, wq_ref, wk_ref, wv_ref, wg_ref, wsl_ref, cw_ref, cb_ref, hp_ref,
                        ngd_ref, wlr_ref, blr_ref, ngl_ref,
                        o_ref, conv_out_ref, sgdn_out_ref, sgla_out_ref, *scratch, tb):
    i = pl.program_id(1)
    first = i == 0
    last = i == pl.num_programs(1) - 1
    u = _normed(x_ref, pg_ref)
    gdn, gdn_finish = _gdn_stages(u, first, last, wm_ref, ws_ref, cw_ref, cb_ref, hp_ref, ngd_ref,
                                  o_ref, conv_out_ref, sgdn_out_ref, *scratch[:N_GDN_SCRATCH], tb=tb)
    gla, gla_finish = _gla_stages(u, first, last, wq_ref, wk_ref, wv_ref, wg_ref, wsl_ref, wlr_ref, blr_ref, ngl_ref,
                                  o_ref, sgla_out_ref, *scratch[N_GDN_SCRATCH:], tb=tb)
    for _ in _interleave(gdn, gla, EVEN_STAGE_RATIO):
        pass
    gdn_finish()
    gla_finish()


def _even_prompt(x, pre_g, w_in, w_gla, cw, cb, hp, ng_gdn, wlr, blr, ng_gla, tb):
    b, l, d = x.shape
    assert tb % (GDN_PREP_CHUNKS * CHUNK) == 0 and l % tb == 0
    kern = functools.partial(_even_prompt_kernel, tb=tb)
    wcol = lambda blk: (lambda bi, i: (0, blk))
    const = lambda bi, i: (0, 0)
    gdn_scratch = [pltpu.VMEM((tb + 8, GDN_CONV_DIM), f32),
                   pltpu.VMEM((tb, GDN_CONV_DIM), f32),
                   pltpu.VMEM((tb, GDN_VW), f32),
                   pltpu.VMEM((tb, LANES), f32),
                   pltpu.VMEM((tb, LANES), f32),
                   pltpu.VMEM((GDN_HEADS, GDN_DK, GDN_DV), f32),
                   pltpu.VMEM((max(8, tb // CHUNK), LANES), f32),
                   pltpu.VMEM((tb, GDN_VW), f32),
                   pltpu.VMEM((tb, GDN_KW), MM_DTYPE),
                   pltpu.VMEM((tb, GDN_KW), MM_DTYPE),
                   pltpu.VMEM((tb, GDN_KW), MM_DTYPE),
                   pltpu.VMEM((GDN_HEADS, tb, CHUNK), MM_DTYPE)]
    assert len(gdn_scratch) == N_GDN_SCRATCH
    gla_scratch = [pltpu.VMEM((tb, GLA_KW), f32),
                   pltpu.VMEM((tb, GLA_KW), f32),
                   pltpu.VMEM((tb, GLA_VW), MM_DTYPE),
                   pltpu.VMEM((tb, GLA_VW), f32),
                   pltpu.VMEM((tb, GLA_KW), f32),
                   pltpu.VMEM((GLA_HEADS, GLA_DV, GLA_DK), f32)]
    return pl.pallas_call(
        kern,
        grid=(b, l // tb),
        in_specs=[pl.BlockSpec((None, tb, d), lambda bi, i: (bi, i, 0)),
                  pl.BlockSpec((1, d), const),
                  pl.BlockSpec((d, E_OFF_GSMALL), const),
                  pl.BlockSpec((d, LANES), wcol(E_OFF_GSMALL // LANES)),
                  pl.BlockSpec((d, GLA_KW), wcol(0)),
                  pl.BlockSpec((d, GLA_KW), wcol(L_OFF_K // GLA_KW)),
                  pl.BlockSpec((d, GLA_VW), wcol(L_OFF_V // GLA_VW)),
                  pl.BlockSpec((d, GLA_VW), wcol(L_OFF_GATE // GLA_VW)),
                  pl.BlockSpec((d, LANES), wcol(L_OFF_SMALL // LANES)),
                  pl.BlockSpec((CONV_W, GDN_CONV_DIM), const),
                  pl.BlockSpec((1, GDN_CONV_DIM), const),
                  pl.BlockSpec((8, LANES), const),
                  pl.BlockSpec((1, GDN_DV), const),
                  pl.BlockSpec((LANES, GLA_KW), const),
                  pl.BlockSpec((1, GLA_KW), const),
                  pl.BlockSpec((1, GLA_DV), const)],
        out_specs=[pl.BlockSpec((None, tb, GDN_VW + GLA_VW), lambda bi, i: (bi, i, 0)),
                   pl.BlockSpec((None, CONV_W - 1, GDN_CONV_DIM), lambda bi, i: (bi, 0, 0)),
                   pl.BlockSpec((None, GDN_HEADS, GDN_DK, GDN_DV), lambda bi, i: (bi, 0, 0, 0)),
                   pl.BlockSpec((None, GLA_HEADS, GLA_DK, GLA_DV), lambda bi, i: (bi, 0, 0, 0))],
        out_shape=[jax.ShapeDtypeStruct((b, l, GDN_VW + GLA_VW), MM_DTYPE),
                   jax.ShapeDtypeStruct((b, CONV_W - 1, GDN_CONV_DIM), f32),
                   jax.ShapeDtypeStruct((b, GDN_HEADS, GDN_DK, GDN_DV), f32),
                   jax.ShapeDtypeStruct((b, GLA_HEADS, GLA_DK, GLA_DV), f32)],
        scratch_shapes=gdn_scratch + gla_scratch,
        compiler_params=pltpu.CompilerParams(dimension_semantics=("parallel", "arbitrary"),
                                             vmem_limit_bytes=VMEM_LIMIT),
        name="even_prompt",
    )(x, pre_g, w_in, w_in, w_gla, w_gla, w_gla, w_gla, w_gla, cw, cb, hp, ng_gdn, wlr, blr, ng_gla)


def _head_lanes(x):
    return jnp.where(lax.broadcasted_iota(jnp.int32, x.shape, 1) < SSD_HEADS, x, 0.0)


def _ssd_prompt_kernel(x_ref, pg_ref, wx_ref, wz_ref, wbc_ref, wdt_ref, cw_ref, cb_ref, hp_ref, dskip_ref, ng_ref,
                       expand_ref, y_ref, conv_out_ref, h_out_ref,
                       xp_scr, xbc_scr, z_ref, lam_scr, lamx_scr, xdt_scr, h_scr, *, tb):
    i = pl.program_id(1)
    first = i == 0
    conv_cols = _conv_block(xp_scr, cw_ref, cb_ref, tb, first)

    @pl.when(first)
    def _():
        h_scr[...] = jnp.zeros(h_scr.shape, f32)

    u = _normed(x_ref, pg_ref)
    for c0 in range(0, SSD_DI, 1024):
        xp_scr[8:8 + tb, c0:c0 + 1024] = _proj(u, wx_ref, c0, c0 + 1024)
    xp_scr[8:8 + tb, SSD_DI:SSD_CONV_DIM] = _proj(u, wbc_ref, 0, 2 * SSD_BC)
    dt_raw = _head_lanes(_proj(u, wdt_ref, 0, LANES))
    for c0 in range(0, SSD_DI, 1024):
        z_ref[:, c0:c0 + 1024] = _proj(u, wz_ref, c0, c0 + 1024)

    for c0 in range(0, SSD_CONV_DIM, 512):
        xbc_scr[:, c0:c0 + 512] = conv_cols(c0, c0 + 512)
    dt = _softplus(dt_raw + hp_ref[0:1, :])
    tr = lax.broadcasted_iota(jnp.int32, (tb, tb), 0)
    tc = lax.broadcasted_iota(jnp.int32, (tb, tb), 1)
    lam_all = _sel_mm(jnp.logical_and(tr >= tc, tr // CHUNK == tc // CHUNK).astype(f32), dt * hp_ref[1:2, :])
    lam_scr[...] = lam_all
    lo_blk = lax.broadcasted_iota(jnp.int32, (tb, LANES), 1) < SSD_P
    for p in range(SSD_HEADS // 2):
        cols = slice(p * LANES, (p + 1) * LANES)
        k0 = 2 * p
        xdt_scr[:, cols] = xbc_scr[:, cols] * jnp.where(lo_blk, dt[:, k0:k0 + 1], dt[:, k0 + 1:k0 + 2])
    for c0 in range(0, SSD_DI, 512):
        lamx_scr[:, c0:c0 + 512] = _sel_rows(lam_all, expand_ref[:, c0:c0 + 512])

    lane = lax.broadcasted_iota(jnp.int32, (CHUNK, LANES), 1)
    lo = lane < SSD_P
    incl2 = lax.broadcasted_iota(jnp.int32, (CHUNK, LANES), 0) >= lane % CHUNK
    top = lax.broadcasted_iota(jnp.int32, (2 * SSD_P, SSD_N), 0) < SSD_P
    n_pairs = SSD_HEADS // 2
    ppg = SSD_K // 2
    groups = range(SSD_G)

    def chunk_body(c):
        rows = pl.ds(c * CHUNK, CHUNK)
        lam = lam_scr[rows, :]
        lam_t = lam.T
        llast = lam[CHUNK - 1:CHUNK, :]
        bm = [xbc_scr[rows, SSD_DI + g * SSD_N:SSD_DI + (g + 1) * SSD_N].astype(MM_DTYPE) for g in groups]
        cm = [xbc_scr[rows, SSD_DI + SSD_BC + g * SSD_N:SSD_DI + SSD_BC + (g + 1) * SSD_N].astype(MM_DTYPE)
              for g in groups]
        h_old = [h_scr[g * SSD_GW:(g + 1) * SSD_GW, :] for g in groups]
        cb2 = [_mm_nt(cm[g], jnp.concatenate([bm[g], bm[g]], axis=0)) for g in groups]
        y_inter = [_mm_nt(cm[g], h_old[g]) for g in groups]
        xs, xdt, xdec, lcol = [], [], [], []
        for p in range(n_pairs):
            cols = slice(p * LANES, (p + 1) * LANES)
            xs.append(xbc_scr[rows, cols])
            lcol.append(lamx_scr[rows, cols])
            xdt.append(xdt_scr[rows, cols])
            xdec.append((xdt[p] * jnp.exp(lcol[p][CHUNK - 1:CHUNK, :] - lcol[p])).astype(MM_DTYPE))
        upd = [_mm_tn(jnp.concatenate(xdec[g * ppg:(g + 1) * ppg], axis=-1), bm[g]) for g in groups]
        y_intra = []
        for p in range(n_pairs):
            k0 = 2 * p
            lrow = jnp.concatenate([lam_t[k0:k0 + 1, :], lam_t[k0 + 1:k0 + 2, :]], axis=-1)
            m = cb2[p // ppg] * _masked_decay(lcol[p], lrow, incl2)
            blockdiag = jnp.concatenate([jnp.where(lo, xdt[p], 0.0), jnp.where(lo, 0.0, xdt[p])], axis=0)
            y_intra.append(_mm(m, blockdiag))
        for g in groups:
            ys = []
            for pp in range(ppg):
                p = g * ppg + pp
                k0 = 2 * p
                ys.append(y_intra[p] + jnp.exp(lcol[p]) * y_inter[g][:, pp * LANES:(pp + 1) * LANES]
                          + dskip_ref[:, p * LANES:(p + 1) * LANES] * xs[p])
                el = jnp.where(top, jnp.exp(llast[:, k0:k0 + 1]), jnp.exp(llast[:, k0 + 1:k0 + 2]))
                h_scr[p * LANES:(p + 1) * LANES, :] = (el * h_old[g][pp * LANES:(pp + 1) * LANES, :]
                                                        + upd[g][pp * LANES:(pp + 1) * LANES, :])
            yg = jnp.concatenate(ys, axis=-1) * _silu(z_ref[rows, g * SSD_GW:(g + 1) * SSD_GW])
            y_ref[rows, g * SSD_GW:(g + 1) * SSD_GW] = (
                _rms(yg, SSD_GW) * ng_ref[:, g * SSD_GW:(g + 1) * SSD_GW]).astype(y_ref.dtype)

    for c in range(tb // CHUNK):
        chunk_body(c)

    @pl.when(i == pl.num_programs(1) - 1)
    def _():
        conv_out_ref[...] = xp_scr[tb + 5:tb + 8, :]
        h_out_ref[...] = h_scr[...]


def _ssd_prompt(x, pre_g, w_in, cw, cb, hp, dskip, ng, expand, tb):
    b, l, d = x.shape
    kern = functools.partial(_ssd_prompt_kernel, tb=tb)
    wcol = lambda blk: (lambda bi, i: (0, blk))
    const = lambda bi, i: (0, 0)
    return pl.pallas_call(
        kern,
        grid=(b, l // tb),
        in_specs=[pl.BlockSpec((None, tb, d), lambda bi, i: (bi, i, 0)),
                  pl.BlockSpec((1, d), const),
                  pl.BlockSpec((d, SSD_DI), wcol(O_OFF_X // SSD_DI)),
                  pl.BlockSpec((d, SSD_DI), wcol(0)),
                  pl.BlockSpec((d, 2 * SSD_BC), wcol(O_OFF_BC // (2 * SSD_BC))),
                  pl.BlockSpec((d, LANES), wcol(O_OFF_DT // LANES)),
                  pl.BlockSpec((CONV_W, SSD_CONV_DIM), const),
                  pl.BlockSpec((1, SSD_CONV_DIM), const),
                  pl.BlockSpec((8, LANES), const),
                  pl.BlockSpec((1, SSD_DI), const),
                  pl.BlockSpec((1, SSD_DI), const),
                  pl.BlockSpec((LANES, SSD_DI), const)],
        out_specs=[pl.BlockSpec((None, tb, SSD_DI), lambda bi, i: (bi, i, 0)),
                   pl.BlockSpec((None, CONV_W - 1, SSD_CONV_DIM), lambda bi, i: (bi, 0, 0)),
                   pl.BlockSpec((None, SSD_HEADS * SSD_P, SSD_N), lambda bi, i: (bi, 0, 0))],
        out_shape=[jax.ShapeDtypeStruct((b, l, SSD_DI), MM_DTYPE),
                   jax.ShapeDtypeStruct((b, CONV_W - 1, SSD_CONV_DIM), f32),
                   jax.ShapeDtypeStruct((b, SSD_HEADS * SSD_P, SSD_N), f32)],
        scratch_shapes=[pltpu.VMEM((tb + 8, SSD_CONV_DIM), f32),
                        pltpu.VMEM((tb, SSD_CONV_DIM), f32),
                        pltpu.VMEM((tb, SSD_DI), f32),
                        pltpu.VMEM((tb, LANES), f32),
                        pltpu.VMEM((tb, SSD_DI), f32),
                        pltpu.VMEM((tb, SSD_DI), f32),
                        pltpu.VMEM((SSD_HEADS * SSD_P, SSD_N), f32)],
        compiler_params=pltpu.CompilerParams(dimension_semantics=("parallel", "arbitrary"),
                                             vmem_limit_bytes=VMEM_LIMIT),
        name="ssd_prompt",
    )(x, pre_g, w_in, w_in, w_in, w_in, cw, cb, hp, dskip, ng, expand)


def _pad_rows(row, n):
    return jnp.concatenate([row, jnp.zeros((n - 1, row.shape[1]), row.dtype)], axis=0)


def _bcast_cols(row, width):
    hi, mid, lo = _split3(row)
    lhs = jnp.concatenate([hi, mid, lo, jnp.zeros((13, row.shape[1]), bf16)], axis=0)
    sel = (lax.broadcasted_iota(jnp.int32, (16, width), 0) < 3).astype(bf16)
    return lax.dot_general(lhs, sel, (((0,), (0,)), ((), ())), preferred_element_type=f32)


def _conv_step(buf_ref, x, cw_ref, cb_ref, new_buf_ref):
    acc = cb_ref[...] + x * cw_ref[3:4, :]
    for s in range(CONV_W - 1):
        acc = acc + buf_ref[s] * cw_ref[s:s + 1, :]
    for s in range(CONV_W - 2):
        new_buf_ref[s] = buf_ref[s + 1]
    new_buf_ref[CONV_W - 2] = x
    return _silu(acc)


def _gdn_decode_kernel(q_ref, k_ref, v_ref, gate_ref, small_ref, buf_ref, s_ref, cw_ref, cb_ref, hp_ref, ng_ref,
                       o_ref, new_buf_ref, s_out_ref, *, bb):
    x = jnp.concatenate([q_ref[...], k_ref[...], v_ref[...]], axis=-1)
    qkv = _conv_step(buf_ref, x, cw_ref, cb_ref, new_buf_ref)
    small = small_ref[...]
    beta = _sigmoid(small)
    eg = jnp.exp(hp_ref[0:1, :] * _softplus(small + hp_ref[1:2, :]))
    ng = ng_ref[...]
    gate = gate_ref[...]

    def head(h):
        c0 = h * GDN_DK
        q = qkv[:, c0:c0 + GDN_DK]
        q = q * lax.rsqrt(jnp.sum(q * q, axis=-1, keepdims=True) + EPS) * (GDN_DK ** -0.5)
        k = qkv[:, GDN_KW + c0:GDN_KW + c0 + GDN_DK]
        k = k * lax.rsqrt(jnp.sum(k * k, axis=-1, keepdims=True) + EPS)
        v = qkv[:, 2 * GDN_KW + c0:2 * GDN_KW + c0 + GDN_DV]
        qk = jnp.sum(q * k, axis=-1, keepdims=True)
        b_h = beta[:, SMALL_BETA + h:SMALL_BETA + h + 1]
        eg_h = eg[:, SMALL_A + h:SMALL_A + h + 1]
        seqs = range(bb)
        s_old = [s_ref[b, h] for b in seqs]
        r = [_mm(jnp.concatenate([k[b:b + 1], q[b:b + 1], jnp.zeros((6, GDN_DK), f32)], axis=0), s_old[b])
             for b in seqs]
        yield
        u = [b_h[b:b + 1] * v[b:b + 1] - (b_h[b:b + 1] * eg_h[b:b + 1]) * r[b][0:1] for b in seqs]
        upd = [_mm_tn(_pad_rows(k[b:b + 1], 16), _pad_rows(u[b], 16)) for b in seqs]
        yield
        for b in seqs:
            s_out_ref[b, h] = eg_h[b:b + 1] * s_old[b] + upd[b]
        o = jnp.concatenate([eg_h[b:b + 1] * r[b][1:2] + qk[b:b + 1] * u[b] for b in seqs], axis=0)
        o_ref[:, c0:c0 + GDN_DV] = (_rms(o, GDN_DV) * ng * _silu(gate[:, c0:c0 + GDN_DV])).astype(o_ref.dtype)

    for _ in _round_robin([head(h) for h in range(GDN_HEADS)]):
        pass


def _gdn_decode(proj, buf, s, cw, cb, hp, ng, bb):
    b = proj.shape[0]
    wq = GDN_KW
    kern = functools.partial(_gdn_decode_kernel, bb=bb)
    col = lambda blk: (lambda i: (i, blk))
    const = lambda i: (0, 0)
    return pl.pallas_call(
        kern,
        grid=(b // bb,),
        in_specs=[pl.BlockSpec((bb, wq), col(0)),
                  pl.BlockSpec((bb, wq), col(1)),
                  pl.BlockSpec((bb, wq), col(2)),
                  pl.BlockSpec((bb, wq), col(3)),
                  pl.BlockSpec((bb, LANES), col(E_OFF_GSMALL // LANES)),
                  pl.BlockSpec((CONV_W - 1, bb, GDN_CONV_DIM), lambda i: (0, i, 0)),
                  pl.BlockSpec((bb, GDN_HEADS, GDN_DK, GDN_DV), lambda i: (i, 0, 0, 0)),
                  pl.BlockSpec((CONV_W, GDN_CONV_DIM), const),
                  pl.BlockSpec((1, GDN_CONV_DIM), const),
                  pl.BlockSpec((8, LANES), const),
                  pl.BlockSpec((1, GDN_DV), const)],
        out_specs=[pl.BlockSpec((bb, GDN_VW), col(0)),
                   pl.BlockSpec((CONV_W - 1, bb, GDN_CONV_DIM), lambda i: (0, i, 0)),
                   pl.BlockSpec((bb, GDN_HEADS, GDN_DK, GDN_DV), lambda i: (i, 0, 0, 0))],
        out_shape=[jax.ShapeDtypeStruct((b, GDN_VW), MM_DTYPE),
                   jax.ShapeDtypeStruct((CONV_W - 1, b, GDN_CONV_DIM), f32),
                   jax.ShapeDtypeStruct((b, GDN_HEADS, GDN_DK, GDN_DV), f32)],
        compiler_params=pltpu.CompilerParams(dimension_semantics=("parallel",), vmem_limit_bytes=VMEM_LIMIT),
        name="gdn_decode",
    )(proj, proj, proj, proj, proj, buf, s, cw, cb, hp, ng)


def _gla_decode_kernel(q_ref, k_ref, v_ref, gate_ref, small_ref, s_ref, wlr_ref, blr_ref, ng_ref,
                       o_ref, s_out_ref, *, bb):
    z = _mm(small_ref[...], wlr_ref[...]) + blr_ref[...]
    la = -_softplus(-z) * (1.0 / GLA_TAU)
    dec = jnp.exp(la)
    qe = q_ref[...] * (GLA_DK ** -0.5) * dec
    kx = k_ref[...]
    ke = kx * jnp.exp(-la)
    vx = v_ref[...]
    ng = ng_ref[...]
    gate = gate_ref[...]
    for h in range(GLA_HEADS):
        k0 = h * GLA_DK
        v0 = h * GLA_DV
        p = jnp.sum(qe[:, k0:k0 + GLA_DK] * ke[:, k0:k0 + GLA_DK], axis=-1, keepdims=True)
        outs = []
        for b in range(bb):
            s_old = s_ref[b, h]
            vrow = vx[b:b + 1, v0:v0 + GLA_DV]
            outs.append(_mm(_pad_rows(qe[b:b + 1, k0:k0 + GLA_DK], 8), s_old)[0:1] + p[b:b + 1] * vrow)
            s_out_ref[b, h] = (_bcast_cols(dec[b:b + 1, k0:k0 + GLA_DK], GLA_DV) * s_old
                               + _mm_tn(_pad_rows(kx[b:b + 1, k0:k0 + GLA_DK], 16), _pad_rows(vrow, 16)))
        o = jnp.concatenate(outs, axis=0)
        o_ref[:, v0:v0 + GLA_DV] = (_rms(o, GLA_DV) * ng * _silu(gate[:, v0:v0 + GLA_DV])).astype(o_ref.dtype)


def _gla_decode(proj, s, wlr, blr, ng, bb):
    b = proj.shape[0]
    kern = functools.partial(_gla_decode_kernel, bb=bb)
    col = lambda blk: (lambda i: (i, blk))
    const = lambda i: (0, 0)
    return pl.pallas_call(
        kern,
        grid=(b // bb,),
        in_specs=[pl.BlockSpec((bb, GLA_KW), col(0)),
                  pl.BlockSpec((bb, GLA_KW), col(L_OFF_K // GLA_KW)),
                  pl.BlockSpec((bb, GLA_VW), col(L_OFF_V // GLA_VW)),
                  pl.BlockSpec((bb, GLA_VW), col(L_OFF_GATE // GLA_VW)),
                  pl.BlockSpec((bb, LANES), col(L_OFF_SMALL // LANES)),
                  pl.BlockSpec((bb, GLA_HEADS, GLA_DK, GLA_DV), lambda i: (i, 0, 0, 0)),
                  pl.BlockSpec((LANES, GLA_KW), const),
                  pl.BlockSpec((1, GLA_KW), const),
                  pl.BlockSpec((1, GLA_DV), const)],
        out_specs=[pl.BlockSpec((bb, GLA_VW), col(0)),
                   pl.BlockSpec((bb, GLA_HEADS, GLA_DK, GLA_DV), lambda i: (i, 0, 0, 0))],
        out_shape=[jax.ShapeDtypeStruct((b, GLA_VW), MM_DTYPE),
                   jax.ShapeDtypeStruct((b, GLA_HEADS, GLA_DK, GLA_DV), f32)],
        compiler_params=pltpu.CompilerParams(dimension_semantics=("parallel",), vmem_limit_bytes=VMEM_LIMIT),
        name="gla_decode",
    )(proj, proj, proj, proj, proj, s, wlr, blr, ng)


def _ssd_decode_kernel(x_ref, z_ref, bc_ref, dt_ref, buf_ref, h_ref, cw_ref, cb_ref, hp_ref, dskip_ref, ng_ref,
                       expand_ref, y_ref, new_buf_ref, h_out_ref, *, bb):
    xin = jnp.concatenate([x_ref[...], bc_ref[...]], axis=-1)
    xbc = _conv_step(buf_ref, xin, cw_ref, cb_ref, new_buf_ref)
    xs = xbc[:, :SSD_DI]
    dt = _softplus(_head_lanes(dt_ref[...]) + hp_ref[0:1, :])
    el = jnp.exp(dt * hp_ref[1:2, :])
    expand = expand_ref[...]
    dt_x = _sel_rows(dt, expand)
    el_x = _sel_rows(el, expand)
    xdt = xs * dt_x
    z = z_ref[...]
    for g in range(SSD_G):
        gs = slice(g * SSD_GW, (g + 1) * SSD_GW)
        bm = xbc[:, SSD_DI + g * SSD_N:SSD_DI + (g + 1) * SSD_N]
        cm = xbc[:, SSD_DI + SSD_BC + g * SSD_N:SSD_DI + SSD_BC + (g + 1) * SSD_N]
        cbs = jnp.sum(cm * bm, axis=-1, keepdims=True)
        outs = []
        for b in range(bb):
            hg = h_ref[b, gs, :]
            y_inter = _mm_nt(_pad_rows(cm[b:b + 1], 8), hg)[0:1]
            outs.append(el_x[b:b + 1, gs] * y_inter)
            h_out_ref[b, gs, :] = (_bcast_cols(el_x[b:b + 1, gs], SSD_N) * hg
                                   + _mm_tn(_pad_rows(xdt[b:b + 1, gs], 16), _pad_rows(bm[b:b + 1], 16)))
        y = jnp.concatenate(outs, axis=0) + cbs * xdt[:, gs] + dskip_ref[:, gs] * xs[:, gs]
        yg = y * _silu(z[:, gs])
        y_ref[:, gs] = (_rms(yg, SSD_GW) * ng_ref[:, gs]).astype(y_ref.dtype)


def _ssd_decode(proj, buf, hst, cw, cb, hp, dskip, ng, expand, bb):
    b = proj.shape[0]
    kern = functools.partial(_ssd_decode_kernel, bb=bb)
    col = lambda blk: (lambda i: (i, blk))
    const = lambda i: (0, 0)
    return pl.pallas_call(
        kern,
        grid=(b // bb,),
        in_specs=[pl.BlockSpec((bb, SSD_DI), col(O_OFF_X // SSD_DI)),
                  pl.BlockSpec((bb, SSD_DI), col(0)),
                  pl.BlockSpec((bb, 2 * SSD_BC), col(O_OFF_BC // (2 * SSD_BC))),
                  pl.BlockSpec((bb, LANES), col(O_OFF_DT // LANES)),
                  pl.BlockSpec((CONV_W - 1, bb, SSD_CONV_DIM), lambda i: (0, i, 0)),
                  pl.BlockSpec((bb, SSD_HEADS * SSD_P, SSD_N), lambda i: (i, 0, 0)),
                  pl.BlockSpec((CONV_W, SSD_CONV_DIM), const),
                  pl.BlockSpec((1, SSD_CONV_DIM), const),
                  pl.BlockSpec((8, LANES), const),
                  pl.BlockSpec((1, SSD_DI), const),
                  pl.BlockSpec((1, SSD_DI), const),
                  pl.BlockSpec((LANES, SSD_DI), const)],
        out_specs=[pl.BlockSpec((bb, SSD_DI), col(0)),
                   pl.BlockSpec((CONV_W - 1, bb, SSD_CONV_DIM), lambda i: (0, i, 0)),
                   pl.BlockSpec((bb, SSD_HEADS * SSD_P, SSD_N), lambda i: (i, 0, 0))],
        out_shape=[jax.ShapeDtypeStruct((b, SSD_DI), MM_DTYPE),
                   jax.ShapeDtypeStruct((CONV_W - 1, b, SSD_CONV_DIM), f32),
                   jax.ShapeDtypeStruct((b, SSD_HEADS * SSD_P, SSD_N), f32)],
        compiler_params=pltpu.CompilerParams(dimension_semantics=("parallel",), vmem_limit_bytes=VMEM_LIMIT),
        name="ssd_decode",
    )(proj, proj, proj, proj, buf, hst, cw, cb, hp, dskip, ng, expand)


def _pack_gla_w_in(w):
    gla = w[:, E_OFF_GLA:]
    return jnp.concatenate([gla, jnp.zeros((w.shape[0], L_N - gla.shape[1]), w.dtype)], axis=1)


def _lane_row(vals, offset):
    return jnp.zeros((LANES,), f32).at[offset:offset + vals.shape[0]].set(vals.astype(f32))


def kernel(x_prompt, x_sample, state_gdn_conv, state_gdn, state_gla, state_ssd_conv, state_ssd, e_pre_g, e_post_g, e_w_in, e_conv_w, e_conv_b, gdn_a_log, gdn_dt_bias, gdn_norm_g, gla_w_lr, gla_b_lr, gla_norm_g, e_w_out, o_pre_g, o_post_g, o_w_in, ssd_conv_w, ssd_conv_b, ssd_dt_bias, ssd_a_log, ssd_d, ssd_norm_g, o_w_out):
    bp, lp, d = x_prompt.shape
    bs = x_sample.shape[0]
    tp = bp * lp
    tb = min(lp, PROMPT_BLOCK)
    bb = min(bs, SAMPLE_BLOCK)
    bb_even = min(bs, SAMPLE_BLOCK_EVEN)

    w_in0 = e_w_in[0].astype(MM_DTYPE)
    w_gla = _pack_gla_w_in(w_in0)
    w_out0 = e_w_out[0].astype(MM_DTYPE)
    pre0 = e_pre_g[0].reshape(1, d)
    post0 = e_post_g[0].reshape(1, d)
    cw0 = e_conv_w[0]
    cb0 = e_conv_b[0].reshape(1, GDN_CONV_DIM)
    hp0 = jnp.zeros((8, LANES), f32)
    hp0 = hp0.at[0].set(_lane_row(-jnp.exp(gdn_a_log[0].astype(f32)), SMALL_A))
    hp0 = hp0.at[1].set(_lane_row(gdn_dt_bias[0], SMALL_A))
    ng_gdn = gdn_norm_g[0].reshape(1, GDN_DV)
    wlr = jnp.zeros((LANES, GLA_KW), f32).at[SMALL_LR:SMALL_LR + GLA_RANK].set(gla_w_lr[0]).astype(MM_DTYPE)
    blr = gla_b_lr[0].reshape(1, GLA_KW)
    ng_gla = gla_norm_g[0].reshape(1, GLA_DV)

    hp_flat = x_prompt.reshape(tp, d)
    hs_flat = x_sample.reshape(bs, d)

    o_p, gdn_conv_p, gdn_p, gla_p = _even_prompt(x_prompt, pre0, w_in0, w_gla, cw0, cb0, hp0, ng_gdn, wlr, blr, ng_gla,
                                                 tb)
    o_p = o_p.reshape(tp, GDN_VW + GLA_VW)
    hp_flat = _out_proj(o_p, o_p, w_out0, post0, hp_flat)

    proj_gdn = _norm_proj(hs_flat, pre0, w_in0, E_GDN_N // 3, E_GDN_N)
    proj_gla = _norm_proj(hs_flat, pre0, w_gla, L_N // 5, L_N)
    o1_s, gdn_conv_s, gdn_s = _gdn_decode(proj_gdn, jnp.swapaxes(state_gdn_conv[0], 0, 1), state_gdn[0],
                                          cw0, cb0, hp0, ng_gdn, bb_even)
    o2_s, gla_s = _gla_decode(proj_gla, state_gla[0], wlr, blr, ng_gla, bb_even)
    hs_flat = _out_proj(o1_s, o2_s, w_out0, post0, hs_flat)

    w_in1 = o_w_in[0].astype(MM_DTYPE)
    w_out1 = o_w_out[0].astype(MM_DTYPE)
    pre1 = o_pre_g[0].reshape(1, d)
    post1 = o_post_g[0].reshape(1, d)
    cw1 = ssd_conv_w[0]
    cb1 = ssd_conv_b[0].reshape(1, SSD_CONV_DIM)
    hp1 = jnp.zeros((8, LANES), f32)
    hp1 = hp1.at[0].set(_lane_row(ssd_dt_bias[0], 0))
    hp1 = hp1.at[1].set(_lane_row(-jnp.exp(ssd_a_log[0].astype(f32)), 0))
    dskip = jnp.repeat(ssd_d[0].astype(f32), SSD_P).reshape(1, SSD_DI)
    ng_ssd = ssd_norm_g[0].reshape(1, SSD_DI)
    expand = (lax.broadcasted_iota(jnp.int32, (LANES, SSD_DI), 0)
              == lax.broadcasted_iota(jnp.int32, (LANES, SSD_DI), 1) // SSD_P).astype(MM_DTYPE)

    y_p, ssd_conv_p, ssd_p = _ssd_prompt(hp_flat.reshape(bp, lp, d), pre1, w_in1, cw1, cb1, hp1, dskip, ng_ssd,
                                         expand, tb)
    y_p = y_p.reshape(tp, SSD_DI)
    hp_flat = _out_proj(y_p, y_p, w_out1, post1, hp_flat)

    proj_s = _norm_proj(hs_flat, pre1, w_in1, O_PROJ_N // 6, O_PROJ_N)
    y_s, ssd_conv_s, ssd_s = _ssd_decode(proj_s, jnp.swapaxes(state_ssd_conv[0], 0, 1),
                                         state_ssd[0].reshape(bs, SSD_HEADS * SSD_P, SSD_N),
                                         cw1, cb1, hp1, dskip, ng_ssd, expand, bb)
    hs_flat = _out_proj(y_s, y_s, w_out1, post1, hs_flat)

    return (hp_flat.reshape(bp, lp, d), hs_flat.reshape(bs, 1, d),
            gdn_conv_p[None], gdn_p[None], gla_p[None],
            ssd_conv_p[None], ssd_p.reshape(bp, SSD_HEADS, SSD_P, SSD_N)[None],
            jnp.swapaxes(gdn_conv_s, 0, 1)[None], gdn_s[None], gla_s[None],
            jnp.swapaxes(ssd_conv_s, 0, 1)[None],
            ssd_s.reshape(bs, SSD_HEADS, SSD_P, SSD_N)[None])
```

```python
import functools

import jax
import jax.numpy as jnp
from jax import lax
from jax.experimental import pallas as pl
from jax.experimental.pallas import tpu as pltpu

f32 = jnp.float32
bf16 = jnp.bfloat16
MM_DTYPE = jnp.bfloat16

D_MODEL = 1024
CONV_W = 4
EPS = 1e-6
LANES = 128

GDN_HEADS = 8
GDN_DK = 128
GDN_DV = 128
GDN_KW = GDN_HEADS * GDN_DK
GDN_VW = GDN_HEADS * GDN_DV
GDN_CONV_DIM = 2 * GDN_KW + GDN_VW
GLA_HEADS = 4
GLA_DK = 128
GLA_DV = 256
GLA_KW = GLA_HEADS * GLA_DK
GLA_VW = GLA_HEADS * GLA_DV
GLA_RANK = 16
GLA_TAU = 16.0
SSD_DI = 2 * D_MODEL
SSD_P = 64
SSD_HEADS = SSD_DI // SSD_P
SSD_N = 128
SSD_G = 4
SSD_K = SSD_HEADS // SSD_G
SSD_GW = SSD_DI // SSD_G
SSD_BC = SSD_G * SSD_N
SSD_CONV_DIM = SSD_DI + 2 * SSD_BC

E_OFF_GATE = 3 * GDN_KW
E_OFF_GSMALL = 4 * GDN_KW
E_GDN_N = E_OFF_GSMALL + LANES
E_OFF_GLA = E_OFF_GSMALL + 2 * GDN_HEADS
SMALL_BETA = 0
SMALL_A = GDN_HEADS
L_OFF_K = GLA_KW
L_OFF_V = 2 * GLA_KW
L_OFF_GATE = L_OFF_V + GLA_VW
L_OFF_SMALL = L_OFF_GATE + GLA_VW
L_N = L_OFF_SMALL + LANES
SMALL_LR = 0
O_OFF_X = SSD_DI
O_OFF_BC = 2 * SSD_DI
O_OFF_DT = 2 * SSD_DI + 2 * SSD_BC
O_PROJ_N = 6 * 1024

CHUNK = 64
GDN_PREP_CHUNKS = 2
assert SSD_P == CHUNK and 2 * SSD_P == LANES
PROMPT_BLOCK = 512
SAMPLE_BLOCK = 8
SAMPLE_BLOCK_EVEN = 16
ROW_TILE = 1024
V7X_VMEM_BYTES = 64 * 1024 * 1024
VMEM_LIMIT = V7X_VMEM_BYTES - 8 * 1024 * 1024


def _mm(a, b):
    return jnp.dot(a.astype(MM_DTYPE), b.astype(MM_DTYPE), preferred_element_type=f32)


def _mm_nt(a, b):
    return lax.dot_general(a.astype(MM_DTYPE), b.astype(MM_DTYPE), (((1,), (1,)), ((), ())),
                           preferred_element_type=f32)


def _mm_tn(a, b):
    return lax.dot_general(a.astype(MM_DTYPE), b.astype(MM_DTYPE), (((0,), (0,)), ((), ())),
                           preferred_element_type=f32)


def _split3(x):
    hi = x.astype(bf16)
    r = x - hi.astype(f32)
    mid = r.astype(bf16)
    lo = (r - mid.astype(f32)).astype(bf16)
    return hi, mid, lo


def _sel_mm(sel, x):
    hi, mid, lo = _split3(x)
    s = sel.astype(bf16)
    return (jnp.dot(s, hi, preferred_element_type=f32) + jnp.dot(s, mid, preferred_element_type=f32)
            + jnp.dot(s, lo, preferred_element_type=f32))


def _sel_rows(x, sel):
    hi, mid, lo = _split3(x)
    s = sel.astype(bf16)
    return (jnp.dot(hi, s, preferred_element_type=f32) + jnp.dot(mid, s, preferred_element_type=f32)
            + jnp.dot(lo, s, preferred_element_type=f32))


def _sigmoid(x):
    return 1.0 / (1.0 + jnp.exp(-x))


def _silu(x):
    return x * _sigmoid(x)


def _softplus(x):
    return jnp.maximum(x, 0.0) + jnp.log1p(jnp.exp(-jnp.abs(x)))


def _tril(n, strict=False):
    r = lax.broadcasted_iota(jnp.int32, (n, n), 0)
    c = lax.broadcasted_iota(jnp.int32, (n, n), 1)
    return (r > c) if strict else (r >= c)


def _masked_decay(col, row, incl):
    return jnp.exp(jnp.where(incl, col - row, -jnp.inf))


def _interleave(main, side, ratio):
    gens = [main, side]
    steps = {id(main): ratio, id(side): 1}
    while gens:
        for g in list(gens):
            for _ in range(steps[id(g)]):
                try:
                    next(g)
                except StopIteration:
                    gens.remove(g)
                    break
                yield


def _round_robin(gens):
    gens = list(gens)
    while gens:
        for g in list(gens):
            try:
                next(g)
            except StopIteration:
                gens.remove(g)
        yield


def _rms(x, width):
    return x * lax.rsqrt(jnp.sum(x * x, axis=-1, keepdims=True) * (1.0 / width) + EPS)


def _norm_proj_kernel(x_ref, g_ref, w_ref, o_ref, u_scr):
    @pl.when(pl.program_id(1) == 0)
    def _():
        u_scr[...] = (_rms(x_ref[...], D_MODEL) * g_ref[...]).astype(u_scr.dtype)

    o_ref[...] = jnp.dot(u_scr[...], w_ref[...], preferred_element_type=f32)


def _norm_proj(x, g, w, tn, n):
    t, d = x.shape
    assert n % tn == 0 and n - tn < w.shape[1]
    tm = min(t, ROW_TILE)
    return pl.pallas_call(
        _norm_proj_kernel,
        grid=(t // tm, n // tn),
        in_specs=[pl.BlockSpec((tm, d), lambda i, j: (i, 0)),
                  pl.BlockSpec((1, d), lambda i, j: (0, 0)),
                  pl.BlockSpec((d, tn), lambda i, j: (0, j))],
        out_specs=pl.BlockSpec((tm, tn), lambda i, j: (i, j)),
        out_shape=jax.ShapeDtypeStruct((t, n), f32),
        scratch_shapes=[pltpu.VMEM((tm, d), MM_DTYPE)],
        compiler_params=pltpu.CompilerParams(dimension_semantics=("parallel", "arbitrary"),
                                             vmem_limit_bytes=VMEM_LIMIT),
        name="norm_proj",
    )(x, g, w)


def _out_proj_kernel(oa_ref, ob_ref, wa_ref, wb_ref, g_ref, h_ref, y_ref):
    p = (jnp.dot(oa_ref[...], wa_ref[...], preferred_element_type=f32)
         + jnp.dot(ob_ref[...], wb_ref[...], preferred_element_type=f32))
    y_ref[...] = h_ref[...] + _rms(p, D_MODEL) * g_ref[...]


def _out_proj(oa, ob, w, g, h):
    t = oa.shape[0]
    e = w.shape[0] // 2
    assert oa.shape[1] == ob.shape[1] and oa.shape[1] in (e, 2 * e)
    col_b = 1 if ob.shape[1] == 2 * e else 0
    d = w.shape[1]
    tm = min(t, ROW_TILE)
    return pl.pallas_call(
        _out_proj_kernel,
        grid=(t // tm,),
        in_specs=[pl.BlockSpec((tm, e), lambda i: (i, 0)),
                  pl.BlockSpec((tm, e), lambda i: (i, col_b)),
                  pl.BlockSpec((e, d), lambda i: (0, 0)),
                  pl.BlockSpec((e, d), lambda i: (1, 0)),
                  pl.BlockSpec((1, d), lambda i: (0, 0)),
                  pl.BlockSpec((tm, d), lambda i: (i, 0))],
        out_specs=pl.BlockSpec((tm, d), lambda i: (i, 0)),
        out_shape=jax.ShapeDtypeStruct((t, d), f32),
        compiler_params=pltpu.CompilerParams(dimension_semantics=("parallel",),
                                             vmem_limit_bytes=VMEM_LIMIT),
        name="out_proj",
    )(oa, ob, w, w, g, h)


def _normed(x_ref, g_ref):
    return (_rms(x_ref[...], D_MODEL) * g_ref[...]).astype(MM_DTYPE)


def _proj(u, w_ref, c0, c1):
    return jnp.dot(u, w_ref[:, c0:c1], preferred_element_type=f32)


def _conv_block(xp_scr, cw_ref, cb_ref, tb, first):
    @pl.when(first)
    def _():
        xp_scr[0:8, :] = jnp.zeros((8, xp_scr.shape[1]), f32)

    @pl.when(jnp.logical_not(first))
    def _():
        xp_scr[5:8, :] = xp_scr[tb + 5:tb + 8, :]

    def conv_cols(c0, c1, r0=0, nr=tb):
        acc = cb_ref[:, c0:c1] + xp_scr[8 + r0:8 + r0 + nr, c0:c1] * cw_ref[3:4, c0:c1]
        for s in range(CONV_W - 1):
            acc = acc + xp_scr[5 + s + r0:5 + s + r0 + nr, c0:c1] * cw_ref[s:s + 1, c0:c1]
        return _silu(acc)

    return conv_cols


def _gdn_stages(u, first, last, wm_ref, ws_ref, cw_ref, cb_ref, hp_ref, ng_ref,
                o_ref, conv_out_ref, s_out_ref,
                xp_scr, qkv_scr, gate_scr, beta_scr, lg_scr, s_scr, gl_scr, up_scr, w_scr, qg_scr, kd_scr,
                p_scr, *, tb):
    conv_cols = _conv_block(xp_scr, cw_ref, cb_ref, tb, first)

    @pl.when(first)
    def _():
        s_scr[...] = jnp.zeros(s_scr.shape, f32)

    def proj_stages():
        for c0 in range(0, GDN_CONV_DIM, GDN_KW):
            xp_scr[8:8 + tb, c0:c0 + GDN_KW] = _proj(u, wm_ref, c0, c0 + GDN_KW)
            yield
        small = _proj(u, ws_ref, 0, LANES)
        beta_scr[...] = _sigmoid(small)
        lg_scr[...] = hp_ref[0:1, :] * _softplus(small + hp_ref[1:2, :])
        gate_scr[...] = _proj(u, wm_ref, E_OFF_GATE, E_OFF_GATE + GDN_VW)
        yield

    span = GDN_PREP_CHUNKS * CHUNK

    def conv_norm(it):
        r0 = it * span
        rr = slice(r0, r0 + span)
        for h in range(GDN_HEADS):
            c0 = h * GDN_DK
            q = conv_cols(c0, c0 + GDN_DK, r0, span)
            q = q * lax.rsqrt(jnp.sum(q * q, axis=-1, keepdims=True) + EPS) * (GDN_DK ** -0.5)
            qkv_scr[rr, c0:c0 + GDN_DK] = q
            k = conv_cols(GDN_KW + c0, GDN_KW + c0 + GDN_DK, r0, span)
            k = k * lax.rsqrt(jnp.sum(k * k, axis=-1, keepdims=True) + EPS)
            qkv_scr[rr, GDN_KW + c0:GDN_KW + c0 + GDN_DK] = k
            qkv_scr[rr, 2 * GDN_KW + c0:2 * GDN_KW + c0 + GDN_DV] = conv_cols(
                2 * GDN_KW + c0, 2 * GDN_KW + c0 + GDN_DV, r0, span)
            yield

    incl = _tril(CHUNK)
    strict = _tril(CHUNK, strict=True)
    eye = jnp.logical_and(incl, jnp.logical_not(strict)).astype(f32)
    sr = lax.broadcasted_iota(jnp.int32, (span, span), 0)
    sc = lax.broadcasted_iota(jnp.int32, (span, span), 1)
    lmat = jnp.logical_and(sr >= sc, sr // CHUNK == sc // CHUNK).astype(f32)
    ng = ng_ref[...]
    heads = range(GDN_HEADS)
    probs = [(ci, h) for ci in range(GDN_PREP_CHUNKS) for h in heads]
    n = range(len(probs))

    def prep(c):
        r0 = c * span
        gam_all = _sel_mm(lmat, lg_scr[pl.ds(r0, span), :])
        gam_t = gam_all.T
        beta_all = beta_scr[pl.ds(r0, span), :]
        rows, gcol, grow, glast, bcol = [], [], [], [], []
        for ci, h in probs:
            lo, hi = ci * CHUNK, (ci + 1) * CHUNK
            rows.append(pl.ds(r0 + lo, CHUNK))
            gcol.append(gam_all[lo:hi, SMALL_A + h:SMALL_A + h + 1])
            grow.append(gam_t[SMALL_A + h:SMALL_A + h + 1, lo:hi])
            glast.append(gam_all[hi - 1:hi, SMALL_A + h:SMALL_A + h + 1])
            bcol.append(beta_all[lo:hi, SMALL_BETA + h:SMALL_BETA + h + 1])
        for ci in range(GDN_PREP_CHUNKS):
            gl_scr[pl.ds(c * GDN_PREP_CHUNKS + ci, 1), :] = jnp.exp(gam_all[(ci + 1) * CHUNK - 1:(ci + 1) * CHUNK, :])
        q = [qkv_scr[rows[j], h * GDN_DK:(h + 1) * GDN_DK] for j, (_, h) in enumerate(probs)]
        k = [qkv_scr[rows[j], GDN_KW + h * GDN_DK:GDN_KW + (h + 1) * GDN_DK] for j, (_, h) in enumerate(probs)]
        v = [qkv_scr[rows[j], 2 * GDN_KW + h * GDN_DV:2 * GDN_KW + (h + 1) * GDN_DV] for j, (_, h) in enumerate(probs)]
        kb = [k[j].astype(MM_DTYPE) for j in n]
        kk = [_mm_nt(kb[j], kb[j]) for j in n]
        yield
        qk = [_mm_nt(q[j], kb[j]) for j in n]
        yield
        decay = [_masked_decay(gcol[j], grow[j], incl) for j in n]
        eg = [jnp.exp(gcol[j]) for j in n]
        for j, (_, h) in enumerate(probs):
            c0 = h * GDN_DK
            qg_scr[rows[j], c0:c0 + GDN_DK] = (eg[j] * q[j]).astype(MM_DTYPE)
            kd_scr[rows[j], c0:c0 + GDN_DK] = (jnp.exp(glast[j] - gcol[j]) * k[j]).astype(MM_DTYPE)
            p_scr[h, rows[j], :] = (qk[j] * decay[j]).astype(MM_DTYPE)
        nmat = [jnp.where(strict, bcol[j] * (kk[j] * decay[j]), 0.0) for j in n]
        tinv = [eye - nmat[j] for j in n]
        npow = nmat
        for _ in range(5):
            npow = [_mm(npow[j], npow[j]) for j in n]
            yield
            tinv = [tinv[j] + _mm(tinv[j], npow[j]) for j in n]
            yield
        x = [_mm(tinv[j], jnp.concatenate([bcol[j] * v[j], (bcol[j] * eg[j]) * k[j]], axis=-1)) for j in n]
        for j, (_, h) in enumerate(probs):
            c0 = h * GDN_DK
            up_scr[rows[j], c0:c0 + GDN_DV] = x[j][:, :GDN_DV]
            w_scr[rows[j], c0:c0 + GDN_DK] = x[j][:, GDN_DV:].astype(MM_DTYPE)

    state = [s_scr[h] for h in heads]

    def scan(c):
        rows = pl.ds(c * CHUNK, CHUNK)
        gl = gl_scr[c:c + 1, :]
        r = [_mm(jnp.concatenate([w_scr[rows, h * GDN_DK:(h + 1) * GDN_DK],
                                  qg_scr[rows, h * GDN_DK:(h + 1) * GDN_DK]], axis=0), state[h]) for h in heads]
        yield
        us = [(up_scr[rows, h * GDN_DV:(h + 1) * GDN_DV] - r[h][:CHUNK]).astype(MM_DTYPE) for h in heads]
        o = [r[h][CHUNK:] + _mm(p_scr[h, rows, :], us[h]) for h in heads]
        yield
        upd = [_mm_tn(kd_scr[rows, h * GDN_DK:(h + 1) * GDN_DK], us[h]) for h in heads]
        yield
        for h in heads:
            c0 = h * GDN_DV
            state[h] = gl[:, SMALL_A + h:SMALL_A + h + 1] * state[h] + upd[h]
            gate = gate_scr[rows, c0:c0 + GDN_DV]
            o_ref[rows, c0:c0 + GDN_DV] = (_rms(o[h], GDN_DV) * ng * _silu(gate)).astype(o_ref.dtype)
        yield

    def scans(chunks):
        for c in chunks:
            yield from scan(c)

    def side_work(it):
        if it + 1 < n_prep:
            yield from conv_norm(it + 1)
        if it >= 1:
            yield from scans(range((it - 1) * GDN_PREP_CHUNKS, it * GDN_PREP_CHUNKS))

    n_prep = tb // span

    def stages():
        yield from proj_stages()
        yield from conv_norm(0)
        for it in range(n_prep):
            yield from _interleave(prep(it), side_work(it), 1)
        yield from scans(range((n_prep - 1) * GDN_PREP_CHUNKS, n_prep * GDN_PREP_CHUNKS))

    def finish():
        for h in heads:
            s_scr[h] = state[h]

        @pl.when(last)
        def _():
            conv_out_ref[...] = xp_scr[tb + 5:tb + 8, :]
            s_out_ref[...] = s_scr[...]

    return stages(), finish


def _gla_stages(u, first, last, wq_ref, wk_ref, wv_ref, wg_ref, ws_ref, wlr_ref, blr_ref, ng_ref,
                o_ref, s_out_ref, q_ref, k_ref, v_ref, gate_ref, la_scr, st_scr, *, tb):
    @pl.when(first)
    def _():
        st_scr[...] = jnp.zeros(st_scr.shape, f32)

    def proj_stages():
        q_ref[...] = _proj(u, wq_ref, 0, GLA_KW)
        k_ref[...] = _proj(u, wk_ref, 0, GLA_KW)
        small = _proj(u, ws_ref, 0, LANES)
        z = _mm(small, wlr_ref[...]) + blr_ref[...]
        la_scr[...] = -_softplus(-z) * (1.0 / GLA_TAU)
        yield
        v_ref[...] = _proj(u, wv_ref, 0, GLA_VW).astype(v_ref.dtype)
        yield
        gate_ref[...] = _proj(u, wg_ref, 0, GLA_VW)
        yield

    incl = _tril(CHUNK)
    lmat = incl.astype(f32)
    ng = ng_ref[...]

    heads = range(GLA_HEADS)
    n_chunks = tb // CHUNK
    staged = [None] * n_chunks
    state = [st_scr[h] for h in heads]

    def prep(c):
        rows = pl.ds(c * CHUNK, CHUNK)
        bcum = [_sel_mm(lmat, la_scr[rows, h * GLA_DK:(h + 1) * GLA_DK]) for h in heads]
        yield
        k = [k_ref[rows, h * GLA_DK:(h + 1) * GLA_DK] for h in heads]
        v = [v_ref[rows, h * GLA_DV:(h + 1) * GLA_DV].astype(MM_DTYPE) for h in heads]
        qe = [(q_ref[rows, h * GLA_DK:(h + 1) * GLA_DK] * (GLA_DK ** -0.5) * jnp.exp(bcum[h])).astype(MM_DTYPE)
              for h in heads]
        ke = [k[h] * jnp.exp(-bcum[h]) for h in heads]
        blast = [bcum[h][CHUNK - 1:CHUNK, :] for h in heads]
        kdec = [(k[h] * jnp.exp(blast[h] - bcum[h])).astype(MM_DTYPE) for h in heads]
        p = [jnp.where(incl, _mm_nt(qe[h], ke[h]), 0.0) for h in heads]
        yield
        o_intra = [_mm(p[h], v[h]) for h in heads]
        staged[c] = (qe, v, kdec, blast, o_intra)
        yield

    def scan(c):
        rows = pl.ds(c * CHUNK, CHUNK)
        qe, v, kdec, blast, o_intra = staged[c]
        o = [o_intra[h] + _mm_nt(qe[h], state[h]) for h in heads]
        upd = [_mm_tn(v[h], kdec[h]) for h in heads]
        yield
        for h in heads:
            v0 = h * GLA_DV
            state[h] = jnp.exp(blast[h]) * state[h] + upd[h]
            gate = gate_ref[rows, v0:v0 + GLA_DV]
            o_ref[rows, GDN_VW + v0:GDN_VW + v0 + GLA_DV] = (
                _rms(o[h], GLA_DV) * ng * _silu(gate)).astype(o_ref.dtype)
        yield

    def stages():
        yield from proj_stages()
        yield from prep(0)
        for c in range(1, n_chunks):
            yield from _interleave(prep(c), scan(c - 1), 1)
        yield from scan(n_chunks - 1)

    def finish():
        for h in heads:
            st_scr[h] = state[h]

        @pl.when(last)
        def _():
            for h in heads:
                s_out_ref[h] = st_scr[h].T

    return stages(), finish


N_GDN_SCRATCH = 12
EVEN_STAGE_RATIO = 2


def _even_prompt_kernel(x_ref, pg_ref, wm_ref, ws_ref, wq_ref, wk_ref, wv_ref, wg_ref, wsl_ref, cw_ref, cb_ref, hp_ref,
                        ngd_ref, wlr_ref, blr_ref, ngl_ref,
                        o_ref, conv_out_ref, sgdn_out_ref, sgla_out_ref, *scratch, tb):
    i = pl.program_id(1)
    first = i == 0
    last = i == pl.num_programs(1) - 1
    u = _normed(x_ref, pg_ref)
    gdn, gdn_finish = _gdn_stages(u, first, last, wm_ref, ws_ref, cw_ref, cb_ref, hp_ref, ngd_ref,
                                  o_ref, conv_out_ref, sgdn_out_ref, *scratch[:N_GDN_SCRATCH], tb=tb)
    gla, gla_finish = _gla_stages(u, first, last, wq_ref, wk_ref, wv_ref, wg_ref, wsl_ref, wlr_ref, blr_ref, ngl_ref,
                                  o_ref, sgla_out_ref, *scratch[N_GDN_SCRATCH:], tb=tb)
    for _ in _interleave(gdn, gla, EVEN_STAGE_RATIO):
        pass
    gdn_finish()
    gla_finish()


def _even_prompt(x, pre_g, w_in, w_gla, cw, cb, hp, ng_gdn, wlr, blr, ng_gla, tb):
    b, l, d = x.shape
    assert tb % (GDN_PREP_CHUNKS * CHUNK) == 0 and l % tb == 0
    kern = functools.partial(_even_prompt_kernel, tb=tb)
    wcol = lambda blk: (lambda bi, i: (0, blk))
    const = lambda bi, i: (0, 0)
    gdn_scratch = [pltpu.VMEM((tb + 8, GDN_CONV_DIM), f32),
                   pltpu.VMEM((tb, GDN_CONV_DIM), f32),
                   pltpu.VMEM((tb, GDN_VW), f32),
                   pltpu.VMEM((tb, LANES), f32),
                   pltpu.VMEM((tb, LANES), f32),
                   pltpu.VMEM((GDN_HEADS, GDN_DK, GDN_DV), f32),
                   pltpu.VMEM((max(8, tb // CHUNK), LANES), f32),
                   pltpu.VMEM((tb, GDN_VW), f32),
                   pltpu.VMEM((tb, GDN_KW), MM_DTYPE),
                   pltpu.VMEM((tb, GDN_KW), MM_DTYPE),
                   pltpu.VMEM((tb, GDN_KW), MM_DTYPE),
                   pltpu.VMEM((GDN_HEADS, tb, CHUNK), MM_DTYPE)]
    assert len(gdn_scratch) == N_GDN_SCRATCH
    gla_scratch = [pltpu.VMEM((tb, GLA_KW), f32),
                   pltpu.VMEM((tb, GLA_KW), f32),
                   pltpu.VMEM((tb, GLA_VW), MM_DTYPE),
                   pltpu.VMEM((tb, GLA_VW), f32),
                   pltpu.VMEM((tb, GLA_KW), f32),
                   pltpu.VMEM((GLA_HEADS, GLA_DV, GLA_DK), f32)]
    return pl.pallas_call(
        kern,
        grid=(b, l // tb),
        in_specs=[pl.BlockSpec((None, tb, d), lambda bi, i: (bi, i, 0)),
                  pl.BlockSpec((1, d), const),
                  pl.BlockSpec((d, E_OFF_GSMALL), const),
                  pl.BlockSpec((d, LANES), wcol(E_OFF_GSMALL // LANES)),
                  pl.BlockSpec((d, GLA_KW), wcol(0)),
                  pl.BlockSpec((d, GLA_KW), wcol(L_OFF_K // GLA_KW)),
                  pl.BlockSpec((d, GLA_VW), wcol(L_OFF_V // GLA_VW)),
                  pl.BlockSpec((d, GLA_VW), wcol(L_OFF_GATE // GLA_VW)),
                  pl.BlockSpec((d, LANES), wcol(L_OFF_SMALL // LANES)),
                  pl.BlockSpec((CONV_W, GDN_CONV_DIM), const),
                  pl.BlockSpec((1, GDN_CONV_DIM), const),
                  pl.BlockSpec((8, LANES), const),
                  pl.BlockSpec((1, GDN_DV), const),
                  pl.BlockSpec((LANES, GLA_KW), const),
                  pl.BlockSpec((1, GLA_KW), const),
                  pl.BlockSpec((1, GLA_DV), const)],
        out_specs=[pl.BlockSpec((None, tb, GDN_VW + GLA_VW), lambda bi, i: (bi, i, 0)),
                   pl.BlockSpec((None, CONV_W - 1, GDN_CONV_DIM), lambda bi, i: (bi, 0, 0)),
                   pl.BlockSpec((None, GDN_HEADS, GDN_DK, GDN_DV), lambda bi, i: (bi, 0, 0, 0)),
                   pl.BlockSpec((None, GLA_HEADS, GLA_DK, GLA_DV), lambda bi, i: (bi, 0, 0, 0))],
        out_shape=[jax.ShapeDtypeStruct((b, l, GDN_VW + GLA_VW), MM_DTYPE),
                   jax.ShapeDtypeStruct((b, CONV_W - 1, GDN_CONV_DIM), f32),
                   jax.ShapeDtypeStruct((b, GDN_HEADS, GDN_DK, GDN_DV), f32),
                   jax.ShapeDtypeStruct((b, GLA_HEADS, GLA_DK, GLA_DV), f32)],
        scratch_shapes=gdn_scratch + gla_scratch,
        compiler_params=pltpu.CompilerParams(dimension_semantics=("parallel", "arbitrary"),
                                             vmem_limit_bytes=VMEM_LIMIT),
        name="even_prompt",
    )(x, pre_g, w_in, w_in, w_gla, w_gla, w_gla, w_gla, w_gla, cw, cb, hp, ng_gdn, wlr, blr, ng_gla)


def _head_lanes(x):
    return jnp.where(lax.broadcasted_iota(jnp.int32, x.shape, 1) < SSD_HEADS, x, 0.0)


def _ssd_prompt_kernel(x_ref, pg_ref, wx_ref, wz_ref, wbc_ref, wdt_ref, cw_ref, cb_ref, hp_ref, dskip_ref, ng_ref,
                       expand_ref, wo_ref, pgo_ref, y_ref, conv_out_ref, h_out_ref,
                       xp_scr, xbc_scr, z_ref, lam_scr, lamx_scr, xdt_scr, h_scr, m_scr, *, tb):
    i = pl.program_id(1)
    first = i == 0
    conv_cols = _conv_block(xp_scr, cw_ref, cb_ref, tb, first)

    @pl.when(first)
    def _():
        h_scr[...] = jnp.zeros(h_scr.shape, f32)

    u = _normed(x_ref, pg_ref)
    for c0 in range(0, SSD_DI, 1024):
        xp_scr[8:8 + tb, c0:c0 + 1024] = _proj(u, wx_ref, c0, c0 + 1024)
    xp_scr[8:8 + tb, SSD_DI:SSD_CONV_DIM] = _proj(u, wbc_ref, 0, 2 * SSD_BC)
    dt_raw = _head_lanes(_proj(u, wdt_ref, 0, LANES))
    for c0 in range(0, SSD_DI, 1024):
        z_ref[:, c0:c0 + 1024] = _proj(u, wz_ref, c0, c0 + 1024)

    for c0 in range(0, SSD_CONV_DIM, 512):
        xbc_scr[:, c0:c0 + 512] = conv_cols(c0, c0 + 512)
    dt = _softplus(dt_raw + hp_ref[0:1, :])
    tr = lax.broadcasted_iota(jnp.int32, (tb, tb), 0)
    tc = lax.broadcasted_iota(jnp.int32, (tb, tb), 1)
    lam_all = _sel_mm(jnp.logical_and(tr >= tc, tr // CHUNK == tc // CHUNK).astype(f32), dt * hp_ref[1:2, :])
    lam_scr[...] = lam_all
    lo_blk = lax.broadcasted_iota(jnp.int32, (tb, LANES), 1) < SSD_P
    for p in range(SSD_HEADS // 2):
        cols = slice(p * LANES, (p + 1) * LANES)
        k0 = 2 * p
        xdt_scr[:, cols] = xbc_scr[:, cols] * jnp.where(lo_blk, dt[:, k0:k0 + 1], dt[:, k0 + 1:k0 + 2])
    for c0 in range(0, SSD_DI, 512):
        lamx_scr[:, c0:c0 + 512] = _sel_rows(lam_all, expand_ref[:, c0:c0 + 512])

    lane = lax.broadcasted_iota(jnp.int32, (CHUNK, LANES), 1)
    lo = lane < SSD_P
    incl2 = lax.broadcasted_iota(jnp.int32, (CHUNK, LANES), 0) >= lane % CHUNK
    top = lax.broadcasted_iota(jnp.int32, (2 * SSD_P, SSD_N), 0) < SSD_P
    n_pairs = SSD_HEADS // 2
    ppg = SSD_K // 2
    groups = range(SSD_G)

    def chunk_body(c):
        rows = pl.ds(c * CHUNK, CHUNK)
        lam = lam_scr[rows, :]
        lam_t = lam.T
        llast = lam[CHUNK - 1:CHUNK, :]
        bm = [xbc_scr[rows, SSD_DI + g * SSD_N:SSD_DI + (g + 1) * SSD_N].astype(MM_DTYPE) for g in groups]
        cm = [xbc_scr[rows, SSD_DI + SSD_BC + g * SSD_N:SSD_DI + SSD_BC + (g + 1) * SSD_N].astype(MM_DTYPE)
              for g in groups]
        h_old = [h_scr[g * SSD_GW:(g + 1) * SSD_GW, :] for g in groups]
        cb2 = [_mm_nt(cm[g], jnp.concatenate([bm[g], bm[g]], axis=0)) for g in groups]
        y_inter = [_mm_nt(cm[g], h_old[g]) for g in groups]
        xs, xdt, xdec, lcol = [], [], [], []
        for p in range(n_pairs):
            cols = slice(p * LANES, (p + 1) * LANES)
            xs.append(xbc_scr[rows, cols])
            lcol.append(lamx_scr[rows, cols])
            xdt.append(xdt_scr[rows, cols])
            xdec.append((xdt[p] * jnp.exp(lcol[p][CHUNK - 1:CHUNK, :] - lcol[p])).astype(MM_DTYPE))
        upd = [_mm_tn(jnp.concatenate(xdec[g * ppg:(g + 1) * ppg], axis=-1), bm[g]) for g in groups]
        y_intra = []
        for p in range(n_pairs):
            k0 = 2 * p
            lrow = jnp.concatenate([lam_t[k0:k0 + 1, :], lam_t[k0 + 1:k0 + 2, :]], axis=-1)
            m = cb2[p // ppg] * _masked_decay(lcol[p], lrow, incl2)
            blockdiag = jnp.concatenate([jnp.where(lo, xdt[p], 0.0), jnp.where(lo, 0.0, xdt[p])], axis=0)
            y_intra.append(_mm(m, blockdiag))
        for g in groups:
            ys = []
            for pp in range(ppg):
                p = g * ppg + pp
                k0 = 2 * p
                ys.append(y_intra[p] + jnp.exp(lcol[p]) * y_inter[g][:, pp * LANES:(pp + 1) * LANES]
                          + dskip_ref[:, p * LANES:(p + 1) * LANES] * xs[p])
                el = jnp.where(top, jnp.exp(llast[:, k0:k0 + 1]), jnp.exp(llast[:, k0 + 1:k0 + 2]))
                h_scr[p * LANES:(p + 1) * LANES, :] = (el * h_old[g][pp * LANES:(pp + 1) * LANES, :]
                                                        + upd[g][pp * LANES:(pp + 1) * LANES, :])
            yg = jnp.concatenate(ys, axis=-1) * _silu(z_ref[rows, g * SSD_GW:(g + 1) * SSD_GW])
            m_scr[rows, g * SSD_GW:(g + 1) * SSD_GW] = (
                _rms(yg, SSD_GW) * ng_ref[:, g * SSD_GW:(g + 1) * SSD_GW]).astype(m_scr.dtype)

    for c in range(tb // CHUNK):
        chunk_body(c)

    out = jnp.dot(m_scr[...], wo_ref[...], preferred_element_type=f32)
    y_ref[...] = x_ref[...] + _rms(out, D_MODEL) * pgo_ref[...]

    @pl.when(i == pl.num_programs(1) - 1)
    def _():
        conv_out_ref[...] = xp_scr[tb + 5:tb + 8, :]
        h_out_ref[...] = h_scr[...]


def _ssd_prompt(x, pre_g, w_in, cw, cb, hp, dskip, ng, expand, w_out, post_g, tb):
    b, l, d = x.shape
    kern = functools.partial(_ssd_prompt_kernel, tb=tb)
    wcol = lambda blk: (lambda bi, i: (0, blk))
    const = lambda bi, i: (0, 0)
    return pl.pallas_call(
        kern,
        grid=(b, l // tb),
        in_specs=[pl.BlockSpec((None, tb, d), lambda bi, i: (bi, i, 0)),
                  pl.BlockSpec((1, d), const),
                  pl.BlockSpec((d, SSD_DI), wcol(O_OFF_X // SSD_DI)),
                  pl.BlockSpec((d, SSD_DI), wcol(0)),
                  pl.BlockSpec((d, 2 * SSD_BC), wcol(O_OFF_BC // (2 * SSD_BC))),
                  pl.BlockSpec((d, LANES), wcol(O_OFF_DT // LANES)),
                  pl.BlockSpec((CONV_W, SSD_CONV_DIM), const),
                  pl.BlockSpec((1, SSD_CONV_DIM), const),
                  pl.BlockSpec((8, LANES), const),
                  pl.BlockSpec((1, SSD_DI), const),
                  pl.BlockSpec((1, SSD_DI), const),
                  pl.BlockSpec((LANES, SSD_DI), const),
                  pl.BlockSpec((SSD_DI, d), const),
                  pl.BlockSpec((1, d), const)],
        out_specs=[pl.BlockSpec((None, tb, d), lambda bi, i: (bi, i, 0)),
                   pl.BlockSpec((None, CONV_W - 1, SSD_CONV_DIM), lambda bi, i: (bi, 0, 0)),
                   pl.BlockSpec((None, SSD_HEADS * SSD_P, SSD_N), lambda bi, i: (bi, 0, 0))],
        out_shape=[jax.ShapeDtypeStruct((b, l, d), f32),
                   jax.ShapeDtypeStruct((b, CONV_W - 1, SSD_CONV_DIM), f32),
                   jax.ShapeDtypeStruct((b, SSD_HEADS * SSD_P, SSD_N), f32)],
        scratch_shapes=[pltpu.VMEM((tb + 8, SSD_CONV_DIM), f32),
                        pltpu.VMEM((tb, SSD_CONV_DIM), f32),
                        pltpu.VMEM((tb, SSD_DI), f32),
                        pltpu.VMEM((tb, LANES), f32),
                        pltpu.VMEM((tb, SSD_DI), f32),
                        pltpu.VMEM((tb, SSD_DI), f32),
                        pltpu.VMEM((SSD_HEADS * SSD_P, SSD_N), f32),
                        pltpu.VMEM((tb, SSD_DI), MM_DTYPE)],
        compiler_params=pltpu.CompilerParams(dimension_semantics=("parallel", "arbitrary"),
                                             vmem_limit_bytes=VMEM_LIMIT),
        name="ssd_prompt",
    )(x, pre_g, w_in, w_in, w_in, w_in, cw, cb, hp, dskip, ng, expand, w_out, post_g)


def _pad_rows(row, n):
    return jnp.concatenate([row, jnp.zeros((n - 1, row.shape[1]), row.dtype)], axis=0)


def _bcast_cols(row, width):
    hi, mid, lo = _split3(row)
    lhs = jnp.concatenate([hi, mid, lo, jnp.zeros((13, row.shape[1]), bf16)], axis=0)
    sel = (lax.broadcasted_iota(jnp.int32, (16, width), 0) < 3).astype(bf16)
    return lax.dot_general(lhs, sel, (((0,), (0,)), ((), ())), preferred_element_type=f32)


def _conv_step(buf_ref, x, cw_ref, cb_ref, new_buf_ref):
    acc = cb_ref[...] + x * cw_ref[3:4, :]
    for s in range(CONV_W - 1):
        acc = acc + buf_ref[s] * cw_ref[s:s + 1, :]
    for s in range(CONV_W - 2):
        new_buf_ref[s] = buf_ref[s + 1]
    new_buf_ref[CONV_W - 2] = x
    return _silu(acc)


def _gdn_decode_kernel(q_ref, k_ref, v_ref, gate_ref, small_ref, buf_ref, s_ref, cw_ref, cb_ref, hp_ref, ng_ref,
                       o_ref, new_buf_ref, s_out_ref, *, bb):
    x = jnp.concatenate([q_ref[...], k_ref[...], v_ref[...]], axis=-1)
    qkv = _conv_step(buf_ref, x, cw_ref, cb_ref, new_buf_ref)
    small = small_ref[...]
    beta = _sigmoid(small)
    eg = jnp.exp(hp_ref[0:1, :] * _softplus(small + hp_ref[1:2, :]))
    ng = ng_ref[...]
    gate = gate_ref[...]

    def head(h):
        c0 = h * GDN_DK
        q = qkv[:, c0:c0 + GDN_DK]
        q = q * lax.rsqrt(jnp.sum(q * q, axis=-1, keepdims=True) + EPS) * (GDN_DK ** -0.5)
        k = qkv[:, GDN_KW + c0:GDN_KW + c0 + GDN_DK]
        k = k * lax.rsqrt(jnp.sum(k * k, axis=-1, keepdims=True) + EPS)
        v = qkv[:, 2 * GDN_KW + c0:2 * GDN_KW + c0 + GDN_DV]
        qk = jnp.sum(q * k, axis=-1, keepdims=True)
        b_h = beta[:, SMALL_BETA + h:SMALL_BETA + h + 1]
        eg_h = eg[:, SMALL_A + h:SMALL_A + h + 1]
        seqs = range(bb)
        s_old = [s_ref[b, h] for b in seqs]
        r = [_mm(jnp.concatenate([k[b:b + 1], q[b:b + 1], jnp.zeros((6, GDN_DK), f32)], axis=0), s_old[b])
             for b in seqs]
        yield
        u = [b_h[b:b + 1] * v[b:b + 1] - (b_h[b:b + 1] * eg_h[b:b + 1]) * r[b][0:1] for b in seqs]
        upd = [_mm_tn(_pad_rows(k[b:b + 1], 16), _pad_rows(u[b], 16)) for b in seqs]
        yield
        for b in seqs:
            s_out_ref[b, h] = eg_h[b:b + 1] * s_old[b] + upd[b]
        o = jnp.concatenate([eg_h[b:b + 1] * r[b][1:2] + qk[b:b + 1] * u[b] for b in seqs], axis=0)
        o_ref[:, c0:c0 + GDN_DV] = (_rms(o, GDN_DV) * ng * _silu(gate[:, c0:c0 + GDN_DV])).astype(o_ref.dtype)

    for _ in _round_robin([head(h) for h in range(GDN_HEADS)]):
        pass


def _gdn_decode(proj, buf, s, cw, cb, hp, ng, bb):
    b = proj.shape[0]
    wq = GDN_KW
    kern = functools.partial(_gdn_decode_kernel, bb=bb)
    col = lambda blk: (lambda i: (i, blk))
    const = lambda i: (0, 0)
    return pl.pallas_call(
        kern,
        grid=(b // bb,),
        in_specs=[pl.BlockSpec((bb, wq), col(0)),
                  pl.BlockSpec((bb, wq), col(1)),
                  pl.BlockSpec((bb, wq), col(2)),
                  pl.BlockSpec((bb, wq), col(3)),
                  pl.BlockSpec((bb, LANES), col(E_OFF_GSMALL // LANES)),
                  pl.BlockSpec((CONV_W - 1, bb, GDN_CONV_DIM), lambda i: (0, i, 0)),
                  pl.BlockSpec((bb, GDN_HEADS, GDN_DK, GDN_DV), lambda i: (i, 0, 0, 0)),
                  pl.BlockSpec((CONV_W, GDN_CONV_DIM), const),
                  pl.BlockSpec((1, GDN_CONV_DIM), const),
                  pl.BlockSpec((8, LANES), const),
                  pl.BlockSpec((1, GDN_DV), const)],
        out_specs=[pl.BlockSpec((bb, GDN_VW), col(0)),
                   pl.BlockSpec((CONV_W - 1, bb, GDN_CONV_DIM), lambda i: (0, i, 0)),
                   pl.BlockSpec((bb, GDN_HEADS, GDN_DK, GDN_DV), lambda i: (i, 0, 0, 0))],
        out_shape=[jax.ShapeDtypeStruct((b, GDN_VW), MM_DTYPE),
                   jax.ShapeDtypeStruct((CONV_W - 1, b, GDN_CONV_DIM), f32),
                   jax.ShapeDtypeStruct((b, GDN_HEADS, GDN_DK, GDN_DV), f32)],
        compiler_params=pltpu.CompilerParams(dimension_semantics=("parallel",), vmem_limit_bytes=VMEM_LIMIT),
        name="gdn_decode",
    )(proj, proj, proj, proj, proj, buf, s, cw, cb, hp, ng)


def _gla_decode_kernel(q_ref, k_ref, v_ref, gate_ref, small_ref, s_ref, wlr_ref, blr_ref, ng_ref,
                       o_ref, s_out_ref, *, bb):
    z = _mm(small_ref[...], wlr_ref[...]) + blr_ref[...]
    la = -_softplus(-z) * (1.0 / GLA_TAU)
    dec = jnp.exp(la)
    qe = q_ref[...] * (GLA_DK ** -0.5) * dec
    kx = k_ref[...]
    ke = kx * jnp.exp(-la)
    vx = v_ref[...]
    ng = ng_ref[...]
    gate = gate_ref[...]
    for h in range(GLA_HEADS):
        k0 = h * GLA_DK
        v0 = h * GLA_DV
        p = jnp.sum(qe[:, k0:k0 + GLA_DK] * ke[:, k0:k0 + GLA_DK], axis=-1, keepdims=True)
        outs = []
        for b in range(bb):
            s_old = s_ref[b, h]
            vrow = vx[b:b + 1, v0:v0 + GLA_DV]
            outs.append(_mm(_pad_rows(qe[b:b + 1, k0:k0 + GLA_DK], 8), s_old)[0:1] + p[b:b + 1] * vrow)
            s_out_ref[b, h] = (_bcast_cols(dec[b:b + 1, k0:k0 + GLA_DK], GLA_DV) * s_old
                               + _mm_tn(_pad_rows(kx[b:b + 1, k0:k0 + GLA_DK], 16), _pad_rows(vrow, 16)))
        o = jnp.concatenate(outs, axis=0)
        o_ref[:, v0:v0 + GLA_DV] = (_rms(o, GLA_DV) * ng * _silu(gate[:, v0:v0 + GLA_DV])).astype(o_ref.dtype)


def _gla_decode(proj, s, wlr, blr, ng, bb):
    b = proj.shape[0]
    kern = functools.partial(_gla_decode_kernel, bb=bb)
    col = lambda blk: (lambda i: (i, blk))
    const = lambda i: (0, 0)
    return pl.pallas_call(
        kern,
        grid=(b // bb,),
        in_specs=[pl.BlockSpec((bb, GLA_KW), col(0)),
                  pl.BlockSpec((bb, GLA_KW), col(L_OFF_K // GLA_KW)),
                  pl.BlockSpec((bb, GLA_VW), col(L_OFF_V // GLA_VW)),
                  pl.BlockSpec((bb, GLA_VW), col(L_OFF_GATE // GLA_VW)),
                  pl.BlockSpec((bb, LANES), col(L_OFF_SMALL // LANES)),
                  pl.BlockSpec((bb, GLA_HEADS, GLA_DK, GLA_DV), lambda i: (i, 0, 0, 0)),
                  pl.BlockSpec((LANES, GLA_KW), const),
                  pl.BlockSpec((1, GLA_KW), const),
                  pl.BlockSpec((1, GLA_DV), const)],
        out_specs=[pl.BlockSpec((bb, GLA_VW), col(0)),
                   pl.BlockSpec((bb, GLA_HEADS, GLA_DK, GLA_DV), lambda i: (i, 0, 0, 0))],
        out_shape=[jax.ShapeDtypeStruct((b, GLA_VW), MM_DTYPE),
                   jax.ShapeDtypeStruct((b, GLA_HEADS, GLA_DK, GLA_DV), f32)],
        compiler_params=pltpu.CompilerParams(dimension_semantics=("parallel",), vmem_limit_bytes=VMEM_LIMIT),
        name="gla_decode",
    )(proj, proj, proj, proj, proj, s, wlr, blr, ng)


def _ssd_decode_kernel(x_ref, z_ref, bc_ref, dt_ref, buf_ref, h_ref, cw_ref, cb_ref, hp_ref, dskip_ref, ng_ref,
                       expand_ref, y_ref, new_buf_ref, h_out_ref, *, bb):
    xin = jnp.concatenate([x_ref[...], bc_ref[...]], axis=-1)
    xbc = _conv_step(buf_ref, xin, cw_ref, cb_ref, new_buf_ref)
    xs = xbc[:, :SSD_DI]
    dt = _softplus(_head_lanes(dt_ref[...]) + hp_ref[0:1, :])
    el = jnp.exp(dt * hp_ref[1:2, :])
    expand = expand_ref[...]
    dt_x = _sel_rows(dt, expand)
    el_x = _sel_rows(el, expand)
    xdt = xs * dt_x
    z = z_ref[...]
    for g in range(SSD_G):
        gs = slice(g * SSD_GW, (g + 1) * SSD_GW)
        bm = xbc[:, SSD_DI + g * SSD_N:SSD_DI + (g + 1) * SSD_N]
        cm = xbc[:, SSD_DI + SSD_BC + g * SSD_N:SSD_DI + SSD_BC + (g + 1) * SSD_N]
        cbs = jnp.sum(cm * bm, axis=-1, keepdims=True)
        outs = []
        for b in range(bb):
            hg = h_ref[b, gs, :]
            y_inter = _mm_nt(_pad_rows(cm[b:b + 1], 8), hg)[0:1]
            outs.append(el_x[b:b + 1, gs] * y_inter)
            h_out_ref[b, gs, :] = (_bcast_cols(el_x[b:b + 1, gs], SSD_N) * hg
                                   + _mm_tn(_pad_rows(xdt[b:b + 1, gs], 16), _pad_rows(bm[b:b + 1], 16)))
        y = jnp.concatenate(outs, axis=0) + cbs * xdt[:, gs] + dskip_ref[:, gs] * xs[:, gs]
        yg = y * _silu(z[:, gs])
        y_ref[:, gs] = (_rms(yg, SSD_GW) * ng_ref[:, gs]).astype(y_ref.dtype)


def _ssd_decode(proj, buf, hst, cw, cb, hp, dskip, ng, expand, bb):
    b = proj.shape[0]
    kern = functools.partial(_ssd_decode_kernel, bb=bb)
    col = lambda blk: (lambda i: (i, blk))
    const = lambda i: (0, 0)
    return pl.pallas_call(
        kern,
        grid=(b // bb,),
        in_specs=[pl.BlockSpec((bb, SSD_DI), col(O_OFF_X // SSD_DI)),
                  pl.BlockSpec((bb, SSD_DI), col(0)),
                  pl.BlockSpec((bb, 2 * SSD_BC), col(O_OFF_BC // (2 * SSD_BC))),
                  pl.BlockSpec((bb, LANES), col(O_OFF_DT // LANES)),
                  pl.BlockSpec((CONV_W - 1, bb, SSD_CONV_DIM), lambda i: (0, i, 0)),
                  pl.BlockSpec((bb, SSD_HEADS * SSD_P, SSD_N), lambda i: (i, 0, 0)),
                  pl.BlockSpec((CONV_W, SSD_CONV_DIM), const),
                  pl.BlockSpec((1, SSD_CONV_DIM), const),
                  pl.BlockSpec((8, LANES), const),
                  pl.BlockSpec((1, SSD_DI), const),
                  pl.BlockSpec((1, SSD_DI), const),
                  pl.BlockSpec((LANES, SSD_DI), const)],
        out_specs=[pl.BlockSpec((bb, SSD_DI), col(0)),
                   pl.BlockSpec((CONV_W - 1, bb, SSD_CONV_DIM), lambda i: (0, i, 0)),
                   pl.BlockSpec((bb, SSD_HEADS * SSD_P, SSD_N), lambda i: (i, 0, 0))],
        out_shape=[jax.ShapeDtypeStruct((b, SSD_DI), MM_DTYPE),
                   jax.ShapeDtypeStruct((CONV_W - 1, b, SSD_CONV_DIM), f32),
                   jax.ShapeDtypeStruct((b, SSD_HEADS * SSD_P, SSD_N), f32)],
        compiler_params=pltpu.CompilerParams(dimension_semantics=("parallel",), vmem_limit_bytes=VMEM_LIMIT),
        name="ssd_decode",
    )(proj, proj, proj, proj, buf, hst, cw, cb, hp, dskip, ng, expand)


def _pack_gla_w_in(w):
    gla = w[:, E_OFF_GLA:]
    return jnp.concatenate([gla, jnp.zeros((w.shape[0], L_N - gla.shape[1]), w.dtype)], axis=1)


def _lane_row(vals, offset):
    return jnp.zeros((LANES,), f32).at[offset:offset + vals.shape[0]].set(vals.astype(f32))


def kernel(x_prompt, x_sample, state_gdn_conv, state_gdn, state_gla, state_ssd_conv, state_ssd, e_pre_g, e_post_g, e_w_in, e_conv_w, e_conv_b, gdn_a_log, gdn_dt_bias, gdn_norm_g, gla_w_lr, gla_b_lr, gla_norm_g, e_w_out, o_pre_g, o_post_g, o_w_in, ssd_conv_w, ssd_conv_b, ssd_dt_bias, ssd_a_log, ssd_d, ssd_norm_g, o_w_out):
    bp, lp, d = x_prompt.shape
    bs = x_sample.shape[0]
    tp = bp * lp
    tb = min(lp, PROMPT_BLOCK)
    bb = min(bs, SAMPLE_BLOCK)
    bb_even = min(bs, SAMPLE_BLOCK_EVEN)

    w_in0 = e_w_in[0].astype(MM_DTYPE)
    w_gla = _pack_gla_w_in(w_in0)
    w_out0 = e_w_out[0].astype(MM_DTYPE)
    pre0 = e_pre_g[0].reshape(1, d)
    post0 = e_post_g[0].reshape(1, d)
    cw0 = e_conv_w[0]
    cb0 = e_conv_b[0].reshape(1, GDN_CONV_DIM)
    hp0 = jnp.zeros((8, LANES), f32)
    hp0 = hp0.at[0].set(_lane_row(-jnp.exp(gdn_a_log[0].astype(f32)), SMALL_A))
    hp0 = hp0.at[1].set(_lane_row(gdn_dt_bias[0], SMALL_A))
    ng_gdn = gdn_norm_g[0].reshape(1, GDN_DV)
    wlr = jnp.zeros((LANES, GLA_KW), f32).at[SMALL_LR:SMALL_LR + GLA_RANK].set(gla_w_lr[0]).astype(MM_DTYPE)
    blr = gla_b_lr[0].reshape(1, GLA_KW)
    ng_gla = gla_norm_g[0].reshape(1, GLA_DV)

    hp_flat = x_prompt.reshape(tp, d)
    hs_flat = x_sample.reshape(bs, d)

    o_p, gdn_conv_p, gdn_p, gla_p = _even_prompt(x_prompt, pre0, w_in0, w_gla, cw0, cb0, hp0, ng_gdn, wlr, blr, ng_gla,
                                                 tb)
    o_p = o_p.reshape(tp, GDN_VW + GLA_VW)
    hp_flat = _out_proj(o_p, o_p, w_out0, post0, hp_flat)

    proj_gdn = _norm_proj(hs_flat, pre0, w_in0, E_GDN_N // 3, E_GDN_N)
    proj_gla = _norm_proj(hs_flat, pre0, w_gla, L_N // 5, L_N)
    o1_s, gdn_conv_s, gdn_s = _gdn_decode(proj_gdn, jnp.swapaxes(state_gdn_conv[0], 0, 1), state_gdn[0],
                                          cw0, cb0, hp0, ng_gdn, bb_even)
    o2_s, gla_s = _gla_decode(proj_gla, state_gla[0], wlr, blr, ng_gla, bb_even)
    hs_flat = _out_proj(o1_s, o2_s, w_out0, post0, hs_flat)

    w_in1 = o_w_in[0].astype(MM_DTYPE)
    w_out1 = o_w_out[0].astype(MM_DTYPE)
    pre1 = o_pre_g[0].reshape(1, d)
    post1 = o_post_g[0].reshape(1, d)
    cw1 = ssd_conv_w[0]
    cb1 = ssd_conv_b[0].reshape(1, SSD_CONV_DIM)
    hp1 = jnp.zeros((8, LANES), f32)
    hp1 = hp1.at[0].set(_lane_row(ssd_dt_bias[0], 0))
    hp1 = hp1.at[1].set(_lane_row(-jnp.exp(ssd_a_log[0].astype(f32)), 0))
    dskip = jnp.repeat(ssd_d[0].astype(f32), SSD_P).reshape(1, SSD_DI)
    ng_ssd = ssd_norm_g[0].reshape(1, SSD_DI)
    expand = (lax.broadcasted_iota(jnp.int32, (LANES, SSD_DI), 0)
              == lax.broadcasted_iota(jnp.int32, (LANES, SSD_DI), 1) // SSD_P).astype(MM_DTYPE)

    y_p, ssd_conv_p, ssd_p = _ssd_prompt(hp_flat.reshape(bp, lp, d), pre1, w_in1, cw1, cb1, hp1, dskip, ng_ssd,
                                         expand, w_out1, post1, tb)
    hp_flat = y_p.reshape(tp, d)

    proj_s = _norm_proj(hs_flat, pre1, w_in1, O_PROJ_N // 6, O_PROJ_N)
    y_s, ssd_conv_s, ssd_s = _ssd_decode(proj_s, jnp.swapaxes(state_ssd_conv[0], 0, 1),
                                         state_ssd[0].reshape(bs, SSD_HEADS * SSD_P, SSD_N),
                                         cw1, cb1, hp1, dskip, ng_ssd, expand, bb)
    hs_flat = _out_proj(y_s, y_s, w_out1, post1, hs_flat)

    return (hp_flat.reshape(bp, lp, d), hs_flat.reshape(bs, 1, d),
            gdn_conv_p[None], gdn_p[None], gla_p[None],
            ssd_conv_p[None], ssd_p.reshape(bp, SSD_HEADS, SSD_P, SSD_N)[None],
            jnp.swapaxes(gdn_conv_s, 0, 1)[None], gdn_s[None], gla_s[None],
            jnp.swapaxes(ssd_conv_s, 0, 1)[None],
            ssd_s.reshape(bs, SSD_HEADS, SSD_P, SSD_N)[None])
```

```python
import functools

import jax
import jax.numpy as jnp
from jax import lax
from jax.experimental import pallas as pl
from jax.experimental.pallas import tpu as pltpu

f32 = jnp.float32
bf16 = jnp.bfloat16
MM_DTYPE = jnp.bfloat16

D_MODEL = 1024
CONV_W = 4
EPS = 1e-6
LANES = 128

GDN_HEADS = 8
GDN_DK = 128
GDN_DV = 128
GDN_KW = GDN_HEADS * GDN_DK
GDN_VW = GDN_HEADS * GDN_DV
GDN_CONV_DIM = 2 * GDN_KW + GDN_VW
GLA_HEADS = 4
GLA_DK = 128
GLA_DV = 256
GLA_KW = GLA_HEADS * GLA_DK
GLA_VW = GLA_HEADS * GLA_DV
GLA_RANK = 16
GLA_TAU = 16.0
SSD_DI = 2 * D_MODEL
SSD_P = 64
SSD_HEADS = SSD_DI // SSD_P
SSD_N = 128
SSD_G = 4
SSD_K = SSD_HEADS // SSD_G
SSD_GW = SSD_DI // SSD_G
SSD_BC = SSD_G * SSD_N
SSD_CONV_DIM = SSD_DI + 2 * SSD_BC

E_OFF_GATE = 3 * GDN_KW
E_OFF_GSMALL = 4 * GDN_KW
E_GDN_N = E_OFF_GSMALL + LANES
E_OFF_GLA = E_OFF_GSMALL + 2 * GDN_HEADS
SMALL_BETA = 0
SMALL_A = GDN_HEADS
L_OFF_K = GLA_KW
L_OFF_V = 2 * GLA_KW
L_OFF_GATE = L_OFF_V + GLA_VW
L_OFF_SMALL = L_OFF_GATE + GLA_VW
L_N = L_OFF_SMALL + LANES
SMALL_LR = 0
O_OFF_X = SSD_DI
O_OFF_BC = 2 * SSD_DI
O_OFF_DT = 2 * SSD_DI + 2 * SSD_BC
O_PROJ_N = 6 * 1024

CHUNK = 64
GDN_PREP_CHUNKS = 2
assert SSD_P == CHUNK and 2 * SSD_P == LANES
PROMPT_BLOCK = 512
SAMPLE_BLOCK = 8
SAMPLE_BLOCK_EVEN = 16
ROW_TILE = 1024
V7X_VMEM_BYTES = 64 * 1024 * 1024
VMEM_LIMIT = V7X_VMEM_BYTES - 8 * 1024 * 1024


def _mm(a, b):
    return jnp.dot(a.astype(MM_DTYPE), b.astype(MM_DTYPE), preferred_element_type=f32)


def _mm_nt(a, b):
    return lax.dot_general(a.astype(MM_DTYPE), b.astype(MM_DTYPE), (((1,), (1,)), ((), ())),
                           preferred_element_type=f32)


def _mm_tn(a, b):
    return lax.dot_general(a.astype(MM_DTYPE), b.astype(MM_DTYPE), (((0,), (0,)), ((), ())),
                           preferred_element_type=f32)


def _split3(x):
    hi = x.astype(bf16)
    r = x - hi.astype(f32)
    mid = r.astype(bf16)
    lo = (r - mid.astype(f32)).astype(bf16)
    return hi, mid, lo


def _sel_mm(sel, x):
    hi, mid, lo = _split3(x)
    s = sel.astype(bf16)
    return (jnp.dot(s, hi, preferred_element_type=f32) + jnp.dot(s, mid, preferred_element_type=f32)
            + jnp.dot(s, lo, preferred_element_type=f32))


def _sel_rows(x, sel):
    hi, mid, lo = _split3(x)
    s = sel.astype(bf16)
    return (jnp.dot(hi, s, preferred_element_type=f32) + jnp.dot(mid, s, preferred_element_type=f32)
            + jnp.dot(lo, s, preferred_element_type=f32))


def _sigmoid(x):
    return 1.0 / (1.0 + jnp.exp(-x))


def _silu(x):
    return x * _sigmoid(x)


def _softplus(x):
    return jnp.maximum(x, 0.0) + jnp.log1p(jnp.exp(-jnp.abs(x)))


def _tril(n, strict=False):
    r = lax.broadcasted_iota(jnp.int32, (n, n), 0)
    c = lax.broadcasted_iota(jnp.int32, (n, n), 1)
    return (r > c) if strict else (r >= c)


def _masked_decay(col, row, incl):
    return jnp.exp(jnp.where(incl, col - row, -jnp.inf))


def _interleave(main, side, ratio):
    gens = [main, side]
    steps = {id(main): ratio, id(side): 1}
    while gens:
        for g in list(gens):
            for _ in range(steps[id(g)]):
                try:
                    next(g)
                except StopIteration:
                    gens.remove(g)
                    break
                yield


def _round_robin(gens):
    gens = list(gens)
    while gens:
        for g in list(gens):
            try:
                next(g)
            except StopIteration:
                gens.remove(g)
        yield


def _rms(x, width):
    return x * lax.rsqrt(jnp.sum(x * x, axis=-1, keepdims=True) * (1.0 / width) + EPS)


def _norm_proj_kernel(x_ref, g_ref, w_ref, o_ref, u_scr):
    @pl.when(pl.program_id(1) == 0)
    def _():
        u_scr[...] = (_rms(x_ref[...], D_MODEL) * g_ref[...]).astype(u_scr.dtype)

    o_ref[...] = jnp.dot(u_scr[...], w_ref[...], preferred_element_type=f32)


def _norm_proj(x, g, w, tn, n):
    t, d = x.shape
    assert n % tn == 0 and n - tn < w.shape[1]
    tm = min(t, ROW_TILE)
    return pl.pallas_call(
        _norm_proj_kernel,
        grid=(t // tm, n // tn),
        in_specs=[pl.BlockSpec((tm, d), lambda i, j: (i, 0)),
                  pl.BlockSpec((1, d), lambda i, j: (0, 0)),
                  pl.BlockSpec((d, tn), lambda i, j: (0, j))],
        out_specs=pl.BlockSpec((tm, tn), lambda i, j: (i, j)),
        out_shape=jax.ShapeDtypeStruct((t, n), f32),
        scratch_shapes=[pltpu.VMEM((tm, d), MM_DTYPE)],
        compiler_params=pltpu.CompilerParams(dimension_semantics=("parallel", "arbitrary"),
                                             vmem_limit_bytes=VMEM_LIMIT),
        name="norm_proj",
    )(x, g, w)


def _out_proj_kernel(oa_ref, ob_ref, wa_ref, wb_ref, g_ref, h_ref, y_ref):
    p = (jnp.dot(oa_ref[...], wa_ref[...], preferred_element_type=f32)
         + jnp.dot(ob_ref[...], wb_ref[...], preferred_element_type=f32))
    y_ref[...] = h_ref[...] + _rms(p, D_MODEL) * g_ref[...]


def _out_proj(oa, ob, w, g, h):
    t = oa.shape[0]
    e = w.shape[0] // 2
    assert oa.shape[1] == ob.shape[1] and oa.shape[1] in (e, 2 * e)
    col_b = 1 if ob.shape[1] == 2 * e else 0
    d = w.shape[1]
    tm = min(t, ROW_TILE)
    return pl.pallas_call(
        _out_proj_kernel,
        grid=(t // tm,),
        in_specs=[pl.BlockSpec((tm, e), lambda i: (i, 0)),
                  pl.BlockSpec((tm, e), lambda i: (i, col_b)),
                  pl.BlockSpec((e, d), lambda i: (0, 0)),
                  pl.BlockSpec((e, d), lambda i: (1, 0)),
                  pl.BlockSpec((1, d), lambda i: (0, 0)),
                  pl.BlockSpec((tm, d), lambda i: (i, 0))],
        out_specs=pl.BlockSpec((tm, d), lambda i: (i, 0)),
        out_shape=jax.ShapeDtypeStruct((t, d), f32),
        compiler_params=pltpu.CompilerParams(dimension_semantics=("parallel",),
                                             vmem_limit_bytes=VMEM_LIMIT),
        name="out_proj",
    )(oa, ob, w, w, g, h)


def _normed(x_ref, g_ref):
    return (_rms(x_ref[...], D_MODEL) * g_ref[...]).astype(MM_DTYPE)


def _proj(u, w_ref, c0, c1):
    return jnp.dot(u, w_ref[:, c0:c1], preferred_element_type=f32)


def _conv_block(xp_scr, cw_ref, cb_ref, tb, first):
    @pl.when(first)
    def _():
        xp_scr[0:8, :] = jnp.zeros((8, xp_scr.shape[1]), f32)

    @pl.when(jnp.logical_not(first))
    def _():
        xp_scr[5:8, :] = xp_scr[tb + 5:tb + 8, :]

    def conv_cols(c0, c1, r0=0, nr=tb):
        acc = cb_ref[:, c0:c1] + xp_scr[8 + r0:8 + r0 + nr, c0:c1] * cw_ref[3:4, c0:c1]
        for s in range(CONV_W - 1):
            acc = acc + xp_scr[5 + s + r0:5 + s + r0 + nr, c0:c1] * cw_ref[s:s + 1, c0:c1]
        return _silu(acc)

    return conv_cols


def _gdn_stages(u, first, last, wm_ref, ws_ref, cw_ref, cb_ref, hp_ref, ng_ref,
                o_ref, conv_out_ref, s_out_ref,
                xp_scr, qkv_scr, gate_scr, beta_scr, lg_scr, s_scr, gl_scr, up_scr, w_scr, qg_scr, kd_scr,
                p_scr, *, tb):
    conv_cols = _conv_block(xp_scr, cw_ref, cb_ref, tb, first)

    @pl.when(first)
    def _():
        s_scr[...] = jnp.zeros(s_scr.shape, f32)

    def proj_stages():
        for c0 in range(0, GDN_CONV_DIM, GDN_KW):
            xp_scr[8:8 + tb, c0:c0 + GDN_KW] = _proj(u, wm_ref, c0, c0 + GDN_KW)
            yield
        small = _proj(u, ws_ref, 0, LANES)
        beta_scr[...] = _sigmoid(small)
        lg_scr[...] = hp_ref[0:1, :] * _softplus(small + hp_ref[1:2, :])
        gate_scr[...] = _proj(u, wm_ref, E_OFF_GATE, E_OFF_GATE + GDN_VW)
        yield

    span = GDN_PREP_CHUNKS * CHUNK

    def conv_norm(it):
        r0 = it * span
        rr = slice(r0, r0 + span)
        for h in range(GDN_HEADS):
            c0 = h * GDN_DK
            q = conv_cols(c0, c0 + GDN_DK, r0, span)
            q = q * lax.rsqrt(jnp.sum(q * q, axis=-1, keepdims=True) + EPS) * (GDN_DK ** -0.5)
            qkv_scr[rr, c0:c0 + GDN_DK] = q
            k = conv_cols(GDN_KW + c0, GDN_KW + c0 + GDN_DK, r0, span)
            k = k * lax.rsqrt(jnp.sum(k * k, axis=-1, keepdims=True) + EPS)
            qkv_scr[rr, GDN_KW + c0:GDN_KW + c0 + GDN_DK] = k
            qkv_scr[rr, 2 * GDN_KW + c0:2 * GDN_KW + c0 + GDN_DV] = conv_cols(
                2 * GDN_KW + c0, 2 * GDN_KW + c0 + GDN_DV, r0, span)
            yield

    incl = _tril(CHUNK)
    strict = _tril(CHUNK, strict=True)
    eye = jnp.logical_and(incl, jnp.logical_not(strict)).astype(f32)
    sr = lax.broadcasted_iota(jnp.int32, (span, span), 0)
    sc = lax.broadcasted_iota(jnp.int32, (span, span), 1)
    lmat = jnp.logical_and(sr >= sc, sr // CHUNK == sc // CHUNK).astype(f32)
    ng = ng_ref[...]
    heads = range(GDN_HEADS)
    probs = [(ci, h) for ci in range(GDN_PREP_CHUNKS) for h in heads]
    n = range(len(probs))

    def prep(c):
        r0 = c * span
        gam_all = _sel_mm(lmat, lg_scr[pl.ds(r0, span), :])
        gam_t = gam_all.T
        beta_all = beta_scr[pl.ds(r0, span), :]
        rows, gcol, grow, glast, bcol = [], [], [], [], []
        for ci, h in probs:
            lo, hi = ci * CHUNK, (ci + 1) * CHUNK
            rows.append(pl.ds(r0 + lo, CHUNK))
            gcol.append(gam_all[lo:hi, SMALL_A + h:SMALL_A + h + 1])
            grow.append(gam_t[SMALL_A + h:SMALL_A + h + 1, lo:hi])
            glast.append(gam_all[hi - 1:hi, SMALL_A + h:SMALL_A + h + 1])
            bcol.append(beta_all[lo:hi, SMALL_BETA + h:SMALL_BETA + h + 1])
        for ci in range(GDN_PREP_CHUNKS):
            gl_scr[pl.ds(c * GDN_PREP_CHUNKS + ci, 1), :] = jnp.exp(gam_all[(ci + 1) * CHUNK - 1:(ci + 1) * CHUNK, :])
        q = [qkv_scr[rows[j], h * GDN_DK:(h + 1) * GDN_DK] for j, (_, h) in enumerate(probs)]
        k = [qkv_scr[rows[j], GDN_KW + h * GDN_DK:GDN_KW + (h + 1) * GDN_DK] for j, (_, h) in enumerate(probs)]
        v = [qkv_scr[rows[j], 2 * GDN_KW + h * GDN_DV:2 * GDN_KW + (h + 1) * GDN_DV] for j, (_, h) in enumerate(probs)]
        kb = [k[j].astype(MM_DTYPE) for j in n]
        kk = [_mm_nt(kb[j], kb[j]) for j in n]
        yield
        qk = [_mm_nt(q[j], kb[j]) for j in n]
        yield
        decay = [_masked_decay(gcol[j], grow[j], incl) for j in n]
        eg = [jnp.exp(gcol[j]) for j in n]
        for j, (_, h) in enumerate(probs):
            c0 = h * GDN_DK
            qg_scr[rows[j], c0:c0 + GDN_DK] = (eg[j] * q[j]).astype(MM_DTYPE)
            kd_scr[rows[j], c0:c0 + GDN_DK] = (jnp.exp(glast[j] - gcol[j]) * k[j]).astype(MM_DTYPE)
            p_scr[h, rows[j], :] = (qk[j] * decay[j]).astype(MM_DTYPE)
        nmat = [jnp.where(strict, bcol[j] * (kk[j] * decay[j]), 0.0) for j in n]
        tinv = [eye - nmat[j] for j in n]
        npow = nmat
        for _ in range(5):
            npow = [_mm(npow[j], npow[j]) for j in n]
            yield
            tinv = [tinv[j] + _mm(tinv[j], npow[j]) for j in n]
            yield
        x = [_mm(tinv[j], jnp.concatenate([bcol[j] * v[j], (bcol[j] * eg[j]) * k[j]], axis=-1)) for j in n]
        for j, (_, h) in enumerate(probs):
            c0 = h * GDN_DK
            up_scr[rows[j], c0:c0 + GDN_DV] = x[j][:, :GDN_DV]
            w_scr[rows[j], c0:c0 + GDN_DK] = x[j][:, GDN_DV:].astype(MM_DTYPE)

    state = [s_scr[h] for h in heads]

    def scan(c):
        rows = pl.ds(c * CHUNK, CHUNK)
        gl = gl_scr[c:c + 1, :]
        r = [_mm(jnp.concatenate([w_scr[rows, h * GDN_DK:(h + 1) * GDN_DK],
                                  qg_scr[rows, h * GDN_DK:(h + 1) * GDN_DK]], axis=0), state[h]) for h in heads]
        yield
        us = [(up_scr[rows, h * GDN_DV:(h + 1) * GDN_DV] - r[h][:CHUNK]).astype(MM_DTYPE) for h in heads]
        o = [r[h][CHUNK:] + _mm(p_scr[h, rows, :], us[h]) for h in heads]
        yield
        upd = [_mm_tn(kd_scr[rows, h * GDN_DK:(h + 1) * GDN_DK], us[h]) for h in heads]
        yield
        for h in heads:
            c0 = h * GDN_DV
            state[h] = gl[:, SMALL_A + h:SMALL_A + h + 1] * state[h] + upd[h]
            gate = gate_scr[rows, c0:c0 + GDN_DV]
            o_ref[rows, c0:c0 + GDN_DV] = (_rms(o[h], GDN_DV) * ng * _silu(gate)).astype(o_ref.dtype)
        yield

    def scans(chunks):
        for c in chunks:
            yield from scan(c)

    def side_work(it):
        if it + 1 < n_prep:
            yield from conv_norm(it + 1)
        if it >= 1:
            yield from scans(range((it - 1) * GDN_PREP_CHUNKS, it * GDN_PREP_CHUNKS))

    n_prep = tb // span

    def stages():
        yield from proj_stages()
        yield from conv_norm(0)
        for it in range(n_prep):
            yield from _interleave(prep(it), side_work(it), 1)
        yield from scans(range((n_prep - 1) * GDN_PREP_CHUNKS, n_prep * GDN_PREP_CHUNKS))

    def finish():
        for h in heads:
            s_scr[h] = state[h]

        @pl.when(last)
        def _():
            conv_out_ref[...] = xp_scr[tb + 5:tb + 8, :]
            s_out_ref[...] = s_scr[...]

    return stages(), finish


def _gla_stages(u, first, last, wq_ref, wk_ref, wv_ref, wg_ref, ws_ref, wlr_ref, blr_ref, ng_ref,
                o_ref, s_out_ref, q_ref, k_ref, v_ref, gate_ref, la_scr, st_scr, *, tb):
    @pl.when(first)
    def _():
        st_scr[...] = jnp.zeros(st_scr.shape, f32)

    def proj_stages():
        q_ref[...] = _proj(u, wq_ref, 0, GLA_KW)
        k_ref[...] = _proj(u, wk_ref, 0, GLA_KW)
        small = _proj(u, ws_ref, 0, LANES)
        z = _mm(small, wlr_ref[...]) + blr_ref[...]
        la_scr[...] = -_softplus(-z) * (1.0 / GLA_TAU)
        yield
        v_ref[...] = _proj(u, wv_ref, 0, GLA_VW).astype(v_ref.dtype)
        yield
        gate_ref[...] = _proj(u, wg_ref, 0, GLA_VW)
        yield

    incl = _tril(CHUNK)
    lmat = incl.astype(f32)
    ng = ng_ref[...]

    heads = range(GLA_HEADS)
    n_chunks = tb // CHUNK
    staged = [None] * n_chunks
    state = [st_scr[h] for h in heads]

    def prep(c):
        rows = pl.ds(c * CHUNK, CHUNK)
        bcum = [_sel_mm(lmat, la_scr[rows, h * GLA_DK:(h + 1) * GLA_DK]) for h in heads]
        yield
        k = [k_ref[rows, h * GLA_DK:(h + 1) * GLA_DK] for h in heads]
        v = [v_ref[rows, h * GLA_DV:(h + 1) * GLA_DV].astype(MM_DTYPE) for h in heads]
        qe = [(q_ref[rows, h * GLA_DK:(h + 1) * GLA_DK] * (GLA_DK ** -0.5) * jnp.exp(bcum[h])).astype(MM_DTYPE)
              for h in heads]
        ke = [k[h] * jnp.exp(-bcum[h]) for h in heads]
        blast = [bcum[h][CHUNK - 1:CHUNK, :] for h in heads]
        kdec = [(k[h] * jnp.exp(blast[h] - bcum[h])).astype(MM_DTYPE) for h in heads]
        p = [jnp.where(incl, _mm_nt(qe[h], ke[h]), 0.0) for h in heads]
        yield
        o_intra = [_mm(p[h], v[h]) for h in heads]
        staged[c] = (qe, v, kdec, blast, o_intra)
        yield

    def scan(c):
        rows = pl.ds(c * CHUNK, CHUNK)
        qe, v, kdec, blast, o_intra = staged[c]
        o = [o_intra[h] + _mm_nt(qe[h], state[h]) for h in heads]
        upd = [_mm_tn(v[h], kdec[h]) for h in heads]
        yield
        for h in heads:
            v0 = h * GLA_DV
            state[h] = jnp.exp(blast[h]) * state[h] + upd[h]
            gate = gate_ref[rows, v0:v0 + GLA_DV]
            o_ref[rows, GDN_VW + v0:GDN_VW + v0 + GLA_DV] = (
                _rms(o[h], GLA_DV) * ng * _silu(gate)).astype(o_ref.dtype)
        yield

    def stages():
        yield from proj_stages()
        yield from prep(0)
        for c in range(1, n_chunks):
            yield from _interleave(prep(c), scan(c - 1), 1)
        yield from scan(n_chunks - 1)

    def finish():
        for h in heads:
            st_scr[h] = state[h]

        @pl.when(last)
        def _():
            for h in heads:
                s_out_ref[h] = st_scr[h].T

    return stages(), finish


N_GDN_SCRATCH = 12
EVEN_STAGE_RATIO = 2


def _even_prompt_kernel(x_ref, pg_ref, wm_ref, ws_ref, wq_ref, wk_ref, wv_ref, wg_ref, wsl_ref, cw_ref, cb_ref, hp_ref,
                        ngd_ref, wlr_ref, blr_ref, ngl_ref,
                        o_ref, conv_out_ref, sgdn_out_ref, sgla_out_ref, *scratch, tb):
    i = pl.program_id(1)
    first = i == 0
    last = i == pl.num_programs(1) - 1
    u = _normed(x_ref, pg_ref)
    gdn, gdn_finish = _gdn_stages(u, first, last, wm_ref, ws_ref, cw_ref, cb_ref, hp_ref, ngd_ref,
                                  o_ref, conv_out_ref, sgdn_out_ref, *scratch[:N_GDN_SCRATCH], tb=tb)
    gla, gla_finish = _gla_stages(u, first, last, wq_ref, wk_ref, wv_ref, wg_ref, wsl_ref, wlr_ref, blr_ref, ngl_ref,
                                  o_ref, sgla_out_ref, *scratch[N_GDN_SCRATCH:], tb=tb)
    for _ in _interleave(gdn, gla, EVEN_STAGE_RATIO):
        pass
    gdn_finish()
    gla_finish()


def _even_prompt(x, pre_g, w_in, w_gla, cw, cb, hp, ng_gdn, wlr, blr, ng_gla, tb):
    b, l, d = x.shape
    assert tb % (GDN_PREP_CHUNKS * CHUNK) == 0 and l % tb == 0
    kern = functools.partial(_even_prompt_kernel, tb=tb)
    wcol = lambda blk: (lambda bi, i: (0, blk))
    const = lambda bi, i: (0, 0)
    gdn_scratch = [pltpu.VMEM((tb + 8, GDN_CONV_DIM), f32),
                   pltpu.VMEM((tb, GDN_CONV_DIM), f32),
                   pltpu.VMEM((tb, GDN_VW), f32),
                   pltpu.VMEM((tb, LANES), f32),
                   pltpu.VMEM((tb, LANES), f32),
                   pltpu.VMEM((GDN_HEADS, GDN_DK, GDN_DV), f32),
                   pltpu.VMEM((max(8, tb // CHUNK), LANES), f32),
                   pltpu.VMEM((tb, GDN_VW), f32),
                   pltpu.VMEM((tb, GDN_KW), MM_DTYPE),
                   pltpu.VMEM((tb, GDN_KW), MM_DTYPE),
                   pltpu.VMEM((tb, GDN_KW), MM_DTYPE),
                   pltpu.VMEM((GDN_HEADS, tb, CHUNK), MM_DTYPE)]
    assert len(gdn_scratch) == N_GDN_SCRATCH
    gla_scratch = [pltpu.VMEM((tb, GLA_KW), f32),
                   pltpu.VMEM((tb, GLA_KW), f32),
                   pltpu.VMEM((tb, GLA_VW), MM_DTYPE),
                   pltpu.VMEM((tb, GLA_VW), f32),
                   pltpu.VMEM((tb, GLA_KW), f32),
                   pltpu.VMEM((GLA_HEADS, GLA_DV, GLA_DK), f32)]
    return pl.pallas_call(
        kern,
        grid=(b, l // tb),
        in_specs=[pl.BlockSpec((None, tb, d), lambda bi, i: (bi, i, 0)),
                  pl.BlockSpec((1, d), const),
                  pl.BlockSpec((d, E_OFF_GSMALL), const),
                  pl.BlockSpec((d, LANES), wcol(E_OFF_GSMALL // LANES)),
                  pl.BlockSpec((d, GLA_KW), wcol(0)),
                  pl.BlockSpec((d, GLA_KW), wcol(L_OFF_K // GLA_KW)),
                  pl.BlockSpec((d, GLA_VW), wcol(L_OFF_V // GLA_VW)),
                  pl.BlockSpec((d, GLA_VW), wcol(L_OFF_GATE // GLA_VW)),
                  pl.BlockSpec((d, LANES), wcol(L_OFF_SMALL // LANES)),
                  pl.BlockSpec((CONV_W, GDN_CONV_DIM), const),
                  pl.BlockSpec((1, GDN_CONV_DIM), const),
                  pl.BlockSpec((8, LANES), const),
                  pl.BlockSpec((1, GDN_DV), const),
                  pl.BlockSpec((LANES, GLA_KW), const),
                  pl.BlockSpec((1, GLA_KW), const),
                  pl.BlockSpec((1, GLA_DV), const)],
        out_specs=[pl.BlockSpec((None, tb, GDN_VW + GLA_VW), lambda bi, i: (bi, i, 0)),
                   pl.BlockSpec((None, CONV_W - 1, GDN_CONV_DIM), lambda bi, i: (bi, 0, 0)),
                   pl.BlockSpec((None, GDN_HEADS, GDN_DK, GDN_DV), lambda bi, i: (bi, 0, 0, 0)),
                   pl.BlockSpec((None, GLA_HEADS, GLA_DK, GLA_DV), lambda bi, i: (bi, 0, 0, 0))],
        out_shape=[jax.ShapeDtypeStruct((b, l, GDN_VW + GLA_VW), MM_DTYPE),
                   jax.ShapeDtypeStruct((b, CONV_W - 1, GDN_CONV_DIM), f32),
                   jax.ShapeDtypeStruct((b, GDN_HEADS, GDN_DK, GDN_DV), f32),
                   jax.ShapeDtypeStruct((b, GLA_HEADS, GLA_DK, GLA_DV), f32)],
        scratch_shapes=gdn_scratch + gla_scratch,
        compiler_params=pltpu.CompilerParams(dimension_semantics=("parallel", "arbitrary"),
                                             vmem_limit_bytes=VMEM_LIMIT),
        name="even_prompt",
    )(x, pre_g, w_in, w_in, w_gla, w_gla, w_gla, w_gla, w_gla, cw, cb, hp, ng_gdn, wlr, blr, ng_gla)


def _head_lanes(x):
    return jnp.where(lax.broadcasted_iota(jnp.int32, x.shape, 1) < SSD_HEADS, x, 0.0)


def _ssd_prompt_kernel(x_ref, pg_ref, wx_ref, wz_ref, wbc_ref, wdt_ref, cw_ref, cb_ref, hp_ref, dskip_ref, ng_ref,
                       expand_ref, wo_ref, pgo_ref, y_ref, conv_out_ref, h_out_ref,
                       xp_scr, xbc_scr, z_ref, lam_scr, lamx_scr, xdt_scr, h_scr, m_scr, *, tb):
    i = pl.program_id(1)
    first = i == 0
    conv_cols = _conv_block(xp_scr, cw_ref, cb_ref, tb, first)

    @pl.when(first)
    def _():
        h_scr[...] = jnp.zeros(h_scr.shape, f32)

    u = _normed(x_ref, pg_ref)
    for c0 in range(0, SSD_DI, 1024):
        xp_scr[8:8 + tb, c0:c0 + 1024] = _proj(u, wx_ref, c0, c0 + 1024)
    xp_scr[8:8 + tb, SSD_DI:SSD_CONV_DIM] = _proj(u, wbc_ref, 0, 2 * SSD_BC)
    dt_raw = _head_lanes(_proj(u, wdt_ref, 0, LANES))
    for c0 in range(0, SSD_DI, 1024):
        z_ref[:, c0:c0 + 1024] = _proj(u, wz_ref, c0, c0 + 1024)

    for c0 in range(0, SSD_CONV_DIM, 512):
        xbc_scr[:, c0:c0 + 512] = conv_cols(c0, c0 + 512)
    dt = _softplus(dt_raw + hp_ref[0:1, :])
    tr = lax.broadcasted_iota(jnp.int32, (tb, tb), 0)
    tc = lax.broadcasted_iota(jnp.int32, (tb, tb), 1)
    lam_all = _sel_mm(jnp.logical_and(tr >= tc, tr // CHUNK == tc // CHUNK).astype(f32), dt * hp_ref[1:2, :])
    lam_scr[...] = lam_all
    lo_blk = lax.broadcasted_iota(jnp.int32, (tb, LANES), 1) < SSD_P
    for p in range(SSD_HEADS // 2):
        cols = slice(p * LANES, (p + 1) * LANES)
        k0 = 2 * p
        xdt_scr[:, cols] = xbc_scr[:, cols] * jnp.where(lo_blk, dt[:, k0:k0 + 1], dt[:, k0 + 1:k0 + 2])
    for c0 in range(0, SSD_DI, 512):
        lamx_scr[:, c0:c0 + 512] = _sel_rows(lam_all, expand_ref[:, c0:c0 + 512])

    lane = lax.broadcasted_iota(jnp.int32, (CHUNK, LANES), 1)
    lo = lane < SSD_P
    incl2 = lax.broadcasted_iota(jnp.int32, (CHUNK, LANES), 0) >= lane % CHUNK
    top = lax.broadcasted_iota(jnp.int32, (2 * SSD_P, SSD_N), 0) < SSD_P
    n_pairs = SSD_HEADS // 2
    ppg = SSD_K // 2
    groups = range(SSD_G)

    def chunk_body(c):
        rows = pl.ds(c * CHUNK, CHUNK)
        lam = lam_scr[rows, :]
        lam_t = lam.T
        llast = lam[CHUNK - 1:CHUNK, :]
        bm = [xbc_scr[rows, SSD_DI + g * SSD_N:SSD_DI + (g + 1) * SSD_N].astype(MM_DTYPE) for g in groups]
        cm = [xbc_scr[rows, SSD_DI + SSD_BC + g * SSD_N:SSD_DI + SSD_BC + (g + 1) * SSD_N].astype(MM_DTYPE)
              for g in groups]
        h_old = [h_scr[g * SSD_GW:(g + 1) * SSD_GW, :] for g in groups]
        cb2 = [_mm_nt(cm[g], jnp.concatenate([bm[g], bm[g]], axis=0)) for g in groups]
        y_inter = [_mm_nt(cm[g], h_old[g]) for g in groups]
        xs, xdt, xdec, lcol = [], [], [], []
        for p in range(n_pairs):
            cols = slice(p * LANES, (p + 1) * LANES)
            xs.append(xbc_scr[rows, cols])
            lcol.append(lamx_scr[rows, cols])
            xdt.append(xdt_scr[rows, cols])
            xdec.append((xdt[p] * jnp.exp(lcol[p][CHUNK - 1:CHUNK, :] - lcol[p])).astype(MM_DTYPE))
        upd = [_mm_tn(jnp.concatenate(xdec[g * ppg:(g + 1) * ppg], axis=-1), bm[g]) for g in groups]
        y_intra = []
        for p in range(n_pairs):
            k0 = 2 * p
            lrow = jnp.concatenate([lam_t[k0:k0 + 1, :], lam_t[k0 + 1:k0 + 2, :]], axis=-1)
            m = cb2[p // ppg] * _masked_decay(lcol[p], lrow, incl2)
            blockdiag = jnp.concatenate([jnp.where(lo, xdt[p], 0.0), jnp.where(lo, 0.0, xdt[p])], axis=0)
            y_intra.append(_mm(m, blockdiag))
        for g in groups:
            ys = []
            for pp in range(ppg):
                p = g * ppg + pp
                k0 = 2 * p
                ys.append(y_intra[p] + jnp.exp(lcol[p]) * y_inter[g][:, pp * LANES:(pp + 1) * LANES]
                          + dskip_ref[:, p * LANES:(p + 1) * LANES] * xs[p])
                el = jnp.where(top, jnp.exp(llast[:, k0:k0 + 1]), jnp.exp(llast[:, k0 + 1:k0 + 2]))
                h_scr[p * LANES:(p + 1) * LANES, :] = (el * h_old[g][pp * LANES:(pp + 1) * LANES, :]
                                                        + upd[g][pp * LANES:(pp + 1) * LANES, :])
            yg = jnp.concatenate(ys, axis=-1) * _silu(z_ref[rows, g * SSD_GW:(g + 1) * SSD_GW])
            m_scr[rows, g * SSD_GW:(g + 1) * SSD_GW] = (
                _rms(yg, SSD_GW) * ng_ref[:, g * SSD_GW:(g + 1) * SSD_GW]).astype(m_scr.dtype)

    def project_out(r0, nr):
        out = jnp.dot(m_scr[r0:r0 + nr, :], wo_ref[...], preferred_element_type=f32)
        y_ref[r0:r0 + nr, :] = x_ref[r0:r0 + nr, :] + _rms(out, D_MODEL) * pgo_ref[...]

    n_chunks = tb // CHUNK
    half = (n_chunks // 2) * CHUNK
    for c in range(n_chunks):
        chunk_body(c)
        if half and c == n_chunks // 2 - 1:
            project_out(0, half)
    project_out(half, tb - half)

    @pl.when(i == pl.num_programs(1) - 1)
    def _():
        conv_out_ref[...] = xp_scr[tb + 5:tb + 8, :]
        h_out_ref[...] = h_scr[...]


def _ssd_prompt(x, pre_g, w_in, cw, cb, hp, dskip, ng, expand, w_out, post_g, tb):
    b, l, d = x.shape
    kern = functools.partial(_ssd_prompt_kernel, tb=tb)
    wcol = lambda blk: (lambda bi, i: (0, blk))
    const = lambda bi, i: (0, 0)
    return pl.pallas_call(
        kern,
        grid=(b, l // tb),
        in_specs=[pl.BlockSpec((None, tb, d), lambda bi, i: (bi, i, 0)),
                  pl.BlockSpec((1, d), const),
                  pl.BlockSpec((d, SSD_DI), wcol(O_OFF_X // SSD_DI)),
                  pl.BlockSpec((d, SSD_DI), wcol(0)),
                  pl.BlockSpec((d, 2 * SSD_BC), wcol(O_OFF_BC // (2 * SSD_BC))),
                  pl.BlockSpec((d, LANES), wcol(O_OFF_DT // LANES)),
                  pl.BlockSpec((CONV_W, SSD_CONV_DIM), const),
                  pl.BlockSpec((1, SSD_CONV_DIM), const),
                  pl.BlockSpec((8, LANES), const),
                  pl.BlockSpec((1, SSD_DI), const),
                  pl.BlockSpec((1, SSD_DI), const),
                  pl.BlockSpec((LANES, SSD_DI), const),
                  pl.BlockSpec((SSD_DI, d), const),
                  pl.BlockSpec((1, d), const)],
        out_specs=[pl.BlockSpec((None, tb, d), lambda bi, i: (bi, i, 0)),
                   pl.BlockSpec((None, CONV_W - 1, SSD_CONV_DIM), lambda bi, i: (bi, 0, 0)),
                   pl.BlockSpec((None, SSD_HEADS * SSD_P, SSD_N), lambda bi, i: (bi, 0, 0))],
        out_shape=[jax.ShapeDtypeStruct((b, l, d), f32),
                   jax.ShapeDtypeStruct((b, CONV_W - 1, SSD_CONV_DIM), f32),
                   jax.ShapeDtypeStruct((b, SSD_HEADS * SSD_P, SSD_N), f32)],
        scratch_shapes=[pltpu.VMEM((tb + 8, SSD_CONV_DIM), f32),
                        pltpu.VMEM((tb, SSD_CONV_DIM), f32),
                        pltpu.VMEM((tb, SSD_DI), f32),
                        pltpu.VMEM((tb, LANES), f32),
                        pltpu.VMEM((tb, SSD_DI), f32),
                        pltpu.VMEM((tb, SSD_DI), f32),
                        pltpu.VMEM((SSD_HEADS * SSD_P, SSD_N), f32),
                        pltpu.VMEM((tb, SSD_DI), MM_DTYPE)],
        compiler_params=pltpu.CompilerParams(dimension_semantics=("parallel", "arbitrary"),
                                             vmem_limit_bytes=VMEM_LIMIT),
        name="ssd_prompt",
    )(x, pre_g, w_in, w_in, w_in, w_in, cw, cb, hp, dskip, ng, expand, w_out, post_g)


def _pad_rows(row, n):
    return jnp.concatenate([row, jnp.zeros((n - 1, row.shape[1]), row.dtype)], axis=0)


def _bcast_cols(row, width):
    hi, mid, lo = _split3(row)
    lhs = jnp.concatenate([hi, mid, lo, jnp.zeros((13, row.shape[1]), bf16)], axis=0)
    sel = (lax.broadcasted_iota(jnp.int32, (16, width), 0) < 3).astype(bf16)
    return lax.dot_general(lhs, sel, (((0,), (0,)), ((), ())), preferred_element_type=f32)


def _conv_step(buf_ref, x, cw_ref, cb_ref, new_buf_ref):
    acc = cb_ref[...] + x * cw_ref[3:4, :]
    for s in range(CONV_W - 1):
        acc = acc + buf_ref[s] * cw_ref[s:s + 1, :]
    for s in range(CONV_W - 2):
        new_buf_ref[s] = buf_ref[s + 1]
    new_buf_ref[CONV_W - 2] = x
    return _silu(acc)


def _gdn_decode_kernel(q_ref, k_ref, v_ref, gate_ref, small_ref, buf_ref, s_ref, cw_ref, cb_ref, hp_ref, ng_ref,
                       o_ref, new_buf_ref, s_out_ref, *, bb):
    x = jnp.concatenate([q_ref[...], k_ref[...], v_ref[...]], axis=-1)
    qkv = _conv_step(buf_ref, x, cw_ref, cb_ref, new_buf_ref)
    small = small_ref[...]
    beta = _sigmoid(small)
    eg = jnp.exp(hp_ref[0:1, :] * _softplus(small + hp_ref[1:2, :]))
    ng = ng_ref[...]
    gate = gate_ref[...]

    def head(h):
        c0 = h * GDN_DK
        q = qkv[:, c0:c0 + GDN_DK]
        q = q * lax.rsqrt(jnp.sum(q * q, axis=-1, keepdims=True) + EPS) * (GDN_DK ** -0.5)
        k = qkv[:, GDN_KW + c0:GDN_KW + c0 + GDN_DK]
        k = k * lax.rsqrt(jnp.sum(k * k, axis=-1, keepdims=True) + EPS)
        v = qkv[:, 2 * GDN_KW + c0:2 * GDN_KW + c0 + GDN_DV]
        qk = jnp.sum(q * k, axis=-1, keepdims=True)
        b_h = beta[:, SMALL_BETA + h:SMALL_BETA + h + 1]
        eg_h = eg[:, SMALL_A + h:SMALL_A + h + 1]
        seqs = range(bb)
        s_old = [s_ref[b, h] for b in seqs]
        r = [_mm(jnp.concatenate([k[b:b + 1], q[b:b + 1], jnp.zeros((6, GDN_DK), f32)], axis=0), s_old[b])
             for b in seqs]
        yield
        u = [b_h[b:b + 1] * v[b:b + 1] - (b_h[b:b + 1] * eg_h[b:b + 1]) * r[b][0:1] for b in seqs]
        upd = [_mm_tn(_pad_rows(k[b:b + 1], 16), _pad_rows(u[b], 16)) for b in seqs]
        yield
        for b in seqs:
            s_out_ref[b, h] = eg_h[b:b + 1] * s_old[b] + upd[b]
        o = jnp.concatenate([eg_h[b:b + 1] * r[b][1:2] + qk[b:b + 1] * u[b] for b in seqs], axis=0)
        o_ref[:, c0:c0 + GDN_DV] = (_rms(o, GDN_DV) * ng * _silu(gate[:, c0:c0 + GDN_DV])).astype(o_ref.dtype)

    for _ in _round_robin([head(h) for h in range(GDN_HEADS)]):
        pass


def _gdn_decode(proj, buf, s, cw, cb, hp, ng, bb):
    b = proj.shape[0]
    wq = GDN_KW
    kern = functools.partial(_gdn_decode_kernel, bb=bb)
    col = lambda blk: (lambda i: (i, blk))
    const = lambda i: (0, 0)
    return pl.pallas_call(
        kern,
        grid=(b // bb,),
        in_specs=[pl.BlockSpec((bb, wq), col(0)),
                  pl.BlockSpec((bb, wq), col(1)),
                  pl.BlockSpec((bb, wq), col(2)),
                  pl.BlockSpec((bb, wq), col(3)),
                  pl.BlockSpec((bb, LANES), col(E_OFF_GSMALL // LANES)),
                  pl.BlockSpec((CONV_W - 1, bb, GDN_CONV_DIM), lambda i: (0, i, 0)),
                  pl.BlockSpec((bb, GDN_HEADS, GDN_DK, GDN_DV), lambda i: (i, 0, 0, 0)),
                  pl.BlockSpec((CONV_W, GDN_CONV_DIM), const),
                  pl.BlockSpec((1, GDN_CONV_DIM), const),
                  pl.BlockSpec((8, LANES), const),
                  pl.BlockSpec((1, GDN_DV), const)],
        out_specs=[pl.BlockSpec((bb, GDN_VW), col(0)),
                   pl.BlockSpec((CONV_W - 1, bb, GDN_CONV_DIM), lambda i: (0, i, 0)),
                   pl.BlockSpec((bb, GDN_HEADS, GDN_DK, GDN_DV), lambda i: (i, 0, 0, 0))],
        out_shape=[jax.ShapeDtypeStruct((b, GDN_VW), MM_DTYPE),
                   jax.ShapeDtypeStruct((CONV_W - 1, b, GDN_CONV_DIM), f32),
                   jax.ShapeDtypeStruct((b, GDN_HEADS, GDN_DK, GDN_DV), f32)],
        compiler_params=pltpu.CompilerParams(dimension_semantics=("parallel",), vmem_limit_bytes=VMEM_LIMIT),
        name="gdn_decode",
    )(proj, proj, proj, proj, proj, buf, s, cw, cb, hp, ng)


def _gla_decode_kernel(q_ref, k_ref, v_ref, gate_ref, small_ref, s_ref, wlr_ref, blr_ref, ng_ref,
                       o_ref, s_out_ref, *, bb):
    z = _mm(small_ref[...], wlr_ref[...]) + blr_ref[...]
    la = -_softplus(-z) * (1.0 / GLA_TAU)
    dec = jnp.exp(la)
    qe = q_ref[...] * (GLA_DK ** -0.5) * dec
    kx = k_ref[...]
    ke = kx * jnp.exp(-la)
    vx = v_ref[...]
    ng = ng_ref[...]
    gate = gate_ref[...]
    for h in range(GLA_HEADS):
        k0 = h * GLA_DK
        v0 = h * GLA_DV
        p = jnp.sum(qe[:, k0:k0 + GLA_DK] * ke[:, k0:k0 + GLA_DK], axis=-1, keepdims=True)
        outs = []
        for b in range(bb):
            s_old = s_ref[b, h]
            vrow = vx[b:b + 1, v0:v0 + GLA_DV]
            outs.append(_mm(_pad_rows(qe[b:b + 1, k0:k0 + GLA_DK], 8), s_old)[0:1] + p[b:b + 1] * vrow)
            s_out_ref[b, h] = (_bcast_cols(dec[b:b + 1, k0:k0 + GLA_DK], GLA_DV) * s_old
                               + _mm_tn(_pad_rows(kx[b:b + 1, k0:k0 + GLA_DK], 16), _pad_rows(vrow, 16)))
        o = jnp.concatenate(outs, axis=0)
        o_ref[:, v0:v0 + GLA_DV] = (_rms(o, GLA_DV) * ng * _silu(gate[:, v0:v0 + GLA_DV])).astype(o_ref.dtype)


def _gla_decode(proj, s, wlr, blr, ng, bb):
    b = proj.shape[0]
    kern = functools.partial(_gla_decode_kernel, bb=bb)
    col = lambda blk: (lambda i: (i, blk))
    const = lambda i: (0, 0)
    return pl.pallas_call(
        kern,
        grid=(b // bb,),
        in_specs=[pl.BlockSpec((bb, GLA_KW), col(0)),
                  pl.BlockSpec((bb, GLA_KW), col(L_OFF_K // GLA_KW)),
                  pl.BlockSpec((bb, GLA_VW), col(L_OFF_V // GLA_VW)),
                  pl.BlockSpec((bb, GLA_VW), col(L_OFF_GATE // GLA_VW)),
                  pl.BlockSpec((bb, LANES), col(L_OFF_SMALL // LANES)),
                  pl.BlockSpec((bb, GLA_HEADS, GLA_DK, GLA_DV), lambda i: (i, 0, 0, 0)),
                  pl.BlockSpec((LANES, GLA_KW), const),
                  pl.BlockSpec((1, GLA_KW), const),
                  pl.BlockSpec((1, GLA_DV), const)],
        out_specs=[pl.BlockSpec((bb, GLA_VW), col(0)),
                   pl.BlockSpec((bb, GLA_HEADS, GLA_DK, GLA_DV), lambda i: (i, 0, 0, 0))],
        out_shape=[jax.ShapeDtypeStruct((b, GLA_VW), MM_DTYPE),
                   jax.ShapeDtypeStruct((b, GLA_HEADS, GLA_DK, GLA_DV), f32)],
        compiler_params=pltpu.CompilerParams(dimension_semantics=("parallel",), vmem_limit_bytes=VMEM_LIMIT),
        name="gla_decode",
    )(proj, proj, proj, proj, proj, s, wlr, blr, ng)


def _ssd_decode_kernel(x_ref, z_ref, bc_ref, dt_ref, buf_ref, h_ref, cw_ref, cb_ref, hp_ref, dskip_ref, ng_ref,
                       expand_ref, y_ref, new_buf_ref, h_out_ref, *, bb):
    xin = jnp.concatenate([x_ref[...], bc_ref[...]], axis=-1)
    xbc = _conv_step(buf_ref, xin, cw_ref, cb_ref, new_buf_ref)
    xs = xbc[:, :SSD_DI]
    dt = _softplus(_head_lanes(dt_ref[...]) + hp_ref[0:1, :])
    el = jnp.exp(dt * hp_ref[1:2, :])
    expand = expand_ref[...]
    dt_x = _sel_rows(dt, expand)
    el_x = _sel_rows(el, expand)
    xdt = xs * dt_x
    z = z_ref[...]
    for g in range(SSD_G):
        gs = slice(g * SSD_GW, (g + 1) * SSD_GW)
        bm = xbc[:, SSD_DI + g * SSD_N:SSD_DI + (g + 1) * SSD_N]
        cm = xbc[:, SSD_DI + SSD_BC + g * SSD_N:SSD_DI + SSD_BC + (g + 1) * SSD_N]
        cbs = jnp.sum(cm * bm, axis=-1, keepdims=True)
        outs = []
        for b in range(bb):
            hg = h_ref[b, gs, :]
            y_inter = _mm_nt(_pad_rows(cm[b:b + 1], 8), hg)[0:1]
            outs.append(el_x[b:b + 1, gs] * y_inter)
            h_out_ref[b, gs, :] = (_bcast_cols(el_x[b:b + 1, gs], SSD_N) * hg
                                   + _mm_tn(_pad_rows(xdt[b:b + 1, gs], 16), _pad_rows(bm[b:b + 1], 16)))
        y = jnp.concatenate(outs, axis=0) + cbs * xdt[:, gs] + dskip_ref[:, gs] * xs[:, gs]
        yg = y * _silu(z[:, gs])
        y_ref[:, gs] = (_rms(yg, SSD_GW) * ng_ref[:, gs]).astype(y_ref.dtype)


def _ssd_decode(proj, buf, hst, cw, cb, hp, dskip, ng, expand, bb):
    b = proj.shape[0]
    kern = functools.partial(_ssd_decode_kernel, bb=bb)
    col = lambda blk: (lambda i: (i, blk))
    const = lambda i: (0, 0)
    return pl.pallas_call(
        kern,
        grid=(b // bb,),
        in_specs=[pl.BlockSpec((bb, SSD_DI), col(O_OFF_X // SSD_DI)),
                  pl.BlockSpec((bb, SSD_DI), col(0)),
                  pl.BlockSpec((bb, 2 * SSD_BC), col(O_OFF_BC // (2 * SSD_BC))),
                  pl.BlockSpec((bb, LANES), col(O_OFF_DT // LANES)),
                  pl.BlockSpec((CONV_W - 1, bb, SSD_CONV_DIM), lambda i: (0, i, 0)),
                  pl.BlockSpec((bb, SSD_HEADS * SSD_P, SSD_N), lambda i: (i, 0, 0)),
                  pl.BlockSpec((CONV_W, SSD_CONV_DIM), const),
                  pl.BlockSpec((1, SSD_CONV_DIM), const),
                  pl.BlockSpec((8, LANES), const),
                  pl.BlockSpec((1, SSD_DI), const),
                  pl.BlockSpec((1, SSD_DI), const),
                  pl.BlockSpec((LANES, SSD_DI), const)],
        out_specs=[pl.BlockSpec((bb, SSD_DI), col(0)),
                   pl.BlockSpec((CONV_W - 1, bb, SSD_CONV_DIM), lambda i: (0, i, 0)),
                   pl.BlockSpec((bb, SSD_HEADS * SSD_P, SSD_N), lambda i: (i, 0, 0))],
        out_shape=[jax.ShapeDtypeStruct((b, SSD_DI), MM_DTYPE),
                   jax.ShapeDtypeStruct((CONV_W - 1, b, SSD_CONV_DIM), f32),
                   jax.ShapeDtypeStruct((b, SSD_HEADS * SSD_P, SSD_N), f32)],
        compiler_params=pltpu.CompilerParams(dimension_semantics=("parallel",), vmem_limit_bytes=VMEM_LIMIT),
        name="ssd_decode",
    )(proj, proj, proj, proj, buf, hst, cw, cb, hp, dskip, ng, expand)


def _pack_gla_w_in(w):
    gla = w[:, E_OFF_GLA:]
    return jnp.concatenate([gla, jnp.zeros((w.shape[0], L_N - gla.shape[1]), w.dtype)], axis=1)


def _lane_row(vals, offset):
    return jnp.zeros((LANES,), f32).at[offset:offset + vals.shape[0]].set(vals.astype(f32))


def kernel(x_prompt, x_sample, state_gdn_conv, state_gdn, state_gla, state_ssd_conv, state_ssd, e_pre_g, e_post_g, e_w_in, e_conv_w, e_conv_b, gdn_a_log, gdn_dt_bias, gdn_norm_g, gla_w_lr, gla_b_lr, gla_norm_g, e_w_out, o_pre_g, o_post_g, o_w_in, ssd_conv_w, ssd_conv_b, ssd_dt_bias, ssd_a_log, ssd_d, ssd_norm_g, o_w_out):
    bp, lp, d = x_prompt.shape
    bs = x_sample.shape[0]
    tp = bp * lp
    tb = min(lp, PROMPT_BLOCK)
    bb = min(bs, SAMPLE_BLOCK)
    bb_even = min(bs, SAMPLE_BLOCK_EVEN)

    w_in0 = e_w_in[0].astype(MM_DTYPE)
    w_gla = _pack_gla_w_in(w_in0)
    w_out0 = e_w_out[0].astype(MM_DTYPE)
    pre0 = e_pre_g[0].reshape(1, d)
    post0 = e_post_g[0].reshape(1, d)
    cw0 = e_conv_w[0]
    cb0 = e_conv_b[0].reshape(1, GDN_CONV_DIM)
    hp0 = jnp.zeros((8, LANES), f32)
    hp0 = hp0.at[0].set(_lane_row(-jnp.exp(gdn_a_log[0].astype(f32)), SMALL_A))
    hp0 = hp0.at[1].set(_lane_row(gdn_dt_bias[0], SMALL_A))
    ng_gdn = gdn_norm_g[0].reshape(1, GDN_DV)
    wlr = jnp.zeros((LANES, GLA_KW), f32).at[SMALL_LR:SMALL_LR + GLA_RANK].set(gla_w_lr[0]).astype(MM_DTYPE)
    blr = gla_b_lr[0].reshape(1, GLA_KW)
    ng_gla = gla_norm_g[0].reshape(1, GLA_DV)

    hp_flat = x_prompt.reshape(tp, d)
    hs_flat = x_sample.reshape(bs, d)

    o_p, gdn_conv_p, gdn_p, gla_p = _even_prompt(x_prompt, pre0, w_in0, w_gla, cw0, cb0, hp0, ng_gdn, wlr, blr, ng_gla,
                                                 tb)
    o_p = o_p.reshape(tp, GDN_VW + GLA_VW)
    hp_flat = _out_proj(o_p, o_p, w_out0, post0, hp_flat)

    proj_gdn = _norm_proj(hs_flat, pre0, w_in0, E_GDN_N // 3, E_GDN_N)
    proj_gla = _norm_proj(hs_flat, pre0, w_gla, L_N // 5, L_N)
    o1_s, gdn_conv_s, gdn_s = _gdn_decode(proj_gdn, jnp.swapaxes(state_gdn_conv[0], 0, 1), state_gdn[0],
                                          cw0, cb0, hp0, ng_gdn, bb_even)
    o2_s, gla_s = _gla_decode(proj_gla, state_gla[0], wlr, blr, ng_gla, bb_even)
    hs_flat = _out_proj(o1_s, o2_s, w_out0, post0, hs_flat)

    w_in1 = o_w_in[0].astype(MM_DTYPE)
    w_out1 = o_w_out[0].astype(MM_DTYPE)
    pre1 = o_pre_g[0].reshape(1, d)
    post1 = o_post_g[0].reshape(1, d)
    cw1 = ssd_conv_w[0]
    cb1 = ssd_conv_b[0].reshape(1, SSD_CONV_DIM)
    hp1 = jnp.zeros((8, LANES), f32)
    hp1 = hp1.at[0].set(_lane_row(ssd_dt_bias[0], 0))
    hp1 = hp1.at[1].set(_lane_row(-jnp.exp(ssd_a_log[0].astype(f32)), 0))
    dskip = jnp.repeat(ssd_d[0].astype(f32), SSD_P).reshape(1, SSD_DI)
    ng_ssd = ssd_norm_g[0].reshape(1, SSD_DI)
    expand = (lax.broadcasted_iota(jnp.int32, (LANES, SSD_DI), 0)
              == lax.broadcasted_iota(jnp.int32, (LANES, SSD_DI), 1) // SSD_P).astype(MM_DTYPE)

    y_p, ssd_conv_p, ssd_p = _ssd_prompt(hp_flat.reshape(bp, lp, d), pre1, w_in1, cw1, cb1, hp1, dskip, ng_ssd,
                                         expand, w_out1, post1, tb)
    hp_flat = y_p.reshape(tp, d)

    proj_s = _norm_proj(hs_flat, pre1, w_in1, O_PROJ_N // 6, O_PROJ_N)
    y_s, ssd_conv_s, ssd_s = _ssd_decode(proj_s, jnp.swapaxes(state_ssd_conv[0], 0, 1),
                                         state_ssd[0].reshape(bs, SSD_HEADS * SSD_P, SSD_N),
                                         cw1, cb1, hp1, dskip, ng_ssd, expand, bb)
    hs_flat = _out_proj(y_s, y_s, w_out1, post1, hs_flat)

    return (hp_flat.reshape(bp, lp, d), hs_flat.reshape(bs, 1, d),
            gdn_conv_p[None], gdn_p[None], gla_p[None],
            ssd_conv_p[None], ssd_p.reshape(bp, SSD_HEADS, SSD_P, SSD_N)[None],
            jnp.swapaxes(gdn_conv_s, 0, 1)[None], gdn_s[None], gla_s[None],
            jnp.swapaxes(ssd_conv_s, 0, 1)[None],
            ssd_s.reshape(bs, SSD_HEADS, SSD_P, SSD_N)[None])
```
